```python
import math
import jax
import jax.numpy as jnp
from jax import lax
import numpy as np

D_MODEL = 1024
BATCH = 2
SEQ = 8192
DEPTH = 2
DEC_BATCH = 32
DEC_SEQ = 8
PAST_LEN = 16384
PAGE_SIZE = 128

N_A_LAYERS = DEPTH // 2
N_B_LAYERS = DEPTH - N_A_LAYERS
GMLP_CHUNK = 128
GMLP_GROUPS = 8
D_GATE = 2 * D_MODEL
N_HEADS = 8
N_KV_HEADS = 4
HEAD_DIM = D_MODEL // N_HEADS
MOBA_BLOCK = 256
MOBA_TOP_K = 3
MOBA_Q_CHUNK = 64
ROPE_THETA = 10000.0
N_GROUPS = 4
EXPERTS_PER_GROUP = 8
N_EXPERTS = N_GROUPS * EXPERTS_PER_GROUP
TOP_K_EXPERTS = 2
D_EXPERT = 512
EXPERT_BLOCK = 128
EPS = 1e-6

kernel_name = 'yoco_gmlp_moba_hmoe_step'


def rmsnorm(x, g):
    xf = x.astype(jnp.float32)
    y = xf * lax.rsqrt(jnp.mean(xf * xf, axis=-1, keepdims=True) + EPS)
    return (y * g.astype(jnp.float32)).astype(x.dtype)


def layernorm(x, g, b):
    xf = x.astype(jnp.float32)
    xc = xf - jnp.mean(xf, axis=-1, keepdims=True)
    var = jnp.mean(xc * xc, axis=-1, keepdims=True)
    return (xc * lax.rsqrt(var + EPS) * g.astype(jnp.float32) + b.astype(jnp.float32)).astype(x.dtype)


def rope(x, pos):
    half = HEAD_DIM // 2
    inv = ROPE_THETA ** (-jnp.arange(half, dtype=jnp.float32) * 2.0 / HEAD_DIM)
    ang = pos.astype(jnp.float32)[:, None] * inv[None, :]
    cos = jnp.cos(ang)[:, None, :]
    sin = jnp.sin(ang)[:, None, :]
    xf = x.astype(jnp.float32)
    x1, x2 = xf[..., :half], xf[..., half:]
    return jnp.concatenate([x1 * cos - x2 * sin, x2 * cos + x1 * sin], axis=-1).astype(x.dtype)


def chunk_spatial_gate(v, w_s, b_s):
    bsz, length, width = v.shape
    n_chunks = -(-length // GMLP_CHUNK)
    vp = jnp.pad(v, ((0, 0), (0, n_chunks * GMLP_CHUNK - length), (0, 0)))
    vp = vp.reshape(bsz, n_chunks, GMLP_CHUNK, GMLP_GROUPS, width // GMLP_GROUPS)
    causal = jnp.tril(jnp.ones((GMLP_CHUNK, GMLP_CHUNK), dtype=bool))
    w = jnp.where(causal[None], w_s, jnp.zeros_like(w_s))
    mixed = jnp.einsum('gts,bnsgc->bntgc', w, vp) + b_s.T[None, None, :, :, None]
    return mixed.reshape(bsz, n_chunks * GMLP_CHUNK, width)[:, :length]


def expert_ffn(x2, e, wts, w_gate, w_up, w_down):
    n_tok = x2.shape[0]
    n_asg = n_tok * TOP_K_EXPERTS
    e_flat = e.reshape(-1)
    order = jnp.argsort(e_flat)
    e_s = e_flat[order]
    tok_s = order // TOP_K_EXPERTS
    w_s = wts.reshape(-1)[order]
    counts = jnp.bincount(e_flat, length=N_EXPERTS)
    padded = (counts + EXPERT_BLOCK - 1) // EXPERT_BLOCK * EXPERT_BLOCK
    start = jnp.cumsum(counts) - counts
    pend = jnp.cumsum(padded)
    pstart = pend - padded
    dest = pstart[e_s] + jnp.arange(n_asg) - start[e_s]
    n_blocks = (n_asg + N_EXPERTS * (EXPERT_BLOCK - 1) + EXPERT_BLOCK - 1) // EXPERT_BLOCK
    buf = jnp.zeros((n_blocks * EXPERT_BLOCK, D_MODEL), x2.dtype).at[dest].set(x2[tok_s])
    blk_e = jnp.minimum(jnp.searchsorted(pend, jnp.arange(n_blocks) * EXPERT_BLOCK, side='right'), N_EXPERTS - 1)

    def run(args):
        xb, ei = args
        hb = jax.nn.silu(xb @ w_gate[ei]) * (xb @ w_up[ei])
        return hb @ w_down[ei]

    out = lax.map(run, (buf.reshape(n_blocks, EXPERT_BLOCK, D_MODEL), blk_e)).reshape(-1, D_MODEL)
    return jnp.zeros_like(x2).at[tok_s].add(out[dest] * w_s[:, None].astype(x2.dtype))


def hier_moe(x, w_grp, b_grp, w_rt, b_rt, w_gate, w_up, w_down):
    shp = x.shape
    x2 = x.reshape(-1, D_MODEL)
    n_tok = x2.shape[0]
    grp_logits = (x2 @ w_grp + b_grp).astype(jnp.float32)
    grp_prob = jax.nn.softmax(grp_logits, axis=-1)
    g = jnp.argmax(grp_logits, axis=-1)
    p_g = grp_prob[jnp.arange(n_tok), g][:, None]
    ex_logits = (x2 @ w_rt + b_rt).astype(jnp.float32).reshape(n_tok, N_GROUPS, EXPERTS_PER_GROUP)
    ex_in = ex_logits[jnp.arange(n_tok), g]
    top_v, top_j = lax.top_k(ex_in, TOP_K_EXPERTS)
    wts = p_g * jax.nn.softmax(top_v, axis=-1)
    e = g[:, None].astype(top_j.dtype) * EXPERTS_PER_GROUP + top_j
    return expert_ffn(x2, e, wts, w_gate, w_up, w_down).reshape(shp)


def moba_sequence(q, q_pos, k, v):
    t = k.shape[0]
    nb = -(-t // MOBA_BLOCK)
    pad = nb * MOBA_BLOCK - t
    kb = jnp.pad(k, ((0, pad), (0, 0), (0, 0))).reshape(nb, MOBA_BLOCK, N_KV_HEADS, HEAD_DIM).transpose(2, 0, 1, 3)
    vb = jnp.pad(v, ((0, pad), (0, 0), (0, 0))).reshape(nb, MOBA_BLOCK, N_KV_HEADS, HEAD_DIM).transpose(2, 0, 1, 3)
    kv_of_head = jnp.arange(N_HEADS) // (N_HEADS // N_KV_HEADS)
    means = jnp.mean(kb.astype(jnp.float32), axis=2)[kv_of_head]
    n_sel = min(MOBA_TOP_K, nb)
    n_q = q.shape[0]
    c = math.gcd(n_q, MOBA_Q_CHUNK)
    blk_off = jnp.arange(MOBA_BLOCK)
    scale = HEAD_DIM ** -0.5
    hsel = kv_of_head[None, :, None]

    def chunk(args):
        qc, pc = args
        own = pc // MOBA_BLOCK
        gate = jnp.einsum('chd,hnd->chn', qc.astype(jnp.float32), means)
        fully_past = jnp.arange(nb)[None, None, :] < own[:, None, None]
        gate = jnp.where(fully_past, gate, -jnp.inf)
        top_val, top_idx = lax.top_k(gate, n_sel)
        own_b = jnp.broadcast_to(own[:, None, None], (c, N_HEADS, 1)).astype(top_idx.dtype)
        idx = jnp.concatenate([top_idx, own_b], axis=-1)
        keep = jnp.concatenate([jnp.isfinite(top_val), jnp.ones((c, N_HEADS, 1), dtype=bool)], axis=-1)
        kg = kb[hsel, idx]
        vg = vb[hsel, idx]
        kpos = idx[..., None] * MOBA_BLOCK + blk_off
        mask = keep[..., None] & (kpos <= pc[:, None, None, None])
        logits = jnp.einsum('chd,chsbd->chsb', qc, kg, preferred_element_type=jnp.float32) * scale
        logits = jnp.where(mask, logits, -jnp.inf).reshape(c, N_HEADS, -1)
        p = jax.nn.softmax(logits, axis=-1).reshape(mask.shape).astype(vg.dtype)
        return jnp.einsum('chsb,chsbd->chd', p, vg)

    out = lax.map(chunk, (q.reshape(n_q // c, c, N_HEADS, HEAD_DIM), q_pos.reshape(n_q // c, c)))
    return out.reshape(n_q, N_HEADS, HEAD_DIM)


def trunk(x, pos, attend, p):
    bsz, length, _ = x.shape
    h = x
    k = None
    v = None
    gmlp_v = []
    for layer in range(DEPTH):
        if layer == N_A_LAYERS:
            kv = rmsnorm(h, p['kv_norm']) @ p['w_kv']
            k = kv[..., :N_KV_HEADS * HEAD_DIM].reshape(bsz, length, N_KV_HEADS, HEAD_DIM)
            k = rope(rmsnorm(k, p['k_norm']), pos)
            v = kv[..., N_KV_HEADS * HEAD_DIM:].reshape(bsz, length, N_KV_HEADS, HEAD_DIM)
        xn = rmsnorm(h, p['norm_mix'][layer])
        if layer < N_A_LAYERS:
            i = layer
            z = jax.nn.gelu(xn @ p['a_w_in'][i])
            u = z[..., :D_GATE]
            vg = layernorm(z[..., D_GATE:], p['a_ln_g'][i], p['a_ln_b'][i])
            gmlp_v.append(vg)
            h = h + (u * chunk_spatial_gate(vg, p['a_w_s'][i], p['a_b_s'][i])) @ p['a_w_out'][i]
        else:
            j = layer - N_A_LAYERS
            q = (xn @ p['b_w_q'][j]).reshape(bsz, length, N_HEADS, HEAD_DIM)
            q = rope(rmsnorm(q, p['b_q_norm'][j]), pos)
            o = attend(q, k, v)
            h = h + o.reshape(bsz, length, N_HEADS * HEAD_DIM) @ p['b_w_o'][j]
        h = h + hier_moe(rmsnorm(h, p['norm_ffn'][layer]), p['moe_w_grp'][layer], p['moe_b_grp'][layer],
                         p['moe_w_rt'][layer], p['moe_b_rt'][layer], p['moe_w_gate'][layer],
                         p['moe_w_up'][layer], p['moe_w_down'][layer])
    return h, k, v, gmlp_v


def setup_inputs(seed: int = 0) -> dict:
    key = jax.random.key(seed)
    ks = jax.random.split(key, 28)
    f32 = jnp.float32
    n_pages = PAST_LEN // PAGE_SIZE
    n_phys = (DEC_BATCH * n_pages * 5) // 4

    def w(k, shape, fan_in):
        return jax.random.normal(k, shape, f32) * fan_in ** -0.5

    def gain(k, shape):
        return 1.0 + 0.02 * jax.random.normal(k, shape, f32)

    page_table = jax.random.permutation(ks[4], n_phys)[:DEC_BATCH * n_pages].reshape(DEC_BATCH, n_pages).astype(jnp.int32)
    return {
        'x_prompt': jax.random.normal(ks[0], (BATCH, SEQ, D_MODEL), f32),
        'x_sample': jax.random.normal(ks[1], (DEC_BATCH, DEC_SEQ, D_MODEL), f32),
        'cache_k': jax.random.normal(ks[2], (n_phys, PAGE_SIZE, N_KV_HEADS, HEAD_DIM), f32),
        'cache_v': jax.random.normal(ks[3], (n_phys, PAGE_SIZE, N_KV_HEADS, HEAD_DIM), f32),
        'page_table': page_table,
        'norm_mix': gain(ks[5], (DEPTH, D_MODEL)),
        'norm_ffn': gain(ks[6], (DEPTH, D_MODEL)),
        'a_w_in': w(ks[7], (N_A_LAYERS, D_MODEL, 2 * D_GATE), D_MODEL),
        'a_ln_g': gain(ks[8], (N_A_LAYERS, D_GATE)),
        'a_ln_b': 0.02 * jax.random.normal(ks[9], (N_A_LAYERS, D_GATE), f32),
        'a_w_s': w(ks[10], (N_A_LAYERS, GMLP_GROUPS, GMLP_CHUNK, GMLP_CHUNK), GMLP_CHUNK),
        'a_b_s': 1.0 + 0.1 * jax.random.normal(ks[11], (N_A_LAYERS, GMLP_GROUPS, GMLP_CHUNK), f32),
        'a_w_out': w(ks[12], (N_A_LAYERS, D_GATE, D_MODEL), D_GATE),
        'kv_norm': gain(ks[13], (D_MODEL,)),
        'w_kv': w(ks[14], (D_MODEL, 2 * N_KV_HEADS * HEAD_DIM), D_MODEL),
        'k_norm': gain(ks[15], (HEAD_DIM,)),
        'b_w_q': w(ks[16], (N_B_LAYERS, D_MODEL, N_HEADS * HEAD_DIM), D_MODEL),
        'b_q_norm': gain(ks[17], (N_B_LAYERS, HEAD_DIM)),
        'b_w_o': w(ks[18], (N_B_LAYERS, N_HEADS * HEAD_DIM, D_MODEL), N_HEADS * HEAD_DIM),
        'moe_w_grp': w(ks[19], (DEPTH, D_MODEL, N_GROUPS), D_MODEL),
        'moe_b_grp': 0.01 * jax.random.normal(ks[20], (DEPTH, N_GROUPS), f32),
        'moe_w_rt': w(ks[21], (DEPTH, D_MODEL, N_EXPERTS), D_MODEL),
        'moe_b_rt': 0.01 * jax.random.normal(ks[22], (DEPTH, N_EXPERTS), f32),
        'moe_w_gate': w(ks[23], (DEPTH, N_EXPERTS, D_MODEL, D_EXPERT), D_MODEL),
        'moe_w_up': w(ks[24], (DEPTH, N_EXPERTS, D_MODEL, D_EXPERT), D_MODEL),
        'moe_w_down': w(ks[25], (DEPTH, N_EXPERTS, D_EXPERT, D_MODEL), D_EXPERT),
    }


def reference(x_prompt, x_sample, cache_k, cache_v, page_table, norm_mix, norm_ffn, a_w_in, a_ln_g, a_ln_b,
              a_w_s, a_b_s, a_w_out, kv_norm, w_kv, k_norm, b_w_q, b_q_norm, b_w_o, moe_w_grp, moe_b_grp,
              moe_w_rt, moe_b_rt, moe_w_gate, moe_w_up, moe_w_down):
    p = {'norm_mix': norm_mix, 'norm_ffn': norm_ffn, 'a_w_in': a_w_in, 'a_ln_g': a_ln_g, 'a_ln_b': a_ln_b,
         'a_w_s': a_w_s, 'a_b_s': a_b_s, 'a_w_out': a_w_out, 'kv_norm': kv_norm, 'w_kv': w_kv,
         'k_norm': k_norm, 'b_w_q': b_w_q, 'b_q_norm': b_q_norm, 'b_w_o': b_w_o, 'moe_w_grp': moe_w_grp,
         'moe_b_grp': moe_b_grp, 'moe_w_rt': moe_w_rt, 'moe_b_rt': moe_b_rt, 'moe_w_gate': moe_w_gate,
         'moe_w_up': moe_w_up, 'moe_w_down': moe_w_down}

    def attend_prompt(q, k, v):
        q_pos = jnp.arange(q.shape[1])
        return lax.map(lambda a: moba_sequence(a[0], q_pos, a[1], a[2]), (q, k, v))

    past_len = page_table.shape[1] * PAGE_SIZE

    def attend_sample(q, k_new, v_new):
        q_pos = past_len + jnp.arange(q.shape[1])

        def one(a):
            qb, kn, vn, pt = a
            k_full = jnp.concatenate([cache_k[pt].reshape(past_len, N_KV_HEADS, HEAD_DIM), kn], axis=0)
            v_full = jnp.concatenate([cache_v[pt].reshape(past_len, N_KV_HEADS, HEAD_DIM), vn], axis=0)
            return moba_sequence(qb, q_pos, k_full, v_full)

        return lax.map(one, (q, k_new, v_new, page_table))

    y_prompt, k_p, v_p, _ = trunk(x_prompt, jnp.arange(x_prompt.shape[1]), attend_prompt, p)
    y_sample, k_s, v_s, gv_s = trunk(x_sample, past_len + jnp.arange(x_sample.shape[1]), attend_sample, p)
    pages = (x_prompt.shape[0], -1, PAGE_SIZE, N_KV_HEADS, HEAD_DIM)
    return (y_prompt, y_sample, k_p.reshape(pages), v_p.reshape(pages), k_s, v_s, jnp.stack(gv_s))
```

```python
import functools
import math

import jax
import jax.numpy as jnp
from jax import lax
from jax.experimental import pallas as pl
from jax.experimental.pallas import tpu as pltpu

F32 = jnp.float32
BF16 = jnp.bfloat16
HIGHEST = lax.Precision.HIGHEST

GMLP_CHUNK = 128
GMLP_GROUPS = 8
N_HEADS = 8
N_KV_HEADS = 4
HEAD_DIM = 128
Q_PER_KV = N_HEADS // N_KV_HEADS
MOBA_BLOCK = 256
MOBA_TOP_K = 3
ROPE_THETA = 10000.0
N_GROUPS = 4
EXPERTS_PER_GROUP = 8
N_EXPERTS = N_GROUPS * EXPERTS_PER_GROUP
TOP_K_EXPERTS = 2
PAGE_SIZE = 128
EPS = 1e-6

LANES = 128
TOKEN_TILE = 256
FFN_ROWS = 256
PAGES_PER_STEP = 8
VMEM_LIMIT = 56 * 1024 * 1024

NEG_INF = float("-inf")


def _cparams(sem):
    return pltpu.CompilerParams(dimension_semantics=sem, vmem_limit_bytes=VMEM_LIMIT)


def _rms(x):
    return x * lax.rsqrt(jnp.mean(x * x, axis=-1, keepdims=True) + EPS)


def _gmlp_kernel(x_ref, g_ref, win_ref, lng_ref, lnb_ref, mix_ref, bias_ref, wout_ref,
                 h_ref, vg_ref, *, d_gate, n_groups):
    i = pl.program_id(0)
    x = x_ref[...]
    xb = (_rms(x) * g_ref[...]).astype(BF16)
    u = jax.nn.gelu(jnp.dot(xb, win_ref[:, :d_gate], preferred_element_type=F32))
    vp = jax.nn.gelu(jnp.dot(xb, win_ref[:, d_gate:], preferred_element_type=F32))
    vc = vp - jnp.mean(vp, axis=-1, keepdims=True)
    var = jnp.mean(vc * vc, axis=-1, keepdims=True)
    vg = vc * lax.rsqrt(var + EPS) * lng_ref[...] + lnb_ref[...]

    @pl.when(i == pl.num_programs(0) - 1)
    def _():
        vg_ref[...] = vg

    vgb = vg.astype(BF16)
    cw = d_gate // n_groups
    bias = bias_ref[0]
    parts = []
    for g in range(n_groups):
        mixed = jnp.dot(mix_ref[0, g], vgb[:, g * cw:(g + 1) * cw], preferred_element_type=F32)
        mixed = mixed + bias[:, g:g + 1]
        parts.append((u[:, g * cw:(g + 1) * cw] * mixed).astype(BF16))
    gated = jnp.concatenate(parts, axis=1)
    h_ref[...] = x + jnp.dot(gated, wout_ref[...], preferred_element_type=F32)


def _gmlp_layer(x, g, w_in, ln_g, ln_b, mix, bias, w_out, n_prompt_tiles):
    n, d = x.shape
    d_gate = w_out.shape[0]
    tm = TOKEN_TILE
    n_tiles = n // tm
    kind = lambda i: jnp.where(i < n_prompt_tiles, 0, 1)
    return pl.pallas_call(
        functools.partial(_gmlp_kernel, d_gate=d_gate, n_groups=GMLP_GROUPS),
        grid=(n_tiles,),
        in_specs=[
            pl.BlockSpec((tm, d), lambda i: (i, 0)),
            pl.BlockSpec((1, d), lambda i: (0, 0)),
            pl.BlockSpec((d, 2 * d_gate), lambda i: (0, 0)),
            pl.BlockSpec((1, d_gate), lambda i: (0, 0)),
            pl.BlockSpec((1, d_gate), lambda i: (0, 0)),
            pl.BlockSpec((1, GMLP_GROUPS, tm, tm), lambda i: (kind(i), 0, 0, 0)),
            pl.BlockSpec((1, tm, LANES), lambda i: (kind(i), 0, 0)),
            pl.BlockSpec((d_gate, d), lambda i: (0, 0)),
        ],
        out_specs=[
            pl.BlockSpec((tm, d), lambda i: (i, 0)),
            pl.BlockSpec((tm, d_gate), lambda i: (0, 0)),
        ],
        out_shape=[
            jax.ShapeDtypeStruct((n, d), F32),
            jax.ShapeDtypeStruct((tm, d_gate), F32),
        ],
        compiler_params=_cparams(("arbitrary",)),
        name="gmlp_layer",
    )(x, g, w_in, ln_g, ln_b, mix, bias, w_out)


def _router_kernel(h_ref, g_ref, wr_ref, br_ref, xn_ref, e_ref, w_ref):
    xn = _rms(h_ref[...]) * g_ref[...]
    xn_ref[...] = xn
    logits = jnp.dot(xn, wr_ref[...], precision=HIGHEST, preferred_element_type=F32) + br_ref[...]
    lane = lax.broadcasted_iota(jnp.int32, logits.shape, 1)
    big = jnp.int32(LANES)
    is_grp = lane < N_GROUPS
    gl = jnp.where(is_grp, logits, NEG_INF)
    gmax = jnp.max(gl, axis=1, keepdims=True)
    gidx = jnp.min(jnp.where(is_grp & (logits == gmax), lane, big), axis=1, keepdims=True)
    p_g = 1.0 / jnp.sum(jnp.where(is_grp, jnp.exp(gl - gmax), 0.0), axis=1, keepdims=True)
    lo = N_GROUPS + gidx * EXPERTS_PER_GROUP
    in_grp = (lane >= lo) & (lane < lo + EXPERTS_PER_GROUP)
    v0 = jnp.max(jnp.where(in_grp, logits, NEG_INF), axis=1, keepdims=True)
    i0 = jnp.min(jnp.where(in_grp & (logits == v0), lane, big), axis=1, keepdims=True)
    rest = in_grp & (lane != i0)
    v1 = jnp.max(jnp.where(rest, logits, NEG_INF), axis=1, keepdims=True)
    i1 = jnp.min(jnp.where(rest & (logits == v1), lane, big), axis=1, keepdims=True)
    t = jnp.exp(v1 - v0)
    w0 = p_g * (1.0 / (1.0 + t))
    w1 = p_g * (t / (1.0 + t))
    e_ref[...] = jnp.where(lane == 0, i0 - N_GROUPS, jnp.where(lane == 1, i1 - N_GROUPS, 0))
    w_ref[...] = jnp.where(lane == 0, w0, jnp.where(lane == 1, w1, 0.0))


def _router(h, g, w_r, b_r):
    n, d = h.shape
    tm = TOKEN_TILE
    return pl.pallas_call(
        _router_kernel,
        grid=(n // tm,),
        in_specs=[
            pl.BlockSpec((tm, d), lambda i: (i, 0)),
            pl.BlockSpec((1, d), lambda i: (0, 0)),
            pl.BlockSpec((d, LANES), lambda i: (0, 0)),
            pl.BlockSpec((1, LANES), lambda i: (0, 0)),
        ],
        out_specs=[
            pl.BlockSpec((tm, d), lambda i: (i, 0)),
            pl.BlockSpec((tm, LANES), lambda i: (i, 0)),
            pl.BlockSpec((tm, LANES), lambda i: (i, 0)),
        ],
        out_shape=[
            jax.ShapeDtypeStruct((n, d), F32),
            jax.ShapeDtypeStruct((n, LANES), jnp.int32),
            jax.ShapeDtypeStruct((n, LANES), F32),
        ],
        compiler_params=_cparams(("arbitrary",)),
        name="moe_router",
    )(h, g, w_r, b_r)


def _ffn_kernel(blk_e_ref, nblk_ref, tok_ref, xn_hbm, wg_ref, wu_ref, wd_ref, wrow_ref,
                out_ref, xbuf, sem):
    i = pl.program_id(0)
    rows = xbuf.shape[0]

    @pl.when(i < nblk_ref[0])
    def _():
        base = i * rows

        def issue(r, carry):
            t = tok_ref[base + r]
            pltpu.make_async_copy(xn_hbm.at[pl.ds(t, 1), :], xbuf.at[pl.ds(r, 1), :], sem).start()
            return carry

        lax.fori_loop(0, rows, issue, 0)
        pltpu.make_async_copy(xn_hbm.at[pl.ds(0, rows), :], xbuf, sem).wait()
        x = xbuf[...].astype(BF16)
        gate = jnp.dot(x, wg_ref[0], preferred_element_type=F32)
        up = jnp.dot(x, wu_ref[0], preferred_element_type=F32)
        hid = (jax.nn.silu(gate) * up).astype(BF16)
        out = jnp.dot(hid, wd_ref[0], preferred_element_type=F32)
        out_ref[...] = out * wrow_ref[...]

    @pl.when(i >= nblk_ref[0])
    def _():
        out_ref[...] = jnp.zeros_like(out_ref)


def _grouped_ffn(xn, blk_e, nblk, tok_pad, wrow, w_gate, w_up, w_down):
    n, d = xn.shape
    n_rows = tok_pad.shape[0]
    rb = FFN_ROWS
    d_e = w_gate.shape[-1]
    grid_spec = pltpu.PrefetchScalarGridSpec(
        num_scalar_prefetch=3,
        grid=(n_rows // rb,),
        in_specs=[
            pl.BlockSpec(memory_space=pl.ANY),
            pl.BlockSpec((1, d, d_e), lambda i, be, nb, tk: (be[i], 0, 0)),
            pl.BlockSpec((1, d, d_e), lambda i, be, nb, tk: (be[i], 0, 0)),
            pl.BlockSpec((1, d_e, d), lambda i, be, nb, tk: (be[i], 0, 0)),
            pl.BlockSpec((rb, 1), lambda i, be, nb, tk: (i, 0)),
        ],
        out_specs=pl.BlockSpec((rb, d), lambda i, be, nb, tk: (i, 0)),
        scratch_shapes=[pltpu.VMEM((rb, d), F32), pltpu.SemaphoreType.DMA(())],
    )
    return pl.pallas_call(
        _ffn_kernel,
        grid_spec=grid_spec,
        out_shape=jax.ShapeDtypeStruct((n_rows, d), F32),
        compiler_params=_cparams(("arbitrary",)),
        name="moe_ffn",
    )(blk_e, nblk, tok_pad, xn, w_gate, w_up, w_down, wrow)


def _combine_kernel(pos_ref, resid_ref, src_hbm, out_ref, buf, sem):
    i = pl.program_id(0)
    tm = resid_ref.shape[0]
    base = i * tm * TOP_K_EXPERTS

    def issue(t, carry):
        for k in range(TOP_K_EXPERTS):
            p = pos_ref[base + t * TOP_K_EXPERTS + k]
            pltpu.make_async_copy(src_hbm.at[pl.ds(p, 1), :], buf.at[k, pl.ds(t, 1), :], sem).start()
        return carry

    lax.fori_loop(0, tm, issue, 0)
    for k in range(TOP_K_EXPERTS):
        pltpu.make_async_copy(src_hbm.at[pl.ds(0, tm), :], buf.at[k], sem).wait()
    acc = resid_ref[...]
    for k in range(TOP_K_EXPERTS):
        acc = acc + buf[k]
    out_ref[...] = acc


def _combine(resid, src, pos):
    n, d = resid.shape
    tm = TOKEN_TILE
    grid_spec = pltpu.PrefetchScalarGridSpec(
        num_scalar_prefetch=1,
        grid=(n // tm,),
        in_specs=[
            pl.BlockSpec((tm, d), lambda i, ps: (i, 0)),
            pl.BlockSpec(memory_space=pl.ANY),
        ],
        out_specs=pl.BlockSpec((tm, d), lambda i, ps: (i, 0)),
        scratch_shapes=[pltpu.VMEM((TOP_K_EXPERTS, tm, d), F32), pltpu.SemaphoreType.DMA(())],
    )
    return pl.pallas_call(
        _combine_kernel,
        grid_spec=grid_spec,
        out_shape=jax.ShapeDtypeStruct((n, d), F32),
        compiler_params=_cparams(("arbitrary",)),
        name="moe_combine",
    )(pos, resid, src)


def _hier_moe(h, g, w_r, b_r, w_gate, w_up, w_down):
    n, _ = h.shape
    xn, e_pad, w_pad = _router(h, g, w_r, b_r)
    e_flat = e_pad[:, :TOP_K_EXPERTS].reshape(-1)
    w_flat = w_pad[:, :TOP_K_EXPERTS].reshape(-1)
    n_asg = n * TOP_K_EXPERTS
    rb = FFN_ROWS
    onehot = (e_flat[:, None] == jnp.arange(N_EXPERTS, dtype=jnp.int32)[None, :]).astype(jnp.int32)
    csum = jnp.cumsum(onehot, axis=0)
    rank = jnp.take_along_axis(csum, e_flat[:, None], axis=1)[:, 0] - 1
    counts = csum[-1]
    padded = (counts + rb - 1) // rb * rb
    pend = jnp.cumsum(padded)
    pstart = pend - padded
    pos = (pstart[e_flat] + rank).astype(jnp.int32)
    n_blocks = (n_asg + N_EXPERTS * (rb - 1) + rb - 1) // rb
    n_rows = n_blocks * rb
    tok_pad = jnp.zeros((n_rows,), jnp.int32).at[pos].set(jnp.arange(n_asg, dtype=jnp.int32) // TOP_K_EXPERTS)
    wrow = jnp.zeros((n_rows,), F32).at[pos].set(w_flat).reshape(n_rows, 1)
    blk_e = jnp.minimum(jnp.searchsorted(pend, jnp.arange(n_blocks, dtype=jnp.int32) * rb, side="right"),
                        N_EXPERTS - 1).astype(jnp.int32)
    nblk = (pend[-1:] // rb).astype(jnp.int32)
    out_sorted = _grouped_ffn(xn, blk_e, nblk, tok_pad, wrow, w_gate, w_up, w_down)
    return _combine(h, out_sorted, pos)


def _proj_kernel(h_ref, gkv_ref, gq_ref, wkv_ref, wq_ref, kn_ref, qn_ref, cos_ref, sin_ref,
                 k_ref, v_ref, q_ref):
    hn = _rms(h_ref[...])
    cos = cos_ref[...]
    sin = sin_ref[...]

    def norm_rope(x, g):
        y = _rms(x) * g
        return y * cos + pltpu.roll(y, HEAD_DIM // 2, 1) * sin

    kv = jnp.dot((hn * gkv_ref[...]).astype(BF16), wkv_ref[...], preferred_element_type=F32)
    kw = N_KV_HEADS * HEAD_DIM
    k_ref[...] = jnp.concatenate(
        [norm_rope(kv[:, j * HEAD_DIM:(j + 1) * HEAD_DIM], kn_ref[...]) for j in range(N_KV_HEADS)], axis=1)
    v_ref[...] = kv[:, kw:]
    q = jnp.dot((hn * gq_ref[...]).astype(BF16), wq_ref[...], preferred_element_type=F32)
    q_ref[...] = jnp.concatenate(
        [norm_rope(q[:, j * HEAD_DIM:(j + 1) * HEAD_DIM], qn_ref[...]) for j in range(N_HEADS)], axis=1)


def _kvq_proj(h, g_kv, g_q, w_kv, w_q, k_norm, q_norm, cos, sin):
    n, d = h.shape
    tm = TOKEN_TILE
    kw = N_KV_HEADS * HEAD_DIM
    qw = N_HEADS * HEAD_DIM
    row = lambda i: (i, 0)
    fixed = lambda i: (0, 0)
    return pl.pallas_call(
        _proj_kernel,
        grid=(n // tm,),
        in_specs=[
            pl.BlockSpec((tm, d), row),
            pl.BlockSpec((1, d), fixed),
            pl.BlockSpec((1, d), fixed),
            pl.BlockSpec((d, 2 * kw), fixed),
            pl.BlockSpec((d, qw), fixed),
            pl.BlockSpec((1, HEAD_DIM), fixed),
            pl.BlockSpec((1, HEAD_DIM), fixed),
            pl.BlockSpec((tm, HEAD_DIM), row),
            pl.BlockSpec((tm, HEAD_DIM), row),
        ],
        out_specs=[
            pl.BlockSpec((tm, kw), row),
            pl.BlockSpec((tm, kw), row),
            pl.BlockSpec((tm, qw), row),
        ],
        out_shape=[
            jax.ShapeDtypeStruct((n, kw), F32),
            jax.ShapeDtypeStruct((n, kw), F32),
            jax.ShapeDtypeStruct((n, qw), F32),
        ],
        compiler_params=_cparams(("arbitrary",)),
        name="kvq_proj",
    )(h, g_kv, g_q, w_kv, w_q, k_norm, q_norm, cos, sin)


def _top_blocks(gate, axis):
    idx = lax.broadcasted_iota(jnp.int32, gate.shape, axis)
    big = jnp.int32(gate.shape[axis])
    sel = jnp.zeros(gate.shape, jnp.bool_)
    for _ in range(MOBA_TOP_K):
        top = jnp.max(gate, axis=axis, keepdims=True)
        first = jnp.min(jnp.where(gate == top, idx, big), axis=axis, keepdims=True)
        hit = idx == first
        sel = sel | (hit & (top > NEG_INF))
        gate = jnp.where(hit, NEG_INF, gate)
    return sel.astype(F32)


def _moba_prompt_kernel(q_ref, k_ref, v_ref, o_ref, kbf, vt, kmean, sel, m_scr, l_scr, acc):
    j = pl.program_id(2)
    blk = MOBA_BLOCK
    n_blocks = k_ref.shape[0] // blk

    @pl.when(j == 0)
    def _():
        k = k_ref[...]
        kbf[...] = k.astype(BF16)
        kmean[...] = jnp.mean(k.reshape(n_blocks, blk, HEAD_DIM), axis=1)
        for n in range(n_blocks):
            vt[:, n * blk:(n + 1) * blk] = v_ref[n * blk:(n + 1) * blk, :].T.astype(BF16)

    q2 = q_ref[...]
    qs = jnp.concatenate([q2[:, h * HEAD_DIM:(h + 1) * HEAD_DIM] for h in range(Q_PER_KV)], axis=0)
    nq = qs.shape[0]
    gate = lax.dot_general(kmean[...], qs, (((1,), (1,)), ((), ())),
                           precision=HIGHEST, preferred_element_type=F32)
    row = lax.broadcasted_iota(jnp.int32, gate.shape, 0)
    sel[...] = _top_blocks(jnp.where(row < j, gate, NEG_INF), 0)
    qt = (qs * (HEAD_DIM ** -0.5)).T.astype(BF16)

    m_scr[...] = jnp.full(m_scr.shape, NEG_INF, F32)
    l_scr[...] = jnp.zeros(l_scr.shape, F32)
    acc[...] = jnp.zeros(acc.shape, F32)

    def update(s, n):
        m_old = m_scr[...]
        m_new = jnp.maximum(m_old, jnp.max(s, axis=0, keepdims=True))
        m_safe = jnp.where(m_new == NEG_INF, 0.0, m_new)
        p = jnp.exp(s - m_safe)
        alpha = jnp.exp(m_old - m_safe)
        l_scr[...] = alpha * l_scr[...] + jnp.sum(p, axis=0, keepdims=True)
        start = pl.multiple_of(n * blk, blk)
        pv = jnp.dot(vt[:, pl.ds(start, blk)], p.astype(BF16), preferred_element_type=F32)
        acc[...] = alpha * acc[...] + pv
        m_scr[...] = m_new

    def scores(n):
        start = pl.multiple_of(n * blk, blk)
        return jnp.dot(kbf[pl.ds(start, blk), :], qt, preferred_element_type=F32)

    def past(n, carry):
        s = jnp.where(sel[pl.ds(n, 1), :] > 0.0, scores(n), NEG_INF)
        update(s, n)
        return carry

    lax.fori_loop(0, j, past, 0)
    key = lax.broadcasted_iota(jnp.int32, (blk, nq), 0)
    qpos = lax.broadcasted_iota(jnp.int32, (blk, nq), 1) % blk
    update(jnp.where(key <= qpos, scores(j), NEG_INF), j)
    o = (acc[...] / l_scr[...]).T
    o_ref[...] = jnp.concatenate([o[h * blk:(h + 1) * blk, :] for h in range(Q_PER_KV)], axis=1)


def _moba_prompt(q, k, v, batch, seq):
    blk = MOBA_BLOCK
    nqb = seq // blk
    qw = Q_PER_KV * HEAD_DIM
    nq = Q_PER_KV * blk
    return pl.pallas_call(
        _moba_prompt_kernel,
        grid=(batch, N_KV_HEADS, nqb),
        in_specs=[
            pl.BlockSpec((blk, qw), lambda b, c, j: (b * nqb + j, c)),
            pl.BlockSpec((seq, HEAD_DIM), lambda b, c, j: (b, c)),
            pl.BlockSpec((seq, HEAD_DIM), lambda b, c, j: (b, c)),
        ],
        out_specs=pl.BlockSpec((blk, qw), lambda b, c, j: (b * nqb + j, c)),
        out_shape=jax.ShapeDtypeStruct((batch * seq, N_HEADS * HEAD_DIM), F32),
        scratch_shapes=[
            pltpu.VMEM((seq, HEAD_DIM), BF16),
            pltpu.VMEM((HEAD_DIM, seq), BF16),
            pltpu.VMEM((nqb, HEAD_DIM), F32),
            pltpu.VMEM((nqb, nq), F32),
            pltpu.VMEM((1, nq), F32),
            pltpu.VMEM((1, nq), F32),
            pltpu.VMEM((HEAD_DIM, nq), F32),
        ],
        compiler_params=_cparams(("arbitrary", "arbitrary", "arbitrary")),
        name="moba_prompt",
    )(q, k, v)


def _stack_heads(q8):
    return jnp.concatenate([q8[:, h * HEAD_DIM:(h + 1) * HEAD_DIM] for h in range(N_HEADS)], axis=0)


def _sample_gate_kernel(pt_ref, q_ref, *refs, dec_seq):
    pages = refs[:PAGES_PER_STEP]
    sel_ref = refs[PAGES_PER_STEP]
    means = refs[PAGES_PER_STEP + 1]
    t = pl.program_id(1)
    pages_per_block = MOBA_BLOCK // PAGE_SIZE
    blocks_per_step = PAGES_PER_STEP // pages_per_block
    sub = 8 // N_KV_HEADS
    for b in range(blocks_per_step):
        tot = jnp.zeros((8, HEAD_DIM), F32)
        for p in range(pages_per_block):
            page = pages[b * pages_per_block + p][0]
            tot = tot + jnp.sum(page.reshape(page.shape[0] // 8, 8, HEAD_DIM), axis=0)
        per_head = tot[:N_KV_HEADS]
        for r in range(1, sub):
            per_head = per_head + tot[r * N_KV_HEADS:(r + 1) * N_KV_HEADS]
        per_head = per_head / MOBA_BLOCK
        for c in range(N_KV_HEADS):
            means[c, pl.ds(t * blocks_per_step + b, 1), :] = per_head[c:c + 1, :]

    @pl.when(t == pl.num_programs(1) - 1)
    def _():
        qs = _stack_heads(q_ref[...])
        n_blocks = means.shape[1]
        rows_per_kv = Q_PER_KV * dec_seq
        row = lax.broadcasted_iota(jnp.int32, (qs.shape[0], n_blocks), 0)
        gate = jnp.zeros((qs.shape[0], n_blocks), F32)
        for c in range(N_KV_HEADS):
            g_c = lax.dot_general(qs, means[c], (((1,), (1,)), ((), ())),
                                  precision=HIGHEST, preferred_element_type=F32)
            gate = jnp.where(row // rows_per_kv == c, g_c, gate)
        chosen = _top_blocks(gate, 1)
        pad = jnp.zeros((qs.shape[0], LANES - n_blocks), F32)
        sel_ref[0] = jnp.concatenate([chosen, pad], axis=1)


def _page_spec(r):
    return pl.BlockSpec((1, PAGE_SIZE * N_KV_HEADS, HEAD_DIM),
                        lambda s, t, pt: (pt[s, t * PAGES_PER_STEP + r], 0, 0))


def _sample_gate(page_table, q_s, cache_k2, dec_seq):
    n_seq, n_pages = page_table.shape
    n_blocks = n_pages * PAGE_SIZE // MOBA_BLOCK
    rows = N_HEADS * dec_seq
    grid_spec = pltpu.PrefetchScalarGridSpec(
        num_scalar_prefetch=1,
        grid=(n_seq, n_pages // PAGES_PER_STEP),
        in_specs=[pl.BlockSpec((dec_seq, N_HEADS * HEAD_DIM), lambda s, t, pt: (s, 0))]
                 + [_page_spec(r) for r in range(PAGES_PER_STEP)],
        out_specs=pl.BlockSpec((1, rows, LANES), lambda s, t, pt: (s, 0, 0)),
        scratch_shapes=[pltpu.VMEM((N_KV_HEADS, n_blocks, HEAD_DIM), F32)],
    )
    return pl.pallas_call(
        functools.partial(_sample_gate_kernel, dec_seq=dec_seq),
        grid_spec=grid_spec,
        out_shape=jax.ShapeDtypeStruct((n_seq, rows, LANES), F32),
        compiler_params=_cparams(("arbitrary", "arbitrary")),
        name="sample_gate",
    )(page_table, q_s, *([cache_k2] * PAGES_PER_STEP))


def _sample_attn_kernel(pt_ref, q_ref, sel_ref, kn_ref, vn_ref, *refs, dec_seq):
    k_pages = refs[:PAGES_PER_STEP]
    v_pages = refs[PAGES_PER_STEP:2 * PAGES_PER_STEP]
    o_ref, qs_scr, m_scr, l_scr, acc = refs[2 * PAGES_PER_STEP:]
    t = pl.program_id(1)
    rows = N_HEADS * dec_seq
    rows_per_kv = Q_PER_KV * dec_seq
    pages_per_block = MOBA_BLOCK // PAGE_SIZE

    @pl.when(t == 0)
    def _():
        qs_scr[...] = (_stack_heads(q_ref[...]) * (HEAD_DIM ** -0.5)).astype(BF16)
        m_scr[...] = jnp.full(m_scr.shape, NEG_INF, F32)
        l_scr[...] = jnp.zeros(l_scr.shape, F32)
        acc[...] = jnp.zeros(acc.shape, F32)

    qs = qs_scr[...]

    def update(s, vals):
        m_old = m_scr[...]
        m_new = jnp.maximum(m_old, jnp.max(s, axis=1, keepdims=True))
        m_safe = jnp.where(m_new == NEG_INF, 0.0, m_new)
        p = jnp.exp(s - m_safe)
        alpha = jnp.exp(m_old - m_safe)
        l_scr[...] = alpha * l_scr[...] + jnp.sum(p, axis=1, keepdims=True)
        acc[...] = alpha * acc[...] + jnp.dot(p.astype(BF16), vals, preferred_element_type=F32)
        m_scr[...] = m_new

    sel = sel_ref[0]
    lane = lax.broadcasted_iota(jnp.int32, sel.shape, 1)
    width = PAGE_SIZE * N_KV_HEADS
    r_idx = lax.broadcasted_iota(jnp.int32, (rows, width), 0)
    c_idx = lax.broadcasted_iota(jnp.int32, (rows, width), 1)
    head_ok = (c_idx % N_KV_HEADS) == (r_idx // rows_per_kv)
    for p in range(PAGES_PER_STEP):
        n = (t * PAGES_PER_STEP + p) // pages_per_block
        chosen = jnp.sum(jnp.where(lane == n, sel, 0.0), axis=1, keepdims=True) > 0.0
        s = lax.dot_general(qs, k_pages[p][0].astype(BF16), (((1,), (1,)), ((), ())),
                            preferred_element_type=F32)
        s = jnp.where(head_ok & chosen, s, NEG_INF)
        update(s, v_pages[p][0].astype(BF16))

    @pl.when(t == pl.num_programs(1) - 1)
    def _():
        kn = kn_ref[...]
        vn = vn_ref[...]
        k_new = jnp.concatenate([kn[:, c * HEAD_DIM:(c + 1) * HEAD_DIM] for c in range(N_KV_HEADS)], axis=0)
        v_new = jnp.concatenate([vn[:, c * HEAD_DIM:(c + 1) * HEAD_DIM] for c in range(N_KV_HEADS)], axis=0)
        s = lax.dot_general(qs, k_new.astype(BF16), (((1,), (1,)), ((), ())),
                            preferred_element_type=F32)
        r2 = lax.broadcasted_iota(jnp.int32, s.shape, 0)
        c2 = lax.broadcasted_iota(jnp.int32, s.shape, 1)
        ok = ((c2 // dec_seq) == (r2 // rows_per_kv)) & ((c2 % dec_seq) <= (r2 % dec_seq))
        update(jnp.where(ok, s, NEG_INF), v_new.astype(BF16))
        o = acc[...] / l_scr[...]
        o_ref[...] = jnp.concatenate([o[h * dec_seq:(h + 1) * dec_seq, :] for h in range(N_HEADS)], axis=1)


def _sample_attn(page_table, q_s, sel, k_new, v_new, cache_k2, cache_v2, dec_seq):
    n_seq, n_pages = page_table.shape
    rows = N_HEADS * dec_seq
    kw = N_KV_HEADS * HEAD_DIM
    qw = N_HEADS * HEAD_DIM
    grid_spec = pltpu.PrefetchScalarGridSpec(
        num_scalar_prefetch=1,
        grid=(n_seq, n_pages // PAGES_PER_STEP),
        in_specs=[
            pl.BlockSpec((dec_seq, qw), lambda s, t, pt: (s, 0)),
            pl.BlockSpec((1, rows, LANES), lambda s, t, pt: (s, 0, 0)),
            pl.BlockSpec((dec_seq, kw), lambda s, t, pt: (s, 0)),
            pl.BlockSpec((dec_seq, kw), lambda s, t, pt: (s, 0)),
        ] + [_page_spec(r) for r in range(PAGES_PER_STEP)] * 2,
        out_specs=pl.BlockSpec((dec_seq, qw), lambda s, t, pt: (s, 0)),
        scratch_shapes=[
            pltpu.VMEM((rows, HEAD_DIM), BF16),
            pltpu.VMEM((rows, 1), F32),
            pltpu.VMEM((rows, 1), F32),
            pltpu.VMEM((rows, HEAD_DIM), F32),
        ],
    )
    return pl.pallas_call(
        functools.partial(_sample_attn_kernel, dec_seq=dec_seq),
        grid_spec=grid_spec,
        out_shape=jax.ShapeDtypeStruct((n_seq * dec_seq, qw), F32),
        compiler_params=_cparams(("arbitrary", "arbitrary")),
        name="sample_attn",
    )(page_table, q_s, sel, k_new, v_new, *([cache_k2] * PAGES_PER_STEP), *([cache_v2] * PAGES_PER_STEP))


def _oproj_kernel(h_ref, o_ref, w_ref, out_ref):
    out_ref[...] = h_ref[...] + jnp.dot(o_ref[...].astype(BF16), w_ref[...], preferred_element_type=F32)


def _oproj(h, o, w_o):
    n, d = h.shape
    tm = TOKEN_TILE
    return pl.pallas_call(
        _oproj_kernel,
        grid=(n // tm,),
        in_specs=[
            pl.BlockSpec((tm, d), lambda i: (i, 0)),
            pl.BlockSpec((tm, o.shape[1]), lambda i: (i, 0)),
            pl.BlockSpec(w_o.shape, lambda i: (0, 0)),
        ],
        out_specs=pl.BlockSpec((tm, d), lambda i: (i, 0)),
        out_shape=jax.ShapeDtypeStruct((n, d), F32),
        compiler_params=_cparams(("arbitrary",)),
        name="attn_oproj",
    )(h, o, w_o)


def _mix_tables(w_s, b_s, dec_seq):
    tm = TOKEN_TILE
    causal = jnp.tril(jnp.ones((GMLP_CHUNK, GMLP_CHUNK), dtype=bool))
    w = jnp.where(causal[None], w_s, jnp.zeros_like(w_s))
    eye_p = jnp.eye(tm // GMLP_CHUNK, dtype=w.dtype)
    mix_p = jnp.einsum("ab,gts->gatbs", eye_p, w).reshape(GMLP_GROUPS, tm, tm)
    eye_s = jnp.eye(tm // dec_seq, dtype=w.dtype)
    mix_s = jnp.einsum("ab,gts->gatbs", eye_s, w[:, :dec_seq, :dec_seq]).reshape(GMLP_GROUPS, tm, tm)
    mix = jnp.stack([mix_p, mix_s]).astype(BF16)
    bias_p = jnp.tile(b_s.T, (tm // GMLP_CHUNK, 1))
    bias_s = jnp.tile(b_s.T[:dec_seq], (tm // dec_seq, 1))
    bias = jnp.stack([bias_p, bias_s])
    bias = jnp.pad(bias, ((0, 0), (0, 0), (0, LANES - GMLP_GROUPS)))
    return mix, bias


def _rope_tables(pos):
    half = HEAD_DIM // 2
    inv = ROPE_THETA ** (-jnp.arange(half, dtype=F32) * 2.0 / HEAD_DIM)
    ang = pos.astype(F32)[:, None] * inv[None, :]
    cos = jnp.cos(ang)
    sin = jnp.sin(ang)
    return jnp.concatenate([cos, cos], axis=1), jnp.concatenate([-sin, sin], axis=1)


def _router_tables(w_grp, b_grp, w_rt, b_rt):
    w = jnp.concatenate([w_grp, w_rt], axis=1)
    b = jnp.concatenate([b_grp, b_rt], axis=0)
    pad = LANES - w.shape[1]
    return jnp.pad(w, ((0, 0), (0, pad))), jnp.pad(b, (0, pad)).reshape(1, LANES)


def kernel(x_prompt, x_sample, cache_k, cache_v, page_table, norm_mix, norm_ffn, a_w_in, a_ln_g, a_ln_b,
           a_w_s, a_b_s, a_w_out, kv_norm, w_kv, k_norm, b_w_q, b_q_norm, b_w_o, moe_w_grp, moe_b_grp,
           moe_w_rt, moe_b_rt, moe_w_gate, moe_w_up, moe_w_down):
    batch, seq, d = x_prompt.shape
    n_seq, dec_seq, _ = x_sample.shape
    n_prompt = batch * seq
    n_sample = n_seq * dec_seq
    assert n_prompt % TOKEN_TILE == 0 and n_sample == TOKEN_TILE and seq % MOBA_BLOCK == 0
    past_len = page_table.shape[1] * PAGE_SIZE
    assert past_len % MOBA_BLOCK == 0 and dec_seq <= MOBA_BLOCK

    x = jnp.concatenate([x_prompt.reshape(n_prompt, d), x_sample.reshape(n_sample, d)], axis=0)
    pos = jnp.concatenate([jnp.tile(jnp.arange(seq), batch), jnp.tile(past_len + jnp.arange(dec_seq), n_seq)])
    cos, sin = _rope_tables(pos)
    row = lambda a: a.reshape(1, -1)

    mix, bias = _mix_tables(a_w_s[0], a_b_s[0], dec_seq)
    h, vg_sample = _gmlp_layer(x, row(norm_mix[0]), a_w_in[0].astype(BF16), row(a_ln_g[0]), row(a_ln_b[0]),
                               mix, bias, a_w_out[0].astype(BF16), n_prompt // TOKEN_TILE)
    moe = []
    for layer in range(2):
        w_r, b_r = _router_tables(moe_w_grp[layer], moe_b_grp[layer], moe_w_rt[layer], moe_b_rt[layer])
        moe.append((row(norm_ffn[layer]), w_r, b_r, moe_w_gate[layer].astype(BF16),
                    moe_w_up[layer].astype(BF16), moe_w_down[layer].astype(BF16)))
    h = _hier_moe(h, *moe[0])

    k, v, q = _kvq_proj(h, row(kv_norm), row(norm_mix[1]), w_kv.astype(BF16), b_w_q[0].astype(BF16),
                        row(k_norm), row(b_q_norm[0]), cos, sin)

    o_prompt = _moba_prompt(q, k, v, batch, seq)
    n_phys = cache_k.shape[0]
    cache_k2 = cache_k.reshape(n_phys, PAGE_SIZE * N_KV_HEADS, HEAD_DIM)
    cache_v2 = cache_v.reshape(n_phys, PAGE_SIZE * N_KV_HEADS, HEAD_DIM)
    q_s, k_s, v_s = q[n_prompt:], k[n_prompt:], v[n_prompt:]
    sel = _sample_gate(page_table, q_s, cache_k2, dec_seq)
    o_sample = _sample_attn(page_table, q_s, sel, k_s, v_s, cache_k2, cache_v2, dec_seq)
    h = _oproj(h, jnp.concatenate([o_prompt, o_sample], axis=0), b_w_o[0].astype(BF16))
    h = _hier_moe(h, *moe[1])

    n_pages_new = seq // PAGE_SIZE
    return (h[:n_prompt].reshape(batch, seq, d),
            h[n_prompt:].reshape(n_seq, dec_seq, d),
            k[:n_prompt].reshape(batch, n_pages_new, PAGE_SIZE, N_KV_HEADS, HEAD_DIM),
            v[:n_prompt].reshape(batch, n_pages_new, PAGE_SIZE, N_KV_HEADS, HEAD_DIM),
            k_s.reshape(n_seq, dec_seq, N_KV_HEADS, HEAD_DIM),
            v_s.reshape(n_seq, dec_seq, N_KV_HEADS, HEAD_DIM),
            vg_sample.reshape(1, n_seq, dec_seq, -1))
```

```python
import functools
import math

import jax
import jax.numpy as jnp
from jax import lax
from jax.experimental import pallas as pl
from jax.experimental.pallas import tpu as pltpu

F32 = jnp.float32
BF16 = jnp.bfloat16
HIGHEST = lax.Precision.HIGHEST

GMLP_CHUNK = 128
GMLP_GROUPS = 8
N_HEADS = 8
N_KV_HEADS = 4
HEAD_DIM = 128
Q_PER_KV = N_HEADS // N_KV_HEADS
MOBA_BLOCK = 256
MOBA_TOP_K = 3
ROPE_THETA = 10000.0
N_GROUPS = 4
EXPERTS_PER_GROUP = 8
N_EXPERTS = N_GROUPS * EXPERTS_PER_GROUP
TOP_K_EXPERTS = 2
PAGE_SIZE = 128
EPS = 1e-6

LANES = 128
ROW_TILE = 8
TOKEN_TILE = 256
FFN_ROWS = 256
PAGES_PER_STEP = 16
KEY_GROUP = 4
ONES_ROWS = 16
LOG2_E = math.log2(math.e)
VMEM_LIMIT = 56 * 1024 * 1024

NEG_INF = float("-inf")


def _cparams(sem):
    return pltpu.CompilerParams(dimension_semantics=sem, vmem_limit_bytes=VMEM_LIMIT)


def _rms(x):
    return x * lax.rsqrt(jnp.mean(x * x, axis=-1, keepdims=True) + EPS)


def _gmlp_kernel(x_ref, g_ref, win_ref, lng_ref, lnb_ref, mix_ref, bias_ref, wout_ref,
                 h_ref, vg_ref, *, d_gate, n_groups):
    i = pl.program_id(0)
    x = x_ref[...]
    xb = (_rms(x) * g_ref[...]).astype(BF16)
    u = jax.nn.gelu(jnp.dot(xb, win_ref[:, :d_gate], preferred_element_type=F32))
    vp = jax.nn.gelu(jnp.dot(xb, win_ref[:, d_gate:], preferred_element_type=F32))
    vc = vp - jnp.mean(vp, axis=-1, keepdims=True)
    var = jnp.mean(vc * vc, axis=-1, keepdims=True)
    vg = vc * lax.rsqrt(var + EPS) * lng_ref[...] + lnb_ref[...]

    @pl.when(i == pl.num_programs(0) - 1)
    def _():
        vg_ref[...] = vg

    vgb = vg.astype(BF16)
    cw = d_gate // n_groups
    bias = bias_ref[0]
    parts = []
    for g in range(n_groups):
        mixed = jnp.dot(mix_ref[0, g], vgb[:, g * cw:(g + 1) * cw], preferred_element_type=F32)
        mixed = mixed + bias[:, g:g + 1]
        parts.append((u[:, g * cw:(g + 1) * cw] * mixed).astype(BF16))
    gated = jnp.concatenate(parts, axis=1)
    h_ref[...] = x + jnp.dot(gated, wout_ref[...], preferred_element_type=F32)


def _gmlp_layer(x, g, w_in, ln_g, ln_b, mix, bias, w_out, n_prompt_tiles):
    n, d = x.shape
    d_gate = w_out.shape[0]
    tm = TOKEN_TILE
    n_tiles = n // tm
    kind = lambda i: jnp.where(i < n_prompt_tiles, 0, 1)
    return pl.pallas_call(
        functools.partial(_gmlp_kernel, d_gate=d_gate, n_groups=GMLP_GROUPS),
        grid=(n_tiles,),
        in_specs=[
            pl.BlockSpec((tm, d), lambda i: (i, 0)),
            pl.BlockSpec((1, d), lambda i: (0, 0)),
            pl.BlockSpec((d, 2 * d_gate), lambda i: (0, 0)),
            pl.BlockSpec((1, d_gate), lambda i: (0, 0)),
            pl.BlockSpec((1, d_gate), lambda i: (0, 0)),
            pl.BlockSpec((1, GMLP_GROUPS, tm, tm), lambda i: (kind(i), 0, 0, 0)),
            pl.BlockSpec((1, tm, LANES), lambda i: (kind(i), 0, 0)),
            pl.BlockSpec((d_gate, d), lambda i: (0, 0)),
        ],
        out_specs=[
            pl.BlockSpec((tm, d), lambda i: (i, 0)),
            pl.BlockSpec((tm, d_gate), lambda i: (0, 0)),
        ],
        out_shape=[
            jax.ShapeDtypeStruct((n, d), F32),
            jax.ShapeDtypeStruct((tm, d_gate), F32),
        ],
        compiler_params=_cparams(("arbitrary",)),
        name="gmlp_layer",
    )(x, g, w_in, ln_g, ln_b, mix, bias, w_out)


def _router_kernel(h_ref, g_ref, wr_ref, br_ref, xn_ref, e_ref, w_ref):
    xn = _rms(h_ref[...]) * g_ref[...]
    _store_row_tiles(xn_ref, xn)
    logits = jnp.dot(xn, wr_ref[...], precision=HIGHEST, preferred_element_type=F32) + br_ref[...]
    lane = lax.broadcasted_iota(jnp.int32, logits.shape, 1)
    big = jnp.int32(LANES)
    is_grp = lane < N_GROUPS
    gl = jnp.where(is_grp, logits, NEG_INF)
    gmax = jnp.max(gl, axis=1, keepdims=True)
    gidx = jnp.min(jnp.where(is_grp & (logits == gmax), lane, big), axis=1, keepdims=True)
    p_g = 1.0 / jnp.sum(jnp.where(is_grp, jnp.exp(gl - gmax), 0.0), axis=1, keepdims=True)
    lo = N_GROUPS + gidx * EXPERTS_PER_GROUP
    in_grp = (lane >= lo) & (lane < lo + EXPERTS_PER_GROUP)
    v0 = jnp.max(jnp.where(in_grp, logits, NEG_INF), axis=1, keepdims=True)
    i0 = jnp.min(jnp.where(in_grp & (logits == v0), lane, big), axis=1, keepdims=True)
    rest = in_grp & (lane != i0)
    v1 = jnp.max(jnp.where(rest, logits, NEG_INF), axis=1, keepdims=True)
    i1 = jnp.min(jnp.where(rest & (logits == v1), lane, big), axis=1, keepdims=True)
    t = jnp.exp(v1 - v0)
    w0 = p_g * (1.0 / (1.0 + t))
    w1 = p_g * (t / (1.0 + t))
    e_ref[...] = jnp.where(lane == 0, i0 - N_GROUPS, jnp.where(lane == 1, i1 - N_GROUPS, 0))
    w_ref[...] = jnp.where(lane == 0, w0, jnp.where(lane == 1, w1, 0.0))


def _router(h, g, w_r, b_r):
    n, d = h.shape
    tm = TOKEN_TILE
    return pl.pallas_call(
        _router_kernel,
        grid=(n // tm,),
        in_specs=[
            pl.BlockSpec((tm, d), lambda i: (i, 0)),
            pl.BlockSpec((1, d), lambda i: (0, 0)),
            pl.BlockSpec((d, LANES), lambda i: (0, 0)),
            pl.BlockSpec((1, LANES), lambda i: (0, 0)),
        ],
        out_specs=[
            pl.BlockSpec((tm * ROW_TILE, LANES), lambda i: (i, 0)),
            pl.BlockSpec((tm, LANES), lambda i: (i, 0)),
            pl.BlockSpec((tm, LANES), lambda i: (i, 0)),
        ],
        out_shape=[
            jax.ShapeDtypeStruct((n * ROW_TILE, LANES), F32),
            jax.ShapeDtypeStruct((n, LANES), jnp.int32),
            jax.ShapeDtypeStruct((n, LANES), F32),
        ],
        compiler_params=_cparams(("arbitrary",)),
        name="moe_router",
    )(h, g, w_r, b_r)


def _store_row_tiles(ref, x, first=0):
    for c in range(ROW_TILE):
        ref[pl.ds(first + c, x.shape[0], stride=ROW_TILE), :] = x[:, c * LANES:(c + 1) * LANES]


def _load_row_tiles(ref, first, rows):
    return jnp.concatenate([ref[pl.ds(first + c, rows, stride=ROW_TILE), :] for c in range(ROW_TILE)], axis=1)


def _row_tile(ref, r):
    return ref.at[pl.ds(pl.multiple_of(r * ROW_TILE, ROW_TILE), ROW_TILE), :]


def _ffn_kernel(blk_e_ref, nblk_ref, tok_ref, xn_hbm, wg_ref, wu_ref, wd_ref, wrow_ref,
                out_ref, xbuf, sems):
    i = pl.program_id(0)
    rows = wrow_ref.shape[0]
    nblk = nblk_ref[0]

    def gather(block, slot):
        base = block * rows

        def issue(r, carry):
            pltpu.make_async_copy(_row_tile(xn_hbm, tok_ref[base + r]), _row_tile(xbuf, slot * rows + r),
                                  sems.at[slot]).start()
            return carry

        lax.fori_loop(0, rows, issue, 0, unroll=8)

    @pl.when((i == 0) & (nblk > 0))
    def _():
        gather(0, 0)

    @pl.when(i + 1 < nblk)
    def _():
        gather(i + 1, (i + 1) % 2)

    @pl.when(i < nblk)
    def _():
        slot = i % 2
        first = pl.multiple_of(slot * rows * ROW_TILE, ROW_TILE)
        pltpu.make_async_copy(xn_hbm.at[pl.ds(0, rows * ROW_TILE), :],
                              xbuf.at[pl.ds(first, rows * ROW_TILE), :], sems.at[slot]).wait()
        x = _load_row_tiles(xbuf, first, rows).astype(BF16)
        gate = jnp.dot(x, wg_ref[0], preferred_element_type=F32)
        up = jnp.dot(x, wu_ref[0], preferred_element_type=F32)
        hid = (jax.nn.silu(gate) * up).astype(BF16)
        out = jnp.dot(hid, wd_ref[0], preferred_element_type=F32)
        _store_row_tiles(out_ref, out * wrow_ref[...])

    @pl.when(i >= nblk)
    def _():
        out_ref[...] = jnp.zeros_like(out_ref)


def _grouped_ffn(xn, blk_e, nblk, tok_pad, wrow, w_gate, w_up, w_down):
    n_rows = tok_pad.shape[0]
    rb = FFN_ROWS
    _, d, d_e = w_gate.shape
    assert d == ROW_TILE * LANES
    grid_spec = pltpu.PrefetchScalarGridSpec(
        num_scalar_prefetch=3,
        grid=(n_rows // rb,),
        in_specs=[
            pl.BlockSpec(memory_space=pl.ANY),
            pl.BlockSpec((1, d, d_e), lambda i, be, nb, tk: (be[i], 0, 0)),
            pl.BlockSpec((1, d, d_e), lambda i, be, nb, tk: (be[i], 0, 0)),
            pl.BlockSpec((1, d_e, d), lambda i, be, nb, tk: (be[i], 0, 0)),
            pl.BlockSpec((rb, 1), lambda i, be, nb, tk: (i, 0)),
        ],
        out_specs=pl.BlockSpec((rb * ROW_TILE, LANES), lambda i, be, nb, tk: (i, 0)),
        scratch_shapes=[pltpu.VMEM((2 * rb * ROW_TILE, LANES), F32), pltpu.SemaphoreType.DMA((2,))],
    )
    return pl.pallas_call(
        _ffn_kernel,
        grid_spec=grid_spec,
        out_shape=jax.ShapeDtypeStruct((n_rows * ROW_TILE, LANES), F32),
        compiler_params=_cparams(("arbitrary",)),
        name="moe_ffn",
    )(blk_e, nblk, tok_pad, xn, w_gate, w_up, w_down, wrow)


def _combine_kernel(pos_ref, resid_ref, src_hbm, out_ref, buf, sems):
    i = pl.program_id(0)
    tm = resid_ref.shape[0]

    def gather(tile, slot):
        base = tile * tm * TOP_K_EXPERTS

        def issue(t, carry):
            for k in range(TOP_K_EXPERTS):
                p = pos_ref[base + t * TOP_K_EXPERTS + k]
                pltpu.make_async_copy(_row_tile(src_hbm, p), _row_tile(buf, (slot * TOP_K_EXPERTS + k) * tm + t),
                                      sems.at[slot]).start()
            return carry

        lax.fori_loop(0, tm, issue, 0, unroll=4)

    @pl.when(i == 0)
    def _():
        gather(0, 0)

    @pl.when(i + 1 < pl.num_programs(0))
    def _():
        gather(i + 1, (i + 1) % 2)

    slot = i % 2
    firsts = [pl.multiple_of((slot * TOP_K_EXPERTS + k) * tm * ROW_TILE, ROW_TILE) for k in range(TOP_K_EXPERTS)]
    for first in firsts:
        pltpu.make_async_copy(src_hbm.at[pl.ds(0, tm * ROW_TILE), :],
                              buf.at[pl.ds(first, tm * ROW_TILE), :], sems.at[slot]).wait()
    acc = resid_ref[...]
    for first in firsts:
        acc = acc + _load_row_tiles(buf, first, tm)
    out_ref[...] = acc


def _combine(resid, src, pos):
    n, d = resid.shape
    assert d == ROW_TILE * LANES
    tm = TOKEN_TILE
    grid_spec = pltpu.PrefetchScalarGridSpec(
        num_scalar_prefetch=1,
        grid=(n // tm,),
        in_specs=[
            pl.BlockSpec((tm, d), lambda i, ps: (i, 0)),
            pl.BlockSpec(memory_space=pl.ANY),
        ],
        out_specs=pl.BlockSpec((tm, d), lambda i, ps: (i, 0)),
        scratch_shapes=[pltpu.VMEM((2 * TOP_K_EXPERTS * tm * ROW_TILE, LANES), F32),
                        pltpu.SemaphoreType.DMA((2,))],
    )
    return pl.pallas_call(
        _combine_kernel,
        grid_spec=grid_spec,
        out_shape=jax.ShapeDtypeStruct((n, d), F32),
        compiler_params=_cparams(("arbitrary",)),
        name="moe_combine",
    )(pos, resid, src)


def _hier_moe(h, g, w_r, b_r, w_gate, w_up, w_down):
    n, _ = h.shape
    xn, e_pad, w_pad = _router(h, g, w_r, b_r)
    e_flat = e_pad[:, :TOP_K_EXPERTS].reshape(-1)
    w_flat = w_pad[:, :TOP_K_EXPERTS].reshape(-1)
    n_asg = n * TOP_K_EXPERTS
    rb = FFN_ROWS
    onehot = (e_flat[:, None] == jnp.arange(N_EXPERTS, dtype=jnp.int32)[None, :]).astype(jnp.int32)
    csum = jnp.cumsum(onehot, axis=0)
    rank = jnp.take_along_axis(csum, e_flat[:, None], axis=1)[:, 0] - 1
    counts = csum[-1]
    padded = (counts + rb - 1) // rb * rb
    pend = jnp.cumsum(padded)
    pstart = pend - padded
    pos = (pstart[e_flat] + rank).astype(jnp.int32)
    n_blocks = (n_asg + N_EXPERTS * (rb - 1) + rb - 1) // rb
    n_rows = n_blocks * rb
    tok_pad = jnp.zeros((n_rows,), jnp.int32).at[pos].set(jnp.arange(n_asg, dtype=jnp.int32) // TOP_K_EXPERTS)
    wrow = jnp.zeros((n_rows,), F32).at[pos].set(w_flat).reshape(n_rows, 1)
    blk_e = jnp.minimum(jnp.searchsorted(pend, jnp.arange(n_blocks, dtype=jnp.int32) * rb, side="right"),
                        N_EXPERTS - 1).astype(jnp.int32)
    nblk = (pend[-1:] // rb).astype(jnp.int32)
    out_sorted = _grouped_ffn(xn, blk_e, nblk, tok_pad, wrow, w_gate, w_up, w_down)
    return _combine(h, out_sorted, pos)


def _proj_kernel(h_ref, gkv_ref, gq_ref, wkv_ref, wq_ref, kn_ref, qn_ref, cos_ref, sin_ref,
                 k_ref, v_ref, q_ref):
    hn = _rms(h_ref[...])
    cos = cos_ref[...]
    sin = sin_ref[...]

    def norm_rope(x, g):
        y = _rms(x) * g
        return y * cos + pltpu.roll(y, HEAD_DIM // 2, 1) * sin

    kv = jnp.dot((hn * gkv_ref[...]).astype(BF16), wkv_ref[...], preferred_element_type=F32)
    kw = N_KV_HEADS * HEAD_DIM
    k_ref[...] = jnp.concatenate(
        [norm_rope(kv[:, j * HEAD_DIM:(j + 1) * HEAD_DIM], kn_ref[...]) for j in range(N_KV_HEADS)], axis=1)
    v_ref[...] = kv[:, kw:]
    q = jnp.dot((hn * gq_ref[...]).astype(BF16), wq_ref[...], preferred_element_type=F32)
    q_ref[...] = jnp.concatenate(
        [norm_rope(q[:, j * HEAD_DIM:(j + 1) * HEAD_DIM], qn_ref[...]) for j in range(N_HEADS)], axis=1)


def _kvq_proj(h, g_kv, g_q, w_kv, w_q, k_norm, q_norm, cos, sin):
    n, d = h.shape
    tm = TOKEN_TILE
    kw = N_KV_HEADS * HEAD_DIM
    qw = N_HEADS * HEAD_DIM
    row = lambda i: (i, 0)
    fixed = lambda i: (0, 0)
    return pl.pallas_call(
        _proj_kernel,
        grid=(n // tm,),
        in_specs=[
            pl.BlockSpec((tm, d), row),
            pl.BlockSpec((1, d), fixed),
            pl.BlockSpec((1, d), fixed),
            pl.BlockSpec((d, 2 * kw), fixed),
            pl.BlockSpec((d, qw), fixed),
            pl.BlockSpec((1, HEAD_DIM), fixed),
            pl.BlockSpec((1, HEAD_DIM), fixed),
            pl.BlockSpec((tm, HEAD_DIM), row),
            pl.BlockSpec((tm, HEAD_DIM), row),
        ],
        out_specs=[
            pl.BlockSpec((tm, kw), row),
            pl.BlockSpec((tm, kw), row),
            pl.BlockSpec((tm, qw), row),
        ],
        out_shape=[
            jax.ShapeDtypeStruct((n, kw), F32),
            jax.ShapeDtypeStruct((n, kw), F32),
            jax.ShapeDtypeStruct((n, qw), F32),
        ],
        compiler_params=_cparams(("arbitrary",)),
        name="kvq_proj",
    )(h, g_kv, g_q, w_kv, w_q, k_norm, q_norm, cos, sin)


def _top_blocks(gate, axis):
    idx = lax.broadcasted_iota(jnp.int32, gate.shape, axis)
    big = jnp.int32(gate.shape[axis])
    sel = jnp.zeros(gate.shape, jnp.bool_)
    for _ in range(MOBA_TOP_K):
        top = jnp.max(gate, axis=axis, keepdims=True)
        first = jnp.min(jnp.where(gate == top, idx, big), axis=axis, keepdims=True)
        hit = idx == first
        sel = sel | (hit & (top > NEG_INF))
        gate = jnp.where(hit, NEG_INF, gate)
    return sel.astype(F32)


def _moba_prompt_kernel(q_ref, k_ref, v_ref, o_ref, kbf, vt, kmean, sel, m_scr, acc):
    j = pl.program_id(2)
    blk = MOBA_BLOCK
    grp = KEY_GROUP
    seq = k_ref.shape[0]
    n_blocks = seq // blk

    @pl.when(j == 0)
    def _():
        k = k_ref[...]
        kbf[...] = k.astype(BF16)
        kmean[...] = jnp.mean(k.reshape(n_blocks, blk, HEAD_DIM), axis=1)
        for n in range(n_blocks):
            vt[:HEAD_DIM, n * blk:(n + 1) * blk] = v_ref[n * blk:(n + 1) * blk, :].T.astype(BF16)
        r = lax.broadcasted_iota(jnp.int32, (ONES_ROWS, seq), 0)
        vt[HEAD_DIM:, :] = jnp.where(r == 0, 1.0, 0.0).astype(BF16)

    q2 = q_ref[...]
    qs = jnp.concatenate([q2[:, h * HEAD_DIM:(h + 1) * HEAD_DIM] for h in range(Q_PER_KV)], axis=0)
    nq = qs.shape[0]
    gate = lax.dot_general(kmean[...], qs, (((1,), (1,)), ((), ())),
                           precision=HIGHEST, preferred_element_type=F32)
    row = lax.broadcasted_iota(jnp.int32, gate.shape, 0)
    sel[...] = _top_blocks(jnp.where(row < j, gate, NEG_INF), 0)
    qt = (qs * (HEAD_DIM ** -0.5 * LOG2_E)).T.astype(BF16)

    m_scr[...] = jnp.full(m_scr.shape, NEG_INF, F32)
    acc[...] = jnp.zeros(acc.shape, F32)

    def update(tiles, start):
        m_old = m_scr[...]
        m_new = m_old
        for s in tiles:
            m_new = jnp.maximum(m_new, jnp.max(s, axis=0, keepdims=True))
        m_safe = jnp.where(m_new == NEG_INF, 0.0, m_new)
        p = jnp.concatenate([jnp.exp2(s - m_safe).astype(BF16) for s in tiles], axis=0)
        alpha = jnp.exp2(m_old - m_safe)
        pv = jnp.dot(vt[:, pl.ds(start, len(tiles) * blk)], p, preferred_element_type=F32)
        acc[...] = alpha * acc[...] + pv
        m_scr[...] = m_new

    def scores(start, n_keys):
        return jnp.dot(kbf[pl.ds(start, n_keys), :], qt, preferred_element_type=F32)

    def past_group(g, carry):
        start = pl.multiple_of(g * (grp * blk), grp * blk)
        s = scores(start, grp * blk)
        update([jnp.where(sel[pl.ds(g * grp + i, 1), :] > 0.0, s[i * blk:(i + 1) * blk], NEG_INF)
                for i in range(grp)], start)
        return carry

    def past_single(n, carry):
        start = pl.multiple_of(n * blk, blk)
        update([jnp.where(sel[pl.ds(n, 1), :] > 0.0, scores(start, blk), NEG_INF)], start)
        return carry

    n_groups = lax.div(j, grp)
    lax.fori_loop(0, n_groups, past_group, 0)
    lax.fori_loop(n_groups * grp, j, past_single, 0)
    key = lax.broadcasted_iota(jnp.int32, (blk, nq), 0)
    qpos = lax.broadcasted_iota(jnp.int32, (blk, nq), 1) % blk
    own = pl.multiple_of(j * blk, blk)
    update([jnp.where(key <= qpos, scores(own, blk), NEG_INF)], own)
    a = acc[...]
    o = (a[:HEAD_DIM] / a[HEAD_DIM:HEAD_DIM + 1]).T
    o_ref[...] = jnp.concatenate([o[h * blk:(h + 1) * blk, :] for h in range(Q_PER_KV)], axis=1)


def _moba_prompt(q, k, v, batch, seq):
    blk = MOBA_BLOCK
    nqb = seq // blk
    qw = Q_PER_KV * HEAD_DIM
    nq = Q_PER_KV * blk
    return pl.pallas_call(
        _moba_prompt_kernel,
        grid=(batch, N_KV_HEADS, nqb),
        in_specs=[
            pl.BlockSpec((blk, qw), lambda b, c, j: (b * nqb + j, c)),
            pl.BlockSpec((seq, HEAD_DIM), lambda b, c, j: (b, c)),
            pl.BlockSpec((seq, HEAD_DIM), lambda b, c, j: (b, c)),
        ],
        out_specs=pl.BlockSpec((blk, qw), lambda b, c, j: (b * nqb + j, c)),
        out_shape=jax.ShapeDtypeStruct((batch * seq, N_HEADS * HEAD_DIM), F32),
        scratch_shapes=[
            pltpu.VMEM((seq, HEAD_DIM), BF16),
            pltpu.VMEM((HEAD_DIM + ONES_ROWS, seq), BF16),
            pltpu.VMEM((nqb, HEAD_DIM), F32),
            pltpu.VMEM((nqb, nq), F32),
            pltpu.VMEM((1, nq), F32),
            pltpu.VMEM((HEAD_DIM + ONES_ROWS, nq), F32),
        ],
        compiler_params=_cparams(("arbitrary", "arbitrary", "arbitrary")),
        name="moba_prompt",
    )(q, k, v)


def _stack_heads(q8):
    return jnp.concatenate([q8[:, h * HEAD_DIM:(h + 1) * HEAD_DIM] for h in range(N_HEADS)], axis=0)


def _sample_attn_kernel(pt_ref, q_ref, kn_ref, vn_ref, *refs, dec_seq, n_pages):
    pps = PAGES_PER_STEP
    k_pages = refs[:pps]
    v_pages = refs[pps:2 * pps]
    o_ref, s_scr, means, sel_scr, qs_scr, m_scr, l_scr, acc = refs[2 * pps:]
    t = pl.program_id(1)
    n_k_steps = n_pages // pps
    rows = N_HEADS * dec_seq
    rows_per_kv = Q_PER_KV * dec_seq
    width = PAGE_SIZE * N_KV_HEADS
    ppb = MOBA_BLOCK // PAGE_SIZE
    bps = pps // ppb
    n_blocks = n_pages // ppb
    nt_dims = (((1,), (1,)), ((), ()))

    @pl.when(t == 0)
    def _():
        qs_scr[...] = (_stack_heads(q_ref[...]) * (HEAD_DIM ** -0.5 * LOG2_E)).astype(BF16)
        m_scr[...] = jnp.full(m_scr.shape, NEG_INF, F32)
        l_scr[...] = jnp.zeros(l_scr.shape, F32)
        acc[...] = jnp.zeros(acc.shape, F32)

    @pl.when(t < n_k_steps)
    def _():
        qs = qs_scr[...]
        sub = ROW_TILE // N_KV_HEADS
        for b in range(bps):
            tot = jnp.zeros((ROW_TILE, HEAD_DIM), F32)
            for p in range(ppb):
                page = k_pages[b * ppb + p][0]
                col = pl.multiple_of((t * pps + b * ppb + p) * width, width)
                s_scr[:, pl.ds(col, width)] = lax.dot_general(qs, page.astype(BF16), nt_dims,
                                                              preferred_element_type=F32)
                tot = tot + jnp.sum(page.reshape(width // ROW_TILE, ROW_TILE, HEAD_DIM), axis=0)
            per_head = tot[:N_KV_HEADS]
            for r in range(1, sub):
                per_head = per_head + tot[r * N_KV_HEADS:(r + 1) * N_KV_HEADS]
            per_head = per_head / MOBA_BLOCK
            for c in range(N_KV_HEADS):
                means[c, pl.ds(t * bps + b, 1), :] = per_head[c:c + 1, :]

    @pl.when(t == n_k_steps - 1)
    def _():
        qf = _stack_heads(q_ref[...])
        row = lax.broadcasted_iota(jnp.int32, (rows, n_blocks), 0)
        gate = jnp.zeros((rows, n_blocks), F32)
        for c in range(N_KV_HEADS):
            g_c = lax.dot_general(qf, means[c], nt_dims, precision=HIGHEST, preferred_element_type=F32)
            gate = jnp.where(row // rows_per_kv == c, g_c, gate)
        chosen = _top_blocks(gate, 1)
        sel_scr[...] = jnp.concatenate([chosen, jnp.zeros((rows, LANES - n_blocks), F32)], axis=1)

    def update(tiles, vals):
        m_old = m_scr[...]
        m_new = m_old
        for s in tiles:
            m_new = jnp.maximum(m_new, jnp.max(s, axis=1, keepdims=True))
        m_safe = jnp.where(m_new == NEG_INF, 0.0, m_new)
        alpha = jnp.exp2(m_old - m_safe)
        l_new = alpha * l_scr[...]
        a_new = alpha * acc[...]
        page = 0
        for s in tiles:
            p = jnp.exp2(s - m_safe)
            l_new = l_new + jnp.sum(p, axis=1, keepdims=True)
            pb = p.astype(BF16)
            for c in range(s.shape[1] // vals[page].shape[0]):
                v = vals[page]
                a_new = a_new + jnp.dot(pb[:, c * v.shape[0]:(c + 1) * v.shape[0]], v, preferred_element_type=F32)
                page += 1
        l_scr[...] = l_new
        acc[...] = a_new
        m_scr[...] = m_new

    @pl.when(t >= n_k_steps)
    def _():
        tv = t - n_k_steps
        sel = sel_scr[...]
        lane = lax.broadcasted_iota(jnp.int32, sel.shape, 1)
        r_idx = lax.broadcasted_iota(jnp.int32, (rows, ppb * width), 0)
        c_idx = lax.broadcasted_iota(jnp.int32, (rows, ppb * width), 1)
        head_ok = (c_idx % N_KV_HEADS) == (r_idx // rows_per_kv)
        tiles = []
        for b in range(bps):
            n = tv * bps + b
            chosen = jnp.sum(jnp.where(lane == n, sel, 0.0), axis=1, keepdims=True) > 0.0
            col = pl.multiple_of(n * (ppb * width), ppb * width)
            tiles.append(jnp.where(head_ok & chosen, s_scr[:, pl.ds(col, ppb * width)], NEG_INF))
        update(tiles, [v_pages[p][0].astype(BF16) for p in range(pps)])

    @pl.when(t == pl.num_programs(1) - 1)
    def _():
        kn = kn_ref[...]
        vn = vn_ref[...]
        k_new = jnp.concatenate([kn[:, c * HEAD_DIM:(c + 1) * HEAD_DIM] for c in range(N_KV_HEADS)], axis=0)
        v_new = jnp.concatenate([vn[:, c * HEAD_DIM:(c + 1) * HEAD_DIM] for c in range(N_KV_HEADS)], axis=0)
        s = lax.dot_general(qs_scr[...], k_new.astype(BF16), nt_dims,
                            preferred_element_type=F32)
        r2 = lax.broadcasted_iota(jnp.int32, s.shape, 0)
        c2 = lax.broadcasted_iota(jnp.int32, s.shape, 1)
        ok = ((c2 // dec_seq) == (r2 // rows_per_kv)) & ((c2 % dec_seq) <= (r2 % dec_seq))
        update([jnp.where(ok, s, NEG_INF)], [v_new.astype(BF16)])
        o = acc[...] / l_scr[...]
        o_ref[...] = jnp.concatenate([o[h * dec_seq:(h + 1) * dec_seq, :] for h in range(N_HEADS)], axis=1)


def _sample_attn(page_table, q_s, k_new, v_new, cache_k2, cache_v2, dec_seq):
    n_seq, n_pages = page_table.shape
    pps = PAGES_PER_STEP
    n_k_steps = n_pages // pps
    n_blocks = n_pages * PAGE_SIZE // MOBA_BLOCK
    assert n_pages % pps == 0 and n_blocks <= LANES
    rows = N_HEADS * dec_seq
    kw = N_KV_HEADS * HEAD_DIM
    qw = N_HEADS * HEAD_DIM
    page_shape = (1, PAGE_SIZE * N_KV_HEADS, HEAD_DIM)

    def k_spec(r):
        return pl.BlockSpec(page_shape, lambda s, t, pt: (pt[s, jnp.minimum(t, n_k_steps - 1) * pps + r], 0, 0))

    def v_spec(r):
        return pl.BlockSpec(page_shape, lambda s, t, pt: (pt[s, jnp.maximum(t - n_k_steps, 0) * pps + r], 0, 0))

    grid_spec = pltpu.PrefetchScalarGridSpec(
        num_scalar_prefetch=1,
        grid=(n_seq, 2 * n_k_steps),
        in_specs=[
            pl.BlockSpec((dec_seq, qw), lambda s, t, pt: (s, 0)),
            pl.BlockSpec((dec_seq, kw), lambda s, t, pt: (s, 0)),
            pl.BlockSpec((dec_seq, kw), lambda s, t, pt: (s, 0)),
        ] + [k_spec(r) for r in range(pps)] + [v_spec(r) for r in range(pps)],
        out_specs=pl.BlockSpec((dec_seq, qw), lambda s, t, pt: (s, 0)),
        scratch_shapes=[
            pltpu.VMEM((rows, n_pages * PAGE_SIZE * N_KV_HEADS), F32),
            pltpu.VMEM((N_KV_HEADS, n_blocks, HEAD_DIM), F32),
            pltpu.VMEM((rows, LANES), F32),
            pltpu.VMEM((rows, HEAD_DIM), BF16),
            pltpu.VMEM((rows, 1), F32),
            pltpu.VMEM((rows, 1), F32),
            pltpu.VMEM((rows, HEAD_DIM), F32),
        ],
    )
    return pl.pallas_call(
        functools.partial(_sample_attn_kernel, dec_seq=dec_seq, n_pages=n_pages),
        grid_spec=grid_spec,
        out_shape=jax.ShapeDtypeStruct((n_seq * dec_seq, qw), F32),
        compiler_params=_cparams(("arbitrary", "arbitrary")),
        name="sample_attn",
    )(page_table, q_s, k_new, v_new, *([cache_k2] * pps), *([cache_v2] * pps))


def _oproj_kernel(h_ref, o_ref, w_ref, out_ref):
    out_ref[...] = h_ref[...] + jnp.dot(o_ref[...].astype(BF16), w_ref[...], preferred_element_type=F32)


def _oproj(h, o, w_o):
    n, d = h.shape
    tm = TOKEN_TILE
    return pl.pallas_call(
        _oproj_kernel,
        grid=(n // tm,),
        in_specs=[
            pl.BlockSpec((tm, d), lambda i: (i, 0)),
            pl.BlockSpec((tm, o.shape[1]), lambda i: (i, 0)),
            pl.BlockSpec(w_o.shape, lambda i: (0, 0)),
        ],
        out_specs=pl.BlockSpec((tm, d), lambda i: (i, 0)),
        out_shape=jax.ShapeDtypeStruct((n, d), F32),
        compiler_params=_cparams(("arbitrary",)),
        name="attn_oproj",
    )(h, o, w_o)


def _mix_tables(w_s, b_s, dec_seq):
    tm = TOKEN_TILE
    causal = jnp.tril(jnp.ones((GMLP_CHUNK, GMLP_CHUNK), dtype=bool))
    w = jnp.where(causal[None], w_s, jnp.zeros_like(w_s))
    eye_p = jnp.eye(tm // GMLP_CHUNK, dtype=w.dtype)
    mix_p = jnp.einsum("ab,gts->gatbs", eye_p, w).reshape(GMLP_GROUPS, tm, tm)
    eye_s = jnp.eye(tm // dec_seq, dtype=w.dtype)
    mix_s = jnp.einsum("ab,gts->gatbs", eye_s, w[:, :dec_seq, :dec_seq]).reshape(GMLP_GROUPS, tm, tm)
    mix = jnp.stack([mix_p, mix_s]).astype(BF16)
    bias_p = jnp.tile(b_s.T, (tm // GMLP_CHUNK, 1))
    bias_s = jnp.tile(b_s.T[:dec_seq], (tm // dec_seq, 1))
    bias = jnp.stack([bias_p, bias_s])
    bias = jnp.pad(bias, ((0, 0), (0, 0), (0, LANES - GMLP_GROUPS)))
    return mix, bias


def _rope_tables(pos):
    half = HEAD_DIM // 2
    inv = ROPE_THETA ** (-jnp.arange(half, dtype=F32) * 2.0 / HEAD_DIM)
    ang = pos.astype(F32)[:, None] * inv[None, :]
    cos = jnp.cos(ang)
    sin = jnp.sin(ang)
    return jnp.concatenate([cos, cos], axis=1), jnp.concatenate([-sin, sin], axis=1)


def _router_tables(w_grp, b_grp, w_rt, b_rt):
    w = jnp.concatenate([w_grp, w_rt], axis=1)
    b = jnp.concatenate([b_grp, b_rt], axis=0)
    pad = LANES - w.shape[1]
    return jnp.pad(w, ((0, 0), (0, pad))), jnp.pad(b, (0, pad)).reshape(1, LANES)


def kernel(x_prompt, x_sample, cache_k, cache_v, page_table, norm_mix, norm_ffn, a_w_in, a_ln_g, a_ln_b,
           a_w_s, a_b_s, a_w_out, kv_norm, w_kv, k_norm, b_w_q, b_q_norm, b_w_o, moe_w_grp, moe_b_grp,
           moe_w_rt, moe_b_rt, moe_w_gate, moe_w_up, moe_w_down):
    batch, seq, d = x_prompt.shape
    n_seq, dec_seq, _ = x_sample.shape
    n_prompt = batch * seq
    n_sample = n_seq * dec_seq
    assert n_prompt % TOKEN_TILE == 0 and n_sample == TOKEN_TILE and seq % MOBA_BLOCK == 0
    past_len = page_table.shape[1] * PAGE_SIZE
    assert past_len % MOBA_BLOCK == 0 and dec_seq <= MOBA_BLOCK

    x = jnp.concatenate([x_prompt.reshape(n_prompt, d), x_sample.reshape(n_sample, d)], axis=0)
    pos = jnp.concatenate([jnp.tile(jnp.arange(seq), batch), jnp.tile(past_len + jnp.arange(dec_seq), n_seq)])
    cos, sin = _rope_tables(pos)
    row = lambda a: a.reshape(1, -1)

    mix, bias = _mix_tables(a_w_s[0], a_b_s[0], dec_seq)
    h, vg_sample = _gmlp_layer(x, row(norm_mix[0]), a_w_in[0].astype(BF16), row(a_ln_g[0]), row(a_ln_b[0]),
                               mix, bias, a_w_out[0].astype(BF16), n_prompt // TOKEN_TILE)
    moe = []
    for layer in range(2):
        w_r, b_r = _router_tables(moe_w_grp[layer], moe_b_grp[layer], moe_w_rt[layer], moe_b_rt[layer])
        moe.append((row(norm_ffn[layer]), w_r, b_r, moe_w_gate[layer].astype(BF16),
                    moe_w_up[layer].astype(BF16), moe_w_down[layer].astype(BF16)))
    h = _hier_moe(h, *moe[0])

    k, v, q = _kvq_proj(h, row(kv_norm), row(norm_mix[1]), w_kv.astype(BF16), b_w_q[0].astype(BF16),
                        row(k_norm), row(b_q_norm[0]), cos, sin)

    o_prompt = _moba_prompt(q, k, v, batch, seq)
    n_phys = cache_k.shape[0]
    cache_k2 = cache_k.reshape(n_phys, PAGE_SIZE * N_KV_HEADS, HEAD_DIM)
    cache_v2 = cache_v.reshape(n_phys, PAGE_SIZE * N_KV_HEADS, HEAD_DIM)
    q_s, k_s, v_s = q[n_prompt:], k[n_prompt:], v[n_prompt:]
    o_sample = _sample_attn(page_table, q_s, k_s, v_s, cache_k2, cache_v2, dec_seq)
    h = _oproj(h, jnp.concatenate([o_prompt, o_sample], axis=0), b_w_o[0].astype(BF16))
    h = _hier_moe(h, *moe[1])

    n_pages_new = seq // PAGE_SIZE
    return (h[:n_prompt].reshape(batch, seq, d),
            h[n_prompt:].reshape(n_seq, dec_seq, d),
            k[:n_prompt].reshape(batch, n_pages_new, PAGE_SIZE, N_KV_HEADS, HEAD_DIM),
            v[:n_prompt].reshape(batch, n_pages_new, PAGE_SIZE, N_KV_HEADS, HEAD_DIM),
            k_s.reshape(n_seq, dec_seq, N_KV_HEADS, HEAD_DIM),
            v_s.reshape(n_seq, dec_seq, N_KV_HEADS, HEAD_DIM),
            vg_sample.reshape(1, n_seq, dec_seq, -1))
```

```python
import functools
import math

import jax
import jax.numpy as jnp
from jax import lax
from jax.experimental import pallas as pl
from jax.experimental.pallas import tpu as pltpu

F32 = jnp.float32
BF16 = jnp.bfloat16
HIGHEST = lax.Precision.HIGHEST

GMLP_CHUNK = 128
GMLP_GROUPS = 8
N_HEADS = 8
N_KV_HEADS = 4
HEAD_DIM = 128
Q_PER_KV = N_HEADS // N_KV_HEADS
MOBA_BLOCK = 256
MOBA_TOP_K = 3
ROPE_THETA = 10000.0
N_GROUPS = 4
EXPERTS_PER_GROUP = 8
N_EXPERTS = N_GROUPS * EXPERTS_PER_GROUP
TOP_K_EXPERTS = 2
PAGE_SIZE = 128
EPS = 1e-6

LANES = 128
ROW_TILE = 8
TOKEN_TILE = 256
FFN_ROWS = 256
PAGES_PER_STEP = 16
KEY_GROUP = 4
ONES_ROWS = 16
LOG2_E = math.log2(math.e)
VMEM_LIMIT = 56 * 1024 * 1024

NEG_INF = float("-inf")


def _cparams(sem):
    return pltpu.CompilerParams(dimension_semantics=sem, vmem_limit_bytes=VMEM_LIMIT)


def _rms(x):
    return x * lax.rsqrt(jnp.mean(x * x, axis=-1, keepdims=True) + EPS)


def _gmlp_kernel(x_ref, g_ref, win_ref, lng_ref, lnb_ref, mix_ref, bias_ref, wout_ref,
                 h_ref, vg_ref, *, d_gate, n_groups):
    i = pl.program_id(0)
    x = x_ref[...]
    xb = (_rms(x) * g_ref[...]).astype(BF16)
    u = jax.nn.gelu(jnp.dot(xb, win_ref[:, :d_gate], preferred_element_type=F32))
    vp = jax.nn.gelu(jnp.dot(xb, win_ref[:, d_gate:], preferred_element_type=F32))
    vc = vp - jnp.mean(vp, axis=-1, keepdims=True)
    var = jnp.mean(vc * vc, axis=-1, keepdims=True)
    vg = vc * lax.rsqrt(var + EPS) * lng_ref[...] + lnb_ref[...]

    @pl.when(i == pl.num_programs(0) - 1)
    def _():
        vg_ref[...] = vg

    vgb = vg.astype(BF16)
    cw = d_gate // n_groups
    bias = bias_ref[0]
    parts = []
    for g in range(n_groups):
        mixed = jnp.dot(mix_ref[0, g], vgb[:, g * cw:(g + 1) * cw], preferred_element_type=F32)
        mixed = mixed + bias[:, g:g + 1]
        parts.append((u[:, g * cw:(g + 1) * cw] * mixed).astype(BF16))
    gated = jnp.concatenate(parts, axis=1)
    h_ref[...] = x + jnp.dot(gated, wout_ref[...], preferred_element_type=F32)


def _gmlp_layer(x, g, w_in, ln_g, ln_b, mix, bias, w_out, n_prompt_tiles):
    n, d = x.shape
    d_gate = w_out.shape[0]
    tm = TOKEN_TILE
    n_tiles = n // tm
    kind = lambda i: jnp.where(i < n_prompt_tiles, 0, 1)
    return pl.pallas_call(
        functools.partial(_gmlp_kernel, d_gate=d_gate, n_groups=GMLP_GROUPS),
        grid=(n_tiles,),
        in_specs=[
            pl.BlockSpec((tm, d), lambda i: (i, 0)),
            pl.BlockSpec((1, d), lambda i: (0, 0)),
            pl.BlockSpec((d, 2 * d_gate), lambda i: (0, 0)),
            pl.BlockSpec((1, d_gate), lambda i: (0, 0)),
            pl.BlockSpec((1, d_gate), lambda i: (0, 0)),
            pl.BlockSpec((1, GMLP_GROUPS, tm, tm), lambda i: (kind(i), 0, 0, 0)),
            pl.BlockSpec((1, tm, LANES), lambda i: (kind(i), 0, 0)),
            pl.BlockSpec((d_gate, d), lambda i: (0, 0)),
        ],
        out_specs=[
            pl.BlockSpec((tm, d), lambda i: (i, 0)),
            pl.BlockSpec((tm, d_gate), lambda i: (0, 0)),
        ],
        out_shape=[
            jax.ShapeDtypeStruct((n, d), F32),
            jax.ShapeDtypeStruct((tm, d_gate), F32),
        ],
        compiler_params=_cparams(("arbitrary",)),
        name="gmlp_layer",
    )(x, g, w_in, ln_g, ln_b, mix, bias, w_out)


def _router_kernel(h_ref, g_ref, wr_ref, br_ref, xn_ref, e_ref, w_ref, rank_ref, hist_ref):
    xn = _rms(h_ref[...]) * g_ref[...]
    _store_row_tiles(xn_ref, xn)
    logits = jnp.dot(xn, wr_ref[...], precision=HIGHEST, preferred_element_type=F32) + br_ref[...]
    lane = lax.broadcasted_iota(jnp.int32, logits.shape, 1)
    big = jnp.int32(LANES)
    is_grp = lane < N_GROUPS
    gl = jnp.where(is_grp, logits, NEG_INF)
    gmax = jnp.max(gl, axis=1, keepdims=True)
    gidx = jnp.min(jnp.where(is_grp & (logits == gmax), lane, big), axis=1, keepdims=True)
    p_g = 1.0 / jnp.sum(jnp.where(is_grp, jnp.exp(gl - gmax), 0.0), axis=1, keepdims=True)
    lo = N_GROUPS + gidx * EXPERTS_PER_GROUP
    in_grp = (lane >= lo) & (lane < lo + EXPERTS_PER_GROUP)
    v0 = jnp.max(jnp.where(in_grp, logits, NEG_INF), axis=1, keepdims=True)
    i0 = jnp.min(jnp.where(in_grp & (logits == v0), lane, big), axis=1, keepdims=True)
    rest = in_grp & (lane != i0)
    v1 = jnp.max(jnp.where(rest, logits, NEG_INF), axis=1, keepdims=True)
    i1 = jnp.min(jnp.where(rest & (logits == v1), lane, big), axis=1, keepdims=True)
    t = jnp.exp(v1 - v0)
    w0 = p_g * (1.0 / (1.0 + t))
    w1 = p_g * (t / (1.0 + t))
    e0 = i0 - N_GROUPS
    e1 = i1 - N_GROUPS
    e_ref[...] = jnp.where(lane == 0, e0, jnp.where(lane == 1, e1, 0))
    w_ref[...] = jnp.where(lane == 0, w0, jnp.where(lane == 1, w1, 0.0))
    tm = logits.shape[0]
    onehot = jnp.concatenate([(lane == e0).astype(F32), (lane == e1).astype(F32)], axis=0)
    a_row = lax.broadcasted_iota(jnp.int32, (2 * tm, 2 * tm), 0)
    a_col = lax.broadcasted_iota(jnp.int32, (2 * tm, 2 * tm), 1)
    earlier = (a_col < a_row).astype(BF16)
    before = jnp.dot(earlier, onehot.astype(BF16), preferred_element_type=F32)
    rank = jnp.sum(before * onehot, axis=1, keepdims=True).astype(jnp.int32)
    rank_ref[...] = jnp.where(lane == 0, rank[:tm], jnp.where(lane == 1, rank[tm:], 0))
    hist = jnp.sum(onehot, axis=0, keepdims=True).astype(jnp.int32)
    hist_ref[...] = jnp.broadcast_to(hist, hist_ref.shape)


def _router(h, g, w_r, b_r):
    n, d = h.shape
    tm = TOKEN_TILE
    return pl.pallas_call(
        _router_kernel,
        grid=(n // tm,),
        in_specs=[
            pl.BlockSpec((tm, d), lambda i: (i, 0)),
            pl.BlockSpec((1, d), lambda i: (0, 0)),
            pl.BlockSpec((d, LANES), lambda i: (0, 0)),
            pl.BlockSpec((1, LANES), lambda i: (0, 0)),
        ],
        out_specs=[
            pl.BlockSpec((tm * ROW_TILE, LANES), lambda i: (i, 0)),
            pl.BlockSpec((tm, LANES), lambda i: (i, 0)),
            pl.BlockSpec((tm, LANES), lambda i: (i, 0)),
            pl.BlockSpec((tm, LANES), lambda i: (i, 0)),
            pl.BlockSpec((ROW_TILE, LANES), lambda i: (i, 0)),
        ],
        out_shape=[
            jax.ShapeDtypeStruct((n * ROW_TILE, LANES), F32),
            jax.ShapeDtypeStruct((n, LANES), jnp.int32),
            jax.ShapeDtypeStruct((n, LANES), F32),
            jax.ShapeDtypeStruct((n, LANES), jnp.int32),
            jax.ShapeDtypeStruct((n // tm * ROW_TILE, LANES), jnp.int32),
        ],
        compiler_params=_cparams(("arbitrary",)),
        name="moe_router",
    )(h, g, w_r, b_r)


def _pos_kernel(e_ref, rank_ref, off_ref, pos_ref):
    e = e_ref[...]
    rank = rank_ref[...]
    off = off_ref[0:1, :]
    lane = lax.broadcasted_iota(jnp.int32, e.shape, 1)
    pos = []
    for k in range(TOP_K_EXPERTS):
        first = jnp.sum(jnp.where(lane == e[:, k:k + 1], off, 0), axis=1, keepdims=True)
        pos.append(first + rank[:, k:k + 1])
    pos_ref[...] = jnp.where(lane == 0, pos[0], jnp.where(lane == 1, pos[1], 0))


def _sorted_positions(e_pad, rank_pad, off):
    n = e_pad.shape[0]
    tm = TOKEN_TILE
    return pl.pallas_call(
        _pos_kernel,
        grid=(n // tm,),
        in_specs=[
            pl.BlockSpec((tm, LANES), lambda i: (i, 0)),
            pl.BlockSpec((tm, LANES), lambda i: (i, 0)),
            pl.BlockSpec((ROW_TILE, LANES), lambda i: (i, 0)),
        ],
        out_specs=pl.BlockSpec((tm, LANES), lambda i: (i, 0)),
        out_shape=jax.ShapeDtypeStruct((n, LANES), jnp.int32),
        compiler_params=_cparams(("arbitrary",)),
        name="moe_pos",
    )(e_pad, rank_pad, off)


def _dispatch_kernel(pos_ref, xn_ref, init_hbm, out_hbm, sem):
    del init_hbm
    i = pl.program_id(0)
    tm = xn_ref.shape[0] // ROW_TILE
    base = i * tm * TOP_K_EXPERTS

    def issue(t, carry):
        for k in range(TOP_K_EXPERTS):
            p = pos_ref[base + t * TOP_K_EXPERTS + k]
            pltpu.make_async_copy(_row_tile(xn_ref, t), _row_tile(out_hbm, p), sem).start()
        return carry

    lax.fori_loop(0, tm, issue, 0, unroll=4)
    for k in range(TOP_K_EXPERTS):
        pltpu.make_async_copy(xn_ref, out_hbm.at[pl.ds(0, tm * ROW_TILE), :], sem).wait()


def _dispatch(xn, pos, n_rows):
    tm = TOKEN_TILE
    n = xn.shape[0] // ROW_TILE
    grid_spec = pltpu.PrefetchScalarGridSpec(
        num_scalar_prefetch=1,
        grid=(n // tm,),
        in_specs=[
            pl.BlockSpec((tm * ROW_TILE, LANES), lambda i, ps: (i, 0)),
            pl.BlockSpec(memory_space=pl.ANY),
        ],
        out_specs=pl.BlockSpec(memory_space=pl.ANY),
        scratch_shapes=[pltpu.SemaphoreType.DMA(())],
    )
    return pl.pallas_call(
        _dispatch_kernel,
        grid_spec=grid_spec,
        out_shape=jax.ShapeDtypeStruct((n_rows * ROW_TILE, LANES), F32),
        input_output_aliases={2: 0},
        compiler_params=_cparams(("arbitrary",)),
        name="moe_dispatch",
    )(pos, xn, jnp.zeros((n_rows * ROW_TILE, LANES), F32))


def _store_row_tiles(ref, x, first=0):
    for c in range(ROW_TILE):
        ref[pl.ds(first + c, x.shape[0], stride=ROW_TILE), :] = x[:, c * LANES:(c + 1) * LANES]


def _load_row_tiles(ref, first, rows):
    return jnp.concatenate([ref[pl.ds(first + c, rows, stride=ROW_TILE), :] for c in range(ROW_TILE)], axis=1)


def _row_tile(ref, r):
    return ref.at[pl.ds(pl.multiple_of(r * ROW_TILE, ROW_TILE), ROW_TILE), :]


def _ffn_kernel(blk_e_ref, nblk_ref, fresh_ref, x_ref, wg_ref, wu_ref, wd_ref, out_ref, wg_s, wu_s, wd_s):
    i = pl.program_id(0)
    rows = x_ref.shape[0] // ROW_TILE
    live = i < nblk_ref[0]

    @pl.when(live & (fresh_ref[i] == 1))
    def _():
        wg_s[...] = wg_ref[0].astype(BF16)
        wu_s[...] = wu_ref[0].astype(BF16)
        wd_s[...] = wd_ref[0].astype(BF16)

    @pl.when(live)
    def _():
        x = _load_row_tiles(x_ref, 0, rows).astype(BF16)
        gate = jnp.dot(x, wg_s[...], preferred_element_type=F32)
        up = jnp.dot(x, wu_s[...], preferred_element_type=F32)
        hid = (jax.nn.silu(gate) * up).astype(BF16)
        _store_row_tiles(out_ref, jnp.dot(hid, wd_s[...], preferred_element_type=F32))

    @pl.when(jnp.logical_not(live))
    def _():
        out_ref[...] = jnp.zeros_like(out_ref)


def _grouped_ffn(x_sorted, blk_e, nblk, fresh, w_gate, w_up, w_down):
    rb = FFN_ROWS
    n_rows = x_sorted.shape[0] // ROW_TILE
    _, d, d_e = w_gate.shape
    assert d == ROW_TILE * LANES
    grid_spec = pltpu.PrefetchScalarGridSpec(
        num_scalar_prefetch=3,
        grid=(n_rows // rb,),
        in_specs=[
            pl.BlockSpec((rb * ROW_TILE, LANES), lambda i, be, nb, fr: (i, 0)),
            pl.BlockSpec((1, d, d_e), lambda i, be, nb, fr: (be[i], 0, 0)),
            pl.BlockSpec((1, d, d_e), lambda i, be, nb, fr: (be[i], 0, 0)),
            pl.BlockSpec((1, d_e, d), lambda i, be, nb, fr: (be[i], 0, 0)),
        ],
        out_specs=pl.BlockSpec((rb * ROW_TILE, LANES), lambda i, be, nb, fr: (i, 0)),
        scratch_shapes=[pltpu.VMEM((d, d_e), BF16), pltpu.VMEM((d, d_e), BF16), pltpu.VMEM((d_e, d), BF16)],
    )
    return pl.pallas_call(
        _ffn_kernel,
        grid_spec=grid_spec,
        out_shape=jax.ShapeDtypeStruct((n_rows * ROW_TILE, LANES), F32),
        compiler_params=_cparams(("arbitrary",)),
        name="moe_ffn",
    )(blk_e, nblk, fresh, x_sorted, w_gate, w_up, w_down)


def _combine_kernel(pos_ref, resid_ref, w_ref, src_hbm, out_ref, buf, sems):
    i = pl.program_id(0)
    tm = resid_ref.shape[0]

    def gather(tile, slot):
        base = tile * tm * TOP_K_EXPERTS

        def issue(t, carry):
            for k in range(TOP_K_EXPERTS):
                p = pos_ref[base + t * TOP_K_EXPERTS + k]
                pltpu.make_async_copy(_row_tile(src_hbm, p), _row_tile(buf, (slot * TOP_K_EXPERTS + k) * tm + t),
                                      sems.at[slot]).start()
            return carry

        lax.fori_loop(0, tm, issue, 0, unroll=4)

    @pl.when(i == 0)
    def _():
        gather(0, 0)

    @pl.when(i + 1 < pl.num_programs(0))
    def _():
        gather(i + 1, (i + 1) % 2)

    slot = i % 2
    firsts = [pl.multiple_of((slot * TOP_K_EXPERTS + k) * tm * ROW_TILE, ROW_TILE) for k in range(TOP_K_EXPERTS)]
    for first in firsts:
        pltpu.make_async_copy(src_hbm.at[pl.ds(0, tm * ROW_TILE), :],
                              buf.at[pl.ds(first, tm * ROW_TILE), :], sems.at[slot]).wait()
    acc = resid_ref[...]
    w = w_ref[...]
    for k, first in enumerate(firsts):
        acc = acc + w[:, k:k + 1] * _load_row_tiles(buf, first, tm)
    out_ref[...] = acc


def _combine(resid, w_pad, src, pos):
    n, d = resid.shape
    assert d == ROW_TILE * LANES
    tm = TOKEN_TILE
    grid_spec = pltpu.PrefetchScalarGridSpec(
        num_scalar_prefetch=1,
        grid=(n // tm,),
        in_specs=[
            pl.BlockSpec((tm, d), lambda i, ps: (i, 0)),
            pl.BlockSpec((tm, LANES), lambda i, ps: (i, 0)),
            pl.BlockSpec(memory_space=pl.ANY),
        ],
        out_specs=pl.BlockSpec((tm, d), lambda i, ps: (i, 0)),
        scratch_shapes=[pltpu.VMEM((2 * TOP_K_EXPERTS * tm * ROW_TILE, LANES), F32),
                        pltpu.SemaphoreType.DMA((2,))],
    )
    return pl.pallas_call(
        _combine_kernel,
        grid_spec=grid_spec,
        out_shape=jax.ShapeDtypeStruct((n, d), F32),
        compiler_params=_cparams(("arbitrary",)),
        name="moe_combine",
    )(pos, resid, w_pad, src)


def _hier_moe(h, g, w_r, b_r, w_gate, w_up, w_down):
    n, _ = h.shape
    tm = TOKEN_TILE
    rb = FFN_ROWS
    xn, e_pad, w_pad, rank_pad, hist_pad = _router(h, g, w_r, b_r)
    hist = hist_pad[::ROW_TILE, :N_EXPERTS]
    counts = jnp.sum(hist, axis=0)
    padded = (counts + rb - 1) // rb * rb
    pend = jnp.cumsum(padded)
    tile_first = (pend - padded)[None, :] + jnp.cumsum(hist, axis=0) - hist
    off = jnp.repeat(jnp.pad(tile_first, ((0, 0), (0, LANES - N_EXPERTS))), ROW_TILE, axis=0).astype(jnp.int32)
    pos_pad = _sorted_positions(e_pad, rank_pad, off)
    pos = pos_pad[:, :TOP_K_EXPERTS].reshape(-1)
    n_blocks = (n * TOP_K_EXPERTS + N_EXPERTS * (rb - 1) + rb - 1) // rb
    blk_e = jnp.minimum(jnp.searchsorted(pend, jnp.arange(n_blocks, dtype=jnp.int32) * rb, side="right"),
                        N_EXPERTS - 1).astype(jnp.int32)
    fresh = jnp.concatenate([jnp.ones((1,), jnp.int32), (blk_e[1:] != blk_e[:-1]).astype(jnp.int32)])
    nblk = (pend[-1:] // rb).astype(jnp.int32)
    x_sorted = _dispatch(xn, pos, n_blocks * rb)
    out_sorted = _grouped_ffn(x_sorted, blk_e, nblk, fresh, w_gate, w_up, w_down)
    return _combine(h, w_pad, out_sorted, pos)


def _proj_kernel(h_ref, gkv_ref, gq_ref, wkv_ref, wq_ref, kn_ref, qn_ref, cos_ref, sin_ref,
                 k_ref, v_ref, q_ref):
    hn = _rms(h_ref[...])
    cos = cos_ref[...]
    sin = sin_ref[...]

    def norm_rope(x, g):
        y = _rms(x) * g
        return y * cos + pltpu.roll(y, HEAD_DIM // 2, 1) * sin

    kv = jnp.dot((hn * gkv_ref[...]).astype(BF16), wkv_ref[...], preferred_element_type=F32)
    kw = N_KV_HEADS * HEAD_DIM
    k_ref[...] = jnp.concatenate(
        [norm_rope(kv[:, j * HEAD_DIM:(j + 1) * HEAD_DIM], kn_ref[...]) for j in range(N_KV_HEADS)], axis=1)
    v_ref[...] = kv[:, kw:]
    q = jnp.dot((hn * gq_ref[...]).astype(BF16), wq_ref[...], preferred_element_type=F32)
    q_ref[...] = jnp.concatenate(
        [norm_rope(q[:, j * HEAD_DIM:(j + 1) * HEAD_DIM], qn_ref[...]) for j in range(N_HEADS)], axis=1)


def _kvq_proj(h, g_kv, g_q, w_kv, w_q, k_norm, q_norm, cos, sin):
    n, d = h.shape
    tm = TOKEN_TILE
    kw = N_KV_HEADS * HEAD_DIM
    qw = N_HEADS * HEAD_DIM
    row = lambda i: (i, 0)
    fixed = lambda i: (0, 0)
    return pl.pallas_call(
        _proj_kernel,
        grid=(n // tm,),
        in_specs=[
            pl.BlockSpec((tm, d), row),
            pl.BlockSpec((1, d), fixed),
            pl.BlockSpec((1, d), fixed),
            pl.BlockSpec((d, 2 * kw), fixed),
            pl.BlockSpec((d, qw), fixed),
            pl.BlockSpec((1, HEAD_DIM), fixed),
            pl.BlockSpec((1, HEAD_DIM), fixed),
            pl.BlockSpec((tm, HEAD_DIM), row),
            pl.BlockSpec((tm, HEAD_DIM), row),
        ],
        out_specs=[
            pl.BlockSpec((tm, kw), row),
            pl.BlockSpec((tm, kw), row),
            pl.BlockSpec((tm, qw), row),
        ],
        out_shape=[
            jax.ShapeDtypeStruct((n, kw), F32),
            jax.ShapeDtypeStruct((n, kw), F32),
            jax.ShapeDtypeStruct((n, qw), F32),
        ],
        compiler_params=_cparams(("arbitrary",)),
        name="kvq_proj",
    )(h, g_kv, g_q, w_kv, w_q, k_norm, q_norm, cos, sin)


def _top_blocks(gate, axis):
    idx = lax.broadcasted_iota(jnp.int32, gate.shape, axis)
    big = jnp.int32(gate.shape[axis])
    sel = jnp.zeros(gate.shape, jnp.bool_)
    for _ in range(MOBA_TOP_K):
        top = jnp.max(gate, axis=axis, keepdims=True)
        first = jnp.min(jnp.where(gate == top, idx, big), axis=axis, keepdims=True)
        hit = idx == first
        sel = sel | (hit & (top > NEG_INF))
        gate = jnp.where(hit, NEG_INF, gate)
    return sel.astype(F32)


def _moba_prompt_kernel(q_ref, k_ref, v_ref, o_ref, kbf, vt, kmean, sel, s_a, s_b, m_scr, acc):
    j = pl.program_id(2)
    blk = MOBA_BLOCK
    grp = KEY_GROUP
    seq = k_ref.shape[0]
    n_blocks = seq // blk

    @pl.when(j == 0)
    def _():
        k = k_ref[...]
        kbf[...] = k.astype(BF16)
        kmean[...] = jnp.mean(k.reshape(n_blocks, blk, HEAD_DIM), axis=1)
        for n in range(n_blocks):
            vt[:HEAD_DIM, n * blk:(n + 1) * blk] = v_ref[n * blk:(n + 1) * blk, :].T.astype(BF16)
        r = lax.broadcasted_iota(jnp.int32, (ONES_ROWS, seq), 0)
        vt[HEAD_DIM:, :] = jnp.where(r == 0, 1.0, 0.0).astype(BF16)

    q2 = q_ref[...]
    qs = jnp.concatenate([q2[:, h * HEAD_DIM:(h + 1) * HEAD_DIM] for h in range(Q_PER_KV)], axis=0)
    nq = qs.shape[0]
    gate = lax.dot_general(kmean[...], qs, (((1,), (1,)), ((), ())),
                           precision=HIGHEST, preferred_element_type=F32)
    row = lax.broadcasted_iota(jnp.int32, gate.shape, 0)
    sel[...] = _top_blocks(jnp.where(row < j, gate, NEG_INF), 0)
    qt = (qs * (HEAD_DIM ** -0.5 * LOG2_E)).T.astype(BF16)

    m_scr[...] = jnp.full(m_scr.shape, NEG_INF, F32)
    acc[...] = jnp.zeros(acc.shape, F32)

    def update(tiles, start):
        m_old = m_scr[...]
        m_new = m_old
        for s in tiles:
            m_new = jnp.maximum(m_new, jnp.max(s, axis=0, keepdims=True))
        m_safe = jnp.where(m_new == NEG_INF, 0.0, m_new)
        p = jnp.concatenate([jnp.exp2(s - m_safe).astype(BF16) for s in tiles], axis=0)
        alpha = jnp.exp2(m_old - m_safe)
        pv = jnp.dot(vt[:, pl.ds(start, len(tiles) * blk)], p, preferred_element_type=F32)
        acc[...] = alpha * acc[...] + pv
        m_scr[...] = m_new

    def scores(start, n_keys):
        return jnp.dot(kbf[pl.ds(start, n_keys), :], qt, preferred_element_type=F32)

    def fill(buf, g):
        start = pl.multiple_of(g * (grp * blk), grp * blk)
        buf[...] = scores(start, grp * blk)

    def consume(buf, g):
        start = pl.multiple_of(g * (grp * blk), grp * blk)
        update([jnp.where(sel[pl.ds(g * grp + i, 1), :] > 0.0, buf[i * blk:(i + 1) * blk, :], NEG_INF)
                for i in range(grp)], start)

    n_groups = lax.div(j + (grp - 1), grp)
    n_pairs = lax.div(n_groups - 1, 2)

    @pl.when(n_groups > 0)
    def _():
        fill(s_a, 0)

        def pair(h, carry):
            g = 2 * h
            fill(s_b, g + 1)
            consume(s_a, g)
            fill(s_a, g + 2)
            consume(s_b, g + 1)
            return carry

        lax.fori_loop(0, n_pairs, pair, 0)
        g = 2 * n_pairs

        @pl.when(n_groups - g == 1)
        def _():
            consume(s_a, g)

        @pl.when(n_groups - g == 2)
        def _():
            fill(s_b, g + 1)
            consume(s_a, g)
            consume(s_b, g + 1)

    key = lax.broadcasted_iota(jnp.int32, (blk, nq), 0)
    qpos = lax.broadcasted_iota(jnp.int32, (blk, nq), 1) % blk
    own = pl.multiple_of(j * blk, blk)
    update([jnp.where(key <= qpos, scores(own, blk), NEG_INF)], own)
    a = acc[...]
    o = (a[:HEAD_DIM] / a[HEAD_DIM:HEAD_DIM + 1]).T
    o_ref[...] = jnp.concatenate([o[h * blk:(h + 1) * blk, :] for h in range(Q_PER_KV)], axis=1)


def _moba_prompt(q, k, v, batch, seq):
    blk = MOBA_BLOCK
    nqb = seq // blk
    qw = Q_PER_KV * HEAD_DIM
    nq = Q_PER_KV * blk
    return pl.pallas_call(
        _moba_prompt_kernel,
        grid=(batch, N_KV_HEADS, nqb),
        in_specs=[
            pl.BlockSpec((blk, qw), lambda b, c, j: (b * nqb + j, c)),
            pl.BlockSpec((seq, HEAD_DIM), lambda b, c, j: (b, c)),
            pl.BlockSpec((seq, HEAD_DIM), lambda b, c, j: (b, c)),
        ],
        out_specs=pl.BlockSpec((blk, qw), lambda b, c, j: (b * nqb + j, c)),
        out_shape=jax.ShapeDtypeStruct((batch * seq, N_HEADS * HEAD_DIM), F32),
        scratch_shapes=[
            pltpu.VMEM((seq, HEAD_DIM), BF16),
            pltpu.VMEM((HEAD_DIM + ONES_ROWS, seq), BF16),
            pltpu.VMEM((nqb, HEAD_DIM), F32),
            pltpu.VMEM((nqb, nq), F32),
            pltpu.VMEM((KEY_GROUP * blk, nq), F32),
            pltpu.VMEM((KEY_GROUP * blk, nq), F32),
            pltpu.VMEM((1, nq), F32),
            pltpu.VMEM((HEAD_DIM + ONES_ROWS, nq), F32),
        ],
        compiler_params=_cparams(("arbitrary", "arbitrary", "arbitrary")),
        name="moba_prompt",
    )(q, k, v)


def _stack_heads(q8):
    return jnp.concatenate([q8[:, h * HEAD_DIM:(h + 1) * HEAD_DIM] for h in range(N_HEADS)], axis=0)


def _sample_attn_kernel(pt_ref, q_ref, kn_ref, vn_ref, *refs, dec_seq, n_pages):
    pps = PAGES_PER_STEP
    k_pages = refs[:pps]
    v_pages = refs[pps:2 * pps]
    o_ref, s_scr, means, sel_scr, qs_scr, m_scr, l_scr, acc = refs[2 * pps:]
    t = pl.program_id(1)
    n_k_steps = n_pages // pps
    rows = N_HEADS * dec_seq
    rows_per_kv = Q_PER_KV * dec_seq
    width = PAGE_SIZE * N_KV_HEADS
    ppb = MOBA_BLOCK // PAGE_SIZE
    bps = pps // ppb
    n_blocks = n_pages // ppb
    nt_dims = (((1,), (1,)), ((), ()))

    @pl.when(t == 0)
    def _():
        qs_scr[...] = (_stack_heads(q_ref[...]) * (HEAD_DIM ** -0.5 * LOG2_E)).astype(BF16)
        m_scr[...] = jnp.full(m_scr.shape, NEG_INF, F32)
        l_scr[...] = jnp.zeros(l_scr.shape, F32)
        acc[...] = jnp.zeros(acc.shape, F32)

    @pl.when(t < n_k_steps)
    def _():
        qs = qs_scr[...]
        sub = ROW_TILE // N_KV_HEADS
        for b in range(bps):
            tot = jnp.zeros((ROW_TILE, HEAD_DIM), F32)
            for p in range(ppb):
                page = k_pages[b * ppb + p][0]
                col = pl.multiple_of((t * pps + b * ppb + p) * width, width)
                s_scr[:, pl.ds(col, width)] = lax.dot_general(qs, page.astype(BF16), nt_dims,
                                                              preferred_element_type=F32)
                tot = tot + jnp.sum(page.reshape(width // ROW_TILE, ROW_TILE, HEAD_DIM), axis=0)
            per_head = tot[:N_KV_HEADS]
            for r in range(1, sub):
                per_head = per_head + tot[r * N_KV_HEADS:(r + 1) * N_KV_HEADS]
            per_head = per_head / MOBA_BLOCK
            for c in range(N_KV_HEADS):
                means[c, pl.ds(t * bps + b, 1), :] = per_head[c:c + 1, :]

    @pl.when(t == n_k_steps - 1)
    def _():
        qf = _stack_heads(q_ref[...])
        row = lax.broadcasted_iota(jnp.int32, (rows, n_blocks), 0)
        gate = jnp.zeros((rows, n_blocks), F32)
        for c in range(N_KV_HEADS):
            g_c = lax.dot_general(qf, means[c], nt_dims, precision=HIGHEST, preferred_element_type=F32)
            gate = jnp.where(row // rows_per_kv == c, g_c, gate)
        chosen = _top_blocks(gate, 1)
        sel_scr[...] = jnp.concatenate([chosen, jnp.zeros((rows, LANES - n_blocks), F32)], axis=1)

    def update(tiles, vals):
        m_old = m_scr[...]
        m_new = m_old
        for s in tiles:
            m_new = jnp.maximum(m_new, jnp.max(s, axis=1, keepdims=True))
        m_safe = jnp.where(m_new == NEG_INF, 0.0, m_new)
        alpha = jnp.exp2(m_old - m_safe)
        l_new = alpha * l_scr[...]
        a_new = alpha * acc[...]
        page = 0
        for s in tiles:
            p = jnp.exp2(s - m_safe)
            l_new = l_new + jnp.sum(p, axis=1, keepdims=True)
            pb = p.astype(BF16)
            for c in range(s.shape[1] // vals[page].shape[0]):
                v = vals[page]
                a_new = a_new + jnp.dot(pb[:, c * v.shape[0]:(c + 1) * v.shape[0]], v, preferred_element_type=F32)
                page += 1
        l_scr[...] = l_new
        acc[...] = a_new
        m_scr[...] = m_new

    @pl.when(t >= n_k_steps)
    def _():
        tv = t - n_k_steps
        sel = sel_scr[...]
        lane = lax.broadcasted_iota(jnp.int32, sel.shape, 1)
        r_idx = lax.broadcasted_iota(jnp.int32, (rows, ppb * width), 0)
        c_idx = lax.broadcasted_iota(jnp.int32, (rows, ppb * width), 1)
        head_ok = (c_idx % N_KV_HEADS) == (r_idx // rows_per_kv)
        tiles = []
        for b in range(bps):
            n = tv * bps + b
            chosen = jnp.sum(jnp.where(lane == n, sel, 0.0), axis=1, keepdims=True) > 0.0
            col = pl.multiple_of(n * (ppb * width), ppb * width)
            tiles.append(jnp.where(head_ok & chosen, s_scr[:, pl.ds(col, ppb * width)], NEG_INF))
        update(tiles, [v_pages[p][0].astype(BF16) for p in range(pps)])

    @pl.when(t == pl.num_programs(1) - 1)
    def _():
        kn = kn_ref[...]
        vn = vn_ref[...]
        k_new = jnp.concatenate([kn[:, c * HEAD_DIM:(c + 1) * HEAD_DIM] for c in range(N_KV_HEADS)], axis=0)
        v_new = jnp.concatenate([vn[:, c * HEAD_DIM:(c + 1) * HEAD_DIM] for c in range(N_KV_HEADS)], axis=0)
        s = lax.dot_general(qs_scr[...], k_new.astype(BF16), nt_dims,
                            preferred_element_type=F32)
        r2 = lax.broadcasted_iota(jnp.int32, s.shape, 0)
        c2 = lax.broadcasted_iota(jnp.int32, s.shape, 1)
        ok = ((c2 // dec_seq) == (r2 // rows_per_kv)) & ((c2 % dec_seq) <= (r2 % dec_seq))
        update([jnp.where(ok, s, NEG_INF)], [v_new.astype(BF16)])
        o = acc[...] / l_scr[...]
        o_ref[...] = jnp.concatenate([o[h * dec_seq:(h + 1) * dec_seq, :] for h in range(N_HEADS)], axis=1)


def _sample_attn(page_table, q_s, k_new, v_new, cache_k2, cache_v2, dec_seq):
    n_seq, n_pages = page_table.shape
    pps = PAGES_PER_STEP
    n_k_steps = n_pages // pps
    n_blocks = n_pages * PAGE_SIZE // MOBA_BLOCK
    assert n_pages % pps == 0 and n_blocks <= LANES
    rows = N_HEADS * dec_seq
    kw = N_KV_HEADS * HEAD_DIM
    qw = N_HEADS * HEAD_DIM
    page_shape = (1, PAGE_SIZE * N_KV_HEADS, HEAD_DIM)

    def k_spec(r):
        return pl.BlockSpec(page_shape, lambda s, t, pt: (pt[s, jnp.minimum(t, n_k_steps - 1) * pps + r], 0, 0))

    def v_spec(r):
        return pl.BlockSpec(page_shape, lambda s, t, pt: (pt[s, jnp.maximum(t - n_k_steps, 0) * pps + r], 0, 0))

    grid_spec = pltpu.PrefetchScalarGridSpec(
        num_scalar_prefetch=1,
        grid=(n_seq, 2 * n_k_steps),
        in_specs=[
            pl.BlockSpec((dec_seq, qw), lambda s, t, pt: (s, 0)),
            pl.BlockSpec((dec_seq, kw), lambda s, t, pt: (s, 0)),
            pl.BlockSpec((dec_seq, kw), lambda s, t, pt: (s, 0)),
        ] + [k_spec(r) for r in range(pps)] + [v_spec(r) for r in range(pps)],
        out_specs=pl.BlockSpec((dec_seq, qw), lambda s, t, pt: (s, 0)),
        scratch_shapes=[
            pltpu.VMEM((rows, n_pages * PAGE_SIZE * N_KV_HEADS), F32),
            pltpu.VMEM((N_KV_HEADS, n_blocks, HEAD_DIM), F32),
            pltpu.VMEM((rows, LANES), F32),
            pltpu.VMEM((rows, HEAD_DIM), BF16),
            pltpu.VMEM((rows, 1), F32),
            pltpu.VMEM((rows, 1), F32),
            pltpu.VMEM((rows, HEAD_DIM), F32),
        ],
    )
    return pl.pallas_call(
        functools.partial(_sample_attn_kernel, dec_seq=dec_seq, n_pages=n_pages),
        grid_spec=grid_spec,
        out_shape=jax.ShapeDtypeStruct((n_seq * dec_seq, qw), F32),
        compiler_params=_cparams(("arbitrary", "arbitrary")),
        name="sample_attn",
    )(page_table, q_s, k_new, v_new, *([cache_k2] * pps), *([cache_v2] * pps))


def _oproj_kernel(h_ref, o_ref, w_ref, out_ref):
    out_ref[...] = h_ref[...] + jnp.dot(o_ref[...].astype(BF16), w_ref[...], preferred_element_type=F32)


def _oproj(h, o, w_o):
    n, d = h.shape
    tm = TOKEN_TILE
    return pl.pallas_call(
        _oproj_kernel,
        grid=(n // tm,),
        in_specs=[
            pl.BlockSpec((tm, d), lambda i: (i, 0)),
            pl.BlockSpec((tm, o.shape[1]), lambda i: (i, 0)),
            pl.BlockSpec(w_o.shape, lambda i: (0, 0)),
        ],
        out_specs=pl.BlockSpec((tm, d), lambda i: (i, 0)),
        out_shape=jax.ShapeDtypeStruct((n, d), F32),
        compiler_params=_cparams(("arbitrary",)),
        name="attn_oproj",
    )(h, o, w_o)


def _mix_tables(w_s, b_s, dec_seq):
    tm = TOKEN_TILE
    causal = jnp.tril(jnp.ones((GMLP_CHUNK, GMLP_CHUNK), dtype=bool))
    w = jnp.where(causal[None], w_s, jnp.zeros_like(w_s))
    eye_p = jnp.eye(tm // GMLP_CHUNK, dtype=w.dtype)
    mix_p = jnp.einsum("ab,gts->gatbs", eye_p, w).reshape(GMLP_GROUPS, tm, tm)
    eye_s = jnp.eye(tm // dec_seq, dtype=w.dtype)
    mix_s = jnp.einsum("ab,gts->gatbs", eye_s, w[:, :dec_seq, :dec_seq]).reshape(GMLP_GROUPS, tm, tm)
    mix = jnp.stack([mix_p, mix_s]).astype(BF16)
    bias_p = jnp.tile(b_s.T, (tm // GMLP_CHUNK, 1))
    bias_s = jnp.tile(b_s.T[:dec_seq], (tm // dec_seq, 1))
    bias = jnp.stack([bias_p, bias_s])
    bias = jnp.pad(bias, ((0, 0), (0, 0), (0, LANES - GMLP_GROUPS)))
    return mix, bias


def _rope_tables(pos):
    half = HEAD_DIM // 2
    inv = ROPE_THETA ** (-jnp.arange(half, dtype=F32) * 2.0 / HEAD_DIM)
    ang = pos.astype(F32)[:, None] * inv[None, :]
    cos = jnp.cos(ang)
    sin = jnp.sin(ang)
    return jnp.concatenate([cos, cos], axis=1), jnp.concatenate([-sin, sin], axis=1)


def _router_tables(w_grp, b_grp, w_rt, b_rt):
    w = jnp.concatenate([w_grp, w_rt], axis=1)
    b = jnp.concatenate([b_grp, b_rt], axis=0)
    pad = LANES - w.shape[1]
    return jnp.pad(w, ((0, 0), (0, pad))), jnp.pad(b, (0, pad)).reshape(1, LANES)


def kernel(x_prompt, x_sample, cache_k, cache_v, page_table, norm_mix, norm_ffn, a_w_in, a_ln_g, a_ln_b,
           a_w_s, a_b_s, a_w_out, kv_norm, w_kv, k_norm, b_w_q, b_q_norm, b_w_o, moe_w_grp, moe_b_grp,
           moe_w_rt, moe_b_rt, moe_w_gate, moe_w_up, moe_w_down):
    batch, seq, d = x_prompt.shape
    n_seq, dec_seq, _ = x_sample.shape
    n_prompt = batch * seq
    n_sample = n_seq * dec_seq
    assert n_prompt % TOKEN_TILE == 0 and n_sample == TOKEN_TILE and seq % MOBA_BLOCK == 0
    past_len = page_table.shape[1] * PAGE_SIZE
    assert past_len % MOBA_BLOCK == 0 and dec_seq <= MOBA_BLOCK

    x = jnp.concatenate([x_prompt.reshape(n_prompt, d), x_sample.reshape(n_sample, d)], axis=0)
    pos = jnp.concatenate([jnp.tile(jnp.arange(seq), batch), jnp.tile(past_len + jnp.arange(dec_seq), n_seq)])
    cos, sin = _rope_tables(pos)
    row = lambda a: a.reshape(1, -1)

    mix, bias = _mix_tables(a_w_s[0], a_b_s[0], dec_seq)
    h, vg_sample = _gmlp_layer(x, row(norm_mix[0]), a_w_in[0].astype(BF16), row(a_ln_g[0]), row(a_ln_b[0]),
                               mix, bias, a_w_out[0].astype(BF16), n_prompt // TOKEN_TILE)
    moe = []
    for layer in range(2):
        w_r, b_r = _router_tables(moe_w_grp[layer], moe_b_grp[layer], moe_w_rt[layer], moe_b_rt[layer])
        moe.append((row(norm_ffn[layer]), w_r, b_r, moe_w_gate[layer], moe_w_up[layer], moe_w_down[layer]))
    h = _hier_moe(h, *moe[0])

    k, v, q = _kvq_proj(h, row(kv_norm), row(norm_mix[1]), w_kv.astype(BF16), b_w_q[0].astype(BF16),
                        row(k_norm), row(b_q_norm[0]), cos, sin)

    o_prompt = _moba_prompt(q, k, v, batch, seq)
    n_phys = cache_k.shape[0]
    cache_k2 = cache_k.reshape(n_phys, PAGE_SIZE * N_KV_HEADS, HEAD_DIM)
    cache_v2 = cache_v.reshape(n_phys, PAGE_SIZE * N_KV_HEADS, HEAD_DIM)
    q_s, k_s, v_s = q[n_prompt:], k[n_prompt:], v[n_prompt:]
    o_sample = _sample_attn(page_table, q_s, k_s, v_s, cache_k2, cache_v2, dec_seq)
    h = _oproj(h, jnp.concatenate([o_prompt, o_sample], axis=0), b_w_o[0].astype(BF16))
    h = _hier_moe(h, *moe[1])

    n_pages_new = seq // PAGE_SIZE
    return (h[:n_prompt].reshape(batch, seq, d),
            h[n_prompt:].reshape(n_seq, dec_seq, d),
            k[:n_prompt].reshape(batch, n_pages_new, PAGE_SIZE, N_KV_HEADS, HEAD_DIM),
            v[:n_prompt].reshape(batch, n_pages_new, PAGE_SIZE, N_KV_HEADS, HEAD_DIM),
            k_s.reshape(n_seq, dec_seq, N_KV_HEADS, HEAD_DIM),
            v_s.reshape(n_seq, dec_seq, N_KV_HEADS, HEAD_DIM),
            vg_sample.reshape(1, n_seq, dec_seq, -1))
```

```python
import functools
import math

import jax
import jax.numpy as jnp
from jax import lax
from jax.experimental import pallas as pl
from jax.experimental.pallas import tpu as pltpu

F32 = jnp.float32
BF16 = jnp.bfloat16
HIGHEST = lax.Precision.HIGHEST

GMLP_CHUNK = 128
GMLP_GROUPS = 8
N_HEADS = 8
N_KV_HEADS = 4
HEAD_DIM = 128
Q_PER_KV = N_HEADS // N_KV_HEADS
MOBA_BLOCK = 256
MOBA_TOP_K = 3
ROPE_THETA = 10000.0
N_GROUPS = 4
EXPERTS_PER_GROUP = 8
N_EXPERTS = N_GROUPS * EXPERTS_PER_GROUP
TOP_K_EXPERTS = 2
PAGE_SIZE = 128
EPS = 1e-6

LANES = 128
ROW_TILE = 8
TOKEN_TILE = 256
FFN_ROWS = 256
PAGES_PER_STEP = 16
KEY_GROUP = 4
ONES_ROWS = 16
LOG2_E = math.log2(math.e)
VMEM_LIMIT = 56 * 1024 * 1024

NEG_INF = float("-inf")


def _cparams(sem):
    return pltpu.CompilerParams(dimension_semantics=sem, vmem_limit_bytes=VMEM_LIMIT)


def _rms(x):
    return x * lax.rsqrt(jnp.mean(x * x, axis=-1, keepdims=True) + EPS)


def _prompt_spec(block, n_prompt_tiles):
    return pl.BlockSpec(block, lambda i, *_: (jnp.minimum(i, n_prompt_tiles - 1), 0))


def _sample_spec(block):
    return pl.BlockSpec(block, lambda i, *_: (0, 0))


def _is_sample_tile():
    return pl.program_id(0) == pl.num_programs(0) - 1


def _stream_tile(prompt_ref, sample_ref):
    return jnp.where(_is_sample_tile(), sample_ref[...], prompt_ref[...])


def _store_stream_tile(prompt_ref, sample_ref, store):
    @pl.when(jnp.logical_not(_is_sample_tile()))
    def _():
        store(prompt_ref)

    @pl.when(_is_sample_tile())
    def _():
        store(sample_ref)


def _gmlp_kernel(xp_ref, xs_ref, g_ref, win_ref, lng_ref, lnb_ref, mix_ref, bias_ref, wout_ref,
                 h_ref, vg_ref, *, d_gate, n_groups):
    i = pl.program_id(0)
    x = _stream_tile(xp_ref, xs_ref)
    xb = (_rms(x) * g_ref[...]).astype(BF16)
    u = jax.nn.gelu(jnp.dot(xb, win_ref[:, :d_gate], preferred_element_type=F32))
    vp = jax.nn.gelu(jnp.dot(xb, win_ref[:, d_gate:], preferred_element_type=F32))
    vc = vp - jnp.mean(vp, axis=-1, keepdims=True)
    var = jnp.mean(vc * vc, axis=-1, keepdims=True)
    vg = vc * lax.rsqrt(var + EPS) * lng_ref[...] + lnb_ref[...]

    @pl.when(i == pl.num_programs(0) - 1)
    def _():
        vg_ref[...] = vg

    vgb = vg.astype(BF16)
    cw = d_gate // n_groups
    bias = bias_ref[0]
    parts = []
    for g in range(n_groups):
        mixed = jnp.dot(mix_ref[0, g], vgb[:, g * cw:(g + 1) * cw], preferred_element_type=F32)
        mixed = mixed + bias[:, g:g + 1]
        parts.append((u[:, g * cw:(g + 1) * cw] * mixed).astype(BF16))
    gated = jnp.concatenate(parts, axis=1)
    h_ref[...] = x + jnp.dot(gated, wout_ref[...], preferred_element_type=F32)


def _gmlp_layer(x_prompt, x_sample, g, w_in, ln_g, ln_b, mix, bias, w_out):
    d = x_prompt.shape[1]
    d_gate = w_out.shape[0]
    tm = TOKEN_TILE
    n_prompt_tiles = x_prompt.shape[0] // tm
    n_tiles = n_prompt_tiles + 1
    n = n_tiles * tm
    kind = lambda i: jnp.where(i < n_prompt_tiles, 0, 1)
    return pl.pallas_call(
        functools.partial(_gmlp_kernel, d_gate=d_gate, n_groups=GMLP_GROUPS),
        grid=(n_tiles,),
        in_specs=[
            _prompt_spec((tm, d), n_prompt_tiles),
            _sample_spec((tm, d)),
            pl.BlockSpec((1, d), lambda i: (0, 0)),
            pl.BlockSpec((d, 2 * d_gate), lambda i: (0, 0)),
            pl.BlockSpec((1, d_gate), lambda i: (0, 0)),
            pl.BlockSpec((1, d_gate), lambda i: (0, 0)),
            pl.BlockSpec((1, GMLP_GROUPS, tm, tm), lambda i: (kind(i), 0, 0, 0)),
            pl.BlockSpec((1, tm, LANES), lambda i: (kind(i), 0, 0)),
            pl.BlockSpec((d_gate, d), lambda i: (0, 0)),
        ],
        out_specs=[
            pl.BlockSpec((tm, d), lambda i: (i, 0)),
            pl.BlockSpec((tm, d_gate), lambda i: (0, 0)),
        ],
        out_shape=[
            jax.ShapeDtypeStruct((n, d), F32),
            jax.ShapeDtypeStruct((tm, d_gate), F32),
        ],
        compiler_params=_cparams(("arbitrary",)),
        name="gmlp_layer",
    )(x_prompt, x_sample, g, w_in, ln_g, ln_b, mix, bias, w_out)


def _router_kernel(h_ref, g_ref, wr_ref, br_ref, xn_ref, e_ref, w_ref, rank_ref, hist_ref):
    xn = _rms(h_ref[...]) * g_ref[...]
    _store_row_tiles(xn_ref, xn)
    logits = jnp.dot(xn, wr_ref[...], precision=HIGHEST, preferred_element_type=F32) + br_ref[...]
    lane = lax.broadcasted_iota(jnp.int32, logits.shape, 1)
    big = jnp.int32(LANES)
    is_grp = lane < N_GROUPS
    gl = jnp.where(is_grp, logits, NEG_INF)
    gmax = jnp.max(gl, axis=1, keepdims=True)
    gidx = jnp.min(jnp.where(is_grp & (logits == gmax), lane, big), axis=1, keepdims=True)
    p_g = 1.0 / jnp.sum(jnp.where(is_grp, jnp.exp(gl - gmax), 0.0), axis=1, keepdims=True)
    lo = N_GROUPS + gidx * EXPERTS_PER_GROUP
    in_grp = (lane >= lo) & (lane < lo + EXPERTS_PER_GROUP)
    v0 = jnp.max(jnp.where(in_grp, logits, NEG_INF), axis=1, keepdims=True)
    i0 = jnp.min(jnp.where(in_grp & (logits == v0), lane, big), axis=1, keepdims=True)
    rest = in_grp & (lane != i0)
    v1 = jnp.max(jnp.where(rest, logits, NEG_INF), axis=1, keepdims=True)
    i1 = jnp.min(jnp.where(rest & (logits == v1), lane, big), axis=1, keepdims=True)
    t = jnp.exp(v1 - v0)
    w0 = p_g * (1.0 / (1.0 + t))
    w1 = p_g * (t / (1.0 + t))
    e0 = i0 - N_GROUPS
    e1 = i1 - N_GROUPS
    e_ref[...] = jnp.where(lane == 0, e0, jnp.where(lane == 1, e1, 0))
    w_ref[...] = jnp.where(lane == 0, w0, jnp.where(lane == 1, w1, 0.0))
    tm = logits.shape[0]
    onehot = jnp.concatenate([(lane == e0).astype(F32), (lane == e1).astype(F32)], axis=0)
    a_row = lax.broadcasted_iota(jnp.int32, (2 * tm, 2 * tm), 0)
    a_col = lax.broadcasted_iota(jnp.int32, (2 * tm, 2 * tm), 1)
    earlier = (a_col < a_row).astype(BF16)
    before = jnp.dot(earlier, onehot.astype(BF16), preferred_element_type=F32)
    rank = jnp.sum(before * onehot, axis=1, keepdims=True).astype(jnp.int32)
    rank_ref[...] = jnp.where(lane == 0, rank[:tm], jnp.where(lane == 1, rank[tm:], 0))
    hist = jnp.sum(onehot, axis=0, keepdims=True).astype(jnp.int32)
    hist_ref[...] = jnp.broadcast_to(hist, hist_ref.shape)


def _router(h, g, w_r, b_r):
    n, d = h.shape
    tm = TOKEN_TILE
    return pl.pallas_call(
        _router_kernel,
        grid=(n // tm,),
        in_specs=[
            pl.BlockSpec((tm, d), lambda i: (i, 0)),
            pl.BlockSpec((1, d), lambda i: (0, 0)),
            pl.BlockSpec((d, LANES), lambda i: (0, 0)),
            pl.BlockSpec((1, LANES), lambda i: (0, 0)),
        ],
        out_specs=[
            pl.BlockSpec((tm * ROW_TILE, LANES), lambda i: (i, 0)),
            pl.BlockSpec((tm, LANES), lambda i: (i, 0)),
            pl.BlockSpec((tm, LANES), lambda i: (i, 0)),
            pl.BlockSpec((tm, LANES), lambda i: (i, 0)),
            pl.BlockSpec((ROW_TILE, LANES), lambda i: (i, 0)),
        ],
        out_shape=[
            jax.ShapeDtypeStruct((n * ROW_TILE, LANES), F32),
            jax.ShapeDtypeStruct((n, LANES), jnp.int32),
            jax.ShapeDtypeStruct((n, LANES), F32),
            jax.ShapeDtypeStruct((n, LANES), jnp.int32),
            jax.ShapeDtypeStruct((n // tm * ROW_TILE, LANES), jnp.int32),
        ],
        compiler_params=_cparams(("arbitrary",)),
        name="moe_router",
    )(h, g, w_r, b_r)


def _pos_kernel(e_ref, rank_ref, off_ref, pos_ref):
    e = e_ref[...]
    rank = rank_ref[...]
    off = off_ref[0:1, :]
    lane = lax.broadcasted_iota(jnp.int32, e.shape, 1)
    pos = []
    for k in range(TOP_K_EXPERTS):
        first = jnp.sum(jnp.where(lane == e[:, k:k + 1], off, 0), axis=1, keepdims=True)
        pos.append(first + rank[:, k:k + 1])
    pos_ref[...] = jnp.where(lane == 0, pos[0], jnp.where(lane == 1, pos[1], 0))


def _sorted_positions(e_pad, rank_pad, off):
    n = e_pad.shape[0]
    tm = TOKEN_TILE
    return pl.pallas_call(
        _pos_kernel,
        grid=(n // tm,),
        in_specs=[
            pl.BlockSpec((tm, LANES), lambda i: (i, 0)),
            pl.BlockSpec((tm, LANES), lambda i: (i, 0)),
            pl.BlockSpec((ROW_TILE, LANES), lambda i: (i, 0)),
        ],
        out_specs=pl.BlockSpec((tm, LANES), lambda i: (i, 0)),
        out_shape=jax.ShapeDtypeStruct((n, LANES), jnp.int32),
        compiler_params=_cparams(("arbitrary",)),
        name="moe_pos",
    )(e_pad, rank_pad, off)


def _dispatch_kernel(pos_ref, gap_ref, xn_ref, out_hbm, zero, sem, zsem):
    i = pl.program_id(0)
    tm = xn_ref.shape[0] // ROW_TILE
    base = i * tm * TOP_K_EXPERTS
    block_rows = zero.shape[0] // ROW_TILE
    n_blocks = out_hbm.shape[0] // zero.shape[0]

    @pl.when(i == 0)
    def _():
        zero[...] = jnp.zeros_like(zero)

        def each_gap(visit):
            def gap(e, carry):
                start = gap_ref[2 * e]

                def row(r, c):
                    visit(pltpu.make_async_copy(_row_tile(zero, 0), _row_tile(out_hbm, start + r), zsem))
                    return c

                lax.fori_loop(0, gap_ref[2 * e + 1], row, 0)
                return carry

            lax.fori_loop(0, N_EXPERTS, gap, 0)

            def unused_block(b, carry):
                first = pl.multiple_of(b * block_rows * ROW_TILE, ROW_TILE)
                visit(pltpu.make_async_copy(zero, out_hbm.at[pl.ds(first, block_rows * ROW_TILE), :], zsem))
                return carry

            lax.fori_loop(gap_ref[2 * N_EXPERTS], n_blocks, unused_block, 0)

        each_gap(lambda copy: copy.start())
        each_gap(lambda copy: copy.wait())

    def issue(t, carry):
        for k in range(TOP_K_EXPERTS):
            p = pos_ref[base + t * TOP_K_EXPERTS + k]
            pltpu.make_async_copy(_row_tile(xn_ref, t), _row_tile(out_hbm, p), sem).start(priority=k)
        return carry

    lax.fori_loop(0, tm, issue, 0, unroll=4)
    for k in range(TOP_K_EXPERTS):
        pltpu.make_async_copy(xn_ref, out_hbm.at[pl.ds(0, tm * ROW_TILE), :], sem).wait()


def _dispatch(xn, pos, gaps, n_rows):
    tm = TOKEN_TILE
    n = xn.shape[0] // ROW_TILE
    grid_spec = pltpu.PrefetchScalarGridSpec(
        num_scalar_prefetch=2,
        grid=(n // tm,),
        in_specs=[pl.BlockSpec((tm * ROW_TILE, LANES), lambda i, ps, gp: (i, 0))],
        out_specs=pl.BlockSpec(memory_space=pl.ANY),
        scratch_shapes=[pltpu.VMEM((FFN_ROWS * ROW_TILE, LANES), F32), pltpu.SemaphoreType.DMA(()),
                        pltpu.SemaphoreType.DMA(())],
    )
    return pl.pallas_call(
        _dispatch_kernel,
        grid_spec=grid_spec,
        out_shape=jax.ShapeDtypeStruct((n_rows * ROW_TILE, LANES), F32),
        compiler_params=_cparams(("arbitrary",)),
        name="moe_dispatch",
    )(pos, gaps, xn)


def _store_row_tiles(ref, x, first=0):
    for c in range(ROW_TILE):
        ref[pl.ds(first + c, x.shape[0], stride=ROW_TILE), :] = x[:, c * LANES:(c + 1) * LANES]


def _load_row_tiles(ref, first, rows):
    return jnp.concatenate([ref[pl.ds(first + c, rows, stride=ROW_TILE), :] for c in range(ROW_TILE)], axis=1)


def _row_tile(ref, r):
    return ref.at[pl.ds(pl.multiple_of(r * ROW_TILE, ROW_TILE), ROW_TILE), :]


def _ffn_kernel(blk_e_ref, nblk_ref, fresh_ref, x_ref, wg_ref, wu_ref, wd_ref, out_ref, wg_s, wu_s, wd_s):
    i = pl.program_id(0)
    rows = x_ref.shape[0] // ROW_TILE
    live = i < nblk_ref[0]

    @pl.when(live & (fresh_ref[i] == 1))
    def _():
        wg_s[...] = wg_ref[0].astype(BF16)
        wu_s[...] = wu_ref[0].astype(BF16)
        wd_s[...] = wd_ref[0].astype(BF16)

    @pl.when(live)
    def _():
        x = _load_row_tiles(x_ref, 0, rows).astype(BF16)
        gate = jnp.dot(x, wg_s[...], preferred_element_type=F32)
        up = jnp.dot(x, wu_s[...], preferred_element_type=F32)
        hid = (jax.nn.silu(gate) * up).astype(BF16)
        _store_row_tiles(out_ref, jnp.dot(hid, wd_s[...], preferred_element_type=F32))

    @pl.when(jnp.logical_not(live))
    def _():
        out_ref[...] = jnp.zeros_like(out_ref)


def _grouped_ffn(x_sorted, blk_e, nblk, fresh, w_gate, w_up, w_down):
    rb = FFN_ROWS
    n_rows = x_sorted.shape[0] // ROW_TILE
    _, d, d_e = w_gate.shape
    assert d == ROW_TILE * LANES
    grid_spec = pltpu.PrefetchScalarGridSpec(
        num_scalar_prefetch=3,
        grid=(n_rows // rb,),
        in_specs=[
            pl.BlockSpec((rb * ROW_TILE, LANES), lambda i, be, nb, fr: (jnp.minimum(i, nb[0] - 1), 0)),
            pl.BlockSpec((1, d, d_e), lambda i, be, nb, fr: (be[i], 0, 0)),
            pl.BlockSpec((1, d, d_e), lambda i, be, nb, fr: (be[i], 0, 0)),
            pl.BlockSpec((1, d_e, d), lambda i, be, nb, fr: (be[i], 0, 0)),
        ],
        out_specs=pl.BlockSpec((rb * ROW_TILE, LANES), lambda i, be, nb, fr: (i, 0)),
        scratch_shapes=[pltpu.VMEM((d, d_e), BF16), pltpu.VMEM((d, d_e), BF16), pltpu.VMEM((d_e, d), BF16)],
    )
    return pl.pallas_call(
        _ffn_kernel,
        grid_spec=grid_spec,
        out_shape=jax.ShapeDtypeStruct((n_rows * ROW_TILE, LANES), F32),
        compiler_params=_cparams(("arbitrary",)),
        name="moe_ffn",
    )(blk_e, nblk, fresh, x_sorted, w_gate, w_up, w_down)


def _combine_kernel(pos_ref, resid_ref, w_ref, src_hbm, *refs, split):
    out_refs, buf, sems = refs[:-2], refs[-2], refs[-1]
    i = pl.program_id(0)
    tm = resid_ref.shape[0]

    def gather(tile, slot):
        base = tile * tm * TOP_K_EXPERTS

        def issue(t, carry):
            for k in range(TOP_K_EXPERTS):
                p = pos_ref[base + t * TOP_K_EXPERTS + k]
                pltpu.make_async_copy(_row_tile(src_hbm, p), _row_tile(buf, (slot * TOP_K_EXPERTS + k) * tm + t),
                                      sems.at[slot]).start(priority=k)
            return carry

        lax.fori_loop(0, tm, issue, 0, unroll=4)

    @pl.when(i == 0)
    def _():
        gather(0, 0)

    @pl.when(i + 1 < pl.num_programs(0))
    def _():
        gather(i + 1, (i + 1) % 2)

    slot = i % 2
    firsts = [pl.multiple_of((slot * TOP_K_EXPERTS + k) * tm * ROW_TILE, ROW_TILE) for k in range(TOP_K_EXPERTS)]
    for first in firsts:
        pltpu.make_async_copy(src_hbm.at[pl.ds(0, tm * ROW_TILE), :],
                              buf.at[pl.ds(first, tm * ROW_TILE), :], sems.at[slot]).wait()
    acc = resid_ref[...]
    w = w_ref[...]
    for k, first in enumerate(firsts):
        acc = acc + w[:, k:k + 1] * _load_row_tiles(buf, first, tm)
    def put(ref):
        ref[...] = acc

    if split:
        _store_stream_tile(out_refs[0], out_refs[1], put)
    else:
        put(out_refs[0])


def _combine(resid, w_pad, src, pos, split):
    n, d = resid.shape
    assert d == ROW_TILE * LANES and TOP_K_EXPERTS == 2
    tm = TOKEN_TILE
    n_prompt_tiles = n // tm - 1
    if split:
        out_specs = [_prompt_spec((tm, d), n_prompt_tiles), _sample_spec((tm, d))]
        out_shape = [jax.ShapeDtypeStruct((n_prompt_tiles * tm, d), F32), jax.ShapeDtypeStruct((tm, d), F32)]
    else:
        out_specs = [pl.BlockSpec((tm, d), lambda i, ps: (i, 0))]
        out_shape = [jax.ShapeDtypeStruct((n, d), F32)]
    grid_spec = pltpu.PrefetchScalarGridSpec(
        num_scalar_prefetch=1,
        grid=(n // tm,),
        in_specs=[
            pl.BlockSpec((tm, d), lambda i, ps: (i, 0)),
            pl.BlockSpec((tm, LANES), lambda i, ps: (i, 0)),
            pl.BlockSpec(memory_space=pl.ANY),
        ],
        out_specs=out_specs,
        scratch_shapes=[pltpu.VMEM((2 * TOP_K_EXPERTS * tm * ROW_TILE, LANES), F32),
                        pltpu.SemaphoreType.DMA((2,))],
    )
    return pl.pallas_call(
        functools.partial(_combine_kernel, split=split),
        grid_spec=grid_spec,
        out_shape=out_shape,
        compiler_params=_cparams(("arbitrary",)),
        name="moe_combine",
    )(pos, resid, w_pad, src)


def _hier_moe(h, g, w_r, b_r, w_gate, w_up, w_down, split=False):
    n, _ = h.shape
    tm = TOKEN_TILE
    rb = FFN_ROWS
    xn, e_pad, w_pad, rank_pad, hist_pad = _router(h, g, w_r, b_r)
    hist = hist_pad[::ROW_TILE, :N_EXPERTS]
    counts = jnp.sum(hist, axis=0)
    padded = (counts + rb - 1) // rb * rb
    pend = jnp.cumsum(padded)
    tile_first = (pend - padded)[None, :] + jnp.cumsum(hist, axis=0) - hist
    off = jnp.repeat(jnp.pad(tile_first, ((0, 0), (0, LANES - N_EXPERTS))), ROW_TILE, axis=0).astype(jnp.int32)
    pos_pad = _sorted_positions(e_pad, rank_pad, off)
    pos = pos_pad[:, :TOP_K_EXPERTS].reshape(-1)
    n_blocks = (n * TOP_K_EXPERTS + N_EXPERTS * (rb - 1) + rb - 1) // rb
    block_first = jnp.arange(n_blocks, dtype=jnp.int32)[:, None] * rb
    blk_e = jnp.minimum(jnp.sum((pend[None, :] <= block_first).astype(jnp.int32), axis=1), N_EXPERTS - 1)
    fresh = jnp.concatenate([jnp.ones((1,), jnp.int32), (blk_e[1:] != blk_e[:-1]).astype(jnp.int32)])
    nblk = (pend[-1:] // rb).astype(jnp.int32)
    gaps = jnp.concatenate([jnp.stack([pend - padded + counts, padded - counts], axis=1).reshape(-1),
                            nblk]).astype(jnp.int32)
    x_sorted = _dispatch(xn, pos, gaps, n_blocks * rb)
    out_sorted = _grouped_ffn(x_sorted, blk_e, nblk, fresh, w_gate, w_up, w_down)
    out = _combine(h, w_pad, out_sorted, pos, split)
    return out if split else out[0]


def _store_heads_as_rows(ref, x):
    for j in range(N_KV_HEADS):
        ref[pl.ds(j, x.shape[0], stride=N_KV_HEADS), :] = x[:, j * HEAD_DIM:(j + 1) * HEAD_DIM]


def _proj_kernel(h_ref, gkv_ref, gq_ref, wkv_ref, wq_ref, kn_ref, qn_ref, cos_ref, sin_ref,
                 k_ref, v_ref, q_ref, kp_ref, ks_ref, vp_ref, vs_ref):
    hn = _rms(h_ref[...])
    cos = cos_ref[...]
    sin = sin_ref[...]

    def norm_rope(x, g):
        y = _rms(x) * g
        return y * cos + pltpu.roll(y, HEAD_DIM // 2, 1) * sin

    kv = jnp.dot((hn * gkv_ref[...]).astype(BF16), wkv_ref[...], preferred_element_type=F32)
    kw = N_KV_HEADS * HEAD_DIM
    k = jnp.concatenate(
        [norm_rope(kv[:, j * HEAD_DIM:(j + 1) * HEAD_DIM], kn_ref[...]) for j in range(N_KV_HEADS)], axis=1)
    v = kv[:, kw:]
    k_ref[...] = k
    v_ref[...] = v
    _store_stream_tile(kp_ref, ks_ref, lambda ref: _store_heads_as_rows(ref, k))
    _store_stream_tile(vp_ref, vs_ref, lambda ref: _store_heads_as_rows(ref, v))
    q = jnp.dot((hn * gq_ref[...]).astype(BF16), wq_ref[...], preferred_element_type=F32)
    q_ref[...] = jnp.concatenate(
        [norm_rope(q[:, j * HEAD_DIM:(j + 1) * HEAD_DIM], qn_ref[...]) for j in range(N_HEADS)], axis=1)


def _kvq_proj(h, g_kv, g_q, w_kv, w_q, k_norm, q_norm, cos, sin):
    n, d = h.shape
    tm = TOKEN_TILE
    kw = N_KV_HEADS * HEAD_DIM
    qw = N_HEADS * HEAD_DIM
    row = lambda i: (i, 0)
    fixed = lambda i: (0, 0)
    n_prompt_tiles = n // tm - 1
    head_rows = tm * N_KV_HEADS
    by_head = [_prompt_spec((head_rows, HEAD_DIM), n_prompt_tiles), _sample_spec((head_rows, HEAD_DIM))]
    by_head_shapes = [jax.ShapeDtypeStruct((n_prompt_tiles * head_rows, HEAD_DIM), F32),
                      jax.ShapeDtypeStruct((head_rows, HEAD_DIM), F32)]
    return pl.pallas_call(
        _proj_kernel,
        grid=(n // tm,),
        in_specs=[
            pl.BlockSpec((tm, d), row),
            pl.BlockSpec((1, d), fixed),
            pl.BlockSpec((1, d), fixed),
            pl.BlockSpec((d, 2 * kw), fixed),
            pl.BlockSpec((d, qw), fixed),
            pl.BlockSpec((1, HEAD_DIM), fixed),
            pl.BlockSpec((1, HEAD_DIM), fixed),
            pl.BlockSpec((tm, HEAD_DIM), row),
            pl.BlockSpec((tm, HEAD_DIM), row),
        ],
        out_specs=[
            pl.BlockSpec((tm, kw), row),
            pl.BlockSpec((tm, kw), row),
            pl.BlockSpec((tm, qw), row),
        ] + by_head + by_head,
        out_shape=[
            jax.ShapeDtypeStruct((n, kw), F32),
            jax.ShapeDtypeStruct((n, kw), F32),
            jax.ShapeDtypeStruct((n, qw), F32),
        ] + by_head_shapes + by_head_shapes,
        compiler_params=_cparams(("arbitrary",)),
        name="kvq_proj",
    )(h, g_kv, g_q, w_kv, w_q, k_norm, q_norm, cos, sin)


def _top_blocks(gate, axis):
    idx = lax.broadcasted_iota(jnp.int32, gate.shape, axis)
    big = jnp.int32(gate.shape[axis])
    sel = jnp.zeros(gate.shape, jnp.bool_)
    for _ in range(MOBA_TOP_K):
        top = jnp.max(gate, axis=axis, keepdims=True)
        first = jnp.min(jnp.where(gate == top, idx, big), axis=axis, keepdims=True)
        hit = idx == first
        sel = sel | (hit & (top > NEG_INF))
        gate = jnp.where(hit, NEG_INF, gate)
    return sel.astype(F32)


def _moba_prompt_kernel(q_ref, k_ref, v_ref, o_ref, kbf, vt, kmean, sel, s_a, s_b, m_scr, acc):
    j = pl.program_id(2)
    blk = MOBA_BLOCK
    grp = KEY_GROUP
    seq = k_ref.shape[0]
    n_blocks = seq // blk

    @pl.when(j == 0)
    def _():
        k = k_ref[...]
        kbf[...] = k.astype(BF16)
        kmean[...] = jnp.mean(k.reshape(n_blocks, blk, HEAD_DIM), axis=1)
        for n in range(n_blocks):
            vt[:HEAD_DIM, n * blk:(n + 1) * blk] = v_ref[n * blk:(n + 1) * blk, :].T.astype(BF16)
        r = lax.broadcasted_iota(jnp.int32, (ONES_ROWS, seq), 0)
        vt[HEAD_DIM:, :] = jnp.where(r == 0, 1.0, 0.0).astype(BF16)

    q2 = q_ref[...]
    qs = jnp.concatenate([q2[:, h * HEAD_DIM:(h + 1) * HEAD_DIM] for h in range(Q_PER_KV)], axis=0)
    nq = qs.shape[0]
    gate = lax.dot_general(kmean[...], qs, (((1,), (1,)), ((), ())),
                           precision=HIGHEST, preferred_element_type=F32)
    row = lax.broadcasted_iota(jnp.int32, gate.shape, 0)
    sel[...] = _top_blocks(jnp.where(row < j, gate, NEG_INF), 0)
    qt = (qs * (HEAD_DIM ** -0.5 * LOG2_E)).T.astype(BF16)

    m_scr[...] = jnp.full(m_scr.shape, NEG_INF, F32)
    acc[...] = jnp.zeros(acc.shape, F32)

    def update(tiles, start):
        m_old = m_scr[...]
        m_new = m_old
        for s in tiles:
            m_new = jnp.maximum(m_new, jnp.max(s, axis=0, keepdims=True))
        m_safe = jnp.where(m_new == NEG_INF, 0.0, m_new)
        p = jnp.concatenate([jnp.exp2(s - m_safe).astype(BF16) for s in tiles], axis=0)
        alpha = jnp.exp2(m_old - m_safe)
        pv = jnp.dot(vt[:, pl.ds(start, len(tiles) * blk)], p, preferred_element_type=F32)
        acc[...] = alpha * acc[...] + pv
        m_scr[...] = m_new

    def scores(start, n_keys):
        return jnp.dot(kbf[pl.ds(start, n_keys), :], qt, preferred_element_type=F32)

    def fill(buf, g):
        start = pl.multiple_of(g * (grp * blk), grp * blk)
        buf[...] = scores(start, grp * blk)

    def consume(buf, g):
        start = pl.multiple_of(g * (grp * blk), grp * blk)
        update([jnp.where(sel[pl.ds(g * grp + i, 1), :] > 0.0, buf[i * blk:(i + 1) * blk, :], NEG_INF)
                for i in range(grp)], start)

    n_groups = lax.div(j + (grp - 1), grp)
    n_pairs = lax.div(n_groups - 1, 2)

    @pl.when(n_groups > 0)
    def _():
        fill(s_a, 0)

        def pair(h, carry):
            g = 2 * h
            fill(s_b, g + 1)
            consume(s_a, g)
            fill(s_a, g + 2)
            consume(s_b, g + 1)
            return carry

        lax.fori_loop(0, n_pairs, pair, 0)
        g = 2 * n_pairs

        @pl.when(n_groups - g == 1)
        def _():
            consume(s_a, g)

        @pl.when(n_groups - g == 2)
        def _():
            fill(s_b, g + 1)
            consume(s_a, g)
            consume(s_b, g + 1)

    key = lax.broadcasted_iota(jnp.int32, (blk, nq), 0)
    qpos = lax.broadcasted_iota(jnp.int32, (blk, nq), 1) % blk
    own = pl.multiple_of(j * blk, blk)
    update([jnp.where(key <= qpos, scores(own, blk), NEG_INF)], own)
    a = acc[...]
    o = (a[:HEAD_DIM] / a[HEAD_DIM:HEAD_DIM + 1]).T
    o_ref[...] = jnp.concatenate([o[h * blk:(h + 1) * blk, :] for h in range(Q_PER_KV)], axis=1)


def _moba_prompt(q, k, v, batch, seq):
    blk = MOBA_BLOCK
    nqb = seq // blk
    qw = Q_PER_KV * HEAD_DIM
    nq = Q_PER_KV * blk
    return pl.pallas_call(
        _moba_prompt_kernel,
        grid=(batch, N_KV_HEADS, nqb),
        in_specs=[
            pl.BlockSpec((blk, qw), lambda b, c, j: (b * nqb + j, c)),
            pl.BlockSpec((seq, HEAD_DIM), lambda b, c, j: (b, c)),
            pl.BlockSpec((seq, HEAD_DIM), lambda b, c, j: (b, c)),
        ],
        out_specs=pl.BlockSpec((blk, qw), lambda b, c, j: (b * nqb + j, c)),
        out_shape=jax.ShapeDtypeStruct((batch * seq, N_HEADS * HEAD_DIM), F32),
        scratch_shapes=[
            pltpu.VMEM((seq, HEAD_DIM), BF16),
            pltpu.VMEM((HEAD_DIM + ONES_ROWS, seq), BF16),
            pltpu.VMEM((nqb, HEAD_DIM), F32),
            pltpu.VMEM((nqb, nq), F32),
            pltpu.VMEM((KEY_GROUP * blk, nq), F32),
            pltpu.VMEM((KEY_GROUP * blk, nq), F32),
            pltpu.VMEM((1, nq), F32),
            pltpu.VMEM((HEAD_DIM + ONES_ROWS, nq), F32),
        ],
        compiler_params=_cparams(("arbitrary", "arbitrary", "arbitrary")),
        name="moba_prompt",
    )(q, k, v)


def _stack_heads(q8):
    return jnp.concatenate([q8[:, h * HEAD_DIM:(h + 1) * HEAD_DIM] for h in range(N_HEADS)], axis=0)


def _sample_attn_kernel(pt_ref, q_ref, kn_ref, vn_ref, *refs, dec_seq, n_pages):
    pps = PAGES_PER_STEP
    k_pages = refs[:pps]
    v_pages = refs[pps:2 * pps]
    o_ref, s_scr, means, sel_scr, qs_scr, m_scr, l_scr, acc = refs[2 * pps:]
    t = pl.program_id(1)
    n_k_steps = n_pages // pps
    rows = N_HEADS * dec_seq
    rkv = Q_PER_KV * dec_seq
    ppb = MOBA_BLOCK // PAGE_SIZE
    bps = pps // ppb
    n_blocks = n_pages // ppb
    nt_dims = (((1,), (1,)), ((), ()))

    def head_rows(page_ref, c):
        return page_ref[0, pl.ds(c, PAGE_SIZE, stride=N_KV_HEADS), :]

    @pl.when(t == 0)
    def _():
        qs_scr[...] = (_stack_heads(q_ref[...]) * (HEAD_DIM ** -0.5 * LOG2_E)).astype(BF16)
        m_scr[...] = jnp.full(m_scr.shape, NEG_INF, F32)
        l_scr[...] = jnp.zeros(l_scr.shape, F32)
        acc[...] = jnp.zeros(acc.shape, F32)

    @pl.when(t < n_k_steps)
    def _():
        qs = qs_scr[...]
        col = pl.multiple_of(t * (pps * PAGE_SIZE), pps * PAGE_SIZE)
        for c in range(N_KV_HEADS):
            kc = jnp.concatenate([head_rows(k_pages[p], c) for p in range(pps)], axis=0)
            s_scr[c * rkv:(c + 1) * rkv, pl.ds(col, pps * PAGE_SIZE)] = lax.dot_general(
                qs[c * rkv:(c + 1) * rkv], kc.astype(BF16), nt_dims, preferred_element_type=F32)
            means[c, pl.ds(t * bps, bps), :] = jnp.sum(kc.reshape(bps, MOBA_BLOCK, HEAD_DIM), axis=1) / MOBA_BLOCK

    @pl.when(t == n_k_steps - 1)
    def _():
        qf = _stack_heads(q_ref[...])
        gate = jnp.concatenate(
            [lax.dot_general(qf[c * rkv:(c + 1) * rkv], means[c], nt_dims, precision=HIGHEST,
                             preferred_element_type=F32) for c in range(N_KV_HEADS)], axis=0)
        chosen = _top_blocks(gate, 1)
        sel_scr[...] = jnp.concatenate([chosen, jnp.zeros((rows, LANES - n_blocks), F32)], axis=1)

    def softmax_step(tiles):
        m_old = m_scr[...]
        m_new = m_old
        for s in tiles:
            m_new = jnp.maximum(m_new, jnp.max(s, axis=1, keepdims=True))
        m_safe = jnp.where(m_new == NEG_INF, 0.0, m_new)
        alpha = jnp.exp2(m_old - m_safe)
        l_new = alpha * l_scr[...]
        probs = []
        for s in tiles:
            p = jnp.exp2(s - m_safe)
            l_new = l_new + jnp.sum(p, axis=1, keepdims=True)
            probs.append(p.astype(BF16))
        l_scr[...] = l_new
        m_scr[...] = m_new
        return probs, alpha

    @pl.when(t >= n_k_steps)
    def _():
        tv = t - n_k_steps
        sel = sel_scr[...]
        lane = lax.broadcasted_iota(jnp.int32, sel.shape, 1)
        tiles = []
        for b in range(bps):
            n = tv * bps + b
            chosen = jnp.sum(jnp.where(lane == n, sel, 0.0), axis=1, keepdims=True) > 0.0
            col = pl.multiple_of(n * MOBA_BLOCK, MOBA_BLOCK)
            tiles.append(jnp.where(chosen, s_scr[:, pl.ds(col, MOBA_BLOCK)], NEG_INF))
        probs, alpha = softmax_step(tiles)
        prob = jnp.concatenate(probs, axis=1)
        pv = []
        for c in range(N_KV_HEADS):
            vc = jnp.concatenate([head_rows(v_pages[p], c) for p in range(pps)], axis=0).astype(BF16)
            pv.append(jnp.dot(prob[c * rkv:(c + 1) * rkv], vc, preferred_element_type=F32))
        acc[...] = alpha * acc[...] + jnp.concatenate(pv, axis=0)

    @pl.when(t == pl.num_programs(1) - 1)
    def _():
        qs = qs_scr[...]
        kn = kn_ref[...].astype(BF16)
        vn = vn_ref[...].astype(BF16)
        s = jnp.concatenate(
            [lax.dot_general(qs[c * rkv:(c + 1) * rkv], kn[:, c * HEAD_DIM:(c + 1) * HEAD_DIM], nt_dims,
                             preferred_element_type=F32) for c in range(N_KV_HEADS)], axis=0)
        r2 = lax.broadcasted_iota(jnp.int32, s.shape, 0)
        c2 = lax.broadcasted_iota(jnp.int32, s.shape, 1)
        probs, alpha = softmax_step([jnp.where(c2 <= r2 % dec_seq, s, NEG_INF)])
        pv = [jnp.dot(probs[0][c * rkv:(c + 1) * rkv], vn[:, c * HEAD_DIM:(c + 1) * HEAD_DIM],
                      preferred_element_type=F32) for c in range(N_KV_HEADS)]
        o = (alpha * acc[...] + jnp.concatenate(pv, axis=0)) / l_scr[...]
        o_ref[...] = jnp.concatenate([o[h * dec_seq:(h + 1) * dec_seq, :] for h in range(N_HEADS)], axis=1)


def _sample_attn(page_table, q, k, v, row0, cache_k2, cache_v2, dec_seq):
    n_seq, n_pages = page_table.shape
    pps = PAGES_PER_STEP
    n_k_steps = n_pages // pps
    n_blocks = n_pages * PAGE_SIZE // MOBA_BLOCK
    assert n_pages % pps == 0 and n_blocks <= LANES
    rows = N_HEADS * dec_seq
    kw = N_KV_HEADS * HEAD_DIM
    qw = N_HEADS * HEAD_DIM
    page_shape = (1, PAGE_SIZE * N_KV_HEADS, HEAD_DIM)

    def k_spec(r):
        return pl.BlockSpec(page_shape, lambda s, t, pt: (pt[s, jnp.minimum(t, n_k_steps - 1) * pps + r], 0, 0))

    def v_spec(r):
        return pl.BlockSpec(page_shape, lambda s, t, pt: (pt[s, jnp.maximum(t - n_k_steps, 0) * pps + r], 0, 0))

    grid_spec = pltpu.PrefetchScalarGridSpec(
        num_scalar_prefetch=1,
        grid=(n_seq, 2 * n_k_steps),
        in_specs=[
            pl.BlockSpec((dec_seq, qw), lambda s, t, pt: (row0 + s, 0)),
            pl.BlockSpec((dec_seq, kw), lambda s, t, pt: (row0 + s, 0)),
            pl.BlockSpec((dec_seq, kw), lambda s, t, pt: (row0 + s, 0)),
        ] + [k_spec(r) for r in range(pps)] + [v_spec(r) for r in range(pps)],
        out_specs=pl.BlockSpec((dec_seq, qw), lambda s, t, pt: (s, 0)),
        scratch_shapes=[
            pltpu.VMEM((rows, n_pages * PAGE_SIZE), F32),
            pltpu.VMEM((N_KV_HEADS, n_blocks, HEAD_DIM), F32),
            pltpu.VMEM((rows, LANES), F32),
            pltpu.VMEM((rows, HEAD_DIM), BF16),
            pltpu.VMEM((rows, 1), F32),
            pltpu.VMEM((rows, 1), F32),
            pltpu.VMEM((rows, HEAD_DIM), F32),
        ],
    )
    return pl.pallas_call(
        functools.partial(_sample_attn_kernel, dec_seq=dec_seq, n_pages=n_pages),
        grid_spec=grid_spec,
        out_shape=jax.ShapeDtypeStruct((n_seq * dec_seq, qw), F32),
        compiler_params=_cparams(("arbitrary", "arbitrary")),
        name="sample_attn",
    )(page_table, q, k, v, *([cache_k2] * pps), *([cache_v2] * pps))


def _oproj_kernel(h_ref, op_ref, os_ref, w_ref, out_ref):
    o = _stream_tile(op_ref, os_ref).astype(BF16)
    out_ref[...] = h_ref[...] + jnp.dot(o, w_ref[...], preferred_element_type=F32)


def _oproj(h, o_prompt, o_sample, w_o):
    n, d = h.shape
    tm = TOKEN_TILE
    ow = o_prompt.shape[1]
    return pl.pallas_call(
        _oproj_kernel,
        grid=(n // tm,),
        in_specs=[
            pl.BlockSpec((tm, d), lambda i: (i, 0)),
            _prompt_spec((tm, ow), n // tm - 1),
            _sample_spec((tm, ow)),
            pl.BlockSpec(w_o.shape, lambda i: (0, 0)),
        ],
        out_specs=pl.BlockSpec((tm, d), lambda i: (i, 0)),
        out_shape=jax.ShapeDtypeStruct((n, d), F32),
        compiler_params=_cparams(("arbitrary",)),
        name="attn_oproj",
    )(h, o_prompt, o_sample, w_o)


def _mix_tables(w_s, b_s, dec_seq):
    tm = TOKEN_TILE
    causal = jnp.tril(jnp.ones((GMLP_CHUNK, GMLP_CHUNK), dtype=bool))
    w = jnp.where(causal[None], w_s, jnp.zeros_like(w_s))
    eye_p = jnp.eye(tm // GMLP_CHUNK, dtype=w.dtype)
    mix_p = jnp.einsum("ab,gts->gatbs", eye_p, w).reshape(GMLP_GROUPS, tm, tm)
    eye_s = jnp.eye(tm // dec_seq, dtype=w.dtype)
    mix_s = jnp.einsum("ab,gts->gatbs", eye_s, w[:, :dec_seq, :dec_seq]).reshape(GMLP_GROUPS, tm, tm)
    mix = jnp.stack([mix_p, mix_s]).astype(BF16)
    bias_p = jnp.tile(b_s.T, (tm // GMLP_CHUNK, 1))
    bias_s = jnp.tile(b_s.T[:dec_seq], (tm // dec_seq, 1))
    bias = jnp.stack([bias_p, bias_s])
    bias = jnp.pad(bias, ((0, 0), (0, 0), (0, LANES - GMLP_GROUPS)))
    return mix, bias


def _rope_tables(pos):
    half = HEAD_DIM // 2
    inv = ROPE_THETA ** (-jnp.arange(half, dtype=F32) * 2.0 / HEAD_DIM)
    ang = pos.astype(F32)[:, None] * inv[None, :]
    cos = jnp.cos(ang)
    sin = jnp.sin(ang)
    return jnp.concatenate([cos, cos], axis=1), jnp.concatenate([-sin, sin], axis=1)


def _router_tables(w_grp, b_grp, w_rt, b_rt):
    w = jnp.concatenate([w_grp, w_rt], axis=1)
    b = jnp.concatenate([b_grp, b_rt], axis=0)
    pad = LANES - w.shape[1]
    return jnp.pad(w, ((0, 0), (0, pad))), jnp.pad(b, (0, pad)).reshape(1, LANES)


def kernel(x_prompt, x_sample, cache_k, cache_v, page_table, norm_mix, norm_ffn, a_w_in, a_ln_g, a_ln_b,
           a_w_s, a_b_s, a_w_out, kv_norm, w_kv, k_norm, b_w_q, b_q_norm, b_w_o, moe_w_grp, moe_b_grp,
           moe_w_rt, moe_b_rt, moe_w_gate, moe_w_up, moe_w_down):
    batch, seq, d = x_prompt.shape
    n_seq, dec_seq, _ = x_sample.shape
    n_prompt = batch * seq
    n_sample = n_seq * dec_seq
    assert n_prompt % TOKEN_TILE == 0 and n_sample == TOKEN_TILE and seq % MOBA_BLOCK == 0
    past_len = page_table.shape[1] * PAGE_SIZE
    assert past_len % MOBA_BLOCK == 0 and dec_seq <= MOBA_BLOCK

    pos = jnp.concatenate([jnp.tile(jnp.arange(seq), batch), jnp.tile(past_len + jnp.arange(dec_seq), n_seq)])
    cos, sin = _rope_tables(pos)
    row = lambda a: a.reshape(1, -1)

    mix, bias = _mix_tables(a_w_s[0], a_b_s[0], dec_seq)
    h, vg_sample = _gmlp_layer(x_prompt.reshape(n_prompt, d), x_sample.reshape(n_sample, d), row(norm_mix[0]),
                               a_w_in[0].astype(BF16), row(a_ln_g[0]), row(a_ln_b[0]), mix, bias,
                               a_w_out[0].astype(BF16))
    moe = []
    for layer in range(2):
        w_r, b_r = _router_tables(moe_w_grp[layer], moe_b_grp[layer], moe_w_rt[layer], moe_b_rt[layer])
        moe.append((row(norm_ffn[layer]), w_r, b_r, moe_w_gate[layer], moe_w_up[layer], moe_w_down[layer]))
    h = _hier_moe(h, *moe[0])

    k, v, q, k_p, k_s, v_p, v_s = _kvq_proj(h, row(kv_norm), row(norm_mix[1]), w_kv.astype(BF16),
                                            b_w_q[0].astype(BF16), row(k_norm), row(b_q_norm[0]), cos, sin)

    o_prompt = _moba_prompt(q, k, v, batch, seq)
    n_phys = cache_k.shape[0]
    cache_k2 = cache_k.reshape(n_phys, PAGE_SIZE * N_KV_HEADS, HEAD_DIM)
    cache_v2 = cache_v.reshape(n_phys, PAGE_SIZE * N_KV_HEADS, HEAD_DIM)
    o_sample = _sample_attn(page_table, q, k, v, n_prompt // dec_seq, cache_k2, cache_v2, dec_seq)
    h = _oproj(h, o_prompt, o_sample, b_w_o[0].astype(BF16))
    y_prompt, y_sample = _hier_moe(h, *moe[1], split=True)

    n_pages_new = seq // PAGE_SIZE
    return (y_prompt.reshape(batch, seq, d),
            y_sample.reshape(n_seq, dec_seq, d),
            k_p.reshape(batch, n_pages_new, PAGE_SIZE, N_KV_HEADS, HEAD_DIM),
            v_p.reshape(batch, n_pages_new, PAGE_SIZE, N_KV_HEADS, HEAD_DIM),
            k_s.reshape(n_seq, dec_seq, N_KV_HEADS, HEAD_DIM),
            v_s.reshape(n_seq, dec_seq, N_KV_HEADS, HEAD_DIM),
            vg_sample.reshape(1, n_seq, dec_seq, -1))
```

```python
import functools
import math

import jax
import jax.numpy as jnp
from jax import lax
from jax.experimental import pallas as pl
from jax.experimental.pallas import tpu as pltpu

F32 = jnp.float32
BF16 = jnp.bfloat16
HIGHEST = lax.Precision.HIGHEST

GMLP_CHUNK = 128
GMLP_GROUPS = 8
N_HEADS = 8
N_KV_HEADS = 4
HEAD_DIM = 128
Q_PER_KV = N_HEADS // N_KV_HEADS
MOBA_BLOCK = 256
MOBA_TOP_K = 3
ROPE_THETA = 10000.0
N_GROUPS = 4
EXPERTS_PER_GROUP = 8
N_EXPERTS = N_GROUPS * EXPERTS_PER_GROUP
TOP_K_EXPERTS = 2
PAGE_SIZE = 128
EPS = 1e-6

LANES = 128
ROW_TILE = 8
TOKEN_TILE = 256
FFN_ROWS = 256
PAGES_PER_STEP = 16
KEY_GROUP = 4
ONES_ROWS = 16
LOG2_E = math.log2(math.e)
VMEM_LIMIT = 56 * 1024 * 1024

NEG_INF = float("-inf")


def _cparams(sem):
    return pltpu.CompilerParams(dimension_semantics=sem, vmem_limit_bytes=VMEM_LIMIT)


def _rms(x):
    return x * lax.rsqrt(jnp.mean(x * x, axis=-1, keepdims=True) + EPS)


def _prompt_spec(block, n_prompt_tiles):
    return pl.BlockSpec(block, lambda i, *_: (jnp.minimum(i, n_prompt_tiles - 1), 0))


def _sample_spec(block):
    return pl.BlockSpec(block, lambda i, *_: (0, 0))


def _is_sample_tile():
    return pl.program_id(0) == pl.num_programs(0) - 1


def _stream_tile(prompt_ref, sample_ref):
    return jnp.where(_is_sample_tile(), sample_ref[...], prompt_ref[...])


def _store_stream_tile(prompt_ref, sample_ref, store):
    @pl.when(jnp.logical_not(_is_sample_tile()))
    def _():
        store(prompt_ref)

    @pl.when(_is_sample_tile())
    def _():
        store(sample_ref)


def _gmlp_kernel(xp_ref, xs_ref, g_ref, win_ref, lng_ref, lnb_ref, mix_ref, bias_ref, wout_ref,
                 h_ref, vg_ref, *, d_gate, n_groups):
    i = pl.program_id(0)
    x = _stream_tile(xp_ref, xs_ref)
    xb = (_rms(x) * g_ref[...]).astype(BF16)
    u = jax.nn.gelu(jnp.dot(xb, win_ref[:, :d_gate], preferred_element_type=F32))
    vp = jax.nn.gelu(jnp.dot(xb, win_ref[:, d_gate:], preferred_element_type=F32))
    vc = vp - jnp.mean(vp, axis=-1, keepdims=True)
    var = jnp.mean(vc * vc, axis=-1, keepdims=True)
    vg = vc * lax.rsqrt(var + EPS) * lng_ref[...] + lnb_ref[...]

    @pl.when(i == pl.num_programs(0) - 1)
    def _():
        vg_ref[...] = vg

    vgb = vg.astype(BF16)
    cw = d_gate // n_groups
    bias = bias_ref[0]
    parts = []
    for g in range(n_groups):
        mixed = jnp.dot(mix_ref[0, g], vgb[:, g * cw:(g + 1) * cw], preferred_element_type=F32)
        mixed = mixed + bias[:, g:g + 1]
        parts.append((u[:, g * cw:(g + 1) * cw] * mixed).astype(BF16))
    gated = jnp.concatenate(parts, axis=1)
    h_ref[...] = x + jnp.dot(gated, wout_ref[...], preferred_element_type=F32)


def _gmlp_layer(x_prompt, x_sample, g, w_in, ln_g, ln_b, mix, bias, w_out):
    d = x_prompt.shape[1]
    d_gate = w_out.shape[0]
    tm = TOKEN_TILE
    n_prompt_tiles = x_prompt.shape[0] // tm
    n_tiles = n_prompt_tiles + 1
    n = n_tiles * tm
    kind = lambda i: jnp.where(i < n_prompt_tiles, 0, 1)
    return pl.pallas_call(
        functools.partial(_gmlp_kernel, d_gate=d_gate, n_groups=GMLP_GROUPS),
        grid=(n_tiles,),
        in_specs=[
            _prompt_spec((tm, d), n_prompt_tiles),
            _sample_spec((tm, d)),
            pl.BlockSpec((1, d), lambda i: (0, 0)),
            pl.BlockSpec((d, 2 * d_gate), lambda i: (0, 0)),
            pl.BlockSpec((1, d_gate), lambda i: (0, 0)),
            pl.BlockSpec((1, d_gate), lambda i: (0, 0)),
            pl.BlockSpec((1, GMLP_GROUPS, tm, tm), lambda i: (kind(i), 0, 0, 0)),
            pl.BlockSpec((1, tm, LANES), lambda i: (kind(i), 0, 0)),
            pl.BlockSpec((d_gate, d), lambda i: (0, 0)),
        ],
        out_specs=[
            pl.BlockSpec((tm, d), lambda i: (i, 0)),
            pl.BlockSpec((tm, d_gate), lambda i: (0, 0)),
        ],
        out_shape=[
            jax.ShapeDtypeStruct((n, d), F32),
            jax.ShapeDtypeStruct((tm, d_gate), F32),
        ],
        compiler_params=_cparams(("arbitrary",)),
        name="gmlp_layer",
    )(x_prompt, x_sample, g, w_in, ln_g, ln_b, mix, bias, w_out)


def _router_kernel(h_ref, g_ref, wr_ref, br_ref, xn_ref, w_ref, code_ref, hist_ref):
    xn = _rms(h_ref[...]) * g_ref[...]
    _store_row_tiles(xn_ref, xn)
    logits = jnp.dot(xn, wr_ref[...], precision=HIGHEST, preferred_element_type=F32) + br_ref[...]
    lane = lax.broadcasted_iota(jnp.int32, logits.shape, 1)
    big = jnp.int32(LANES)
    is_grp = lane < N_GROUPS
    gl = jnp.where(is_grp, logits, NEG_INF)
    gmax = jnp.max(gl, axis=1, keepdims=True)
    gidx = jnp.min(jnp.where(is_grp & (logits == gmax), lane, big), axis=1, keepdims=True)
    p_g = 1.0 / jnp.sum(jnp.where(is_grp, jnp.exp(gl - gmax), 0.0), axis=1, keepdims=True)
    lo = N_GROUPS + gidx * EXPERTS_PER_GROUP
    in_grp = (lane >= lo) & (lane < lo + EXPERTS_PER_GROUP)
    v0 = jnp.max(jnp.where(in_grp, logits, NEG_INF), axis=1, keepdims=True)
    i0 = jnp.min(jnp.where(in_grp & (logits == v0), lane, big), axis=1, keepdims=True)
    rest = in_grp & (lane != i0)
    v1 = jnp.max(jnp.where(rest, logits, NEG_INF), axis=1, keepdims=True)
    i1 = jnp.min(jnp.where(rest & (logits == v1), lane, big), axis=1, keepdims=True)
    t = jnp.exp(v1 - v0)
    w0 = p_g * (1.0 / (1.0 + t))
    w1 = p_g * (t / (1.0 + t))
    e0 = i0 - N_GROUPS
    e1 = i1 - N_GROUPS
    w_ref[...] = jnp.where(lane == 0, w0, jnp.where(lane == 1, w1, 0.0))
    tm = logits.shape[0]
    onehot = jnp.concatenate([(lane == e0).astype(F32), (lane == e1).astype(F32)], axis=0)
    a_row = lax.broadcasted_iota(jnp.int32, (2 * tm, 2 * tm), 0)
    a_col = lax.broadcasted_iota(jnp.int32, (2 * tm, 2 * tm), 1)
    earlier = (a_col < a_row).astype(BF16)
    before = jnp.dot(earlier, onehot.astype(BF16), preferred_element_type=F32)
    rank = jnp.sum(before * onehot, axis=1, keepdims=True).astype(jnp.int32)
    code_ref[...] = jnp.where(lane == 0, rank[:tm] * N_EXPERTS + e0,
                              jnp.where(lane == 1, rank[tm:] * N_EXPERTS + e1, 0))
    hist = jnp.sum(onehot, axis=0, keepdims=True).astype(jnp.int32)
    hist_ref[...] = jnp.broadcast_to(hist, hist_ref.shape)


def _router(h, g, w_r, b_r):
    n, d = h.shape
    tm = TOKEN_TILE
    return pl.pallas_call(
        _router_kernel,
        grid=(n // tm,),
        in_specs=[
            pl.BlockSpec((tm, d), lambda i: (i, 0)),
            pl.BlockSpec((1, d), lambda i: (0, 0)),
            pl.BlockSpec((d, LANES), lambda i: (0, 0)),
            pl.BlockSpec((1, LANES), lambda i: (0, 0)),
        ],
        out_specs=[
            pl.BlockSpec((tm * ROW_TILE, LANES), lambda i: (i, 0)),
            pl.BlockSpec((tm, LANES), lambda i: (i, 0)),
            pl.BlockSpec((tm, LANES), lambda i: (i, 0)),
            pl.BlockSpec((ROW_TILE, LANES), lambda i: (i, 0)),
        ],
        out_shape=[
            jax.ShapeDtypeStruct((n * ROW_TILE, LANES), F32),
            jax.ShapeDtypeStruct((n, LANES), F32),
            jax.ShapeDtypeStruct((n, LANES), jnp.int32),
            jax.ShapeDtypeStruct((n // tm * ROW_TILE, LANES), jnp.int32),
        ],
        compiler_params=_cparams(("arbitrary",)),
        name="moe_router",
    )(h, g, w_r, b_r)


def _sorted_row(code_ref, first_ref, tile, a):
    code = code_ref[a]
    expert = lax.bitwise_and(code, N_EXPERTS - 1)
    return first_ref[tile * N_EXPERTS + expert] + lax.shift_right_logical(code, N_EXPERTS.bit_length() - 1)


def _dispatch_kernel(code_ref, first_ref, gap_ref, xn_ref, out_hbm, zero, sem, zsem):
    i = pl.program_id(0)
    tm = xn_ref.shape[0] // ROW_TILE
    base = i * tm * TOP_K_EXPERTS
    block_rows = zero.shape[0] // ROW_TILE
    n_blocks = out_hbm.shape[0] // zero.shape[0]

    @pl.when(i == 0)
    def _():
        zero[...] = jnp.zeros_like(zero)

        def each_gap(visit):
            def gap(e, carry):
                start = gap_ref[2 * e]

                def row(r, c):
                    visit(pltpu.make_async_copy(_row_tile(zero, 0), _row_tile(out_hbm, start + r), zsem))
                    return c

                lax.fori_loop(0, gap_ref[2 * e + 1], row, 0)
                return carry

            lax.fori_loop(0, N_EXPERTS, gap, 0)

            def unused_block(b, carry):
                first = pl.multiple_of(b * block_rows * ROW_TILE, ROW_TILE)
                visit(pltpu.make_async_copy(zero, out_hbm.at[pl.ds(first, block_rows * ROW_TILE), :], zsem))
                return carry

            lax.fori_loop(gap_ref[2 * N_EXPERTS], n_blocks, unused_block, 0)

        each_gap(lambda copy: copy.start())
        each_gap(lambda copy: copy.wait())

    def issue(t, carry):
        for k in range(TOP_K_EXPERTS):
            p = _sorted_row(code_ref, first_ref, i, base + t * TOP_K_EXPERTS + k)
            pltpu.make_async_copy(_row_tile(xn_ref, t), _row_tile(out_hbm, p), sem).start(priority=k)
        return carry

    lax.fori_loop(0, tm, issue, 0, unroll=4)
    for k in range(TOP_K_EXPERTS):
        pltpu.make_async_copy(xn_ref, out_hbm.at[pl.ds(0, tm * ROW_TILE), :], sem).wait()


def _dispatch(xn, code, first, gaps, n_rows):
    tm = TOKEN_TILE
    n = xn.shape[0] // ROW_TILE
    grid_spec = pltpu.PrefetchScalarGridSpec(
        num_scalar_prefetch=3,
        grid=(n // tm,),
        in_specs=[pl.BlockSpec((tm * ROW_TILE, LANES), lambda i, cd, fs, gp: (i, 0))],
        out_specs=pl.BlockSpec(memory_space=pl.ANY),
        scratch_shapes=[pltpu.VMEM((FFN_ROWS * ROW_TILE, LANES), F32), pltpu.SemaphoreType.DMA(()),
                        pltpu.SemaphoreType.DMA(())],
    )
    return pl.pallas_call(
        _dispatch_kernel,
        grid_spec=grid_spec,
        out_shape=jax.ShapeDtypeStruct((n_rows * ROW_TILE, LANES), F32),
        compiler_params=_cparams(("arbitrary",)),
        name="moe_dispatch",
    )(code, first, gaps, xn)


def _store_row_tiles(ref, x, first=0):
    for c in range(ROW_TILE):
        ref[pl.ds(first + c, x.shape[0], stride=ROW_TILE), :] = x[:, c * LANES:(c + 1) * LANES]


def _load_row_tiles(ref, first, rows):
    return jnp.concatenate([ref[pl.ds(first + c, rows, stride=ROW_TILE), :] for c in range(ROW_TILE)], axis=1)


def _row_tile(ref, r):
    return ref.at[pl.ds(pl.multiple_of(r * ROW_TILE, ROW_TILE), ROW_TILE), :]


def _ffn_kernel(blk_e_ref, nblk_ref, fresh_ref, x_ref, wg_ref, wu_ref, wd_ref, out_ref, wg_s, wu_s, wd_s):
    i = pl.program_id(0)
    rows = x_ref.shape[0] // ROW_TILE
    live = i < nblk_ref[0]

    @pl.when(live & (fresh_ref[i] == 1))
    def _():
        wg_s[...] = wg_ref[0, 0].astype(BF16)
        wu_s[...] = wu_ref[0, 0].astype(BF16)
        wd_s[...] = wd_ref[0, 0].astype(BF16)

    @pl.when(live)
    def _():
        x = _load_row_tiles(x_ref, 0, rows).astype(BF16)
        gate = jnp.dot(x, wg_s[...], preferred_element_type=F32)
        up = jnp.dot(x, wu_s[...], preferred_element_type=F32)
        hid = (jax.nn.silu(gate) * up).astype(BF16)
        _store_row_tiles(out_ref, jnp.dot(hid, wd_s[...], preferred_element_type=F32))

    @pl.when(jnp.logical_not(live))
    def _():
        out_ref[...] = jnp.zeros_like(out_ref)


def _grouped_ffn(x_sorted, blk_e, nblk, fresh, w_gate, w_up, w_down, layer):
    rb = FFN_ROWS
    n_rows = x_sorted.shape[0] // ROW_TILE
    _, _, d, d_e = w_gate.shape
    assert d == ROW_TILE * LANES
    grid_spec = pltpu.PrefetchScalarGridSpec(
        num_scalar_prefetch=3,
        grid=(n_rows // rb,),
        in_specs=[
            pl.BlockSpec((rb * ROW_TILE, LANES), lambda i, be, nb, fr: (jnp.minimum(i, nb[0] - 1), 0)),
            pl.BlockSpec((1, 1, d, d_e), lambda i, be, nb, fr: (layer, be[i], 0, 0)),
            pl.BlockSpec((1, 1, d, d_e), lambda i, be, nb, fr: (layer, be[i], 0, 0)),
            pl.BlockSpec((1, 1, d_e, d), lambda i, be, nb, fr: (layer, be[i], 0, 0)),
        ],
        out_specs=pl.BlockSpec((rb * ROW_TILE, LANES), lambda i, be, nb, fr: (i, 0)),
        scratch_shapes=[pltpu.VMEM((d, d_e), BF16), pltpu.VMEM((d, d_e), BF16), pltpu.VMEM((d_e, d), BF16)],
    )
    return pl.pallas_call(
        _ffn_kernel,
        grid_spec=grid_spec,
        out_shape=jax.ShapeDtypeStruct((n_rows * ROW_TILE, LANES), F32),
        compiler_params=_cparams(("arbitrary",)),
        name="moe_ffn",
    )(blk_e, nblk, fresh, x_sorted, w_gate, w_up, w_down)


def _combine_kernel(code_ref, first_ref, resid_ref, w_ref, src_hbm, *refs, split):
    out_refs, buf, sems = refs[:-2], refs[-2], refs[-1]
    i = pl.program_id(0)
    tm = resid_ref.shape[0]

    def gather(tile, slot):
        base = tile * tm * TOP_K_EXPERTS

        def issue(t, carry):
            for k in range(TOP_K_EXPERTS):
                p = _sorted_row(code_ref, first_ref, tile, base + t * TOP_K_EXPERTS + k)
                pltpu.make_async_copy(_row_tile(src_hbm, p), _row_tile(buf, (slot * TOP_K_EXPERTS + k) * tm + t),
                                      sems.at[slot]).start(priority=k)
            return carry

        lax.fori_loop(0, tm, issue, 0, unroll=4)

    @pl.when(i == 0)
    def _():
        gather(0, 0)

    @pl.when(i + 1 < pl.num_programs(0))
    def _():
        gather(i + 1, (i + 1) % 2)

    slot = i % 2
    firsts = [pl.multiple_of((slot * TOP_K_EXPERTS + k) * tm * ROW_TILE, ROW_TILE) for k in range(TOP_K_EXPERTS)]
    for first in firsts:
        pltpu.make_async_copy(src_hbm.at[pl.ds(0, tm * ROW_TILE), :],
                              buf.at[pl.ds(first, tm * ROW_TILE), :], sems.at[slot]).wait()
    acc = resid_ref[...]
    w = w_ref[...]
    for k, first in enumerate(firsts):
        acc = acc + w[:, k:k + 1] * _load_row_tiles(buf, first, tm)
    def put(ref):
        ref[...] = acc

    if split:
        _store_stream_tile(out_refs[0], out_refs[1], put)
    else:
        put(out_refs[0])


def _combine(resid, w_pad, src, code, first, split):
    n, d = resid.shape
    assert d == ROW_TILE * LANES and TOP_K_EXPERTS == 2
    tm = TOKEN_TILE
    n_prompt_tiles = n // tm - 1
    if split:
        out_specs = [_prompt_spec((tm, d), n_prompt_tiles), _sample_spec((tm, d))]
        out_shape = [jax.ShapeDtypeStruct((n_prompt_tiles * tm, d), F32), jax.ShapeDtypeStruct((tm, d), F32)]
    else:
        out_specs = [pl.BlockSpec((tm, d), lambda i, *_: (i, 0))]
        out_shape = [jax.ShapeDtypeStruct((n, d), F32)]
    grid_spec = pltpu.PrefetchScalarGridSpec(
        num_scalar_prefetch=2,
        grid=(n // tm,),
        in_specs=[
            pl.BlockSpec((tm, d), lambda i, *_: (i, 0)),
            pl.BlockSpec((tm, LANES), lambda i, *_: (i, 0)),
            pl.BlockSpec(memory_space=pl.ANY),
        ],
        out_specs=out_specs,
        scratch_shapes=[pltpu.VMEM((2 * TOP_K_EXPERTS * tm * ROW_TILE, LANES), F32),
                        pltpu.SemaphoreType.DMA((2,))],
    )
    return pl.pallas_call(
        functools.partial(_combine_kernel, split=split),
        grid_spec=grid_spec,
        out_shape=out_shape,
        compiler_params=_cparams(("arbitrary",)),
        name="moe_combine",
    )(code, first, resid, w_pad, src)


def _hier_moe(h, g, w_r, b_r, w_gate, w_up, w_down, layer, split=False):
    assert N_EXPERTS & (N_EXPERTS - 1) == 0
    n, _ = h.shape
    rb = FFN_ROWS
    xn, w_pad, code_pad, hist_pad = _router(h, g, w_r, b_r)
    code = code_pad[:, :TOP_K_EXPERTS].reshape(-1)
    hist = hist_pad[::ROW_TILE, :N_EXPERTS]
    counts = jnp.sum(hist, axis=0)
    padded = (counts + rb - 1) // rb * rb
    pend = jnp.cumsum(padded)
    first = ((pend - padded)[None, :] + jnp.cumsum(hist, axis=0) - hist).reshape(-1).astype(jnp.int32)
    n_blocks = (n * TOP_K_EXPERTS + N_EXPERTS * (rb - 1) + rb - 1) // rb
    block_first = jnp.arange(n_blocks, dtype=jnp.int32)[:, None] * rb
    blk_e = jnp.minimum(jnp.sum((pend[None, :] <= block_first).astype(jnp.int32), axis=1), N_EXPERTS - 1)
    fresh = jnp.concatenate([jnp.ones((1,), jnp.int32), (blk_e[1:] != blk_e[:-1]).astype(jnp.int32)])
    nblk = (pend[-1:] // rb).astype(jnp.int32)
    gaps = jnp.concatenate([jnp.stack([pend - padded + counts, padded - counts], axis=1).reshape(-1),
                            nblk]).astype(jnp.int32)
    x_sorted = _dispatch(xn, code, first, gaps, n_blocks * rb)
    out_sorted = _grouped_ffn(x_sorted, blk_e, nblk, fresh, w_gate, w_up, w_down, layer)
    out = _combine(h, w_pad, out_sorted, code, first, split)
    return out if split else out[0]


def _store_heads_as_rows(ref, x):
    for j in range(N_KV_HEADS):
        ref[pl.ds(j, x.shape[0], stride=N_KV_HEADS), :] = x[:, j * HEAD_DIM:(j + 1) * HEAD_DIM]


def _proj_kernel(h_ref, gkv_ref, gq_ref, wkv_ref, wq_ref, kn_ref, qn_ref, cos_ref, sin_ref,
                 k_ref, v_ref, q_ref, kp_ref, ks_ref, vp_ref, vs_ref):
    hn = _rms(h_ref[...])
    cos = cos_ref[...]
    sin = sin_ref[...]

    def norm_rope(x, g):
        y = _rms(x) * g
        return y * cos + pltpu.roll(y, HEAD_DIM // 2, 1) * sin

    kv = jnp.dot((hn * gkv_ref[...]).astype(BF16), wkv_ref[...], preferred_element_type=F32)
    kw = N_KV_HEADS * HEAD_DIM
    k = jnp.concatenate(
        [norm_rope(kv[:, j * HEAD_DIM:(j + 1) * HEAD_DIM], kn_ref[...]) for j in range(N_KV_HEADS)], axis=1)
    v = kv[:, kw:]
    k_ref[...] = k
    v_ref[...] = v
    _store_stream_tile(kp_ref, ks_ref, lambda ref: _store_heads_as_rows(ref, k))
    _store_stream_tile(vp_ref, vs_ref, lambda ref: _store_heads_as_rows(ref, v))
    q = jnp.dot((hn * gq_ref[...]).astype(BF16), wq_ref[...], preferred_element_type=F32)
    q_ref[...] = jnp.concatenate(
        [norm_rope(q[:, j * HEAD_DIM:(j + 1) * HEAD_DIM], qn_ref[...]) for j in range(N_HEADS)], axis=1)


def _kvq_proj(h, g_kv, g_q, w_kv, w_q, k_norm, q_norm, cos, sin):
    n, d = h.shape
    tm = TOKEN_TILE
    kw = N_KV_HEADS * HEAD_DIM
    qw = N_HEADS * HEAD_DIM
    row = lambda i: (i, 0)
    fixed = lambda i: (0, 0)
    n_prompt_tiles = n // tm - 1
    head_rows = tm * N_KV_HEADS
    by_head = [_prompt_spec((head_rows, HEAD_DIM), n_prompt_tiles), _sample_spec((head_rows, HEAD_DIM))]
    by_head_shapes = [jax.ShapeDtypeStruct((n_prompt_tiles * head_rows, HEAD_DIM), F32),
                      jax.ShapeDtypeStruct((head_rows, HEAD_DIM), F32)]
    return pl.pallas_call(
        _proj_kernel,
        grid=(n // tm,),
        in_specs=[
            pl.BlockSpec((tm, d), row),
            pl.BlockSpec((1, d), fixed),
            pl.BlockSpec((1, d), fixed),
            pl.BlockSpec((d, 2 * kw), fixed),
            pl.BlockSpec((d, qw), fixed),
            pl.BlockSpec((1, HEAD_DIM), fixed),
            pl.BlockSpec((1, HEAD_DIM), fixed),
            pl.BlockSpec((tm, HEAD_DIM), row),
            pl.BlockSpec((tm, HEAD_DIM), row),
        ],
        out_specs=[
            pl.BlockSpec((tm, kw), row),
            pl.BlockSpec((tm, kw), row),
            pl.BlockSpec((tm, qw), row),
        ] + by_head + by_head,
        out_shape=[
            jax.ShapeDtypeStruct((n, kw), F32),
            jax.ShapeDtypeStruct((n, kw), F32),
            jax.ShapeDtypeStruct((n, qw), F32),
        ] + by_head_shapes + by_head_shapes,
        compiler_params=_cparams(("arbitrary",)),
        name="kvq_proj",
    )(h, g_kv, g_q, w_kv, w_q, k_norm, q_norm, cos, sin)


def _top_blocks(gate, axis):
    idx = lax.broadcasted_iota(jnp.int32, gate.shape, axis)
    big = jnp.int32(gate.shape[axis])
    sel = jnp.zeros(gate.shape, jnp.bool_)
    for _ in range(MOBA_TOP_K):
        top = jnp.max(gate, axis=axis, keepdims=True)
        first = jnp.min(jnp.where(gate == top, idx, big), axis=axis, keepdims=True)
        hit = idx == first
        sel = sel | (hit & (top > NEG_INF))
        gate = jnp.where(hit, NEG_INF, gate)
    return sel.astype(F32)


def _moba_prompt_kernel(q_ref, k_ref, v_ref, o_ref, kbf, vt, kmean, sel, s_a, s_b, m_scr, acc):
    j = pl.program_id(2)
    blk = MOBA_BLOCK
    grp = KEY_GROUP
    seq = k_ref.shape[0]
    n_blocks = seq // blk

    @pl.when(j == 0)
    def _():
        k = k_ref[...]
        kbf[...] = k.astype(BF16)
        kmean[...] = jnp.mean(k.reshape(n_blocks, blk, HEAD_DIM), axis=1)
        for n in range(n_blocks):
            vt[:HEAD_DIM, n * blk:(n + 1) * blk] = v_ref[n * blk:(n + 1) * blk, :].T.astype(BF16)
        r = lax.broadcasted_iota(jnp.int32, (ONES_ROWS, seq), 0)
        vt[HEAD_DIM:, :] = jnp.where(r == 0, 1.0, 0.0).astype(BF16)

    q2 = q_ref[...]
    qs = jnp.concatenate([q2[:, h * HEAD_DIM:(h + 1) * HEAD_DIM] for h in range(Q_PER_KV)], axis=0)
    nq = qs.shape[0]
    gate = lax.dot_general(kmean[...], qs, (((1,), (1,)), ((), ())),
                           precision=HIGHEST, preferred_element_type=F32)
    row = lax.broadcasted_iota(jnp.int32, gate.shape, 0)
    sel[...] = _top_blocks(jnp.where(row < j, gate, NEG_INF), 0)
    qt = (qs * (HEAD_DIM ** -0.5 * LOG2_E)).T.astype(BF16)

    m_scr[...] = jnp.full(m_scr.shape, NEG_INF, F32)
    acc[...] = jnp.zeros(acc.shape, F32)

    def update(tiles, start):
        m_old = m_scr[...]
        m_new = m_old
        for s in tiles:
            m_new = jnp.maximum(m_new, jnp.max(s, axis=0, keepdims=True))
        m_safe = jnp.where(m_new == NEG_INF, 0.0, m_new)
        p = jnp.concatenate([jnp.exp2(s - m_safe).astype(BF16) for s in tiles], axis=0)
        alpha = jnp.exp2(m_old - m_safe)
        pv = jnp.dot(vt[:, pl.ds(start, len(tiles) * blk)], p, preferred_element_type=F32)
        acc[...] = alpha * acc[...] + pv
        m_scr[...] = m_new

    def scores(start, n_keys):
        return jnp.dot(kbf[pl.ds(start, n_keys), :], qt, preferred_element_type=F32)

    def fill(buf, g):
        start = pl.multiple_of(g * (grp * blk), grp * blk)
        buf[...] = scores(start, grp * blk)

    def consume(buf, g):
        start = pl.multiple_of(g * (grp * blk), grp * blk)
        update([jnp.where(sel[pl.ds(g * grp + i, 1), :] > 0.0, buf[i * blk:(i + 1) * blk, :], NEG_INF)
                for i in range(grp)], start)

    n_groups = lax.div(j + (grp - 1), grp)
    n_pairs = lax.div(n_groups - 1, 2)

    @pl.when(n_groups > 0)
    def _():
        fill(s_a, 0)

        def pair(h, carry):
            g = 2 * h
            fill(s_b, g + 1)
            consume(s_a, g)
            fill(s_a, g + 2)
            consume(s_b, g + 1)
            return carry

        lax.fori_loop(0, n_pairs, pair, 0)
        g = 2 * n_pairs

        @pl.when(n_groups - g == 1)
        def _():
            consume(s_a, g)

        @pl.when(n_groups - g == 2)
        def _():
            fill(s_b, g + 1)
            consume(s_a, g)
            consume(s_b, g + 1)

    key = lax.broadcasted_iota(jnp.int32, (blk, nq), 0)
    qpos = lax.broadcasted_iota(jnp.int32, (blk, nq), 1) % blk
    own = pl.multiple_of(j * blk, blk)
    update([jnp.where(key <= qpos, scores(own, blk), NEG_INF)], own)
    a = acc[...]
    o = (a[:HEAD_DIM] / a[HEAD_DIM:HEAD_DIM + 1]).T
    o_ref[...] = jnp.concatenate([o[h * blk:(h + 1) * blk, :] for h in range(Q_PER_KV)], axis=1)


def _moba_prompt(q, k, v, batch, seq):
    blk = MOBA_BLOCK
    nqb = seq // blk
    qw = Q_PER_KV * HEAD_DIM
    nq = Q_PER_KV * blk
    return pl.pallas_call(
        _moba_prompt_kernel,
        grid=(batch, N_KV_HEADS, nqb),
        in_specs=[
            pl.BlockSpec((blk, qw), lambda b, c, j: (b * nqb + j, c)),
            pl.BlockSpec((seq, HEAD_DIM), lambda b, c, j: (b, c)),
            pl.BlockSpec((seq, HEAD_DIM), lambda b, c, j: (b, c)),
        ],
        out_specs=pl.BlockSpec((blk, qw), lambda b, c, j: (b * nqb + j, c)),
        out_shape=jax.ShapeDtypeStruct((batch * seq, N_HEADS * HEAD_DIM), F32),
        scratch_shapes=[
            pltpu.VMEM((seq, HEAD_DIM), BF16),
            pltpu.VMEM((HEAD_DIM + ONES_ROWS, seq), BF16),
            pltpu.VMEM((nqb, HEAD_DIM), F32),
            pltpu.VMEM((nqb, nq), F32),
            pltpu.VMEM((KEY_GROUP * blk, nq), F32),
            pltpu.VMEM((KEY_GROUP * blk, nq), F32),
            pltpu.VMEM((1, nq), F32),
            pltpu.VMEM((HEAD_DIM + ONES_ROWS, nq), F32),
        ],
        compiler_params=_cparams(("arbitrary", "arbitrary", "arbitrary")),
        name="moba_prompt",
    )(q, k, v)


def _stack_heads(q8):
    return jnp.concatenate([q8[:, h * HEAD_DIM:(h + 1) * HEAD_DIM] for h in range(N_HEADS)], axis=0)


def _sample_attn_kernel(pt_ref, q_ref, kn_ref, vn_ref, *refs, dec_seq, n_pages):
    pps = PAGES_PER_STEP
    k_pages = refs[:pps]
    v_pages = refs[pps:2 * pps]
    o_ref, s_scr, means, sel_scr, qs_scr, m_scr, l_scr, acc = refs[2 * pps:]
    t = pl.program_id(1)
    n_k_steps = n_pages // pps
    rows = N_HEADS * dec_seq
    rkv = Q_PER_KV * dec_seq
    ppb = MOBA_BLOCK // PAGE_SIZE
    bps = pps // ppb
    n_blocks = n_pages // ppb
    nt_dims = (((1,), (1,)), ((), ()))

    def head_rows(page_ref, c):
        return page_ref[0, pl.ds(c, PAGE_SIZE, stride=N_KV_HEADS), :]

    @pl.when(t == 0)
    def _():
        qs_scr[...] = (_stack_heads(q_ref[...]) * (HEAD_DIM ** -0.5 * LOG2_E)).astype(BF16)
        m_scr[...] = jnp.full(m_scr.shape, NEG_INF, F32)
        l_scr[...] = jnp.zeros(l_scr.shape, F32)
        acc[...] = jnp.zeros(acc.shape, F32)

    @pl.when(t < n_k_steps)
    def _():
        qs = qs_scr[...]
        col = pl.multiple_of(t * (pps * PAGE_SIZE), pps * PAGE_SIZE)
        for c in range(N_KV_HEADS):
            kc = jnp.concatenate([head_rows(k_pages[p], c) for p in range(pps)], axis=0)
            s_scr[c * rkv:(c + 1) * rkv, pl.ds(col, pps * PAGE_SIZE)] = lax.dot_general(
                qs[c * rkv:(c + 1) * rkv], kc.astype(BF16), nt_dims, preferred_element_type=F32)
            means[c, pl.ds(t * bps, bps), :] = jnp.sum(kc.reshape(bps, MOBA_BLOCK, HEAD_DIM), axis=1) / MOBA_BLOCK

    @pl.when(t == n_k_steps - 1)
    def _():
        qf = _stack_heads(q_ref[...])
        gate = jnp.concatenate(
            [lax.dot_general(qf[c * rkv:(c + 1) * rkv], means[c], nt_dims, precision=HIGHEST,
                             preferred_element_type=F32) for c in range(N_KV_HEADS)], axis=0)
        chosen = _top_blocks(gate, 1)
        sel_scr[...] = jnp.concatenate([chosen, jnp.zeros((rows, LANES - n_blocks), F32)], axis=1)

    def softmax_step(tiles):
        m_old = m_scr[...]
        m_new = m_old
        for s in tiles:
            m_new = jnp.maximum(m_new, jnp.max(s, axis=1, keepdims=True))
        m_safe = jnp.where(m_new == NEG_INF, 0.0, m_new)
        alpha = jnp.exp2(m_old - m_safe)
        l_new = alpha * l_scr[...]
        probs = []
        for s in tiles:
            p = jnp.exp2(s - m_safe)
            l_new = l_new + jnp.sum(p, axis=1, keepdims=True)
            probs.append(p.astype(BF16))
        l_scr[...] = l_new
        m_scr[...] = m_new
        return probs, alpha

    @pl.when(t >= n_k_steps)
    def _():
        tv = t - n_k_steps
        sel = sel_scr[...]
        lane = lax.broadcasted_iota(jnp.int32, sel.shape, 1)
        tiles = []
        for b in range(bps):
            n = tv * bps + b
            chosen = jnp.sum(jnp.where(lane == n, sel, 0.0), axis=1, keepdims=True) > 0.0
            col = pl.multiple_of(n * MOBA_BLOCK, MOBA_BLOCK)
            tiles.append(jnp.where(chosen, s_scr[:, pl.ds(col, MOBA_BLOCK)], NEG_INF))
        probs, alpha = softmax_step(tiles)
        prob = jnp.concatenate(probs, axis=1)
        pv = []
        for c in range(N_KV_HEADS):
            vc = jnp.concatenate([head_rows(v_pages[p], c) for p in range(pps)], axis=0).astype(BF16)
            pv.append(jnp.dot(prob[c * rkv:(c + 1) * rkv], vc, preferred_element_type=F32))
        acc[...] = alpha * acc[...] + jnp.concatenate(pv, axis=0)

    @pl.when(t == pl.num_programs(1) - 1)
    def _():
        qs = qs_scr[...]
        kn = kn_ref[...].astype(BF16)
        vn = vn_ref[...].astype(BF16)
        s = jnp.concatenate(
            [lax.dot_general(qs[c * rkv:(c + 1) * rkv], kn[:, c * HEAD_DIM:(c + 1) * HEAD_DIM], nt_dims,
                             preferred_element_type=F32) for c in range(N_KV_HEADS)], axis=0)
        r2 = lax.broadcasted_iota(jnp.int32, s.shape, 0)
        c2 = lax.broadcasted_iota(jnp.int32, s.shape, 1)
        probs, alpha = softmax_step([jnp.where(c2 <= r2 % dec_seq, s, NEG_INF)])
        pv = [jnp.dot(probs[0][c * rkv:(c + 1) * rkv], vn[:, c * HEAD_DIM:(c + 1) * HEAD_DIM],
                      preferred_element_type=F32) for c in range(N_KV_HEADS)]
        o = (alpha * acc[...] + jnp.concatenate(pv, axis=0)) / l_scr[...]
        o_ref[...] = jnp.concatenate([o[h * dec_seq:(h + 1) * dec_seq, :] for h in range(N_HEADS)], axis=1)


def _sample_attn(page_table, q, k, v, row0, cache_k2, cache_v2, dec_seq):
    n_seq, n_pages = page_table.shape
    pps = PAGES_PER_STEP
    n_k_steps = n_pages // pps
    n_blocks = n_pages * PAGE_SIZE // MOBA_BLOCK
    assert n_pages % pps == 0 and n_blocks <= LANES
    rows = N_HEADS * dec_seq
    kw = N_KV_HEADS * HEAD_DIM
    qw = N_HEADS * HEAD_DIM
    page_shape = (1, PAGE_SIZE * N_KV_HEADS, HEAD_DIM)

    def k_spec(r):
        return pl.BlockSpec(page_shape, lambda s, t, pt: (pt[s, jnp.minimum(t, n_k_steps - 1) * pps + r], 0, 0))

    def v_spec(r):
        return pl.BlockSpec(page_shape, lambda s, t, pt: (pt[s, jnp.maximum(t - n_k_steps, 0) * pps + r], 0, 0))

    grid_spec = pltpu.PrefetchScalarGridSpec(
        num_scalar_prefetch=1,
        grid=(n_seq, 2 * n_k_steps),
        in_specs=[
            pl.BlockSpec((dec_seq, qw), lambda s, t, pt: (row0 + s, 0)),
            pl.BlockSpec((dec_seq, kw), lambda s, t, pt: (row0 + s, 0)),
            pl.BlockSpec((dec_seq, kw), lambda s, t, pt: (row0 + s, 0)),
        ] + [k_spec(r) for r in range(pps)] + [v_spec(r) for r in range(pps)],
        out_specs=pl.BlockSpec((dec_seq, qw), lambda s, t, pt: (s, 0)),
        scratch_shapes=[
            pltpu.VMEM((rows, n_pages * PAGE_SIZE), F32),
            pltpu.VMEM((N_KV_HEADS, n_blocks, HEAD_DIM), F32),
            pltpu.VMEM((rows, LANES), F32),
            pltpu.VMEM((rows, HEAD_DIM), BF16),
            pltpu.VMEM((rows, 1), F32),
            pltpu.VMEM((rows, 1), F32),
            pltpu.VMEM((rows, HEAD_DIM), F32),
        ],
    )
    return pl.pallas_call(
        functools.partial(_sample_attn_kernel, dec_seq=dec_seq, n_pages=n_pages),
        grid_spec=grid_spec,
        out_shape=jax.ShapeDtypeStruct((n_seq * dec_seq, qw), F32),
        compiler_params=_cparams(("arbitrary", "arbitrary")),
        name="sample_attn",
    )(page_table, q, k, v, *([cache_k2] * pps), *([cache_v2] * pps))


def _oproj_kernel(h_ref, op_ref, os_ref, w_ref, out_ref):
    o = _stream_tile(op_ref, os_ref).astype(BF16)
    out_ref[...] = h_ref[...] + jnp.dot(o, w_ref[...], preferred_element_type=F32)


def _oproj(h, o_prompt, o_sample, w_o):
    n, d = h.shape
    tm = TOKEN_TILE
    ow = o_prompt.shape[1]
    return pl.pallas_call(
        _oproj_kernel,
        grid=(n // tm,),
        in_specs=[
            pl.BlockSpec((tm, d), lambda i: (i, 0)),
            _prompt_spec((tm, ow), n // tm - 1),
            _sample_spec((tm, ow)),
            pl.BlockSpec(w_o.shape, lambda i: (0, 0)),
        ],
        out_specs=pl.BlockSpec((tm, d), lambda i: (i, 0)),
        out_shape=jax.ShapeDtypeStruct((n, d), F32),
        compiler_params=_cparams(("arbitrary",)),
        name="attn_oproj",
    )(h, o_prompt, o_sample, w_o)


def _mix_tables(w_s, b_s, dec_seq):
    tm = TOKEN_TILE
    causal = jnp.tril(jnp.ones((GMLP_CHUNK, GMLP_CHUNK), dtype=bool))
    w = jnp.where(causal[None], w_s, jnp.zeros_like(w_s))
    eye_p = jnp.eye(tm // GMLP_CHUNK, dtype=w.dtype)
    mix_p = jnp.einsum("ab,gts->gatbs", eye_p, w).reshape(GMLP_GROUPS, tm, tm)
    eye_s = jnp.eye(tm // dec_seq, dtype=w.dtype)
    mix_s = jnp.einsum("ab,gts->gatbs", eye_s, w[:, :dec_seq, :dec_seq]).reshape(GMLP_GROUPS, tm, tm)
    mix = jnp.stack([mix_p, mix_s]).astype(BF16)
    bias_p = jnp.tile(b_s.T, (tm // GMLP_CHUNK, 1))
    bias_s = jnp.tile(b_s.T[:dec_seq], (tm // dec_seq, 1))
    bias = jnp.stack([bias_p, bias_s])
    bias = jnp.pad(bias, ((0, 0), (0, 0), (0, LANES - GMLP_GROUPS)))
    return mix, bias


def _rope_tables(pos):
    half = HEAD_DIM // 2
    inv = ROPE_THETA ** (-jnp.arange(half, dtype=F32) * 2.0 / HEAD_DIM)
    ang = pos.astype(F32)[:, None] * inv[None, :]
    cos = jnp.cos(ang)
    sin = jnp.sin(ang)
    return jnp.concatenate([cos, cos], axis=1), jnp.concatenate([-sin, sin], axis=1)


def _router_tables(w_grp, b_grp, w_rt, b_rt):
    w = jnp.concatenate([w_grp, w_rt], axis=1)
    b = jnp.concatenate([b_grp, b_rt], axis=0)
    pad = LANES - w.shape[1]
    return jnp.pad(w, ((0, 0), (0, pad))), jnp.pad(b, (0, pad)).reshape(1, LANES)


def kernel(x_prompt, x_sample, cache_k, cache_v, page_table, norm_mix, norm_ffn, a_w_in, a_ln_g, a_ln_b,
           a_w_s, a_b_s, a_w_out, kv_norm, w_kv, k_norm, b_w_q, b_q_norm, b_w_o, moe_w_grp, moe_b_grp,
           moe_w_rt, moe_b_rt, moe_w_gate, moe_w_up, moe_w_down):
    batch, seq, d = x_prompt.shape
    n_seq, dec_seq, _ = x_sample.shape
    n_prompt = batch * seq
    n_sample = n_seq * dec_seq
    assert n_prompt % TOKEN_TILE == 0 and n_sample == TOKEN_TILE and seq % MOBA_BLOCK == 0
    past_len = page_table.shape[1] * PAGE_SIZE
    assert past_len % MOBA_BLOCK == 0 and dec_seq <= MOBA_BLOCK

    pos = jnp.concatenate([jnp.tile(jnp.arange(seq), batch), jnp.tile(past_len + jnp.arange(dec_seq), n_seq)])
    cos, sin = _rope_tables(pos)
    row = lambda a: a.reshape(1, -1)

    mix, bias = _mix_tables(a_w_s[0], a_b_s[0], dec_seq)
    h, vg_sample = _gmlp_layer(x_prompt.reshape(n_prompt, d), x_sample.reshape(n_sample, d), row(norm_mix[0]),
                               a_w_in[0].astype(BF16), row(a_ln_g[0]), row(a_ln_b[0]), mix, bias,
                               a_w_out[0].astype(BF16))
    moe = []
    for layer in range(2):
        w_r, b_r = _router_tables(moe_w_grp[layer], moe_b_grp[layer], moe_w_rt[layer], moe_b_rt[layer])
        moe.append((row(norm_ffn[layer]), w_r, b_r, moe_w_gate, moe_w_up, moe_w_down, layer))
    h = _hier_moe(h, *moe[0])

    k, v, q, k_p, k_s, v_p, v_s = _kvq_proj(h, row(kv_norm), row(norm_mix[1]), w_kv.astype(BF16),
                                            b_w_q[0].astype(BF16), row(k_norm), row(b_q_norm[0]), cos, sin)

    o_prompt = _moba_prompt(q, k, v, batch, seq)
    n_phys = cache_k.shape[0]
    cache_k2 = cache_k.reshape(n_phys, PAGE_SIZE * N_KV_HEADS, HEAD_DIM)
    cache_v2 = cache_v.reshape(n_phys, PAGE_SIZE * N_KV_HEADS, HEAD_DIM)
    o_sample = _sample_attn(page_table, q, k, v, n_prompt // dec_seq, cache_k2, cache_v2, dec_seq)
    h = _oproj(h, o_prompt, o_sample, b_w_o[0].astype(BF16))
    y_prompt, y_sample = _hier_moe(h, *moe[1], split=True)

    n_pages_new = seq // PAGE_SIZE
    return (y_prompt.reshape(batch, seq, d),
            y_sample.reshape(n_seq, dec_seq, d),
            k_p.reshape(batch, n_pages_new, PAGE_SIZE, N_KV_HEADS, HEAD_DIM),
            v_p.reshape(batch, n_pages_new, PAGE_SIZE, N_KV_HEADS, HEAD_DIM),
            k_s.reshape(n_seq, dec_seq, N_KV_HEADS, HEAD_DIM),
            v_s.reshape(n_seq, dec_seq, N_KV_HEADS, HEAD_DIM),
            vg_sample.reshape(1, n_seq, dec_seq, -1))
```

```python
import functools
import math

import jax
import jax.numpy as jnp
from jax import lax
from jax.experimental import pallas as pl
from jax.experimental.pallas import tpu as pltpu

F32 = jnp.float32
BF16 = jnp.bfloat16
HIGHEST = lax.Precision.HIGHEST

GMLP_CHUNK = 128
GMLP_GROUPS = 8
N_HEADS = 8
N_KV_HEADS = 4
HEAD_DIM = 128
Q_PER_KV = N_HEADS // N_KV_HEADS
MOBA_BLOCK = 256
MOBA_TOP_K = 3
ROPE_THETA = 10000.0
N_GROUPS = 4
EXPERTS_PER_GROUP = 8
N_EXPERTS = N_GROUPS * EXPERTS_PER_GROUP
TOP_K_EXPERTS = 2
PAGE_SIZE = 128
EPS = 1e-6

LANES = 128
ROW_TILE = 8
TOKEN_TILE = 256
FFN_ROWS = 256
PAGES_PER_STEP = 16
PAGE_SLOTS = 4
KEY_GROUP = 4
ONES_ROWS = 16
LOG2_E = math.log2(math.e)
VMEM_LIMIT = 56 * 1024 * 1024

NEG_INF = float("-inf")


def _cparams(sem):
    return pltpu.CompilerParams(dimension_semantics=sem, vmem_limit_bytes=VMEM_LIMIT)


def _rms(x):
    return x * lax.rsqrt(jnp.mean(x * x, axis=-1, keepdims=True) + EPS)


def _prompt_spec(block, n_prompt_tiles):
    return pl.BlockSpec(block, lambda i, *_: (jnp.minimum(i, n_prompt_tiles - 1), 0))


def _sample_spec(block):
    return pl.BlockSpec(block, lambda i, *_: (0, 0))


def _is_sample_tile():
    return pl.program_id(0) == pl.num_programs(0) - 1


def _stream_tile(prompt_ref, sample_ref):
    return jnp.where(_is_sample_tile(), sample_ref[...], prompt_ref[...])


def _store_stream_tile(prompt_ref, sample_ref, store):
    @pl.when(jnp.logical_not(_is_sample_tile()))
    def _():
        store(prompt_ref)

    @pl.when(_is_sample_tile())
    def _():
        store(sample_ref)


def _gmlp_kernel(xp_ref, xs_ref, g_ref, win_ref, lng_ref, lnb_ref, mix_ref, bias_ref, wout_ref,
                 h_ref, vg_ref, *, d_gate, n_groups):
    i = pl.program_id(0)
    x = _stream_tile(xp_ref, xs_ref)
    xb = (_rms(x) * g_ref[...]).astype(BF16)
    u = jax.nn.gelu(jnp.dot(xb, win_ref[:, :d_gate], preferred_element_type=F32))
    vp = jax.nn.gelu(jnp.dot(xb, win_ref[:, d_gate:], preferred_element_type=F32))
    vc = vp - jnp.mean(vp, axis=-1, keepdims=True)
    var = jnp.mean(vc * vc, axis=-1, keepdims=True)
    vg = vc * lax.rsqrt(var + EPS) * lng_ref[...] + lnb_ref[...]

    @pl.when(i == pl.num_programs(0) - 1)
    def _():
        vg_ref[...] = vg

    vgb = vg.astype(BF16)
    cw = d_gate // n_groups
    bias = bias_ref[0]
    parts = []
    for g in range(n_groups):
        mixed = jnp.dot(mix_ref[0, g], vgb[:, g * cw:(g + 1) * cw], preferred_element_type=F32)
        mixed = mixed + bias[:, g:g + 1]
        parts.append((u[:, g * cw:(g + 1) * cw] * mixed).astype(BF16))
    gated = jnp.concatenate(parts, axis=1)
    h_ref[...] = x + jnp.dot(gated, wout_ref[...], preferred_element_type=F32)


def _gmlp_layer(x_prompt, x_sample, g, w_in, ln_g, ln_b, mix, bias, w_out):
    d = x_prompt.shape[1]
    d_gate = w_out.shape[0]
    tm = TOKEN_TILE
    n_prompt_tiles = x_prompt.shape[0] // tm
    n_tiles = n_prompt_tiles + 1
    n = n_tiles * tm
    kind = lambda i: jnp.where(i < n_prompt_tiles, 0, 1)
    return pl.pallas_call(
        functools.partial(_gmlp_kernel, d_gate=d_gate, n_groups=GMLP_GROUPS),
        grid=(n_tiles,),
        in_specs=[
            _prompt_spec((tm, d), n_prompt_tiles),
            _sample_spec((tm, d)),
            pl.BlockSpec((1, d), lambda i: (0, 0)),
            pl.BlockSpec((d, 2 * d_gate), lambda i: (0, 0)),
            pl.BlockSpec((1, d_gate), lambda i: (0, 0)),
            pl.BlockSpec((1, d_gate), lambda i: (0, 0)),
            pl.BlockSpec((1, GMLP_GROUPS, tm, tm), lambda i: (kind(i), 0, 0, 0)),
            pl.BlockSpec((1, tm, LANES), lambda i: (kind(i), 0, 0)),
            pl.BlockSpec((d_gate, d), lambda i: (0, 0)),
        ],
        out_specs=[
            pl.BlockSpec((tm, d), lambda i: (i, 0)),
            pl.BlockSpec((tm, d_gate), lambda i: (0, 0)),
        ],
        out_shape=[
            jax.ShapeDtypeStruct((n, d), F32),
            jax.ShapeDtypeStruct((tm, d_gate), F32),
        ],
        compiler_params=_cparams(("arbitrary",)),
        name="gmlp_layer",
    )(x_prompt, x_sample, g, w_in, ln_g, ln_b, mix, bias, w_out)


def _router_kernel(h_ref, g_ref, wr_ref, br_ref, xn_ref, w_ref, code_ref, hist_ref):
    xn = _rms(h_ref[...]) * g_ref[...]
    _store_row_tiles(xn_ref, xn)
    logits = jnp.dot(xn, wr_ref[...], precision=HIGHEST, preferred_element_type=F32) + br_ref[...]
    lane = lax.broadcasted_iota(jnp.int32, logits.shape, 1)
    big = jnp.int32(LANES)
    is_grp = lane < N_GROUPS
    gl = jnp.where(is_grp, logits, NEG_INF)
    gmax = jnp.max(gl, axis=1, keepdims=True)
    gidx = jnp.min(jnp.where(is_grp & (logits == gmax), lane, big), axis=1, keepdims=True)
    p_g = 1.0 / jnp.sum(jnp.where(is_grp, jnp.exp(gl - gmax), 0.0), axis=1, keepdims=True)
    lo = N_GROUPS + gidx * EXPERTS_PER_GROUP
    in_grp = (lane >= lo) & (lane < lo + EXPERTS_PER_GROUP)
    v0 = jnp.max(jnp.where(in_grp, logits, NEG_INF), axis=1, keepdims=True)
    i0 = jnp.min(jnp.where(in_grp & (logits == v0), lane, big), axis=1, keepdims=True)
    rest = in_grp & (lane != i0)
    v1 = jnp.max(jnp.where(rest, logits, NEG_INF), axis=1, keepdims=True)
    i1 = jnp.min(jnp.where(rest & (logits == v1), lane, big), axis=1, keepdims=True)
    t = jnp.exp(v1 - v0)
    w0 = p_g * (1.0 / (1.0 + t))
    w1 = p_g * (t / (1.0 + t))
    e0 = i0 - N_GROUPS
    e1 = i1 - N_GROUPS
    w_ref[...] = jnp.where(lane == 0, w0, jnp.where(lane == 1, w1, 0.0))
    tm = logits.shape[0]
    onehot = jnp.concatenate([(lane == e0).astype(F32), (lane == e1).astype(F32)], axis=0)
    a_row = lax.broadcasted_iota(jnp.int32, (2 * tm, 2 * tm), 0)
    a_col = lax.broadcasted_iota(jnp.int32, (2 * tm, 2 * tm), 1)
    earlier = (a_col < a_row).astype(BF16)
    before = jnp.dot(earlier, onehot.astype(BF16), preferred_element_type=F32)
    rank = jnp.sum(before * onehot, axis=1, keepdims=True).astype(jnp.int32)
    code_ref[...] = jnp.where(lane == 0, rank[:tm] * N_EXPERTS + e0,
                              jnp.where(lane == 1, rank[tm:] * N_EXPERTS + e1, 0))
    hist = jnp.sum(onehot, axis=0, keepdims=True).astype(jnp.int32)
    hist_ref[...] = jnp.broadcast_to(hist, hist_ref.shape)


def _router(h, g, w_r, b_r):
    n, d = h.shape
    tm = TOKEN_TILE
    return pl.pallas_call(
        _router_kernel,
        grid=(n // tm,),
        in_specs=[
            pl.BlockSpec((tm, d), lambda i: (i, 0)),
            pl.BlockSpec((1, d), lambda i: (0, 0)),
            pl.BlockSpec((d, LANES), lambda i: (0, 0)),
            pl.BlockSpec((1, LANES), lambda i: (0, 0)),
        ],
        out_specs=[
            pl.BlockSpec((tm * ROW_TILE, LANES), lambda i: (i, 0)),
            pl.BlockSpec((tm, LANES), lambda i: (i, 0)),
            pl.BlockSpec((tm, LANES), lambda i: (i, 0)),
            pl.BlockSpec((ROW_TILE, LANES), lambda i: (i, 0)),
        ],
        out_shape=[
            jax.ShapeDtypeStruct((n * ROW_TILE, LANES), F32),
            jax.ShapeDtypeStruct((n, LANES), F32),
            jax.ShapeDtypeStruct((n, LANES), jnp.int32),
            jax.ShapeDtypeStruct((n // tm * ROW_TILE, LANES), jnp.int32),
        ],
        compiler_params=_cparams(("arbitrary",)),
        name="moe_router",
    )(h, g, w_r, b_r)


def _pos_kernel(code_ref, first_ref, pos_ref, *, tiles_per_step):
    i = pl.program_id(0)
    tm = TOKEN_TILE
    shift = N_EXPERTS.bit_length() - 1
    for s in range(tiles_per_step):
        code = code_ref[s * tm:(s + 1) * tm, :]
        expert = lax.bitwise_and(code, N_EXPERTS - 1)
        rank = lax.shift_right_logical(code, shift)
        off = first_ref[pl.ds(i * tiles_per_step + s, 1), :]
        lane = lax.broadcasted_iota(jnp.int32, code.shape, 1)
        pos = [jnp.sum(jnp.where(lane == expert[:, k:k + 1], off, 0), axis=1, keepdims=True) + rank[:, k:k + 1]
               for k in range(TOP_K_EXPERTS)]
        pos_ref[s * tm:(s + 1) * tm, :] = jnp.where(lane == 0, pos[0], jnp.where(lane == 1, pos[1], 0))


def _sorted_positions(code_pad, first):
    n = code_pad.shape[0]
    n_tiles = n // TOKEN_TILE
    tiles_per_step = max(t for t in range(1, 17) if n_tiles % t == 0)
    rows = tiles_per_step * TOKEN_TILE
    return pl.pallas_call(
        functools.partial(_pos_kernel, tiles_per_step=tiles_per_step),
        grid=(n_tiles // tiles_per_step,),
        in_specs=[
            pl.BlockSpec((rows, LANES), lambda i: (i, 0)),
            pl.BlockSpec(first.shape, lambda i: (0, 0)),
        ],
        out_specs=pl.BlockSpec((rows, LANES), lambda i: (i, 0)),
        out_shape=jax.ShapeDtypeStruct((n, LANES), jnp.int32),
        compiler_params=_cparams(("arbitrary",)),
        name="moe_pos",
    )(code_pad, first)


def _dispatch_kernel(pos_ref, gap_ref, xn_ref, out_hbm, zero, sem, zsem):
    i = pl.program_id(0)
    tm = xn_ref.shape[0] // ROW_TILE
    base = i * tm * TOP_K_EXPERTS
    block_rows = zero.shape[0] // ROW_TILE
    n_blocks = out_hbm.shape[0] // zero.shape[0]

    @pl.when(i == 0)
    def _():
        zero[...] = jnp.zeros_like(zero)

        def each_gap(visit):
            def gap(e, carry):
                start = gap_ref[2 * e]

                def row(r, c):
                    visit(pltpu.make_async_copy(_row_tile(zero, 0), _row_tile(out_hbm, start + r), zsem))
                    return c

                lax.fori_loop(0, gap_ref[2 * e + 1], row, 0)
                return carry

            lax.fori_loop(0, N_EXPERTS, gap, 0)

            def unused_block(b, carry):
                first = pl.multiple_of(b * block_rows * ROW_TILE, ROW_TILE)
                visit(pltpu.make_async_copy(zero, out_hbm.at[pl.ds(first, block_rows * ROW_TILE), :], zsem))
                return carry

            lax.fori_loop(gap_ref[2 * N_EXPERTS], n_blocks, unused_block, 0)

        each_gap(lambda copy: copy.start())
        each_gap(lambda copy: copy.wait())

    def issue(t, carry):
        for k in range(TOP_K_EXPERTS):
            p = pos_ref[base + t * TOP_K_EXPERTS + k]
            pltpu.make_async_copy(_row_tile(xn_ref, t), _row_tile(out_hbm, p), sem).start(priority=k)
        return carry

    lax.fori_loop(0, tm, issue, 0, unroll=4)
    for k in range(TOP_K_EXPERTS):
        pltpu.make_async_copy(xn_ref, out_hbm.at[pl.ds(0, tm * ROW_TILE), :], sem).wait()


def _dispatch(xn, pos, gaps, n_rows):
    tm = TOKEN_TILE
    n = xn.shape[0] // ROW_TILE
    grid_spec = pltpu.PrefetchScalarGridSpec(
        num_scalar_prefetch=2,
        grid=(n // tm,),
        in_specs=[pl.BlockSpec((tm * ROW_TILE, LANES), lambda i, ps, gp: (i, 0))],
        out_specs=pl.BlockSpec(memory_space=pl.ANY),
        scratch_shapes=[pltpu.VMEM((FFN_ROWS * ROW_TILE, LANES), F32), pltpu.SemaphoreType.DMA(()),
                        pltpu.SemaphoreType.DMA(())],
    )
    return pl.pallas_call(
        _dispatch_kernel,
        grid_spec=grid_spec,
        out_shape=jax.ShapeDtypeStruct((n_rows * ROW_TILE, LANES), F32),
        compiler_params=_cparams(("arbitrary",)),
        name="moe_dispatch",
    )(pos, gaps, xn)


def _store_row_tiles(ref, x, first=0):
    for c in range(ROW_TILE):
        ref[pl.ds(first + c, x.shape[0], stride=ROW_TILE), :] = x[:, c * LANES:(c + 1) * LANES]


def _load_row_tiles(ref, first, rows):
    return jnp.concatenate([ref[pl.ds(first + c, rows, stride=ROW_TILE), :] for c in range(ROW_TILE)], axis=1)


def _row_tile(ref, r):
    return ref.at[pl.ds(pl.multiple_of(r * ROW_TILE, ROW_TILE), ROW_TILE), :]


def _ffn_kernel(blk_e_ref, nblk_ref, fresh_ref, x_ref, wg_ref, wu_ref, wd_ref, out_ref, wg_s, wu_s, wd_s):
    i = pl.program_id(0)
    rows = x_ref.shape[0] // ROW_TILE
    live = i < nblk_ref[0]

    @pl.when(live & (fresh_ref[i] == 1))
    def _():
        wg_s[...] = wg_ref[0, 0].astype(BF16)
        wu_s[...] = wu_ref[0, 0].astype(BF16)
        wd_s[...] = wd_ref[0, 0].astype(BF16)

    @pl.when(live)
    def _():
        x = _load_row_tiles(x_ref, 0, rows).astype(BF16)
        gate = jnp.dot(x, wg_s[...], preferred_element_type=F32)
        up = jnp.dot(x, wu_s[...], preferred_element_type=F32)
        hid = (jax.nn.silu(gate) * up).astype(BF16)
        _store_row_tiles(out_ref, jnp.dot(hid, wd_s[...], preferred_element_type=F32))

    @pl.when(jnp.logical_not(live))
    def _():
        out_ref[...] = jnp.zeros_like(out_ref)


def _grouped_ffn(x_sorted, blk_e, nblk, fresh, w_gate, w_up, w_down, layer):
    rb = FFN_ROWS
    n_rows = x_sorted.shape[0] // ROW_TILE
    _, _, d, d_e = w_gate.shape
    assert d == ROW_TILE * LANES
    grid_spec = pltpu.PrefetchScalarGridSpec(
        num_scalar_prefetch=3,
        grid=(n_rows // rb,),
        in_specs=[
            pl.BlockSpec((rb * ROW_TILE, LANES), lambda i, be, nb, fr: (jnp.minimum(i, nb[0] - 1), 0)),
            pl.BlockSpec((1, 1, d, d_e), lambda i, be, nb, fr: (layer, be[i], 0, 0)),
            pl.BlockSpec((1, 1, d, d_e), lambda i, be, nb, fr: (layer, be[i], 0, 0)),
            pl.BlockSpec((1, 1, d_e, d), lambda i, be, nb, fr: (layer, be[i], 0, 0)),
        ],
        out_specs=pl.BlockSpec((rb * ROW_TILE, LANES), lambda i, be, nb, fr: (i, 0)),
        scratch_shapes=[pltpu.VMEM((d, d_e), BF16), pltpu.VMEM((d, d_e), BF16), pltpu.VMEM((d_e, d), BF16)],
    )
    return pl.pallas_call(
        _ffn_kernel,
        grid_spec=grid_spec,
        out_shape=jax.ShapeDtypeStruct((n_rows * ROW_TILE, LANES), F32),
        compiler_params=_cparams(("arbitrary",)),
        name="moe_ffn",
    )(blk_e, nblk, fresh, x_sorted, w_gate, w_up, w_down)


def _combine_kernel(pos_ref, resid_ref, w_ref, src_hbm, *refs, split):
    out_refs, buf, sems = refs[:-2], refs[-2], refs[-1]
    i = pl.program_id(0)
    tm = resid_ref.shape[0]

    def gather(tile, slot):
        base = tile * tm * TOP_K_EXPERTS

        def issue(t, carry):
            for k in range(TOP_K_EXPERTS):
                p = pos_ref[base + t * TOP_K_EXPERTS + k]
                pltpu.make_async_copy(_row_tile(src_hbm, p), _row_tile(buf, (slot * TOP_K_EXPERTS + k) * tm + t),
                                      sems.at[slot]).start(priority=k)
            return carry

        lax.fori_loop(0, tm, issue, 0, unroll=4)

    @pl.when(i == 0)
    def _():
        gather(0, 0)

    @pl.when(i + 1 < pl.num_programs(0))
    def _():
        gather(i + 1, (i + 1) % 2)

    slot = i % 2
    firsts = [pl.multiple_of((slot * TOP_K_EXPERTS + k) * tm * ROW_TILE, ROW_TILE) for k in range(TOP_K_EXPERTS)]
    for first in firsts:
        pltpu.make_async_copy(src_hbm.at[pl.ds(0, tm * ROW_TILE), :],
                              buf.at[pl.ds(first, tm * ROW_TILE), :], sems.at[slot]).wait()
    acc = resid_ref[...]
    w = w_ref[...]
    for k, first in enumerate(firsts):
        acc = acc + w[:, k:k + 1] * _load_row_tiles(buf, first, tm)
    def put(ref):
        ref[...] = acc

    if split:
        _store_stream_tile(out_refs[0], out_refs[1], put)
    else:
        put(out_refs[0])


def _combine(resid, w_pad, src, pos, split):
    n, d = resid.shape
    assert d == ROW_TILE * LANES and TOP_K_EXPERTS == 2
    tm = TOKEN_TILE
    n_prompt_tiles = n // tm - 1
    if split:
        out_specs = [_prompt_spec((tm, d), n_prompt_tiles), _sample_spec((tm, d))]
        out_shape = [jax.ShapeDtypeStruct((n_prompt_tiles * tm, d), F32), jax.ShapeDtypeStruct((tm, d), F32)]
    else:
        out_specs = [pl.BlockSpec((tm, d), lambda i, *_: (i, 0))]
        out_shape = [jax.ShapeDtypeStruct((n, d), F32)]
    grid_spec = pltpu.PrefetchScalarGridSpec(
        num_scalar_prefetch=1,
        grid=(n // tm,),
        in_specs=[
            pl.BlockSpec((tm, d), lambda i, *_: (i, 0)),
            pl.BlockSpec((tm, LANES), lambda i, *_: (i, 0)),
            pl.BlockSpec(memory_space=pl.ANY),
        ],
        out_specs=out_specs,
        scratch_shapes=[pltpu.VMEM((2 * TOP_K_EXPERTS * tm * ROW_TILE, LANES), F32),
                        pltpu.SemaphoreType.DMA((2,))],
    )
    return pl.pallas_call(
        functools.partial(_combine_kernel, split=split),
        grid_spec=grid_spec,
        out_shape=out_shape,
        compiler_params=_cparams(("arbitrary",)),
        name="moe_combine",
    )(pos, resid, w_pad, src)


def _hier_moe(h, g, w_r, b_r, w_gate, w_up, w_down, layer, split=False):
    assert N_EXPERTS & (N_EXPERTS - 1) == 0
    n, _ = h.shape
    rb = FFN_ROWS
    xn, w_pad, code_pad, hist_pad = _router(h, g, w_r, b_r)
    hist = hist_pad[::ROW_TILE, :N_EXPERTS]
    counts = jnp.sum(hist, axis=0)
    padded = (counts + rb - 1) // rb * rb
    pend = jnp.cumsum(padded)
    first = ((pend - padded)[None, :] + jnp.cumsum(hist, axis=0) - hist).astype(jnp.int32)
    first = jnp.pad(first, ((0, 0), (0, LANES - N_EXPERTS)))
    pos = _sorted_positions(code_pad, first)[:, :TOP_K_EXPERTS].reshape(-1)
    n_blocks = (n * TOP_K_EXPERTS + N_EXPERTS * (rb - 1) + rb - 1) // rb
    block_first = jnp.arange(n_blocks, dtype=jnp.int32)[:, None] * rb
    blk_e = jnp.minimum(jnp.sum((pend[None, :] <= block_first).astype(jnp.int32), axis=1), N_EXPERTS - 1)
    fresh = jnp.concatenate([jnp.ones((1,), jnp.int32), (blk_e[1:] != blk_e[:-1]).astype(jnp.int32)])
    nblk = (pend[-1:] // rb).astype(jnp.int32)
    gaps = jnp.concatenate([jnp.stack([pend - padded + counts, padded - counts], axis=1).reshape(-1),
                            nblk]).astype(jnp.int32)
    x_sorted = _dispatch(xn, pos, gaps, n_blocks * rb)
    out_sorted = _grouped_ffn(x_sorted, blk_e, nblk, fresh, w_gate, w_up, w_down, layer)
    out = _combine(h, w_pad, out_sorted, pos, split)
    return out if split else out[0]


def _store_heads_as_rows(ref, x):
    for j in range(N_KV_HEADS):
        ref[pl.ds(j, x.shape[0], stride=N_KV_HEADS), :] = x[:, j * HEAD_DIM:(j + 1) * HEAD_DIM]


def _proj_kernel(h_ref, gkv_ref, gq_ref, wkv_ref, wq_ref, kn_ref, qn_ref, cos_ref, sin_ref,
                 k_ref, v_ref, q_ref, kp_ref, ks_ref, vp_ref, vs_ref):
    hn = _rms(h_ref[...])
    cos = cos_ref[...]
    sin = sin_ref[...]

    def norm_rope(x, g):
        y = _rms(x) * g
        return y * cos + pltpu.roll(y, HEAD_DIM // 2, 1) * sin

    kv = jnp.dot((hn * gkv_ref[...]).astype(BF16), wkv_ref[...], preferred_element_type=F32)
    kw = N_KV_HEADS * HEAD_DIM
    k = jnp.concatenate(
        [norm_rope(kv[:, j * HEAD_DIM:(j + 1) * HEAD_DIM], kn_ref[...]) for j in range(N_KV_HEADS)], axis=1)
    v = kv[:, kw:]
    k_ref[...] = k
    v_ref[...] = v
    _store_stream_tile(kp_ref, ks_ref, lambda ref: _store_heads_as_rows(ref, k))
    _store_stream_tile(vp_ref, vs_ref, lambda ref: _store_heads_as_rows(ref, v))
    q = jnp.dot((hn * gq_ref[...]).astype(BF16), wq_ref[...], preferred_element_type=F32)
    q_ref[...] = jnp.concatenate(
        [norm_rope(q[:, j * HEAD_DIM:(j + 1) * HEAD_DIM], qn_ref[...]) for j in range(N_HEADS)], axis=1)


def _kvq_proj(h, g_kv, g_q, w_kv, w_q, k_norm, q_norm, cos, sin):
    n, d = h.shape
    tm = TOKEN_TILE
    kw = N_KV_HEADS * HEAD_DIM
    qw = N_HEADS * HEAD_DIM
    row = lambda i: (i, 0)
    fixed = lambda i: (0, 0)
    n_prompt_tiles = n // tm - 1
    head_rows = tm * N_KV_HEADS
    by_head = [_prompt_spec((head_rows, HEAD_DIM), n_prompt_tiles), _sample_spec((head_rows, HEAD_DIM))]
    by_head_shapes = [jax.ShapeDtypeStruct((n_prompt_tiles * head_rows, HEAD_DIM), F32),
                      jax.ShapeDtypeStruct((head_rows, HEAD_DIM), F32)]
    return pl.pallas_call(
        _proj_kernel,
        grid=(n // tm,),
        in_specs=[
            pl.BlockSpec((tm, d), row),
            pl.BlockSpec((1, d), fixed),
            pl.BlockSpec((1, d), fixed),
            pl.BlockSpec((d, 2 * kw), fixed),
            pl.BlockSpec((d, qw), fixed),
            pl.BlockSpec((1, HEAD_DIM), fixed),
            pl.BlockSpec((1, HEAD_DIM), fixed),
            pl.BlockSpec((tm, HEAD_DIM), row),
            pl.BlockSpec((tm, HEAD_DIM), row),
        ],
        out_specs=[
            pl.BlockSpec((tm, kw), row),
            pl.BlockSpec((tm, kw), row),
            pl.BlockSpec((tm, qw), row),
        ] + by_head + by_head,
        out_shape=[
            jax.ShapeDtypeStruct((n, kw), F32),
            jax.ShapeDtypeStruct((n, kw), F32),
            jax.ShapeDtypeStruct((n, qw), F32),
        ] + by_head_shapes + by_head_shapes,
        compiler_params=_cparams(("arbitrary",)),
        name="kvq_proj",
    )(h, g_kv, g_q, w_kv, w_q, k_norm, q_norm, cos, sin)


def _top_blocks(gate, axis):
    idx = lax.broadcasted_iota(jnp.int32, gate.shape, axis)
    big = jnp.int32(gate.shape[axis])
    sel = jnp.zeros(gate.shape, jnp.bool_)
    for _ in range(MOBA_TOP_K):
        top = jnp.max(gate, axis=axis, keepdims=True)
        first = jnp.min(jnp.where(gate == top, idx, big), axis=axis, keepdims=True)
        hit = idx == first
        sel = sel | (hit & (top > NEG_INF))
        gate = jnp.where(hit, NEG_INF, gate)
    return sel.astype(F32)


def _moba_prompt_kernel(q_ref, k_ref, v_ref, o_ref, kbf, vt, kmean, sel, s_a, s_b, m_scr, acc):
    j = pl.program_id(2)
    blk = MOBA_BLOCK
    grp = KEY_GROUP
    seq = k_ref.shape[0]
    n_blocks = seq // blk

    @pl.when(j == 0)
    def _():
        k = k_ref[...]
        kbf[...] = k.astype(BF16)
        kmean[...] = jnp.mean(k.reshape(n_blocks, blk, HEAD_DIM), axis=1)
        for n in range(n_blocks):
            vt[:HEAD_DIM, n * blk:(n + 1) * blk] = v_ref[n * blk:(n + 1) * blk, :].T.astype(BF16)
        r = lax.broadcasted_iota(jnp.int32, (ONES_ROWS, seq), 0)
        vt[HEAD_DIM:, :] = jnp.where(r == 0, 1.0, 0.0).astype(BF16)

    q2 = q_ref[...]
    qs = jnp.concatenate([q2[:, h * HEAD_DIM:(h + 1) * HEAD_DIM] for h in range(Q_PER_KV)], axis=0)
    nq = qs.shape[0]
    gate = lax.dot_general(kmean[...], qs, (((1,), (1,)), ((), ())),
                           precision=HIGHEST, preferred_element_type=F32)
    row = lax.broadcasted_iota(jnp.int32, gate.shape, 0)
    sel[...] = _top_blocks(jnp.where(row < j, gate, NEG_INF), 0)
    qt = (qs * (HEAD_DIM ** -0.5 * LOG2_E)).T.astype(BF16)

    m_scr[...] = jnp.full(m_scr.shape, NEG_INF, F32)
    acc[...] = jnp.zeros(acc.shape, F32)

    def update(tiles, start):
        m_old = m_scr[...]
        m_new = m_old
        for s in tiles:
            m_new = jnp.maximum(m_new, jnp.max(s, axis=0, keepdims=True))
        m_safe = jnp.where(m_new == NEG_INF, 0.0, m_new)
        p = jnp.concatenate([jnp.exp2(s - m_safe).astype(BF16) for s in tiles], axis=0)
        alpha = jnp.exp2(m_old - m_safe)
        pv = jnp.dot(vt[:, pl.ds(start, len(tiles) * blk)], p, preferred_element_type=F32)
        acc[...] = alpha * acc[...] + pv
        m_scr[...] = m_new

    def scores(start, n_keys):
        return jnp.dot(kbf[pl.ds(start, n_keys), :], qt, preferred_element_type=F32)

    def fill(buf, g):
        start = pl.multiple_of(g * (grp * blk), grp * blk)
        buf[...] = scores(start, grp * blk)

    def consume(buf, g):
        start = pl.multiple_of(g * (grp * blk), grp * blk)
        update([jnp.where(sel[pl.ds(g * grp + i, 1), :] > 0.0, buf[i * blk:(i + 1) * blk, :], NEG_INF)
                for i in range(grp)], start)

    n_groups = lax.div(j + (grp - 1), grp)
    n_pairs = lax.div(n_groups - 1, 2)

    @pl.when(n_groups > 0)
    def _():
        fill(s_a, 0)

        def pair(h, carry):
            g = 2 * h
            fill(s_b, g + 1)
            consume(s_a, g)
            fill(s_a, g + 2)
            consume(s_b, g + 1)
            return carry

        lax.fori_loop(0, n_pairs, pair, 0)
        g = 2 * n_pairs

        @pl.when(n_groups - g == 1)
        def _():
            consume(s_a, g)

        @pl.when(n_groups - g == 2)
        def _():
            fill(s_b, g + 1)
            consume(s_a, g)
            consume(s_b, g + 1)

    key = lax.broadcasted_iota(jnp.int32, (blk, nq), 0)
    qpos = lax.broadcasted_iota(jnp.int32, (blk, nq), 1) % blk
    own = pl.multiple_of(j * blk, blk)
    update([jnp.where(key <= qpos, scores(own, blk), NEG_INF)], own)
    a = acc[...]
    o = (a[:HEAD_DIM] / a[HEAD_DIM:HEAD_DIM + 1]).T
    o_ref[...] = jnp.concatenate([o[h * blk:(h + 1) * blk, :] for h in range(Q_PER_KV)], axis=1)


def _moba_prompt(q, k, v, batch, seq):
    blk = MOBA_BLOCK
    nqb = seq // blk
    qw = Q_PER_KV * HEAD_DIM
    nq = Q_PER_KV * blk
    return pl.pallas_call(
        _moba_prompt_kernel,
        grid=(batch, N_KV_HEADS, nqb),
        in_specs=[
            pl.BlockSpec((blk, qw), lambda b, c, j: (b * nqb + j, c)),
            pl.BlockSpec((seq, HEAD_DIM), lambda b, c, j: (b, c)),
            pl.BlockSpec((seq, HEAD_DIM), lambda b, c, j: (b, c)),
        ],
        out_specs=pl.BlockSpec((blk, qw), lambda b, c, j: (b * nqb + j, c)),
        out_shape=jax.ShapeDtypeStruct((batch * seq, N_HEADS * HEAD_DIM), F32),
        scratch_shapes=[
            pltpu.VMEM((seq, HEAD_DIM), BF16),
            pltpu.VMEM((HEAD_DIM + ONES_ROWS, seq), BF16),
            pltpu.VMEM((nqb, HEAD_DIM), F32),
            pltpu.VMEM((nqb, nq), F32),
            pltpu.VMEM((KEY_GROUP * blk, nq), F32),
            pltpu.VMEM((KEY_GROUP * blk, nq), F32),
            pltpu.VMEM((1, nq), F32),
            pltpu.VMEM((HEAD_DIM + ONES_ROWS, nq), F32),
        ],
        compiler_params=_cparams(("arbitrary", "arbitrary", "arbitrary")),
        name="moba_prompt",
    )(q, k, v)


def _stack_heads(q8):
    return jnp.concatenate([q8[:, h * HEAD_DIM:(h + 1) * HEAD_DIM] for h in range(N_HEADS)], axis=0)


def _sample_attn_kernel(pt_ref, q_ref, kn_ref, vn_ref, ck_hbm, cv_hbm, o_ref, pages, sems,
                        s_scr, means, sel_scr, qs_scr, m_scr, l_scr, acc, *, dec_seq, n_pages):
    pps = PAGES_PER_STEP
    t = pl.program_id(1)
    steps = pl.num_programs(1)
    n_k_steps = n_pages // pps
    rows = N_HEADS * dec_seq
    rkv = Q_PER_KV * dec_seq
    ppb = MOBA_BLOCK // PAGE_SIZE
    bps = pps // ppb
    n_blocks = n_pages // ppb
    nt_dims = (((1,), (1,)), ((), ()))

    chunk = pl.program_id(0) * steps + t
    n_chunks = pl.num_programs(0) * steps

    def start_chunk(ci):
        seq = lax.div(ci, steps)
        step = lax.rem(ci, steps)
        slot = lax.rem(ci, PAGE_SLOTS)

        def start_pages(cache_hbm, first_page):
            for r in range(pps):
                pltpu.make_async_copy(cache_hbm.at[pt_ref[seq, first_page + r]], pages.at[slot, r],
                                      sems.at[slot]).start()

        @pl.when(step < n_k_steps)
        def _():
            start_pages(ck_hbm, step * pps)

        @pl.when(step >= n_k_steps)
        def _():
            start_pages(cv_hbm, (step - n_k_steps) * pps)

    @pl.when(chunk == 0)
    def _():
        for ci in range(PAGE_SLOTS - 1):
            start_chunk(jnp.int32(ci))

    @pl.when(chunk + (PAGE_SLOTS - 1) < n_chunks)
    def _():
        start_chunk(chunk + (PAGE_SLOTS - 1))

    slot = lax.rem(chunk, PAGE_SLOTS)
    pltpu.make_async_copy(ck_hbm.at[pl.ds(0, pps)], pages.at[slot], sems.at[slot]).wait()

    def head_rows(p, c):
        return pages[slot, p, pl.ds(c, PAGE_SIZE, stride=N_KV_HEADS), :]

    @pl.when(t == 0)
    def _():
        qs_scr[...] = (_stack_heads(q_ref[...]) * (HEAD_DIM ** -0.5 * LOG2_E)).astype(BF16)
        m_scr[...] = jnp.full(m_scr.shape, NEG_INF, F32)
        l_scr[...] = jnp.zeros(l_scr.shape, F32)
        acc[...] = jnp.zeros(acc.shape, F32)

    @pl.when(t < n_k_steps)
    def _():
        qs = qs_scr[...]
        col = pl.multiple_of(t * (pps * PAGE_SIZE), pps * PAGE_SIZE)
        for c in range(N_KV_HEADS):
            kc = jnp.concatenate([head_rows(p, c) for p in range(pps)], axis=0)
            s_scr[c * rkv:(c + 1) * rkv, pl.ds(col, pps * PAGE_SIZE)] = lax.dot_general(
                qs[c * rkv:(c + 1) * rkv], kc.astype(BF16), nt_dims, preferred_element_type=F32)
            means[c, pl.ds(t * bps, bps), :] = jnp.sum(kc.reshape(bps, MOBA_BLOCK, HEAD_DIM), axis=1) / MOBA_BLOCK

    @pl.when(t == n_k_steps - 1)
    def _():
        qf = _stack_heads(q_ref[...])
        gate = jnp.concatenate(
            [lax.dot_general(qf[c * rkv:(c + 1) * rkv], means[c], nt_dims, precision=HIGHEST,
                             preferred_element_type=F32) for c in range(N_KV_HEADS)], axis=0)
        chosen = _top_blocks(gate, 1)
        sel_scr[...] = jnp.concatenate([chosen, jnp.zeros((rows, LANES - n_blocks), F32)], axis=1)

    def softmax_step(tiles):
        m_old = m_scr[...]
        m_new = m_old
        for s in tiles:
            m_new = jnp.maximum(m_new, jnp.max(s, axis=1, keepdims=True))
        m_safe = jnp.where(m_new == NEG_INF, 0.0, m_new)
        alpha = jnp.exp2(m_old - m_safe)
        l_new = alpha * l_scr[...]
        probs = []
        for s in tiles:
            p = jnp.exp2(s - m_safe)
            l_new = l_new + jnp.sum(p, axis=1, keepdims=True)
            probs.append(p.astype(BF16))
        l_scr[...] = l_new
        m_scr[...] = m_new
        return probs, alpha

    @pl.when(t >= n_k_steps)
    def _():
        tv = t - n_k_steps
        sel = sel_scr[...]
        lane = lax.broadcasted_iota(jnp.int32, sel.shape, 1)
        tiles = []
        for b in range(bps):
            n = tv * bps + b
            chosen = jnp.sum(jnp.where(lane == n, sel, 0.0), axis=1, keepdims=True) > 0.0
            col = pl.multiple_of(n * MOBA_BLOCK, MOBA_BLOCK)
            tiles.append(jnp.where(chosen, s_scr[:, pl.ds(col, MOBA_BLOCK)], NEG_INF))
        probs, alpha = softmax_step(tiles)
        prob = jnp.concatenate(probs, axis=1)
        pv = []
        for c in range(N_KV_HEADS):
            vc = jnp.concatenate([head_rows(p, c) for p in range(pps)], axis=0).astype(BF16)
            pv.append(jnp.dot(prob[c * rkv:(c + 1) * rkv], vc, preferred_element_type=F32))
        acc[...] = alpha * acc[...] + jnp.concatenate(pv, axis=0)

    @pl.when(t == pl.num_programs(1) - 1)
    def _():
        qs = qs_scr[...]
        kn = kn_ref[...].astype(BF16)
        vn = vn_ref[...].astype(BF16)
        s = jnp.concatenate(
            [lax.dot_general(qs[c * rkv:(c + 1) * rkv], kn[:, c * HEAD_DIM:(c + 1) * HEAD_DIM], nt_dims,
                             preferred_element_type=F32) for c in range(N_KV_HEADS)], axis=0)
        r2 = lax.broadcasted_iota(jnp.int32, s.shape, 0)
        c2 = lax.broadcasted_iota(jnp.int32, s.shape, 1)
        probs, alpha = softmax_step([jnp.where(c2 <= r2 % dec_seq, s, NEG_INF)])
        pv = [jnp.dot(probs[0][c * rkv:(c + 1) * rkv], vn[:, c * HEAD_DIM:(c + 1) * HEAD_DIM],
                      preferred_element_type=F32) for c in range(N_KV_HEADS)]
        o = (alpha * acc[...] + jnp.concatenate(pv, axis=0)) / l_scr[...]
        o_ref[...] = jnp.concatenate([o[h * dec_seq:(h + 1) * dec_seq, :] for h in range(N_HEADS)], axis=1)


def _sample_attn(page_table, q, k, v, row0, cache_k2, cache_v2, dec_seq):
    n_seq, n_pages = page_table.shape
    pps = PAGES_PER_STEP
    n_k_steps = n_pages // pps
    n_blocks = n_pages * PAGE_SIZE // MOBA_BLOCK
    assert n_pages % pps == 0 and n_blocks <= LANES
    rows = N_HEADS * dec_seq
    kw = N_KV_HEADS * HEAD_DIM
    qw = N_HEADS * HEAD_DIM
    grid_spec = pltpu.PrefetchScalarGridSpec(
        num_scalar_prefetch=1,
        grid=(n_seq, 2 * n_k_steps),
        in_specs=[
            pl.BlockSpec((dec_seq, qw), lambda s, t, pt: (row0 + s, 0)),
            pl.BlockSpec((dec_seq, kw), lambda s, t, pt: (row0 + s, 0)),
            pl.BlockSpec((dec_seq, kw), lambda s, t, pt: (row0 + s, 0)),
            pl.BlockSpec(memory_space=pl.ANY),
            pl.BlockSpec(memory_space=pl.ANY),
        ],
        out_specs=pl.BlockSpec((dec_seq, qw), lambda s, t, pt: (s, 0)),
        scratch_shapes=[
            pltpu.VMEM((PAGE_SLOTS, pps, PAGE_SIZE * N_KV_HEADS, HEAD_DIM), F32),
            pltpu.SemaphoreType.DMA((PAGE_SLOTS,)),
            pltpu.VMEM((rows, n_pages * PAGE_SIZE), F32),
            pltpu.VMEM((N_KV_HEADS, n_blocks, HEAD_DIM), F32),
            pltpu.VMEM((rows, LANES), F32),
            pltpu.VMEM((rows, HEAD_DIM), BF16),
            pltpu.VMEM((rows, 1), F32),
            pltpu.VMEM((rows, 1), F32),
            pltpu.VMEM((rows, HEAD_DIM), F32),
        ],
    )
    return pl.pallas_call(
        functools.partial(_sample_attn_kernel, dec_seq=dec_seq, n_pages=n_pages),
        grid_spec=grid_spec,
        out_shape=jax.ShapeDtypeStruct((n_seq * dec_seq, qw), F32),
        compiler_params=_cparams(("arbitrary", "arbitrary")),
        name="sample_attn",
    )(page_table, q, k, v, cache_k2, cache_v2)


def _oproj_kernel(h_ref, op_ref, os_ref, w_ref, out_ref):
    o = _stream_tile(op_ref, os_ref).astype(BF16)
    out_ref[...] = h_ref[...] + jnp.dot(o, w_ref[...], preferred_element_type=F32)


def _oproj(h, o_prompt, o_sample, w_o):
    n, d = h.shape
    tm = TOKEN_TILE
    ow = o_prompt.shape[1]
    return pl.pallas_call(
        _oproj_kernel,
        grid=(n // tm,),
        in_specs=[
            pl.BlockSpec((tm, d), lambda i: (i, 0)),
            _prompt_spec((tm, ow), n // tm - 1),
            _sample_spec((tm, ow)),
            pl.BlockSpec(w_o.shape, lambda i: (0, 0)),
        ],
        out_specs=pl.BlockSpec((tm, d), lambda i: (i, 0)),
        out_shape=jax.ShapeDtypeStruct((n, d), F32),
        compiler_params=_cparams(("arbitrary",)),
        name="attn_oproj",
    )(h, o_prompt, o_sample, w_o)


def _mix_tables(w_s, b_s, dec_seq):
    tm = TOKEN_TILE
    causal = jnp.tril(jnp.ones((GMLP_CHUNK, GMLP_CHUNK), dtype=bool))
    w = jnp.where(causal[None], w_s, jnp.zeros_like(w_s))
    eye_p = jnp.eye(tm // GMLP_CHUNK, dtype=w.dtype)
    mix_p = jnp.einsum("ab,gts->gatbs", eye_p, w).reshape(GMLP_GROUPS, tm, tm)
    eye_s = jnp.eye(tm // dec_seq, dtype=w.dtype)
    mix_s = jnp.einsum("ab,gts->gatbs", eye_s, w[:, :dec_seq, :dec_seq]).reshape(GMLP_GROUPS, tm, tm)
    mix = jnp.stack([mix_p, mix_s]).astype(BF16)
    bias_p = jnp.tile(b_s.T, (tm // GMLP_CHUNK, 1))
    bias_s = jnp.tile(b_s.T[:dec_seq], (tm // dec_seq, 1))
    bias = jnp.stack([bias_p, bias_s])
    bias = jnp.pad(bias, ((0, 0), (0, 0), (0, LANES - GMLP_GROUPS)))
    return mix, bias


def _rope_tables(pos):
    half = HEAD_DIM // 2
    inv = ROPE_THETA ** (-jnp.arange(half, dtype=F32) * 2.0 / HEAD_DIM)
    ang = pos.astype(F32)[:, None] * inv[None, :]
    cos = jnp.cos(ang)
    sin = jnp.sin(ang)
    return jnp.concatenate([cos, cos], axis=1), jnp.concatenate([-sin, sin], axis=1)


def _router_tables(w_grp, b_grp, w_rt, b_rt):
    w = jnp.concatenate([w_grp, w_rt], axis=1)
    b = jnp.concatenate([b_grp, b_rt], axis=0)
    pad = LANES - w.shape[1]
    return jnp.pad(w, ((0, 0), (0, pad))), jnp.pad(b, (0, pad)).reshape(1, LANES)


def kernel(x_prompt, x_sample, cache_k, cache_v, page_table, norm_mix, norm_ffn, a_w_in, a_ln_g, a_ln_b,
           a_w_s, a_b_s, a_w_out, kv_norm, w_kv, k_norm, b_w_q, b_q_norm, b_w_o, moe_w_grp, moe_b_grp,
           moe_w_rt, moe_b_rt, moe_w_gate, moe_w_up, moe_w_down):
    batch, seq, d = x_prompt.shape
    n_seq, dec_seq, _ = x_sample.shape
    n_prompt = batch * seq
    n_sample = n_seq * dec_seq
    assert n_prompt % TOKEN_TILE == 0 and n_sample == TOKEN_TILE and seq % MOBA_BLOCK == 0
    past_len = page_table.shape[1] * PAGE_SIZE
    assert past_len % MOBA_BLOCK == 0 and dec_seq <= MOBA_BLOCK

    pos = jnp.concatenate([jnp.tile(jnp.arange(seq), batch), jnp.tile(past_len + jnp.arange(dec_seq), n_seq)])
    cos, sin = _rope_tables(pos)
    row = lambda a: a.reshape(1, -1)

    mix, bias = _mix_tables(a_w_s[0], a_b_s[0], dec_seq)
    h, vg_sample = _gmlp_layer(x_prompt.reshape(n_prompt, d), x_sample.reshape(n_sample, d), row(norm_mix[0]),
                               a_w_in[0].astype(BF16), row(a_ln_g[0]), row(a_ln_b[0]), mix, bias,
                               a_w_out[0].astype(BF16))
    moe = []
    for layer in range(2):
        w_r, b_r = _router_tables(moe_w_grp[layer], moe_b_grp[layer], moe_w_rt[layer], moe_b_rt[layer])
        moe.append((row(norm_ffn[layer]), w_r, b_r, moe_w_gate, moe_w_up, moe_w_down, layer))
    h = _hier_moe(h, *moe[0])

    k, v, q, k_p, k_s, v_p, v_s = _kvq_proj(h, row(kv_norm), row(norm_mix[1]), w_kv.astype(BF16),
                                            b_w_q[0].astype(BF16), row(k_norm), row(b_q_norm[0]), cos, sin)

    o_prompt = _moba_prompt(q, k, v, batch, seq)
    n_phys = cache_k.shape[0]
    cache_k2 = cache_k.reshape(n_phys, PAGE_SIZE * N_KV_HEADS, HEAD_DIM)
    cache_v2 = cache_v.reshape(n_phys, PAGE_SIZE * N_KV_HEADS, HEAD_DIM)
    o_sample = _sample_attn(page_table, q, k, v, n_prompt // dec_seq, cache_k2, cache_v2, dec_seq)
    h = _oproj(h, o_prompt, o_sample, b_w_o[0].astype(BF16))
    y_prompt, y_sample = _hier_moe(h, *moe[1], split=True)

    n_pages_new = seq // PAGE_SIZE
    return (y_prompt.reshape(batch, seq, d),
            y_sample.reshape(n_seq, dec_seq, d),
            k_p.reshape(batch, n_pages_new, PAGE_SIZE, N_KV_HEADS, HEAD_DIM),
            v_p.reshape(batch, n_pages_new, PAGE_SIZE, N_KV_HEADS, HEAD_DIM),
            k_s.reshape(n_seq, dec_seq, N_KV_HEADS, HEAD_DIM),
            v_s.reshape(n_seq, dec_seq, N_KV_HEADS, HEAD_DIM),
            vg_sample.reshape(1, n_seq, dec_seq, -1))
```

```python
import functools
import math

import jax
import jax.numpy as jnp
from jax import lax
from jax.experimental import pallas as pl
from jax.experimental.pallas import tpu as pltpu

F32 = jnp.float32
BF16 = jnp.bfloat16
HIGHEST = lax.Precision.HIGHEST

GMLP_CHUNK = 128
GMLP_GROUPS = 8
N_HEADS = 8
N_KV_HEADS = 4
HEAD_DIM = 128
Q_PER_KV = N_HEADS // N_KV_HEADS
MOBA_BLOCK = 256
MOBA_TOP_K = 3
ROPE_THETA = 10000.0
N_GROUPS = 4
EXPERTS_PER_GROUP = 8
N_EXPERTS = N_GROUPS * EXPERTS_PER_GROUP
TOP_K_EXPERTS = 2
PAGE_SIZE = 128
EPS = 1e-6

LANES = 128
ROW_TILE = 8
TOKEN_TILE = 256
FFN_ROWS = 256
FFN_X_SLOTS = 3
PAGES_PER_STEP = 16
PAGE_SLOTS = 4
KEY_GROUP = 4
ONES_ROWS = 16
LOG2_E = math.log2(math.e)
VMEM_LIMIT = 56 * 1024 * 1024

NEG_INF = float("-inf")


def _cparams(sem):
    return pltpu.CompilerParams(dimension_semantics=sem, vmem_limit_bytes=VMEM_LIMIT)


def _rms(x):
    return x * lax.rsqrt(jnp.mean(x * x, axis=-1, keepdims=True) + EPS)


def _dot_3pass(a, b, dims):
    a_hi = a.astype(BF16)
    b_hi = b.astype(BF16)
    a_lo = (a - a_hi.astype(F32)).astype(BF16)
    b_lo = (b - b_hi.astype(F32)).astype(BF16)
    dot = functools.partial(lax.dot_general, dimension_numbers=dims, preferred_element_type=F32)
    return dot(a_hi, b_hi) + (dot(a_hi, b_lo) + dot(a_lo, b_hi))


def _dot_nt_3pass(a, b):
    return _dot_3pass(a, b, (((1,), (1,)), ((), ())))


def _prompt_spec(block, n_prompt_tiles):
    return pl.BlockSpec(block, lambda i, *_: (jnp.minimum(i, n_prompt_tiles - 1), 0))


def _sample_spec(block):
    return pl.BlockSpec(block, lambda i, *_: (0, 0))


def _is_sample_tile():
    return pl.program_id(0) == pl.num_programs(0) - 1


def _stream_tile(prompt_ref, sample_ref):
    return jnp.where(_is_sample_tile(), sample_ref[...], prompt_ref[...])


def _store_stream_tile(prompt_ref, sample_ref, store):
    @pl.when(jnp.logical_not(_is_sample_tile()))
    def _():
        store(prompt_ref)

    @pl.when(_is_sample_tile())
    def _():
        store(sample_ref)


def _gmlp_kernel(xp_ref, xs_ref, g_ref, win_ref, lng_ref, lnb_ref, mix_ref, bias_ref, wout_ref,
                 h_ref, vg_ref, *, d_gate, n_groups):
    i = pl.program_id(0)
    x = _stream_tile(xp_ref, xs_ref)
    xb = (_rms(x) * g_ref[...]).astype(BF16)
    u = jax.nn.gelu(jnp.dot(xb, win_ref[:, :d_gate], preferred_element_type=F32))
    vp = jax.nn.gelu(jnp.dot(xb, win_ref[:, d_gate:], preferred_element_type=F32))
    vc = vp - jnp.mean(vp, axis=-1, keepdims=True)
    var = jnp.mean(vc * vc, axis=-1, keepdims=True)
    vg = vc * lax.rsqrt(var + EPS) * lng_ref[...] + lnb_ref[...]

    @pl.when(i == pl.num_programs(0) - 1)
    def _():
        vg_ref[...] = vg

    vgb = vg.astype(BF16)
    cw = d_gate // n_groups
    bias = bias_ref[0]
    parts = []
    for g in range(n_groups):
        mixed = jnp.dot(mix_ref[0, g], vgb[:, g * cw:(g + 1) * cw], preferred_element_type=F32)
        mixed = mixed + bias[:, g:g + 1]
        parts.append((u[:, g * cw:(g + 1) * cw] * mixed).astype(BF16))
    gated = jnp.concatenate(parts, axis=1)
    h_ref[...] = x + jnp.dot(gated, wout_ref[...], preferred_element_type=F32)


def _gmlp_layer(x_prompt, x_sample, g, w_in, ln_g, ln_b, mix, bias, w_out):
    d = x_prompt.shape[1]
    d_gate = w_out.shape[0]
    tm = TOKEN_TILE
    n_prompt_tiles = x_prompt.shape[0] // tm
    n_tiles = n_prompt_tiles + 1
    n = n_tiles * tm
    kind = lambda i: jnp.where(i < n_prompt_tiles, 0, 1)
    return pl.pallas_call(
        functools.partial(_gmlp_kernel, d_gate=d_gate, n_groups=GMLP_GROUPS),
        grid=(n_tiles,),
        in_specs=[
            _prompt_spec((tm, d), n_prompt_tiles),
            _sample_spec((tm, d)),
            pl.BlockSpec((1, d), lambda i: (0, 0)),
            pl.BlockSpec((d, 2 * d_gate), lambda i: (0, 0)),
            pl.BlockSpec((1, d_gate), lambda i: (0, 0)),
            pl.BlockSpec((1, d_gate), lambda i: (0, 0)),
            pl.BlockSpec((1, GMLP_GROUPS, tm, tm), lambda i: (kind(i), 0, 0, 0)),
            pl.BlockSpec((1, tm, LANES), lambda i: (kind(i), 0, 0)),
            pl.BlockSpec((d_gate, d), lambda i: (0, 0)),
        ],
        out_specs=[
            pl.BlockSpec((tm, d), lambda i: (i, 0)),
            pl.BlockSpec((tm, d_gate), lambda i: (0, 0)),
        ],
        out_shape=[
            jax.ShapeDtypeStruct((n, d), F32),
            jax.ShapeDtypeStruct((tm, d_gate), F32),
        ],
        compiler_params=_cparams(("arbitrary",)),
        name="gmlp_layer",
    )(x_prompt, x_sample, g, w_in, ln_g, ln_b, mix, bias, w_out)


def _router_kernel(h_ref, g_ref, wr_ref, br_ref, xn_ref, w_ref, code_ref, hist_ref):
    xn = _rms(h_ref[...]) * g_ref[...]
    _store_row_tiles(xn_ref, xn)
    logits = _dot_3pass(xn, wr_ref[...], (((1,), (0,)), ((), ()))) + br_ref[...]
    lane = lax.broadcasted_iota(jnp.int32, logits.shape, 1)
    big = jnp.int32(LANES)
    is_grp = lane < N_GROUPS
    gl = jnp.where(is_grp, logits, NEG_INF)
    gmax = jnp.max(gl, axis=1, keepdims=True)
    gidx = jnp.min(jnp.where(is_grp & (logits == gmax), lane, big), axis=1, keepdims=True)
    p_g = 1.0 / jnp.sum(jnp.where(is_grp, jnp.exp(gl - gmax), 0.0), axis=1, keepdims=True)
    lo = N_GROUPS + gidx * EXPERTS_PER_GROUP
    in_grp = (lane >= lo) & (lane < lo + EXPERTS_PER_GROUP)
    v0 = jnp.max(jnp.where(in_grp, logits, NEG_INF), axis=1, keepdims=True)
    i0 = jnp.min(jnp.where(in_grp & (logits == v0), lane, big), axis=1, keepdims=True)
    rest = in_grp & (lane != i0)
    v1 = jnp.max(jnp.where(rest, logits, NEG_INF), axis=1, keepdims=True)
    i1 = jnp.min(jnp.where(rest & (logits == v1), lane, big), axis=1, keepdims=True)
    t = jnp.exp(v1 - v0)
    w0 = p_g * (1.0 / (1.0 + t))
    w1 = p_g * (t / (1.0 + t))
    e0 = i0 - N_GROUPS
    e1 = i1 - N_GROUPS
    w_ref[...] = jnp.where(lane == 0, w0, jnp.where(lane == 1, w1, 0.0))
    tm = logits.shape[0]
    onehot = jnp.concatenate([(lane == e0).astype(F32), (lane == e1).astype(F32)], axis=0)
    a_row = lax.broadcasted_iota(jnp.int32, (2 * tm, 2 * tm), 0)
    a_col = lax.broadcasted_iota(jnp.int32, (2 * tm, 2 * tm), 1)
    earlier = (a_col < a_row).astype(BF16)
    before = jnp.dot(earlier, onehot.astype(BF16), preferred_element_type=F32)
    rank = jnp.sum(before * onehot, axis=1, keepdims=True).astype(jnp.int32)
    code_ref[...] = jnp.where(lane == 0, rank[:tm] * N_EXPERTS + e0,
                              jnp.where(lane == 1, rank[tm:] * N_EXPERTS + e1, 0))
    hist = jnp.sum(onehot, axis=0, keepdims=True).astype(jnp.int32)
    hist_ref[...] = jnp.broadcast_to(hist, hist_ref.shape)


def _router(h, g, w_r, b_r):
    n, d = h.shape
    tm = TOKEN_TILE
    return pl.pallas_call(
        _router_kernel,
        grid=(n // tm,),
        in_specs=[
            pl.BlockSpec((tm, d), lambda i: (i, 0)),
            pl.BlockSpec((1, d), lambda i: (0, 0)),
            pl.BlockSpec((d, LANES), lambda i: (0, 0)),
            pl.BlockSpec((1, LANES), lambda i: (0, 0)),
        ],
        out_specs=[
            pl.BlockSpec((tm * ROW_TILE, LANES), lambda i: (i, 0)),
            pl.BlockSpec((tm, LANES), lambda i: (i, 0)),
            pl.BlockSpec((tm, LANES), lambda i: (i, 0)),
            pl.BlockSpec((ROW_TILE, LANES), lambda i: (i, 0)),
        ],
        out_shape=[
            jax.ShapeDtypeStruct((n * ROW_TILE, LANES), F32),
            jax.ShapeDtypeStruct((n, LANES), F32),
            jax.ShapeDtypeStruct((n, LANES), jnp.int32),
            jax.ShapeDtypeStruct((n // tm * ROW_TILE, LANES), jnp.int32),
        ],
        compiler_params=_cparams(("arbitrary",)),
        name="moe_router",
    )(h, g, w_r, b_r)


def _pos_kernel(code_ref, first_ref, pos_ref, *, tiles_per_step):
    i = pl.program_id(0)
    tm = TOKEN_TILE
    shift = N_EXPERTS.bit_length() - 1
    for s in range(tiles_per_step):
        code = code_ref[s * tm:(s + 1) * tm, :]
        expert = lax.bitwise_and(code, N_EXPERTS - 1)
        rank = lax.shift_right_logical(code, shift)
        off = first_ref[pl.ds(i * tiles_per_step + s, 1), :]
        lane = lax.broadcasted_iota(jnp.int32, code.shape, 1)
        pos = [jnp.sum(jnp.where(lane == expert[:, k:k + 1], off, 0), axis=1, keepdims=True) + rank[:, k:k + 1]
               for k in range(TOP_K_EXPERTS)]
        pos_ref[s * tm:(s + 1) * tm, :] = jnp.where(lane == 0, pos[0], jnp.where(lane == 1, pos[1], 0))


def _sorted_positions(code_pad, first):
    n = code_pad.shape[0]
    n_tiles = n // TOKEN_TILE
    tiles_per_step = max(t for t in range(1, 17) if n_tiles % t == 0)
    rows = tiles_per_step * TOKEN_TILE
    return pl.pallas_call(
        functools.partial(_pos_kernel, tiles_per_step=tiles_per_step),
        grid=(n_tiles // tiles_per_step,),
        in_specs=[
            pl.BlockSpec((rows, LANES), lambda i: (i, 0)),
            pl.BlockSpec(first.shape, lambda i: (0, 0)),
        ],
        out_specs=pl.BlockSpec((rows, LANES), lambda i: (i, 0)),
        out_shape=jax.ShapeDtypeStruct((n, LANES), jnp.int32),
        compiler_params=_cparams(("arbitrary",)),
        name="moe_pos",
    )(code_pad, first)


def _dispatch_kernel(pos_ref, gap_ref, xn_ref, out_hbm, zero, sem, zsem):
    i = pl.program_id(0)
    tm = xn_ref.shape[0] // ROW_TILE
    base = i * tm * TOP_K_EXPERTS
    block_rows = zero.shape[0] // ROW_TILE
    n_blocks = out_hbm.shape[0] // zero.shape[0]

    @pl.when(i == 0)
    def _():
        zero[...] = jnp.zeros_like(zero)

        def each_gap(visit):
            def gap(e, carry):
                start = gap_ref[2 * e]

                def row(r, c):
                    visit(pltpu.make_async_copy(_row_tile(zero, 0), _row_tile(out_hbm, start + r), zsem))
                    return c

                lax.fori_loop(0, gap_ref[2 * e + 1], row, 0)
                return carry

            lax.fori_loop(0, N_EXPERTS, gap, 0)

            def unused_block(b, carry):
                first = pl.multiple_of(b * block_rows * ROW_TILE, ROW_TILE)
                visit(pltpu.make_async_copy(zero, out_hbm.at[pl.ds(first, block_rows * ROW_TILE), :], zsem))
                return carry

            lax.fori_loop(gap_ref[2 * N_EXPERTS], n_blocks, unused_block, 0)

        each_gap(lambda copy: copy.start())
        each_gap(lambda copy: copy.wait())

    def issue(t, carry):
        for k in range(TOP_K_EXPERTS):
            p = pos_ref[base + t * TOP_K_EXPERTS + k]
            pltpu.make_async_copy(_row_tile(xn_ref, t), _row_tile(out_hbm, p), sem).start(priority=k)
        return carry

    lax.fori_loop(0, tm, issue, 0, unroll=4)
    for k in range(TOP_K_EXPERTS):
        pltpu.make_async_copy(xn_ref, out_hbm.at[pl.ds(0, tm * ROW_TILE), :], sem).wait()


def _dispatch(xn, pos, gaps, n_rows):
    tm = TOKEN_TILE
    n = xn.shape[0] // ROW_TILE
    grid_spec = pltpu.PrefetchScalarGridSpec(
        num_scalar_prefetch=2,
        grid=(n // tm,),
        in_specs=[pl.BlockSpec((tm * ROW_TILE, LANES), lambda i, ps, gp: (i, 0))],
        out_specs=pl.BlockSpec(memory_space=pl.ANY),
        scratch_shapes=[pltpu.VMEM((FFN_ROWS * ROW_TILE, LANES), F32), pltpu.SemaphoreType.DMA(()),
                        pltpu.SemaphoreType.DMA(())],
    )
    return pl.pallas_call(
        _dispatch_kernel,
        grid_spec=grid_spec,
        out_shape=jax.ShapeDtypeStruct((n_rows * ROW_TILE, LANES), F32),
        compiler_params=_cparams(("arbitrary",)),
        name="moe_dispatch",
    )(pos, gaps, xn)


def _store_row_tiles(ref, x, first=0):
    for c in range(ROW_TILE):
        ref[pl.ds(first + c, x.shape[0], stride=ROW_TILE), :] = x[:, c * LANES:(c + 1) * LANES]


def _load_row_tiles(ref, first, rows, lead=()):
    return jnp.concatenate([ref[lead + (pl.ds(first + c, rows, stride=ROW_TILE), slice(None))]
                            for c in range(ROW_TILE)], axis=1)


def _row_tile(ref, r):
    return ref.at[pl.ds(pl.multiple_of(r * ROW_TILE, ROW_TILE), ROW_TILE), :]


def _ffn_kernel(blk_e_ref, nblk_ref, fresh_ref, run_ref, next_e_ref, x_hbm, wg_hbm, wu_hbm, wd_hbm, out_ref,
                xbuf, xsems, wg_f, wu_f, wd_f, wsems, wg_s, wu_s, wd_s, *, layer):
    i = pl.program_id(0)
    nblk = nblk_ref[0]
    tile_rows = out_ref.shape[0]
    rows = tile_rows // ROW_TILE
    live = i < nblk

    def x_copy(block, slot):
        first = pl.multiple_of(block * tile_rows, tile_rows)
        return pltpu.make_async_copy(x_hbm.at[pl.ds(first, tile_rows), :], xbuf.at[slot], xsems.at[slot])

    def w_copies(expert, slot):
        return [pltpu.make_async_copy(src.at[layer, expert], dst.at[slot], wsems.at[slot])
                for src, dst in ((wg_hbm, wg_f), (wu_hbm, wu_f), (wd_hbm, wd_f))]

    @pl.when(i == 0)
    def _():
        for b in range(FFN_X_SLOTS - 1):
            @pl.when(b < nblk)
            def _():
                x_copy(b, b).start()

        @pl.when(nblk > 0)
        def _():
            for copy in w_copies(blk_e_ref[0], 0):
                copy.start()

    ahead = i + (FFN_X_SLOTS - 1)

    @pl.when(ahead < nblk)
    def _():
        x_copy(ahead, lax.rem(ahead, FFN_X_SLOTS)).start()

    @pl.when(live & (fresh_ref[i] == 1))
    def _():
        slot = lax.rem(run_ref[i], 2)
        for copy in w_copies(0, slot):
            copy.wait()
        wg_s[...] = wg_f[slot].astype(BF16)
        wu_s[...] = wu_f[slot].astype(BF16)
        wd_s[...] = wd_f[slot].astype(BF16)

        @pl.when(next_e_ref[i] >= 0)
        def _():
            for copy in w_copies(next_e_ref[i], 1 - slot):
                copy.start()

    @pl.when(live)
    def _():
        slot = lax.rem(i, FFN_X_SLOTS)
        x_copy(0, slot).wait()
        x = _load_row_tiles(xbuf, 0, rows, lead=(slot,)).astype(BF16)
        gate = jnp.dot(x, wg_s[...], preferred_element_type=F32)
        up = jnp.dot(x, wu_s[...], preferred_element_type=F32)
        hid = (jax.nn.silu(gate) * up).astype(BF16)
        _store_row_tiles(out_ref, jnp.dot(hid, wd_s[...], preferred_element_type=F32))

    @pl.when(jnp.logical_not(live))
    def _():
        out_ref[...] = jnp.zeros_like(out_ref)


def _grouped_ffn(x_sorted, blk_e, nblk, fresh, run, next_e, w_gate, w_up, w_down, layer):
    rb = FFN_ROWS
    n_rows = x_sorted.shape[0] // ROW_TILE
    _, _, d, d_e = w_gate.shape
    assert d == ROW_TILE * LANES
    any_spec = pl.BlockSpec(memory_space=pl.ANY)
    grid_spec = pltpu.PrefetchScalarGridSpec(
        num_scalar_prefetch=5,
        grid=(n_rows // rb,),
        in_specs=[any_spec, any_spec, any_spec, any_spec],
        out_specs=pl.BlockSpec((rb * ROW_TILE, LANES), lambda i, *_: (i, 0)),
        scratch_shapes=[
            pltpu.VMEM((FFN_X_SLOTS, rb * ROW_TILE, LANES), F32), pltpu.SemaphoreType.DMA((FFN_X_SLOTS,)),
            pltpu.VMEM((2, d, d_e), F32), pltpu.VMEM((2, d, d_e), F32), pltpu.VMEM((2, d_e, d), F32),
            pltpu.SemaphoreType.DMA((2,)),
            pltpu.VMEM((d, d_e), BF16), pltpu.VMEM((d, d_e), BF16), pltpu.VMEM((d_e, d), BF16),
        ],
    )
    return pl.pallas_call(
        functools.partial(_ffn_kernel, layer=layer),
        grid_spec=grid_spec,
        out_shape=jax.ShapeDtypeStruct((n_rows * ROW_TILE, LANES), F32),
        compiler_params=_cparams(("arbitrary",)),
        name="moe_ffn",
    )(blk_e, nblk, fresh, run, next_e, x_sorted, w_gate, w_up, w_down)


def _combine_kernel(pos_ref, resid_ref, w_ref, src_hbm, *refs, split):
    out_refs, buf, sems = refs[:-2], refs[-2], refs[-1]
    i = pl.program_id(0)
    tm = resid_ref.shape[0]

    def gather(tile, slot):
        base = tile * tm * TOP_K_EXPERTS

        def issue(t, carry):
            for k in range(TOP_K_EXPERTS):
                p = pos_ref[base + t * TOP_K_EXPERTS + k]
                pltpu.make_async_copy(_row_tile(src_hbm, p), _row_tile(buf, (slot * TOP_K_EXPERTS + k) * tm + t),
                                      sems.at[slot]).start(priority=k)
            return carry

        lax.fori_loop(0, tm, issue, 0, unroll=4)

    @pl.when(i == 0)
    def _():
        gather(0, 0)

    @pl.when(i + 1 < pl.num_programs(0))
    def _():
        gather(i + 1, (i + 1) % 2)

    slot = i % 2
    firsts = [pl.multiple_of((slot * TOP_K_EXPERTS + k) * tm * ROW_TILE, ROW_TILE) for k in range(TOP_K_EXPERTS)]
    for first in firsts:
        pltpu.make_async_copy(src_hbm.at[pl.ds(0, tm * ROW_TILE), :],
                              buf.at[pl.ds(first, tm * ROW_TILE), :], sems.at[slot]).wait()
    acc = resid_ref[...]
    w = w_ref[...]
    for k, first in enumerate(firsts):
        acc = acc + w[:, k:k + 1] * _load_row_tiles(buf, first, tm)
    def put(ref):
        ref[...] = acc

    if split:
        _store_stream_tile(out_refs[0], out_refs[1], put)
    else:
        put(out_refs[0])


def _combine(resid, w_pad, src, pos, split):
    n, d = resid.shape
    assert d == ROW_TILE * LANES and TOP_K_EXPERTS == 2
    tm = TOKEN_TILE
    n_prompt_tiles = n // tm - 1
    if split:
        out_specs = [_prompt_spec((tm, d), n_prompt_tiles), _sample_spec((tm, d))]
        out_shape = [jax.ShapeDtypeStruct((n_prompt_tiles * tm, d), F32), jax.ShapeDtypeStruct((tm, d), F32)]
    else:
        out_specs = [pl.BlockSpec((tm, d), lambda i, *_: (i, 0))]
        out_shape = [jax.ShapeDtypeStruct((n, d), F32)]
    grid_spec = pltpu.PrefetchScalarGridSpec(
        num_scalar_prefetch=1,
        grid=(n // tm,),
        in_specs=[
            pl.BlockSpec((tm, d), lambda i, *_: (i, 0)),
            pl.BlockSpec((tm, LANES), lambda i, *_: (i, 0)),
            pl.BlockSpec(memory_space=pl.ANY),
        ],
        out_specs=out_specs,
        scratch_shapes=[pltpu.VMEM((2 * TOP_K_EXPERTS * tm * ROW_TILE, LANES), F32),
                        pltpu.SemaphoreType.DMA((2,))],
    )
    return pl.pallas_call(
        functools.partial(_combine_kernel, split=split),
        grid_spec=grid_spec,
        out_shape=out_shape,
        compiler_params=_cparams(("arbitrary",)),
        name="moe_combine",
    )(pos, resid, w_pad, src)


def _hier_moe(h, g, w_r, b_r, w_gate, w_up, w_down, layer, split=False):
    assert N_EXPERTS & (N_EXPERTS - 1) == 0
    n, _ = h.shape
    rb = FFN_ROWS
    xn, w_pad, code_pad, hist_pad = _router(h, g, w_r, b_r)
    hist = hist_pad[::ROW_TILE, :N_EXPERTS]
    counts = jnp.sum(hist, axis=0)
    padded = (counts + rb - 1) // rb * rb
    pend = jnp.cumsum(padded)
    first = ((pend - padded)[None, :] + jnp.cumsum(hist, axis=0) - hist).astype(jnp.int32)
    first = jnp.pad(first, ((0, 0), (0, LANES - N_EXPERTS)))
    pos = _sorted_positions(code_pad, first)[:, :TOP_K_EXPERTS].reshape(-1)
    n_blocks = (n * TOP_K_EXPERTS + N_EXPERTS * (rb - 1) + rb - 1) // rb
    block_first = jnp.arange(n_blocks, dtype=jnp.int32)[:, None] * rb
    blk_e = jnp.minimum(jnp.sum((pend[None, :] <= block_first).astype(jnp.int32), axis=1), N_EXPERTS - 1)
    fresh = jnp.concatenate([jnp.ones((1,), jnp.int32), (blk_e[1:] != blk_e[:-1]).astype(jnp.int32)])
    run = (jnp.cumsum(fresh) - 1).astype(jnp.int32)
    experts = jnp.arange(N_EXPERTS, dtype=jnp.int32)
    later = jnp.where((counts[None, :] > 0) & (experts[None, :] > experts[:, None]), experts[None, :], N_EXPERTS)
    next_owner = jnp.min(later, axis=1)
    next_e = jnp.where(next_owner < N_EXPERTS, next_owner, -1)[blk_e].astype(jnp.int32)
    nblk = (pend[-1:] // rb).astype(jnp.int32)
    gaps = jnp.concatenate([jnp.stack([pend - padded + counts, padded - counts], axis=1).reshape(-1),
                            nblk]).astype(jnp.int32)
    x_sorted = _dispatch(xn, pos, gaps, n_blocks * rb)
    out_sorted = _grouped_ffn(x_sorted, blk_e, nblk, fresh, run, next_e, w_gate, w_up, w_down, layer)
    out = _combine(h, w_pad, out_sorted, pos, split)
    return out if split else out[0]


def _store_heads_as_rows(ref, x):
    for j in range(N_KV_HEADS):
        ref[pl.ds(j, x.shape[0], stride=N_KV_HEADS), :] = x[:, j * HEAD_DIM:(j + 1) * HEAD_DIM]


def _proj_kernel(h_ref, gkv_ref, gq_ref, wkv_ref, wq_ref, kn_ref, qn_ref, cos_ref, sin_ref,
                 k_ref, v_ref, q_ref, kp_ref, ks_ref, vp_ref, vs_ref):
    hn = _rms(h_ref[...])
    cos = cos_ref[...]
    sin = sin_ref[...]

    def norm_rope(x, g):
        y = _rms(x) * g
        return y * cos + pltpu.roll(y, HEAD_DIM // 2, 1) * sin

    kv = jnp.dot((hn * gkv_ref[...]).astype(BF16), wkv_ref[...], preferred_element_type=F32)
    kw = N_KV_HEADS * HEAD_DIM
    k = jnp.concatenate(
        [norm_rope(kv[:, j * HEAD_DIM:(j + 1) * HEAD_DIM], kn_ref[...]) for j in range(N_KV_HEADS)], axis=1)
    v = kv[:, kw:]
    k_ref[...] = k
    v_ref[...] = v
    _store_stream_tile(kp_ref, ks_ref, lambda ref: _store_heads_as_rows(ref, k))
    _store_stream_tile(vp_ref, vs_ref, lambda ref: _store_heads_as_rows(ref, v))
    q = jnp.dot((hn * gq_ref[...]).astype(BF16), wq_ref[...], preferred_element_type=F32)
    q_ref[...] = jnp.concatenate(
        [norm_rope(q[:, j * HEAD_DIM:(j + 1) * HEAD_DIM], qn_ref[...]) for j in range(N_HEADS)], axis=1)


def _kvq_proj(h, g_kv, g_q, w_kv, w_q, k_norm, q_norm, cos, sin):
    n, d = h.shape
    tm = TOKEN_TILE
    kw = N_KV_HEADS * HEAD_DIM
    qw = N_HEADS * HEAD_DIM
    row = lambda i: (i, 0)
    fixed = lambda i: (0, 0)
    n_prompt_tiles = n // tm - 1
    head_rows = tm * N_KV_HEADS
    by_head = [_prompt_spec((head_rows, HEAD_DIM), n_prompt_tiles), _sample_spec((head_rows, HEAD_DIM))]
    by_head_shapes = [jax.ShapeDtypeStruct((n_prompt_tiles * head_rows, HEAD_DIM), F32),
                      jax.ShapeDtypeStruct((head_rows, HEAD_DIM), F32)]
    return pl.pallas_call(
        _proj_kernel,
        grid=(n // tm,),
        in_specs=[
            pl.BlockSpec((tm, d), row),
            pl.BlockSpec((1, d), fixed),
            pl.BlockSpec((1, d), fixed),
            pl.BlockSpec((d, 2 * kw), fixed),
            pl.BlockSpec((d, qw), fixed),
            pl.BlockSpec((1, HEAD_DIM), fixed),
            pl.BlockSpec((1, HEAD_DIM), fixed),
            pl.BlockSpec((tm, HEAD_DIM), row),
            pl.BlockSpec((tm, HEAD_DIM), row),
        ],
        out_specs=[
            pl.BlockSpec((tm, kw), row),
            pl.BlockSpec((tm, kw), row),
            pl.BlockSpec((tm, qw), row),
        ] + by_head + by_head,
        out_shape=[
            jax.ShapeDtypeStruct((n, kw), F32),
            jax.ShapeDtypeStruct((n, kw), F32),
            jax.ShapeDtypeStruct((n, qw), F32),
        ] + by_head_shapes + by_head_shapes,
        compiler_params=_cparams(("arbitrary",)),
        name="kvq_proj",
    )(h, g_kv, g_q, w_kv, w_q, k_norm, q_norm, cos, sin)


def _top_blocks(gate, axis):
    idx = lax.broadcasted_iota(jnp.int32, gate.shape, axis)
    big = jnp.int32(gate.shape[axis])
    sel = jnp.zeros(gate.shape, jnp.bool_)
    for _ in range(MOBA_TOP_K):
        top = jnp.max(gate, axis=axis, keepdims=True)
        first = jnp.min(jnp.where(gate == top, idx, big), axis=axis, keepdims=True)
        hit = idx == first
        sel = sel | (hit & (top > NEG_INF))
        gate = jnp.where(hit, NEG_INF, gate)
    return sel.astype(F32)


def _moba_prompt_kernel(q_ref, k_ref, v_ref, o_ref, kbf, vt, kmean, sel, s_a, s_b, m_scr, acc):
    j = pl.program_id(2)
    blk = MOBA_BLOCK
    grp = KEY_GROUP
    seq = k_ref.shape[0]
    n_blocks = seq // blk

    @pl.when(j == 0)
    def _():
        k = k_ref[...]
        kbf[...] = k.astype(BF16)
        kmean[...] = jnp.mean(k.reshape(n_blocks, blk, HEAD_DIM), axis=1)
        for n in range(n_blocks):
            vt[:HEAD_DIM, n * blk:(n + 1) * blk] = v_ref[n * blk:(n + 1) * blk, :].T.astype(BF16)
        r = lax.broadcasted_iota(jnp.int32, (ONES_ROWS, seq), 0)
        vt[HEAD_DIM:, :] = jnp.where(r == 0, 1.0, 0.0).astype(BF16)

    q2 = q_ref[...]
    qs = jnp.concatenate([q2[:, h * HEAD_DIM:(h + 1) * HEAD_DIM] for h in range(Q_PER_KV)], axis=0)
    nq = qs.shape[0]
    qt = (qs * (HEAD_DIM ** -0.5 * LOG2_E)).T.astype(BF16)

    def scores(start, n_keys):
        return jnp.dot(kbf[pl.ds(start, n_keys), :], qt, preferred_element_type=F32)

    key = lax.broadcasted_iota(jnp.int32, (blk, nq), 0)
    qpos = lax.broadcasted_iota(jnp.int32, (blk, nq), 1) % blk
    own = pl.multiple_of(j * blk, blk)
    s_own = jnp.where(key <= qpos, scores(own, blk), NEG_INF)
    m_own = jnp.max(s_own, axis=0, keepdims=True)
    m_scr[...] = m_own
    acc[...] = jnp.dot(vt[:, pl.ds(own, blk)], jnp.exp2(s_own - m_own).astype(BF16), preferred_element_type=F32)

    gate = _dot_nt_3pass(kmean[...], qs)
    row = lax.broadcasted_iota(jnp.int32, gate.shape, 0)
    sel[...] = _top_blocks(jnp.where(row < j, gate, NEG_INF), 0)

    def update(tiles, start):
        m_old = m_scr[...]
        m_new = m_old
        for s in tiles:
            m_new = jnp.maximum(m_new, jnp.max(s, axis=0, keepdims=True))
        p = jnp.concatenate([jnp.exp2(s - m_new).astype(BF16) for s in tiles], axis=0)
        alpha = jnp.exp2(m_old - m_new)
        pv = jnp.dot(vt[:, pl.ds(start, len(tiles) * blk)], p, preferred_element_type=F32)
        acc[...] = alpha * acc[...] + pv
        m_scr[...] = m_new

    def fill(buf, g):
        start = pl.multiple_of(g * (grp * blk), grp * blk)
        buf[...] = scores(start, grp * blk)

    def consume(buf, g):
        start = pl.multiple_of(g * (grp * blk), grp * blk)
        update([jnp.where(sel[pl.ds(g * grp + i, 1), :] > 0.0, buf[i * blk:(i + 1) * blk, :], NEG_INF)
                for i in range(grp)], start)

    n_groups = lax.div(j + (grp - 1), grp)
    n_pairs = lax.div(n_groups - 1, 2)
    fill(s_a, 0)

    @pl.when(n_groups > 0)
    def _():
        def pair(h, carry):
            g = 2 * h
            fill(s_b, g + 1)
            consume(s_a, g)
            fill(s_a, g + 2)
            consume(s_b, g + 1)
            return carry

        lax.fori_loop(0, n_pairs, pair, 0)
        g = 2 * n_pairs

        @pl.when(n_groups - g == 1)
        def _():
            consume(s_a, g)

        @pl.when(n_groups - g == 2)
        def _():
            fill(s_b, g + 1)
            consume(s_a, g)
            consume(s_b, g + 1)

    a = acc[...]
    o = (a[:HEAD_DIM] / a[HEAD_DIM:HEAD_DIM + 1]).T
    o_ref[...] = jnp.concatenate([o[h * blk:(h + 1) * blk, :] for h in range(Q_PER_KV)], axis=1)


def _moba_prompt(q, k, v, batch, seq):
    blk = MOBA_BLOCK
    nqb = seq // blk
    qw = Q_PER_KV * HEAD_DIM
    nq = Q_PER_KV * blk
    return pl.pallas_call(
        _moba_prompt_kernel,
        grid=(batch, N_KV_HEADS, nqb),
        in_specs=[
            pl.BlockSpec((blk, qw), lambda b, c, j: (b * nqb + j, c)),
            pl.BlockSpec((seq, HEAD_DIM), lambda b, c, j: (b, c)),
            pl.BlockSpec((seq, HEAD_DIM), lambda b, c, j: (b, c)),
        ],
        out_specs=pl.BlockSpec((blk, qw), lambda b, c, j: (b * nqb + j, c)),
        out_shape=jax.ShapeDtypeStruct((batch * seq, N_HEADS * HEAD_DIM), F32),
        scratch_shapes=[
            pltpu.VMEM((seq, HEAD_DIM), BF16),
            pltpu.VMEM((HEAD_DIM + ONES_ROWS, seq), BF16),
            pltpu.VMEM((nqb, HEAD_DIM), F32),
            pltpu.VMEM((nqb, nq), F32),
            pltpu.VMEM((KEY_GROUP * blk, nq), F32),
            pltpu.VMEM((KEY_GROUP * blk, nq), F32),
            pltpu.VMEM((1, nq), F32),
            pltpu.VMEM((HEAD_DIM + ONES_ROWS, nq), F32),
        ],
        compiler_params=_cparams(("arbitrary", "arbitrary", "arbitrary")),
        name="moba_prompt",
    )(q, k, v)


def _stack_heads(q8):
    return jnp.concatenate([q8[:, h * HEAD_DIM:(h + 1) * HEAD_DIM] for h in range(N_HEADS)], axis=0)


def _sample_attn_kernel(pt_ref, q_ref, kn_ref, vn_ref, ck_hbm, cv_hbm, o_ref, pages, sems,
                        s_scr, means, sel_scr, qs_scr, m_scr, l_scr, acc, *, dec_seq, n_pages):
    pps = PAGES_PER_STEP
    t = pl.program_id(1)
    steps = pl.num_programs(1)
    n_k_steps = n_pages // pps
    rows = N_HEADS * dec_seq
    rkv = Q_PER_KV * dec_seq
    ppb = MOBA_BLOCK // PAGE_SIZE
    bps = pps // ppb
    n_blocks = n_pages // ppb
    nt_dims = (((1,), (1,)), ((), ()))

    chunk = pl.program_id(0) * steps + t
    n_chunks = pl.num_programs(0) * steps

    def start_chunk(ci):
        seq = lax.div(ci, steps)
        step = lax.rem(ci, steps)
        slot = lax.rem(ci, PAGE_SLOTS)

        def start_pages(cache_hbm, first_page):
            for r in range(pps):
                pltpu.make_async_copy(cache_hbm.at[pt_ref[seq, first_page + r]], pages.at[slot, r],
                                      sems.at[slot]).start()

        @pl.when(step < n_k_steps)
        def _():
            start_pages(ck_hbm, step * pps)

        @pl.when(step >= n_k_steps)
        def _():
            start_pages(cv_hbm, (step - n_k_steps) * pps)

    @pl.when(chunk == 0)
    def _():
        for ci in range(PAGE_SLOTS - 1):
            start_chunk(jnp.int32(ci))

    @pl.when(chunk + (PAGE_SLOTS - 1) < n_chunks)
    def _():
        start_chunk(chunk + (PAGE_SLOTS - 1))

    slot = lax.rem(chunk, PAGE_SLOTS)
    pltpu.make_async_copy(ck_hbm.at[pl.ds(0, pps)], pages.at[slot], sems.at[slot]).wait()

    def head_rows(p, c):
        return pages[slot, p, pl.ds(c, PAGE_SIZE, stride=N_KV_HEADS), :]

    @pl.when(t == 0)
    def _():
        qs_scr[...] = (_stack_heads(q_ref[...]) * (HEAD_DIM ** -0.5 * LOG2_E)).astype(BF16)
        m_scr[...] = jnp.full(m_scr.shape, NEG_INF, F32)
        l_scr[...] = jnp.zeros(l_scr.shape, F32)
        acc[...] = jnp.zeros(acc.shape, F32)

    @pl.when(t < n_k_steps)
    def _():
        qs = qs_scr[...]
        col = pl.multiple_of(t * (pps * PAGE_SIZE), pps * PAGE_SIZE)
        for c in range(N_KV_HEADS):
            kc = jnp.concatenate([head_rows(p, c) for p in range(pps)], axis=0)
            s_scr[c * rkv:(c + 1) * rkv, pl.ds(col, pps * PAGE_SIZE)] = lax.dot_general(
                qs[c * rkv:(c + 1) * rkv], kc.astype(BF16), nt_dims, preferred_element_type=F32)
            means[c, pl.ds(t * bps, bps), :] = jnp.sum(kc.reshape(bps, MOBA_BLOCK, HEAD_DIM), axis=1) / MOBA_BLOCK

    @pl.when(t == n_k_steps - 1)
    def _():
        qf = _stack_heads(q_ref[...])
        gate = jnp.concatenate(
            [lax.dot_general(qf[c * rkv:(c + 1) * rkv], means[c], nt_dims, precision=HIGHEST,
                             preferred_element_type=F32) for c in range(N_KV_HEADS)], axis=0)
        chosen = _top_blocks(gate, 1)
        sel_scr[...] = jnp.concatenate([chosen, jnp.zeros((rows, LANES - n_blocks), F32)], axis=1)

    def softmax_step(tiles):
        m_old = m_scr[...]
        m_new = m_old
        for s in tiles:
            m_new = jnp.maximum(m_new, jnp.max(s, axis=1, keepdims=True))
        m_safe = jnp.where(m_new == NEG_INF, 0.0, m_new)
        alpha = jnp.exp2(m_old - m_safe)
        l_new = alpha * l_scr[...]
        probs = []
        for s in tiles:
            p = jnp.exp2(s - m_safe)
            l_new = l_new + jnp.sum(p, axis=1, keepdims=True)
            probs.append(p.astype(BF16))
        l_scr[...] = l_new
        m_scr[...] = m_new
        return probs, alpha

    @pl.when(t >= n_k_steps)
    def _():
        tv = t - n_k_steps
        sel = sel_scr[...]
        lane = lax.broadcasted_iota(jnp.int32, sel.shape, 1)
        tiles = []
        for b in range(bps):
            n = tv * bps + b
            chosen = jnp.sum(jnp.where(lane == n, sel, 0.0), axis=1, keepdims=True) > 0.0
            col = pl.multiple_of(n * MOBA_BLOCK, MOBA_BLOCK)
            tiles.append(jnp.where(chosen, s_scr[:, pl.ds(col, MOBA_BLOCK)], NEG_INF))
        probs, alpha = softmax_step(tiles)
        prob = jnp.concatenate(probs, axis=1)
        pv = []
        for c in range(N_KV_HEADS):
            vc = jnp.concatenate([head_rows(p, c) for p in range(pps)], axis=0).astype(BF16)
            pv.append(jnp.dot(prob[c * rkv:(c + 1) * rkv], vc, preferred_element_type=F32))
        acc[...] = alpha * acc[...] + jnp.concatenate(pv, axis=0)

    @pl.when(t == pl.num_programs(1) - 1)
    def _():
        qs = qs_scr[...]
        kn = kn_ref[...].astype(BF16)
        vn = vn_ref[...].astype(BF16)
        s = jnp.concatenate(
            [lax.dot_general(qs[c * rkv:(c + 1) * rkv], kn[:, c * HEAD_DIM:(c + 1) * HEAD_DIM], nt_dims,
                             preferred_element_type=F32) for c in range(N_KV_HEADS)], axis=0)
        r2 = lax.broadcasted_iota(jnp.int32, s.shape, 0)
        c2 = lax.broadcasted_iota(jnp.int32, s.shape, 1)
        probs, alpha = softmax_step([jnp.where(c2 <= r2 % dec_seq, s, NEG_INF)])
        pv = [jnp.dot(probs[0][c * rkv:(c + 1) * rkv], vn[:, c * HEAD_DIM:(c + 1) * HEAD_DIM],
                      preferred_element_type=F32) for c in range(N_KV_HEADS)]
        o = (alpha * acc[...] + jnp.concatenate(pv, axis=0)) / l_scr[...]
        o_ref[...] = jnp.concatenate([o[h * dec_seq:(h + 1) * dec_seq, :] for h in range(N_HEADS)], axis=1)


def _sample_attn(page_table, q, k, v, row0, cache_k2, cache_v2, dec_seq):
    n_seq, n_pages = page_table.shape
    pps = PAGES_PER_STEP
    n_k_steps = n_pages // pps
    n_blocks = n_pages * PAGE_SIZE // MOBA_BLOCK
    assert n_pages % pps == 0 and n_blocks <= LANES
    rows = N_HEADS * dec_seq
    kw = N_KV_HEADS * HEAD_DIM
    qw = N_HEADS * HEAD_DIM
    grid_spec = pltpu.PrefetchScalarGridSpec(
        num_scalar_prefetch=1,
        grid=(n_seq, 2 * n_k_steps),
        in_specs=[
            pl.BlockSpec((dec_seq, qw), lambda s, t, pt: (row0 + s, 0)),
            pl.BlockSpec((dec_seq, kw), lambda s, t, pt: (row0 + s, 0)),
            pl.BlockSpec((dec_seq, kw), lambda s, t, pt: (row0 + s, 0)),
            pl.BlockSpec(memory_space=pl.ANY),
            pl.BlockSpec(memory_space=pl.ANY),
        ],
        out_specs=pl.BlockSpec((dec_seq, qw), lambda s, t, pt: (s, 0)),
        scratch_shapes=[
            pltpu.VMEM((PAGE_SLOTS, pps, PAGE_SIZE * N_KV_HEADS, HEAD_DIM), F32),
            pltpu.SemaphoreType.DMA((PAGE_SLOTS,)),
            pltpu.VMEM((rows, n_pages * PAGE_SIZE), F32),
            pltpu.VMEM((N_KV_HEADS, n_blocks, HEAD_DIM), F32),
            pltpu.VMEM((rows, LANES), F32),
            pltpu.VMEM((rows, HEAD_DIM), BF16),
            pltpu.VMEM((rows, 1), F32),
            pltpu.VMEM((rows, 1), F32),
            pltpu.VMEM((rows, HEAD_DIM), F32),
        ],
    )
    return pl.pallas_call(
        functools.partial(_sample_attn_kernel, dec_seq=dec_seq, n_pages=n_pages),
        grid_spec=grid_spec,
        out_shape=jax.ShapeDtypeStruct((n_seq * dec_seq, qw), F32),
        compiler_params=_cparams(("arbitrary", "arbitrary")),
        name="sample_attn",
    )(page_table, q, k, v, cache_k2, cache_v2)


def _oproj_kernel(h_ref, op_ref, os_ref, w_ref, out_ref):
    o = _stream_tile(op_ref, os_ref).astype(BF16)
    out_ref[...] = h_ref[...] + jnp.dot(o, w_ref[...], preferred_element_type=F32)


def _oproj(h, o_prompt, o_sample, w_o):
    n, d = h.shape
    tm = TOKEN_TILE
    ow = o_prompt.shape[1]
    return pl.pallas_call(
        _oproj_kernel,
        grid=(n // tm,),
        in_specs=[
            pl.BlockSpec((tm, d), lambda i: (i, 0)),
            _prompt_spec((tm, ow), n // tm - 1),
            _sample_spec((tm, ow)),
            pl.BlockSpec(w_o.shape, lambda i: (0, 0)),
        ],
        out_specs=pl.BlockSpec((tm, d), lambda i: (i, 0)),
        out_shape=jax.ShapeDtypeStruct((n, d), F32),
        compiler_params=_cparams(("arbitrary",)),
        name="attn_oproj",
    )(h, o_prompt, o_sample, w_o)


def _mix_tables(w_s, b_s, dec_seq):
    tm = TOKEN_TILE
    causal = jnp.tril(jnp.ones((GMLP_CHUNK, GMLP_CHUNK), dtype=bool))
    w = jnp.where(causal[None], w_s, jnp.zeros_like(w_s))
    eye_p = jnp.eye(tm // GMLP_CHUNK, dtype=w.dtype)
    mix_p = jnp.einsum("ab,gts->gatbs", eye_p, w).reshape(GMLP_GROUPS, tm, tm)
    eye_s = jnp.eye(tm // dec_seq, dtype=w.dtype)
    mix_s = jnp.einsum("ab,gts->gatbs", eye_s, w[:, :dec_seq, :dec_seq]).reshape(GMLP_GROUPS, tm, tm)
    mix = jnp.stack([mix_p, mix_s]).astype(BF16)
    bias_p = jnp.tile(b_s.T, (tm // GMLP_CHUNK, 1))
    bias_s = jnp.tile(b_s.T[:dec_seq], (tm // dec_seq, 1))
    bias = jnp.stack([bias_p, bias_s])
    bias = jnp.pad(bias, ((0, 0), (0, 0), (0, LANES - GMLP_GROUPS)))
    return mix, bias


def _rope_tables(pos):
    half = HEAD_DIM // 2
    inv = ROPE_THETA ** (-jnp.arange(half, dtype=F32) * 2.0 / HEAD_DIM)
    ang = pos.astype(F32)[:, None] * inv[None, :]
    cos = jnp.cos(ang)
    sin = jnp.sin(ang)
    return jnp.concatenate([cos, cos], axis=1), jnp.concatenate([-sin, sin], axis=1)


def _router_tables(w_grp, b_grp, w_rt, b_rt):
    w = jnp.concatenate([w_grp, w_rt], axis=1)
    b = jnp.concatenate([b_grp, b_rt], axis=0)
    pad = LANES - w.shape[1]
    return jnp.pad(w, ((0, 0), (0, pad))), jnp.pad(b, (0, pad)).reshape(1, LANES)


def kernel(x_prompt, x_sample, cache_k, cache_v, page_table, norm_mix, norm_ffn, a_w_in, a_ln_g, a_ln_b,
           a_w_s, a_b_s, a_w_out, kv_norm, w_kv, k_norm, b_w_q, b_q_norm, b_w_o, moe_w_grp, moe_b_grp,
           moe_w_rt, moe_b_rt, moe_w_gate, moe_w_up, moe_w_down):
    batch, seq, d = x_prompt.shape
    n_seq, dec_seq, _ = x_sample.shape
    n_prompt = batch * seq
    n_sample = n_seq * dec_seq
    assert n_prompt % TOKEN_TILE == 0 and n_sample == TOKEN_TILE and seq % MOBA_BLOCK == 0
    past_len = page_table.shape[1] * PAGE_SIZE
    assert past_len % MOBA_BLOCK == 0 and dec_seq <= MOBA_BLOCK

    pos = jnp.concatenate([jnp.tile(jnp.arange(seq), batch), jnp.tile(past_len + jnp.arange(dec_seq), n_seq)])
    cos, sin = _rope_tables(pos)
    row = lambda a: a.reshape(1, -1)

    mix, bias = _mix_tables(a_w_s[0], a_b_s[0], dec_seq)
    h, vg_sample = _gmlp_layer(x_prompt.reshape(n_prompt, d), x_sample.reshape(n_sample, d), row(norm_mix[0]),
                               a_w_in[0].astype(BF16), row(a_ln_g[0]), row(a_ln_b[0]), mix, bias,
                               a_w_out[0].astype(BF16))
    moe = []
    for layer in range(2):
        w_r, b_r = _router_tables(moe_w_grp[layer], moe_b_grp[layer], moe_w_rt[layer], moe_b_rt[layer])
        moe.append((row(norm_ffn[layer]), w_r, b_r, moe_w_gate, moe_w_up, moe_w_down, layer))
    h = _hier_moe(h, *moe[0])

    k, v, q, k_p, k_s, v_p, v_s = _kvq_proj(h, row(kv_norm), row(norm_mix[1]), w_kv.astype(BF16),
                                            b_w_q[0].astype(BF16), row(k_norm), row(b_q_norm[0]), cos, sin)

    o_prompt = _moba_prompt(q, k, v, batch, seq)
    n_phys = cache_k.shape[0]
    cache_k2 = cache_k.reshape(n_phys, PAGE_SIZE * N_KV_HEADS, HEAD_DIM)
    cache_v2 = cache_v.reshape(n_phys, PAGE_SIZE * N_KV_HEADS, HEAD_DIM)
    o_sample = _sample_attn(page_table, q, k, v, n_prompt // dec_seq, cache_k2, cache_v2, dec_seq)
    h = _oproj(h, o_prompt, o_sample, b_w_o[0].astype(BF16))
    y_prompt, y_sample = _hier_moe(h, *moe[1], split=True)

    n_pages_new = seq // PAGE_SIZE
    return (y_prompt.reshape(batch, seq, d),
            y_sample.reshape(n_seq, dec_seq, d),
            k_p.reshape(batch, n_pages_new, PAGE_SIZE, N_KV_HEADS, HEAD_DIM),
            v_p.reshape(batch, n_pages_new, PAGE_SIZE, N_KV_HEADS, HEAD_DIM),
            k_s.reshape(n_seq, dec_seq, N_KV_HEADS, HEAD_DIM),
            v_s.reshape(n_seq, dec_seq, N_KV_HEADS, HEAD_DIM),
            vg_sample.reshape(1, n_seq, dec_seq, -1))
```

```python
import functools
import math

import jax
import jax.numpy as jnp
from jax import lax
from jax.experimental import pallas as pl
from jax.experimental.pallas import tpu as pltpu

F32 = jnp.float32
BF16 = jnp.bfloat16
HIGHEST = lax.Precision.HIGHEST

GMLP_CHUNK = 128
GMLP_GROUPS = 8
N_HEADS = 8
N_KV_HEADS = 4
HEAD_DIM = 128
Q_PER_KV = N_HEADS // N_KV_HEADS
MOBA_BLOCK = 256
MOBA_TOP_K = 3
ROPE_THETA = 10000.0
N_GROUPS = 4
EXPERTS_PER_GROUP = 8
N_EXPERTS = N_GROUPS * EXPERTS_PER_GROUP
TOP_K_EXPERTS = 2
PAGE_SIZE = 128
EPS = 1e-6

LANES = 128
ROW_TILE = 8
TOKEN_TILE = 256
FFN_ROWS = 256
FFN_X_SLOTS = 3
PAGES_PER_STEP = 16
PAGE_SLOTS = 4
KEY_GROUP = 4
PROMPT_Q_BLOCKS = 1
ONES_ROWS = 16
LOG2_E = math.log2(math.e)
VMEM_LIMIT = 56 * 1024 * 1024

NEG_INF = float("-inf")


def _cparams(sem):
    return pltpu.CompilerParams(dimension_semantics=sem, vmem_limit_bytes=VMEM_LIMIT)


def _rms(x):
    return x * lax.rsqrt(jnp.mean(x * x, axis=-1, keepdims=True) + EPS)


def _dot_3pass(a, b, dims):
    a_hi = a.astype(BF16)
    b_hi = b.astype(BF16)
    a_lo = (a - a_hi.astype(F32)).astype(BF16)
    b_lo = (b - b_hi.astype(F32)).astype(BF16)
    dot = functools.partial(lax.dot_general, dimension_numbers=dims, preferred_element_type=F32)
    return dot(a_hi, b_hi) + (dot(a_hi, b_lo) + dot(a_lo, b_hi))


def _dot_nt_3pass(a, b):
    return _dot_3pass(a, b, (((1,), (1,)), ((), ())))


def _prompt_spec(block, n_prompt_tiles):
    return pl.BlockSpec(block, lambda i, *_: (jnp.minimum(i, n_prompt_tiles - 1), 0))


def _sample_spec(block):
    return pl.BlockSpec(block, lambda i, *_: (0, 0))


def _is_sample_tile():
    return pl.program_id(0) == pl.num_programs(0) - 1


def _stream_tile(prompt_ref, sample_ref):
    return jnp.where(_is_sample_tile(), sample_ref[...], prompt_ref[...])


def _store_stream_tile(prompt_ref, sample_ref, store):
    @pl.when(jnp.logical_not(_is_sample_tile()))
    def _():
        store(prompt_ref)

    @pl.when(_is_sample_tile())
    def _():
        store(sample_ref)


def _gmlp_kernel(xp_ref, xs_ref, g_ref, win_ref, lng_ref, lnb_ref, mix_ref, bias_ref, wout_ref,
                 h_ref, vg_ref, *, d_gate, n_groups):
    i = pl.program_id(0)
    x = _stream_tile(xp_ref, xs_ref)
    xb = (_rms(x) * g_ref[...]).astype(BF16)
    u = jax.nn.gelu(jnp.dot(xb, win_ref[:, :d_gate], preferred_element_type=F32))
    vp = jax.nn.gelu(jnp.dot(xb, win_ref[:, d_gate:], preferred_element_type=F32))
    vc = vp - jnp.mean(vp, axis=-1, keepdims=True)
    var = jnp.mean(vc * vc, axis=-1, keepdims=True)
    vg = vc * lax.rsqrt(var + EPS) * lng_ref[...] + lnb_ref[...]

    @pl.when(i == pl.num_programs(0) - 1)
    def _():
        vg_ref[...] = vg

    vgb = vg.astype(BF16)
    cw = d_gate // n_groups
    bias = bias_ref[0]
    parts = []
    for g in range(n_groups):
        mixed = jnp.dot(mix_ref[0, g], vgb[:, g * cw:(g + 1) * cw], preferred_element_type=F32)
        mixed = mixed + bias[:, g:g + 1]
        parts.append((u[:, g * cw:(g + 1) * cw] * mixed).astype(BF16))
    gated = jnp.concatenate(parts, axis=1)
    h_ref[...] = x + jnp.dot(gated, wout_ref[...], preferred_element_type=F32)


def _gmlp_layer(x_prompt, x_sample, g, w_in, ln_g, ln_b, mix, bias, w_out):
    d = x_prompt.shape[1]
    d_gate = w_out.shape[0]
    tm = TOKEN_TILE
    n_prompt_tiles = x_prompt.shape[0] // tm
    n_tiles = n_prompt_tiles + 1
    n = n_tiles * tm
    kind = lambda i: jnp.where(i < n_prompt_tiles, 0, 1)
    return pl.pallas_call(
        functools.partial(_gmlp_kernel, d_gate=d_gate, n_groups=GMLP_GROUPS),
        grid=(n_tiles,),
        in_specs=[
            _prompt_spec((tm, d), n_prompt_tiles),
            _sample_spec((tm, d)),
            pl.BlockSpec((1, d), lambda i: (0, 0)),
            pl.BlockSpec((d, 2 * d_gate), lambda i: (0, 0)),
            pl.BlockSpec((1, d_gate), lambda i: (0, 0)),
            pl.BlockSpec((1, d_gate), lambda i: (0, 0)),
            pl.BlockSpec((1, GMLP_GROUPS, tm, tm), lambda i: (kind(i), 0, 0, 0)),
            pl.BlockSpec((1, tm, LANES), lambda i: (kind(i), 0, 0)),
            pl.BlockSpec((d_gate, d), lambda i: (0, 0)),
        ],
        out_specs=[
            pl.BlockSpec((tm, d), lambda i: (i, 0)),
            pl.BlockSpec((tm, d_gate), lambda i: (0, 0)),
        ],
        out_shape=[
            jax.ShapeDtypeStruct((n, d), F32),
            jax.ShapeDtypeStruct((tm, d_gate), F32),
        ],
        compiler_params=_cparams(("arbitrary",)),
        name="gmlp_layer",
    )(x_prompt, x_sample, g, w_in, ln_g, ln_b, mix, bias, w_out)


def _router_kernel(*refs, with_attn):
    if with_attn:
        h_ref, op_ref, os_ref, wo_ref, g_ref, wr_ref, br_ref, h_out, xn_ref, w_ref, code_ref, hist_ref = refs
        o = _stream_tile(op_ref, os_ref).astype(BF16)
        h = h_ref[...] + jnp.dot(o, wo_ref[...], preferred_element_type=F32)
        h_out[...] = h
    else:
        h_ref, g_ref, wr_ref, br_ref, xn_ref, w_ref, code_ref, hist_ref = refs
        h = h_ref[...]
    xn = _rms(h) * g_ref[...]
    _store_row_tiles(xn_ref, xn)
    logits = _dot_3pass(xn, wr_ref[...], (((1,), (0,)), ((), ()))) + br_ref[...]
    lane = lax.broadcasted_iota(jnp.int32, logits.shape, 1)
    big = jnp.int32(LANES)
    is_grp = lane < N_GROUPS
    gl = jnp.where(is_grp, logits, NEG_INF)
    gmax = jnp.max(gl, axis=1, keepdims=True)
    gidx = jnp.min(jnp.where(is_grp & (logits == gmax), lane, big), axis=1, keepdims=True)
    p_g = 1.0 / jnp.sum(jnp.where(is_grp, jnp.exp(gl - gmax), 0.0), axis=1, keepdims=True)
    lo = N_GROUPS + gidx * EXPERTS_PER_GROUP
    in_grp = (lane >= lo) & (lane < lo + EXPERTS_PER_GROUP)
    v0 = jnp.max(jnp.where(in_grp, logits, NEG_INF), axis=1, keepdims=True)
    i0 = jnp.min(jnp.where(in_grp & (logits == v0), lane, big), axis=1, keepdims=True)
    rest = in_grp & (lane != i0)
    v1 = jnp.max(jnp.where(rest, logits, NEG_INF), axis=1, keepdims=True)
    i1 = jnp.min(jnp.where(rest & (logits == v1), lane, big), axis=1, keepdims=True)
    t = jnp.exp(v1 - v0)
    w0 = p_g * (1.0 / (1.0 + t))
    w1 = p_g * (t / (1.0 + t))
    e0 = i0 - N_GROUPS
    e1 = i1 - N_GROUPS
    w_ref[...] = jnp.where(lane == 0, w0, jnp.where(lane == 1, w1, 0.0))
    tm = logits.shape[0]
    onehot = jnp.concatenate([(lane == e0).astype(F32), (lane == e1).astype(F32)], axis=0)
    a_row = lax.broadcasted_iota(jnp.int32, (2 * tm, 2 * tm), 0)
    a_col = lax.broadcasted_iota(jnp.int32, (2 * tm, 2 * tm), 1)
    earlier = (a_col < a_row).astype(BF16)
    before = jnp.dot(earlier, onehot.astype(BF16), preferred_element_type=F32)
    rank = jnp.sum(before * onehot, axis=1, keepdims=True).astype(jnp.int32)
    code_ref[...] = jnp.where(lane == 0, rank[:tm] * N_EXPERTS + e0,
                              jnp.where(lane == 1, rank[tm:] * N_EXPERTS + e1, 0))
    hist = jnp.sum(onehot, axis=0, keepdims=True).astype(jnp.int32)
    hist_ref[...] = jnp.broadcast_to(hist, hist_ref.shape)


def _router(h, g, w_r, b_r, attn=None):
    n, d = h.shape
    tm = TOKEN_TILE
    row = lambda i: (i, 0)
    fixed = lambda i: (0, 0)
    in_specs = [pl.BlockSpec((tm, d), row)]
    out_specs, out_shape, operands = [], [], [h]
    if attn is not None:
        o_prompt, o_sample, w_o = attn
        ow = o_prompt.shape[1]
        in_specs += [_prompt_spec((tm, ow), n // tm - 1), _sample_spec((tm, ow)), pl.BlockSpec(w_o.shape, fixed)]
        operands += [o_prompt, o_sample, w_o]
        out_specs.append(pl.BlockSpec((tm, d), row))
        out_shape.append(jax.ShapeDtypeStruct((n, d), F32))
    in_specs += [pl.BlockSpec((1, d), fixed), pl.BlockSpec((d, LANES), fixed), pl.BlockSpec((1, LANES), fixed)]
    operands += [g, w_r, b_r]
    out_specs += [
        pl.BlockSpec((tm * ROW_TILE, LANES), row),
        pl.BlockSpec((tm, LANES), row),
        pl.BlockSpec((tm, LANES), row),
        pl.BlockSpec((ROW_TILE, LANES), row),
    ]
    out_shape += [
        jax.ShapeDtypeStruct((n * ROW_TILE, LANES), F32),
        jax.ShapeDtypeStruct((n, LANES), F32),
        jax.ShapeDtypeStruct((n, LANES), jnp.int32),
        jax.ShapeDtypeStruct((n // tm * ROW_TILE, LANES), jnp.int32),
    ]
    return pl.pallas_call(
        functools.partial(_router_kernel, with_attn=attn is not None),
        grid=(n // tm,),
        in_specs=in_specs,
        out_specs=out_specs,
        out_shape=out_shape,
        compiler_params=_cparams(("arbitrary",)),
        name="moe_router",
    )(*operands)


def _pos_kernel(code_ref, first_ref, pos_ref, *, tiles_per_step):
    i = pl.program_id(0)
    tm = TOKEN_TILE
    shift = N_EXPERTS.bit_length() - 1
    for s in range(tiles_per_step):
        code = code_ref[s * tm:(s + 1) * tm, :]
        expert = lax.bitwise_and(code, N_EXPERTS - 1)
        rank = lax.shift_right_logical(code, shift)
        off = first_ref[pl.ds(i * tiles_per_step + s, 1), :]
        lane = lax.broadcasted_iota(jnp.int32, code.shape, 1)
        pos = [jnp.sum(jnp.where(lane == expert[:, k:k + 1], off, 0), axis=1, keepdims=True) + rank[:, k:k + 1]
               for k in range(TOP_K_EXPERTS)]
        pos_ref[s * tm:(s + 1) * tm, :] = jnp.where(lane == 0, pos[0], jnp.where(lane == 1, pos[1], 0))


def _sorted_positions(code_pad, first):
    n = code_pad.shape[0]
    n_tiles = n // TOKEN_TILE
    tiles_per_step = max(t for t in range(1, 17) if n_tiles % t == 0)
    rows = tiles_per_step * TOKEN_TILE
    return pl.pallas_call(
        functools.partial(_pos_kernel, tiles_per_step=tiles_per_step),
        grid=(n_tiles // tiles_per_step,),
        in_specs=[
            pl.BlockSpec((rows, LANES), lambda i: (i, 0)),
            pl.BlockSpec(first.shape, lambda i: (0, 0)),
        ],
        out_specs=pl.BlockSpec((rows, LANES), lambda i: (i, 0)),
        out_shape=jax.ShapeDtypeStruct((n, LANES), jnp.int32),
        compiler_params=_cparams(("arbitrary",)),
        name="moe_pos",
    )(code_pad, first)


def _dispatch_kernel(pos_ref, gap_ref, xn_ref, out_hbm, zero, sem, zsem):
    i = pl.program_id(0)
    tm = xn_ref.shape[0] // ROW_TILE
    base = i * tm * TOP_K_EXPERTS
    block_rows = zero.shape[0] // ROW_TILE
    n_blocks = out_hbm.shape[0] // zero.shape[0]

    @pl.when(i == 0)
    def _():
        zero[...] = jnp.zeros_like(zero)

        def each_gap(visit):
            def gap(e, carry):
                start = gap_ref[2 * e]

                def row(r, c):
                    visit(pltpu.make_async_copy(_row_tile(zero, 0), _row_tile(out_hbm, start + r), zsem))
                    return c

                lax.fori_loop(0, gap_ref[2 * e + 1], row, 0)
                return carry

            lax.fori_loop(0, N_EXPERTS, gap, 0)

            def unused_block(b, carry):
                first = pl.multiple_of(b * block_rows * ROW_TILE, ROW_TILE)
                visit(pltpu.make_async_copy(zero, out_hbm.at[pl.ds(first, block_rows * ROW_TILE), :], zsem))
                return carry

            lax.fori_loop(gap_ref[2 * N_EXPERTS], n_blocks, unused_block, 0)

        each_gap(lambda copy: copy.start())
        each_gap(lambda copy: copy.wait())

    def issue(t, carry):
        for k in range(TOP_K_EXPERTS):
            p = pos_ref[base + t * TOP_K_EXPERTS + k]
            pltpu.make_async_copy(_row_tile(xn_ref, t), _row_tile(out_hbm, p), sem).start(priority=k)
        return carry

    lax.fori_loop(0, tm, issue, 0, unroll=4)
    for k in range(TOP_K_EXPERTS):
        pltpu.make_async_copy(xn_ref, out_hbm.at[pl.ds(0, tm * ROW_TILE), :], sem).wait()


def _dispatch(xn, pos, gaps, n_rows):
    tm = TOKEN_TILE
    n = xn.shape[0] // ROW_TILE
    grid_spec = pltpu.PrefetchScalarGridSpec(
        num_scalar_prefetch=2,
        grid=(n // tm,),
        in_specs=[pl.BlockSpec((tm * ROW_TILE, LANES), lambda i, ps, gp: (i, 0))],
        out_specs=pl.BlockSpec(memory_space=pl.ANY),
        scratch_shapes=[pltpu.VMEM((FFN_ROWS * ROW_TILE, LANES), F32), pltpu.SemaphoreType.DMA(()),
                        pltpu.SemaphoreType.DMA(())],
    )
    return pl.pallas_call(
        _dispatch_kernel,
        grid_spec=grid_spec,
        out_shape=jax.ShapeDtypeStruct((n_rows * ROW_TILE, LANES), F32),
        compiler_params=_cparams(("arbitrary",)),
        name="moe_dispatch",
    )(pos, gaps, xn)


def _store_row_tiles(ref, x, first=0):
    for c in range(ROW_TILE):
        ref[pl.ds(first + c, x.shape[0], stride=ROW_TILE), :] = x[:, c * LANES:(c + 1) * LANES]


def _load_row_tiles(ref, first, rows, lead=()):
    return jnp.concatenate([ref[lead + (pl.ds(first + c, rows, stride=ROW_TILE), slice(None))]
                            for c in range(ROW_TILE)], axis=1)


def _row_tile(ref, r):
    return ref.at[pl.ds(pl.multiple_of(r * ROW_TILE, ROW_TILE), ROW_TILE), :]


def _ffn_kernel(blk_e_ref, nblk_ref, fresh_ref, run_ref, next_e_ref, x_hbm, wg_hbm, wu_hbm, wd_hbm, out_ref,
                xbuf, xsems, wg_f, wu_f, wd_f, wsems, wg_s, wu_s, wd_s, *, layer):
    i = pl.program_id(0)
    nblk = nblk_ref[0]
    tile_rows = out_ref.shape[0]
    rows = tile_rows // ROW_TILE
    live = i < nblk

    def x_copy(block, slot):
        first = pl.multiple_of(block * tile_rows, tile_rows)
        return pltpu.make_async_copy(x_hbm.at[pl.ds(first, tile_rows), :], xbuf.at[slot], xsems.at[slot])

    def w_copies(expert, slot):
        return [pltpu.make_async_copy(src.at[layer, expert], dst.at[slot], wsems.at[slot])
                for src, dst in ((wg_hbm, wg_f), (wu_hbm, wu_f), (wd_hbm, wd_f))]

    @pl.when(i == 0)
    def _():
        for b in range(FFN_X_SLOTS - 1):
            @pl.when(b < nblk)
            def _():
                x_copy(b, b).start()

        @pl.when(nblk > 0)
        def _():
            for copy in w_copies(blk_e_ref[0], 0):
                copy.start()

    ahead = i + (FFN_X_SLOTS - 1)

    @pl.when(ahead < nblk)
    def _():
        x_copy(ahead, lax.rem(ahead, FFN_X_SLOTS)).start()

    @pl.when(live & (fresh_ref[i] == 1))
    def _():
        slot = lax.rem(run_ref[i], 2)
        for copy in w_copies(0, slot):
            copy.wait()
        wg_s[...] = wg_f[slot].astype(BF16)
        wu_s[...] = wu_f[slot].astype(BF16)
        wd_s[...] = wd_f[slot].astype(BF16)

        @pl.when(next_e_ref[i] >= 0)
        def _():
            for copy in w_copies(next_e_ref[i], 1 - slot):
                copy.start()

    @pl.when(live)
    def _():
        slot = lax.rem(i, FFN_X_SLOTS)
        x_copy(0, slot).wait()
        x = _load_row_tiles(xbuf, 0, rows, lead=(slot,)).astype(BF16)
        gate = jnp.dot(x, wg_s[...], preferred_element_type=F32)
        up = jnp.dot(x, wu_s[...], preferred_element_type=F32)
        hid = (jax.nn.silu(gate) * up).astype(BF16)
        _store_row_tiles(out_ref, jnp.dot(hid, wd_s[...], preferred_element_type=F32))

    @pl.when(jnp.logical_not(live))
    def _():
        out_ref[...] = jnp.zeros_like(out_ref)


def _grouped_ffn(x_sorted, blk_e, nblk, fresh, run, next_e, w_gate, w_up, w_down, layer):
    rb = FFN_ROWS
    n_rows = x_sorted.shape[0] // ROW_TILE
    _, _, d, d_e = w_gate.shape
    assert d == ROW_TILE * LANES
    any_spec = pl.BlockSpec(memory_space=pl.ANY)
    grid_spec = pltpu.PrefetchScalarGridSpec(
        num_scalar_prefetch=5,
        grid=(n_rows // rb,),
        in_specs=[any_spec, any_spec, any_spec, any_spec],
        out_specs=pl.BlockSpec((rb * ROW_TILE, LANES), lambda i, *_: (i, 0)),
        scratch_shapes=[
            pltpu.VMEM((FFN_X_SLOTS, rb * ROW_TILE, LANES), F32), pltpu.SemaphoreType.DMA((FFN_X_SLOTS,)),
            pltpu.VMEM((2, d, d_e), F32), pltpu.VMEM((2, d, d_e), F32), pltpu.VMEM((2, d_e, d), F32),
            pltpu.SemaphoreType.DMA((2,)),
            pltpu.VMEM((d, d_e), BF16), pltpu.VMEM((d, d_e), BF16), pltpu.VMEM((d_e, d), BF16),
        ],
    )
    return pl.pallas_call(
        functools.partial(_ffn_kernel, layer=layer),
        grid_spec=grid_spec,
        out_shape=jax.ShapeDtypeStruct((n_rows * ROW_TILE, LANES), F32),
        compiler_params=_cparams(("arbitrary",)),
        name="moe_ffn",
    )(blk_e, nblk, fresh, run, next_e, x_sorted, w_gate, w_up, w_down)


def _combine_kernel(pos_ref, resid_ref, w_ref, src_hbm, *refs, split):
    out_refs, buf, sems = refs[:-2], refs[-2], refs[-1]
    i = pl.program_id(0)
    tm = resid_ref.shape[0]

    def gather(tile, slot):
        base = tile * tm * TOP_K_EXPERTS

        def issue(t, carry):
            for k in range(TOP_K_EXPERTS):
                p = pos_ref[base + t * TOP_K_EXPERTS + k]
                pltpu.make_async_copy(_row_tile(src_hbm, p), _row_tile(buf, (slot * TOP_K_EXPERTS + k) * tm + t),
                                      sems.at[slot]).start(priority=k)
            return carry

        lax.fori_loop(0, tm, issue, 0, unroll=4)

    @pl.when(i == 0)
    def _():
        gather(0, 0)

    @pl.when(i + 1 < pl.num_programs(0))
    def _():
        gather(i + 1, (i + 1) % 2)

    slot = i % 2
    firsts = [pl.multiple_of((slot * TOP_K_EXPERTS + k) * tm * ROW_TILE, ROW_TILE) for k in range(TOP_K_EXPERTS)]
    for first in firsts:
        pltpu.make_async_copy(src_hbm.at[pl.ds(0, tm * ROW_TILE), :],
                              buf.at[pl.ds(first, tm * ROW_TILE), :], sems.at[slot]).wait()
    acc = resid_ref[...]
    w = w_ref[...]
    for k, first in enumerate(firsts):
        acc = acc + w[:, k:k + 1] * _load_row_tiles(buf, first, tm)
    def put(ref):
        ref[...] = acc

    if split:
        _store_stream_tile(out_refs[0], out_refs[1], put)
    else:
        put(out_refs[0])


def _combine(resid, w_pad, src, pos, split):
    n, d = resid.shape
    assert d == ROW_TILE * LANES and TOP_K_EXPERTS == 2
    tm = TOKEN_TILE
    n_prompt_tiles = n // tm - 1
    if split:
        out_specs = [_prompt_spec((tm, d), n_prompt_tiles), _sample_spec((tm, d))]
        out_shape = [jax.ShapeDtypeStruct((n_prompt_tiles * tm, d), F32), jax.ShapeDtypeStruct((tm, d), F32)]
    else:
        out_specs = [pl.BlockSpec((tm, d), lambda i, *_: (i, 0))]
        out_shape = [jax.ShapeDtypeStruct((n, d), F32)]
    grid_spec = pltpu.PrefetchScalarGridSpec(
        num_scalar_prefetch=1,
        grid=(n // tm,),
        in_specs=[
            pl.BlockSpec((tm, d), lambda i, *_: (i, 0)),
            pl.BlockSpec((tm, LANES), lambda i, *_: (i, 0)),
            pl.BlockSpec(memory_space=pl.ANY),
        ],
        out_specs=out_specs,
        scratch_shapes=[pltpu.VMEM((2 * TOP_K_EXPERTS * tm * ROW_TILE, LANES), F32),
                        pltpu.SemaphoreType.DMA((2,))],
    )
    return pl.pallas_call(
        functools.partial(_combine_kernel, split=split),
        grid_spec=grid_spec,
        out_shape=out_shape,
        compiler_params=_cparams(("arbitrary",)),
        name="moe_combine",
    )(pos, resid, w_pad, src)


def _hier_moe(h, g, w_r, b_r, w_gate, w_up, w_down, layer, split=False, attn=None):
    assert N_EXPERTS & (N_EXPERTS - 1) == 0
    n, _ = h.shape
    rb = FFN_ROWS
    if attn is None:
        xn, w_pad, code_pad, hist_pad = _router(h, g, w_r, b_r)
    else:
        h, xn, w_pad, code_pad, hist_pad = _router(h, g, w_r, b_r, attn)
    hist = hist_pad[::ROW_TILE, :N_EXPERTS]
    counts = jnp.sum(hist, axis=0)
    padded = (counts + rb - 1) // rb * rb
    pend = jnp.cumsum(padded)
    first = ((pend - padded)[None, :] + jnp.cumsum(hist, axis=0) - hist).astype(jnp.int32)
    first = jnp.pad(first, ((0, 0), (0, LANES - N_EXPERTS)))
    pos = _sorted_positions(code_pad, first)[:, :TOP_K_EXPERTS].reshape(-1)
    n_blocks = (n * TOP_K_EXPERTS + N_EXPERTS * (rb - 1) + rb - 1) // rb
    block_first = jnp.arange(n_blocks, dtype=jnp.int32)[:, None] * rb
    blk_e = jnp.minimum(jnp.sum((pend[None, :] <= block_first).astype(jnp.int32), axis=1), N_EXPERTS - 1)
    fresh = jnp.concatenate([jnp.ones((1,), jnp.int32), (blk_e[1:] != blk_e[:-1]).astype(jnp.int32)])
    run = (jnp.cumsum(fresh) - 1).astype(jnp.int32)
    experts = jnp.arange(N_EXPERTS, dtype=jnp.int32)
    later = jnp.where((counts[None, :] > 0) & (experts[None, :] > experts[:, None]), experts[None, :], N_EXPERTS)
    next_owner = jnp.min(later, axis=1)
    next_e = jnp.where(next_owner < N_EXPERTS, next_owner, -1)[blk_e].astype(jnp.int32)
    nblk = (pend[-1:] // rb).astype(jnp.int32)
    gaps = jnp.concatenate([jnp.stack([pend - padded + counts, padded - counts], axis=1).reshape(-1),
                            nblk]).astype(jnp.int32)
    x_sorted = _dispatch(xn, pos, gaps, n_blocks * rb)
    out_sorted = _grouped_ffn(x_sorted, blk_e, nblk, fresh, run, next_e, w_gate, w_up, w_down, layer)
    out = _combine(h, w_pad, out_sorted, pos, split)
    return out if split else out[0]


def _store_heads_as_rows(ref, x):
    for j in range(N_KV_HEADS):
        ref[pl.ds(j, x.shape[0], stride=N_KV_HEADS), :] = x[:, j * HEAD_DIM:(j + 1) * HEAD_DIM]


def _proj_kernel(h_ref, gkv_ref, gq_ref, wkv_ref, wq_ref, kn_ref, qn_ref, cos_ref, sin_ref,
                 k_ref, v_ref, q_ref, kp_ref, ks_ref, vp_ref, vs_ref):
    hn = _rms(h_ref[...])
    cos = cos_ref[...]
    sin = sin_ref[...]

    def norm_rope(x, g):
        y = _rms(x) * g
        return y * cos + pltpu.roll(y, HEAD_DIM // 2, 1) * sin

    kv = jnp.dot((hn * gkv_ref[...]).astype(BF16), wkv_ref[...], preferred_element_type=F32)
    kw = N_KV_HEADS * HEAD_DIM
    k = jnp.concatenate(
        [norm_rope(kv[:, j * HEAD_DIM:(j + 1) * HEAD_DIM], kn_ref[...]) for j in range(N_KV_HEADS)], axis=1)
    v = kv[:, kw:]
    k_ref[...] = k
    v_ref[...] = v
    _store_stream_tile(kp_ref, ks_ref, lambda ref: _store_heads_as_rows(ref, k))
    _store_stream_tile(vp_ref, vs_ref, lambda ref: _store_heads_as_rows(ref, v))
    q = jnp.dot((hn * gq_ref[...]).astype(BF16), wq_ref[...], preferred_element_type=F32)
    q_ref[...] = jnp.concatenate(
        [norm_rope(q[:, j * HEAD_DIM:(j + 1) * HEAD_DIM], qn_ref[...]) for j in range(N_HEADS)], axis=1)


def _kvq_proj(h, g_kv, g_q, w_kv, w_q, k_norm, q_norm, cos, sin):
    n, d = h.shape
    tm = TOKEN_TILE
    kw = N_KV_HEADS * HEAD_DIM
    qw = N_HEADS * HEAD_DIM
    row = lambda i: (i, 0)
    fixed = lambda i: (0, 0)
    n_prompt_tiles = n // tm - 1
    seq_tiles = cos.shape[0] // tm - 1
    table_row = lambda i: (jnp.where(i < n_prompt_tiles, lax.rem(i, seq_tiles), seq_tiles), 0)
    head_rows = tm * N_KV_HEADS
    by_head = [_prompt_spec((head_rows, HEAD_DIM), n_prompt_tiles), _sample_spec((head_rows, HEAD_DIM))]
    by_head_shapes = [jax.ShapeDtypeStruct((n_prompt_tiles * head_rows, HEAD_DIM), F32),
                      jax.ShapeDtypeStruct((head_rows, HEAD_DIM), F32)]
    return pl.pallas_call(
        _proj_kernel,
        grid=(n // tm,),
        in_specs=[
            pl.BlockSpec((tm, d), row),
            pl.BlockSpec((1, d), fixed),
            pl.BlockSpec((1, d), fixed),
            pl.BlockSpec((d, 2 * kw), fixed),
            pl.BlockSpec((d, qw), fixed),
            pl.BlockSpec((1, HEAD_DIM), fixed),
            pl.BlockSpec((1, HEAD_DIM), fixed),
            pl.BlockSpec((tm, HEAD_DIM), table_row),
            pl.BlockSpec((tm, HEAD_DIM), table_row),
        ],
        out_specs=[
            pl.BlockSpec((tm, kw), row),
            pl.BlockSpec((tm, kw), row),
            pl.BlockSpec((tm, qw), row),
        ] + by_head + by_head,
        out_shape=[
            jax.ShapeDtypeStruct((n, kw), F32),
            jax.ShapeDtypeStruct((n, kw), F32),
            jax.ShapeDtypeStruct((n, qw), F32),
        ] + by_head_shapes + by_head_shapes,
        compiler_params=_cparams(("arbitrary",)),
        name="kvq_proj",
    )(h, g_kv, g_q, w_kv, w_q, k_norm, q_norm, cos, sin)


def _top_blocks(gate, axis):
    idx = lax.broadcasted_iota(jnp.int32, gate.shape, axis)
    big = jnp.int32(gate.shape[axis])
    sel = jnp.zeros(gate.shape, jnp.bool_)
    for _ in range(MOBA_TOP_K):
        top = jnp.max(gate, axis=axis, keepdims=True)
        first = jnp.min(jnp.where(gate == top, idx, big), axis=axis, keepdims=True)
        hit = idx == first
        sel = sel | (hit & (top > NEG_INF))
        gate = jnp.where(hit, NEG_INF, gate)
    return sel.astype(F32)


def _moba_prompt_kernel(q_ref, k_ref, v_ref, o_ref, kbf, vt, kmean, sel, s_a, s_b, m_scr, acc):
    blk = MOBA_BLOCK
    grp = KEY_GROUP
    qb = PROMPT_Q_BLOCKS
    tq = qb * blk
    j0 = pl.program_id(2) * qb
    seq = k_ref.shape[0]
    n_blocks = seq // blk

    @pl.when(j0 == 0)
    def _():
        k = k_ref[...]
        kbf[...] = k.astype(BF16)
        kmean[...] = jnp.mean(k.reshape(n_blocks, blk, HEAD_DIM), axis=1)
        for n in range(n_blocks):
            vt[:HEAD_DIM, n * blk:(n + 1) * blk] = v_ref[n * blk:(n + 1) * blk, :].T.astype(BF16)
        r = lax.broadcasted_iota(jnp.int32, (ONES_ROWS, seq), 0)
        vt[HEAD_DIM:, :] = jnp.where(r == 0, 1.0, 0.0).astype(BF16)

    q2 = q_ref[...]
    qs = jnp.concatenate([q2[:, h * HEAD_DIM:(h + 1) * HEAD_DIM] for h in range(Q_PER_KV)], axis=0)
    nq = qs.shape[0]
    qt = (qs * (HEAD_DIM ** -0.5 * LOG2_E)).T.astype(BF16)

    def scores(start, n_keys):
        return jnp.dot(kbf[pl.ds(start, n_keys), :], qt, preferred_element_type=F32)

    gate = _dot_nt_3pass(kmean[...], qs)
    row = lax.broadcasted_iota(jnp.int32, gate.shape, 0)
    q_blk_row = (lax.broadcasted_iota(jnp.int32, (1, nq), 1) % tq) // blk
    chosen = _top_blocks(jnp.where(row < j0 + q_blk_row, gate, NEG_INF), 0)
    sel[...] = chosen

    own = pl.multiple_of(j0 * blk, tq)
    s_own = scores(own, tq)
    kpos = lax.broadcasted_iota(jnp.int32, (blk, nq), 0)
    lane = lax.broadcasted_iota(jnp.int32, (blk, nq), 1)
    causal = kpos <= lane % blk
    q_blk = (lane % tq) // blk
    own_tiles = []
    for kb in range(qb):
        taken = jnp.broadcast_to(sel[pl.ds(j0 + kb, 1), :], (blk, nq)) > 0.0
        visible = ((q_blk == kb) & causal) | ((q_blk > kb) & taken)
        own_tiles.append(jnp.where(visible, s_own[kb * blk:(kb + 1) * blk, :], NEG_INF))
    m_own = own_tiles[0].max(axis=0, keepdims=True)
    for s in own_tiles[1:]:
        m_own = jnp.maximum(m_own, jnp.max(s, axis=0, keepdims=True))
    m_scr[...] = m_own
    p_own = jnp.concatenate([jnp.exp2(s - m_own).astype(BF16) for s in own_tiles], axis=0)
    acc[...] = jnp.dot(vt[:, pl.ds(own, tq)], p_own, preferred_element_type=F32)

    def update(tiles, start):
        m_old = m_scr[...]
        m_new = m_old
        for s in tiles:
            m_new = jnp.maximum(m_new, jnp.max(s, axis=0, keepdims=True))
        p = jnp.concatenate([jnp.exp2(s - m_new).astype(BF16) for s in tiles], axis=0)
        alpha = jnp.exp2(m_old - m_new)
        pv = jnp.dot(vt[:, pl.ds(start, len(tiles) * blk)], p, preferred_element_type=F32)
        acc[...] = alpha * acc[...] + pv
        m_scr[...] = m_new

    def fill(buf, g):
        start = pl.multiple_of(g * (grp * blk), grp * blk)
        buf[...] = scores(start, grp * blk)

    def consume(buf, g):
        start = pl.multiple_of(g * (grp * blk), grp * blk)
        tiles = []
        for i in range(grp):
            n = g * grp + i
            taken = jnp.where(n < j0, sel[pl.ds(n, 1), :], 0.0) > 0.0
            tiles.append(jnp.where(taken, buf[i * blk:(i + 1) * blk, :], NEG_INF))
        update(tiles, start)

    n_groups = lax.div(j0 + (grp - 1), grp)
    n_pairs = lax.div(n_groups - 1, 2)
    fill(s_a, 0)

    @pl.when(n_groups > 0)
    def _():
        def pair(h, carry):
            g = 2 * h
            fill(s_b, g + 1)
            consume(s_a, g)
            fill(s_a, g + 2)
            consume(s_b, g + 1)
            return carry

        lax.fori_loop(0, n_pairs, pair, 0)
        g = 2 * n_pairs

        @pl.when(n_groups - g == 1)
        def _():
            consume(s_a, g)

        @pl.when(n_groups - g == 2)
        def _():
            fill(s_b, g + 1)
            consume(s_a, g)
            consume(s_b, g + 1)

    a = acc[...]
    o = (a[:HEAD_DIM] / a[HEAD_DIM:HEAD_DIM + 1]).T
    o_ref[...] = jnp.concatenate([o[h * tq:(h + 1) * tq, :] for h in range(Q_PER_KV)], axis=1)


def _moba_prompt(q, k, v, batch, seq):
    blk = MOBA_BLOCK
    tq = PROMPT_Q_BLOCKS * blk
    assert seq % tq == 0 and seq % (KEY_GROUP * blk) == 0
    n_steps = seq // tq
    qw = Q_PER_KV * HEAD_DIM
    nq = Q_PER_KV * tq
    return pl.pallas_call(
        _moba_prompt_kernel,
        grid=(batch, N_KV_HEADS, n_steps),
        in_specs=[
            pl.BlockSpec((tq, qw), lambda b, c, j: (b * n_steps + j, c)),
            pl.BlockSpec((seq, HEAD_DIM), lambda b, c, j: (b, c)),
            pl.BlockSpec((seq, HEAD_DIM), lambda b, c, j: (b, c)),
        ],
        out_specs=pl.BlockSpec((tq, qw), lambda b, c, j: (b * n_steps + j, c)),
        out_shape=jax.ShapeDtypeStruct((batch * seq, N_HEADS * HEAD_DIM), F32),
        scratch_shapes=[
            pltpu.VMEM((seq, HEAD_DIM), BF16),
            pltpu.VMEM((HEAD_DIM + ONES_ROWS, seq), BF16),
            pltpu.VMEM((seq // blk, HEAD_DIM), F32),
            pltpu.VMEM((seq // blk, nq), F32),
            pltpu.VMEM((KEY_GROUP * blk, nq), F32),
            pltpu.VMEM((KEY_GROUP * blk, nq), F32),
            pltpu.VMEM((1, nq), F32),
            pltpu.VMEM((HEAD_DIM + ONES_ROWS, nq), F32),
        ],
        compiler_params=_cparams(("arbitrary", "arbitrary", "arbitrary")),
        name="moba_prompt",
    )(q, k, v)


def _stack_heads(q8):
    return jnp.concatenate([q8[:, h * HEAD_DIM:(h + 1) * HEAD_DIM] for h in range(N_HEADS)], axis=0)


def _sample_attn_kernel(pt_ref, q_ref, kn_ref, vn_ref, ck_hbm, cv_hbm, o_ref, pages, sems,
                        s_scr, means, sel_scr, qs_scr, m_scr, l_scr, acc, *, dec_seq, n_pages):
    pps = PAGES_PER_STEP
    t = pl.program_id(1)
    steps = pl.num_programs(1)
    n_k_steps = n_pages // pps
    rows = N_HEADS * dec_seq
    rkv = Q_PER_KV * dec_seq
    ppb = MOBA_BLOCK // PAGE_SIZE
    bps = pps // ppb
    n_blocks = n_pages // ppb
    nt_dims = (((1,), (1,)), ((), ()))

    chunk = pl.program_id(0) * steps + t
    n_chunks = pl.num_programs(0) * steps

    def start_chunk(ci):
        seq = lax.div(ci, steps)
        step = lax.rem(ci, steps)
        slot = lax.rem(ci, PAGE_SLOTS)

        def start_pages(cache_hbm, first_page):
            for r in range(pps):
                pltpu.make_async_copy(cache_hbm.at[pt_ref[seq, first_page + r]], pages.at[slot, r],
                                      sems.at[slot]).start()

        @pl.when(step < n_k_steps)
        def _():
            start_pages(ck_hbm, step * pps)

        @pl.when(step >= n_k_steps)
        def _():
            start_pages(cv_hbm, (step - n_k_steps) * pps)

    @pl.when(chunk == 0)
    def _():
        for ci in range(PAGE_SLOTS - 1):
            start_chunk(jnp.int32(ci))

    @pl.when(chunk + (PAGE_SLOTS - 1) < n_chunks)
    def _():
        start_chunk(chunk + (PAGE_SLOTS - 1))

    slot = lax.rem(chunk, PAGE_SLOTS)
    pltpu.make_async_copy(ck_hbm.at[pl.ds(0, pps)], pages.at[slot], sems.at[slot]).wait()

    def head_rows(p, c):
        return pages[slot, p, pl.ds(c, PAGE_SIZE, stride=N_KV_HEADS), :]

    @pl.when(t == 0)
    def _():
        qs_scr[...] = (_stack_heads(q_ref[...]) * (HEAD_DIM ** -0.5 * LOG2_E)).astype(BF16)
        m_scr[...] = jnp.full(m_scr.shape, NEG_INF, F32)
        l_scr[...] = jnp.zeros(l_scr.shape, F32)
        acc[...] = jnp.zeros(acc.shape, F32)

    @pl.when(t < n_k_steps)
    def _():
        qs = qs_scr[...]
        col = pl.multiple_of(t * (pps * PAGE_SIZE), pps * PAGE_SIZE)
        for c in range(N_KV_HEADS):
            kc = jnp.concatenate([head_rows(p, c) for p in range(pps)], axis=0)
            s_scr[c * rkv:(c + 1) * rkv, pl.ds(col, pps * PAGE_SIZE)] = lax.dot_general(
                qs[c * rkv:(c + 1) * rkv], kc.astype(BF16), nt_dims, preferred_element_type=F32)
            means[c, pl.ds(t * bps, bps), :] = jnp.sum(kc.reshape(bps, MOBA_BLOCK, HEAD_DIM), axis=1) / MOBA_BLOCK

    @pl.when(t == n_k_steps - 1)
    def _():
        qf = _stack_heads(q_ref[...])
        gate = jnp.concatenate(
            [lax.dot_general(qf[c * rkv:(c + 1) * rkv], means[c], nt_dims, precision=HIGHEST,
                             preferred_element_type=F32) for c in range(N_KV_HEADS)], axis=0)
        chosen = _top_blocks(gate, 1)
        sel_scr[...] = jnp.concatenate([chosen, jnp.zeros((rows, LANES - n_blocks), F32)], axis=1)

    def softmax_step(tiles):
        m_old = m_scr[...]
        m_new = m_old
        for s in tiles:
            m_new = jnp.maximum(m_new, jnp.max(s, axis=1, keepdims=True))
        m_safe = jnp.where(m_new == NEG_INF, 0.0, m_new)
        alpha = jnp.exp2(m_old - m_safe)
        l_new = alpha * l_scr[...]
        probs = []
        for s in tiles:
            p = jnp.exp2(s - m_safe)
            l_new = l_new + jnp.sum(p, axis=1, keepdims=True)
            probs.append(p.astype(BF16))
        l_scr[...] = l_new
        m_scr[...] = m_new
        return probs, alpha

    @pl.when(t >= n_k_steps)
    def _():
        tv = t - n_k_steps
        sel = sel_scr[...]
        lane = lax.broadcasted_iota(jnp.int32, sel.shape, 1)
        tiles = []
        for b in range(bps):
            n = tv * bps + b
            chosen = jnp.sum(jnp.where(lane == n, sel, 0.0), axis=1, keepdims=True) > 0.0
            col = pl.multiple_of(n * MOBA_BLOCK, MOBA_BLOCK)
            tiles.append(jnp.where(chosen, s_scr[:, pl.ds(col, MOBA_BLOCK)], NEG_INF))
        probs, alpha = softmax_step(tiles)
        prob = jnp.concatenate(probs, axis=1)
        pv = []
        for c in range(N_KV_HEADS):
            vc = jnp.concatenate([head_rows(p, c) for p in range(pps)], axis=0).astype(BF16)
            pv.append(jnp.dot(prob[c * rkv:(c + 1) * rkv], vc, preferred_element_type=F32))
        acc[...] = alpha * acc[...] + jnp.concatenate(pv, axis=0)

    @pl.when(t == pl.num_programs(1) - 1)
    def _():
        qs = qs_scr[...]
        kn = kn_ref[...].astype(BF16)
        vn = vn_ref[...].astype(BF16)
        s = jnp.concatenate(
            [lax.dot_general(qs[c * rkv:(c + 1) * rkv], kn[:, c * HEAD_DIM:(c + 1) * HEAD_DIM], nt_dims,
                             preferred_element_type=F32) for c in range(N_KV_HEADS)], axis=0)
        r2 = lax.broadcasted_iota(jnp.int32, s.shape, 0)
        c2 = lax.broadcasted_iota(jnp.int32, s.shape, 1)
        probs, alpha = softmax_step([jnp.where(c2 <= r2 % dec_seq, s, NEG_INF)])
        pv = [jnp.dot(probs[0][c * rkv:(c + 1) * rkv], vn[:, c * HEAD_DIM:(c + 1) * HEAD_DIM],
                      preferred_element_type=F32) for c in range(N_KV_HEADS)]
        o = (alpha * acc[...] + jnp.concatenate(pv, axis=0)) / l_scr[...]
        o_ref[...] = jnp.concatenate([o[h * dec_seq:(h + 1) * dec_seq, :] for h in range(N_HEADS)], axis=1)


def _sample_attn(page_table, q, k, v, row0, cache_k2, cache_v2, dec_seq):
    n_seq, n_pages = page_table.shape
    pps = PAGES_PER_STEP
    n_k_steps = n_pages // pps
    n_blocks = n_pages * PAGE_SIZE // MOBA_BLOCK
    assert n_pages % pps == 0 and n_blocks <= LANES
    rows = N_HEADS * dec_seq
    kw = N_KV_HEADS * HEAD_DIM
    qw = N_HEADS * HEAD_DIM
    grid_spec = pltpu.PrefetchScalarGridSpec(
        num_scalar_prefetch=1,
        grid=(n_seq, 2 * n_k_steps),
        in_specs=[
            pl.BlockSpec((dec_seq, qw), lambda s, t, pt: (row0 + s, 0)),
            pl.BlockSpec((dec_seq, kw), lambda s, t, pt: (row0 + s, 0)),
            pl.BlockSpec((dec_seq, kw), lambda s, t, pt: (row0 + s, 0)),
            pl.BlockSpec(memory_space=pl.ANY),
            pl.BlockSpec(memory_space=pl.ANY),
        ],
        out_specs=pl.BlockSpec((dec_seq, qw), lambda s, t, pt: (s, 0)),
        scratch_shapes=[
            pltpu.VMEM((PAGE_SLOTS, pps, PAGE_SIZE * N_KV_HEADS, HEAD_DIM), F32),
            pltpu.SemaphoreType.DMA((PAGE_SLOTS,)),
            pltpu.VMEM((rows, n_pages * PAGE_SIZE), F32),
            pltpu.VMEM((N_KV_HEADS, n_blocks, HEAD_DIM), F32),
            pltpu.VMEM((rows, LANES), F32),
            pltpu.VMEM((rows, HEAD_DIM), BF16),
            pltpu.VMEM((rows, 1), F32),
            pltpu.VMEM((rows, 1), F32),
            pltpu.VMEM((rows, HEAD_DIM), F32),
        ],
    )
    return pl.pallas_call(
        functools.partial(_sample_attn_kernel, dec_seq=dec_seq, n_pages=n_pages),
        grid_spec=grid_spec,
        out_shape=jax.ShapeDtypeStruct((n_seq * dec_seq, qw), F32),
        compiler_params=_cparams(("arbitrary", "arbitrary")),
        name="sample_attn",
    )(page_table, q, k, v, cache_k2, cache_v2)


def _mix_tables(w_s, b_s, dec_seq):
    tm = TOKEN_TILE
    causal = jnp.tril(jnp.ones((GMLP_CHUNK, GMLP_CHUNK), dtype=bool))
    w = jnp.where(causal[None], w_s, jnp.zeros_like(w_s))
    eye_p = jnp.eye(tm // GMLP_CHUNK, dtype=w.dtype)
    mix_p = jnp.einsum("ab,gts->gatbs", eye_p, w).reshape(GMLP_GROUPS, tm, tm)
    eye_s = jnp.eye(tm // dec_seq, dtype=w.dtype)
    mix_s = jnp.einsum("ab,gts->gatbs", eye_s, w[:, :dec_seq, :dec_seq]).reshape(GMLP_GROUPS, tm, tm)
    mix = jnp.stack([mix_p, mix_s]).astype(BF16)
    bias_p = jnp.tile(b_s.T, (tm // GMLP_CHUNK, 1))
    bias_s = jnp.tile(b_s.T[:dec_seq], (tm // dec_seq, 1))
    bias = jnp.stack([bias_p, bias_s])
    bias = jnp.pad(bias, ((0, 0), (0, 0), (0, LANES - GMLP_GROUPS)))
    return mix, bias


def _rope_tables(pos):
    half = HEAD_DIM // 2
    inv = ROPE_THETA ** (-jnp.arange(half, dtype=F32) * 2.0 / HEAD_DIM)
    ang = pos.astype(F32)[:, None] * inv[None, :]
    cos = jnp.cos(ang)
    sin = jnp.sin(ang)
    return jnp.concatenate([cos, cos], axis=1), jnp.concatenate([-sin, sin], axis=1)


def _router_tables(w_grp, b_grp, w_rt, b_rt):
    w = jnp.concatenate([w_grp, w_rt], axis=1)
    b = jnp.concatenate([b_grp, b_rt], axis=0)
    pad = LANES - w.shape[1]
    return jnp.pad(w, ((0, 0), (0, pad))), jnp.pad(b, (0, pad)).reshape(1, LANES)


def kernel(x_prompt, x_sample, cache_k, cache_v, page_table, norm_mix, norm_ffn, a_w_in, a_ln_g, a_ln_b,
           a_w_s, a_b_s, a_w_out, kv_norm, w_kv, k_norm, b_w_q, b_q_norm, b_w_o, moe_w_grp, moe_b_grp,
           moe_w_rt, moe_b_rt, moe_w_gate, moe_w_up, moe_w_down):
    batch, seq, d = x_prompt.shape
    n_seq, dec_seq, _ = x_sample.shape
    n_prompt = batch * seq
    n_sample = n_seq * dec_seq
    assert n_prompt % TOKEN_TILE == 0 and n_sample == TOKEN_TILE and seq % MOBA_BLOCK == 0
    past_len = page_table.shape[1] * PAGE_SIZE
    assert past_len % MOBA_BLOCK == 0 and dec_seq <= MOBA_BLOCK

    pos = jnp.concatenate([jnp.arange(seq), jnp.tile(past_len + jnp.arange(dec_seq), n_seq)])
    cos, sin = _rope_tables(pos)
    row = lambda a: a.reshape(1, -1)

    mix, bias = _mix_tables(a_w_s[0], a_b_s[0], dec_seq)
    h, vg_sample = _gmlp_layer(x_prompt.reshape(n_prompt, d), x_sample.reshape(n_sample, d), row(norm_mix[0]),
                               a_w_in[0].astype(BF16), row(a_ln_g[0]), row(a_ln_b[0]), mix, bias,
                               a_w_out[0].astype(BF16))
    moe = []
    for layer in range(2):
        w_r, b_r = _router_tables(moe_w_grp[layer], moe_b_grp[layer], moe_w_rt[layer], moe_b_rt[layer])
        moe.append((row(norm_ffn[layer]), w_r, b_r, moe_w_gate, moe_w_up, moe_w_down, layer))
    h = _hier_moe(h, *moe[0])

    k, v, q, k_p, k_s, v_p, v_s = _kvq_proj(h, row(kv_norm), row(norm_mix[1]), w_kv.astype(BF16),
                                            b_w_q[0].astype(BF16), row(k_norm), row(b_q_norm[0]), cos, sin)

    o_prompt = _moba_prompt(q, k, v, batch, seq)
    n_phys = cache_k.shape[0]
    cache_k2 = cache_k.reshape(n_phys, PAGE_SIZE * N_KV_HEADS, HEAD_DIM)
    cache_v2 = cache_v.reshape(n_phys, PAGE_SIZE * N_KV_HEADS, HEAD_DIM)
    o_sample = _sample_attn(page_table, q, k, v, n_prompt // dec_seq, cache_k2, cache_v2, dec_seq)
    y_prompt, y_sample = _hier_moe(h, *moe[1], split=True, attn=(o_prompt, o_sample, b_w_o[0].astype(BF16)))

    n_pages_new = seq // PAGE_SIZE
    return (y_prompt.reshape(batch, seq, d),
            y_sample.reshape(n_seq, dec_seq, d),
            k_p.reshape(batch, n_pages_new, PAGE_SIZE, N_KV_HEADS, HEAD_DIM),
            v_p.reshape(batch, n_pages_new, PAGE_SIZE, N_KV_HEADS, HEAD_DIM),
            k_s.reshape(n_seq, dec_seq, N_KV_HEADS, HEAD_DIM),
            v_s.reshape(n_seq, dec_seq, N_KV_HEADS, HEAD_DIM),
            vg_sample.reshape(1, n_seq, dec_seq, -1))
```

```python
import functools
import math

import jax
import jax.numpy as jnp
from jax import lax
from jax.experimental import pallas as pl
from jax.experimental.pallas import tpu as pltpu

F32 = jnp.float32
BF16 = jnp.bfloat16
HIGHEST = lax.Precision.HIGHEST

GMLP_CHUNK = 128
GMLP_GROUPS = 8
N_HEADS = 8
N_KV_HEADS = 4
HEAD_DIM = 128
Q_PER_KV = N_HEADS // N_KV_HEADS
MOBA_BLOCK = 256
MOBA_TOP_K = 3
ROPE_THETA = 10000.0
N_GROUPS = 4
EXPERTS_PER_GROUP = 8
N_EXPERTS = N_GROUPS * EXPERTS_PER_GROUP
TOP_K_EXPERTS = 2
PAGE_SIZE = 128
EPS = 1e-6

LANES = 128
ROW_TILE = 8
TOKEN_TILE = 256
FFN_ROWS = 256
FFN_X_SLOTS = 4
DISPATCH_SLOTS = 4
PAGES_PER_STEP = 16
PAGE_SLOTS = 4
KEY_GROUP = 4
ONES_ROWS = 16
LOG2_E = math.log2(math.e)
VMEM_LIMIT = 56 * 1024 * 1024

NEG_INF = float("-inf")


def _cparams(sem):
    return pltpu.CompilerParams(dimension_semantics=sem, vmem_limit_bytes=VMEM_LIMIT)


def _rms(x):
    return x * lax.rsqrt(jnp.mean(x * x, axis=-1, keepdims=True) + EPS)


def _dot_3pass(a, b, dims):
    a_hi = a.astype(BF16)
    b_hi = b.astype(BF16)
    a_lo = (a - a_hi.astype(F32)).astype(BF16)
    b_lo = (b - b_hi.astype(F32)).astype(BF16)
    dot = functools.partial(lax.dot_general, dimension_numbers=dims, preferred_element_type=F32)
    return dot(a_hi, b_hi) + (dot(a_hi, b_lo) + dot(a_lo, b_hi))


def _dot_nt_3pass(a, b):
    return _dot_3pass(a, b, (((1,), (1,)), ((), ())))


def _prompt_spec(block, n_prompt_tiles):
    return pl.BlockSpec(block, lambda i, *_: (jnp.minimum(i, n_prompt_tiles - 1), 0))


def _sample_spec(block):
    return pl.BlockSpec(block, lambda i, *_: (0, 0))


def _is_sample_tile():
    return pl.program_id(0) == pl.num_programs(0) - 1


def _stream_tile(prompt_ref, sample_ref):
    return jnp.where(_is_sample_tile(), sample_ref[...], prompt_ref[...])


def _store_stream_tile(prompt_ref, sample_ref, store):
    @pl.when(jnp.logical_not(_is_sample_tile()))
    def _():
        store(prompt_ref)

    @pl.when(_is_sample_tile())
    def _():
        store(sample_ref)


def _gmlp_kernel(xp_ref, xs_ref, g_ref, win_ref, lng_ref, lnb_ref, mix_ref, bias_ref, wout_ref,
                 h_ref, vg_ref, *, d_gate, n_groups):
    i = pl.program_id(0)
    x = _stream_tile(xp_ref, xs_ref)
    xb = (_rms(x) * g_ref[...]).astype(BF16)
    u = jax.nn.gelu(jnp.dot(xb, win_ref[:, :d_gate], preferred_element_type=F32))
    vp = jax.nn.gelu(jnp.dot(xb, win_ref[:, d_gate:], preferred_element_type=F32))
    vc = vp - jnp.mean(vp, axis=-1, keepdims=True)
    var = jnp.mean(vc * vc, axis=-1, keepdims=True)
    vg = vc * lax.rsqrt(var + EPS) * lng_ref[...] + lnb_ref[...]

    @pl.when(i == pl.num_programs(0) - 1)
    def _():
        vg_ref[...] = vg

    vgb = vg.astype(BF16)
    cw = d_gate // n_groups
    bias = bias_ref[0]
    parts = []
    for g in range(n_groups):
        mixed = jnp.dot(mix_ref[0, g], vgb[:, g * cw:(g + 1) * cw], preferred_element_type=F32)
        mixed = mixed + bias[:, g:g + 1]
        parts.append((u[:, g * cw:(g + 1) * cw] * mixed).astype(BF16))
    gated = jnp.concatenate(parts, axis=1)
    h_ref[...] = x + jnp.dot(gated, wout_ref[...], preferred_element_type=F32)


def _gmlp_layer(x_prompt, x_sample, g, w_in, ln_g, ln_b, mix, bias, w_out):
    d = x_prompt.shape[1]
    d_gate = w_out.shape[0]
    tm = TOKEN_TILE
    n_prompt_tiles = x_prompt.shape[0] // tm
    n_tiles = n_prompt_tiles + 1
    n = n_tiles * tm
    kind = lambda i: jnp.where(i < n_prompt_tiles, 0, 1)
    return pl.pallas_call(
        functools.partial(_gmlp_kernel, d_gate=d_gate, n_groups=GMLP_GROUPS),
        grid=(n_tiles,),
        in_specs=[
            _prompt_spec((tm, d), n_prompt_tiles),
            _sample_spec((tm, d)),
            pl.BlockSpec((1, d), lambda i: (0, 0)),
            pl.BlockSpec((d, 2 * d_gate), lambda i: (0, 0)),
            pl.BlockSpec((1, d_gate), lambda i: (0, 0)),
            pl.BlockSpec((1, d_gate), lambda i: (0, 0)),
            pl.BlockSpec((1, GMLP_GROUPS, tm, tm), lambda i: (kind(i), 0, 0, 0)),
            pl.BlockSpec((1, tm, LANES), lambda i: (kind(i), 0, 0)),
            pl.BlockSpec((d_gate, d), lambda i: (0, 0)),
        ],
        out_specs=[
            pl.BlockSpec((tm, d), lambda i: (i, 0)),
            pl.BlockSpec((tm, d_gate), lambda i: (0, 0)),
        ],
        out_shape=[
            jax.ShapeDtypeStruct((n, d), F32),
            jax.ShapeDtypeStruct((tm, d_gate), F32),
        ],
        compiler_params=_cparams(("arbitrary",)),
        name="gmlp_layer",
    )(x_prompt, x_sample, g, w_in, ln_g, ln_b, mix, bias, w_out)


def _router_kernel(*refs, with_attn):
    if with_attn:
        h_ref, op_ref, os_ref, wo_ref, g_ref, wr_ref, br_ref, h_out, xn_ref, w_ref, code_ref, hist_ref = refs
        o = _stream_tile(op_ref, os_ref).astype(BF16)
        h = h_ref[...] + jnp.dot(o, wo_ref[...], preferred_element_type=F32)
        h_out[...] = h
    else:
        h_ref, g_ref, wr_ref, br_ref, xn_ref, w_ref, code_ref, hist_ref = refs
        h = h_ref[...]
    xn = _rms(h) * g_ref[...]
    _store_row_tiles(xn_ref, xn)
    logits = _dot_3pass(xn, wr_ref[...], (((1,), (0,)), ((), ()))) + br_ref[...]
    lane = lax.broadcasted_iota(jnp.int32, logits.shape, 1)
    big = jnp.int32(LANES)
    is_grp = lane < N_GROUPS
    gl = jnp.where(is_grp, logits, NEG_INF)
    gmax = jnp.max(gl, axis=1, keepdims=True)
    gidx = jnp.min(jnp.where(is_grp & (logits == gmax), lane, big), axis=1, keepdims=True)
    p_g = 1.0 / jnp.sum(jnp.where(is_grp, jnp.exp(gl - gmax), 0.0), axis=1, keepdims=True)
    lo = N_GROUPS + gidx * EXPERTS_PER_GROUP
    in_grp = (lane >= lo) & (lane < lo + EXPERTS_PER_GROUP)
    v0 = jnp.max(jnp.where(in_grp, logits, NEG_INF), axis=1, keepdims=True)
    i0 = jnp.min(jnp.where(in_grp & (logits == v0), lane, big), axis=1, keepdims=True)
    rest = in_grp & (lane != i0)
    v1 = jnp.max(jnp.where(rest, logits, NEG_INF), axis=1, keepdims=True)
    i1 = jnp.min(jnp.where(rest & (logits == v1), lane, big), axis=1, keepdims=True)
    t = jnp.exp(v1 - v0)
    w0 = p_g * (1.0 / (1.0 + t))
    w1 = p_g * (t / (1.0 + t))
    e0 = i0 - N_GROUPS
    e1 = i1 - N_GROUPS
    w_ref[...] = jnp.where(lane == 0, w0, jnp.where(lane == 1, w1, 0.0))
    tm = logits.shape[0]
    onehot = jnp.concatenate([(lane == e0).astype(F32), (lane == e1).astype(F32)], axis=0)
    a_row = lax.broadcasted_iota(jnp.int32, (2 * tm, 2 * tm), 0)
    a_col = lax.broadcasted_iota(jnp.int32, (2 * tm, 2 * tm), 1)
    earlier = (a_col < a_row).astype(BF16)
    before = jnp.dot(earlier, onehot.astype(BF16), preferred_element_type=F32)
    rank = jnp.sum(before * onehot, axis=1, keepdims=True).astype(jnp.int32)
    code_ref[...] = jnp.where(lane == 0, rank[:tm] * N_EXPERTS + e0,
                              jnp.where(lane == 1, rank[tm:] * N_EXPERTS + e1, 0))
    hist = jnp.sum(onehot, axis=0, keepdims=True).astype(jnp.int32)
    hist_ref[0] = hist


def _router(h, g, w_r, b_r, attn=None):
    n, d = h.shape
    tm = TOKEN_TILE
    row = lambda i: (i, 0)
    fixed = lambda i: (0, 0)
    in_specs = [pl.BlockSpec((tm, d), row)]
    out_specs, out_shape, operands = [], [], [h]
    if attn is not None:
        o_prompt, o_sample, w_o = attn
        ow = o_prompt.shape[1]
        in_specs += [_prompt_spec((tm, ow), n // tm - 1), _sample_spec((tm, ow)), pl.BlockSpec(w_o.shape, fixed)]
        operands += [o_prompt, o_sample, w_o]
        out_specs.append(pl.BlockSpec((tm, d), row))
        out_shape.append(jax.ShapeDtypeStruct((n, d), F32))
    in_specs += [pl.BlockSpec((1, d), fixed), pl.BlockSpec((d, LANES), fixed), pl.BlockSpec((1, LANES), fixed)]
    operands += [g, w_r, b_r]
    out_specs += [
        pl.BlockSpec((tm * ROW_TILE, LANES), row),
        pl.BlockSpec((tm, LANES), row),
        pl.BlockSpec((tm, LANES), row),
        pl.BlockSpec((1, 1, LANES), lambda i: (i, 0, 0)),
    ]
    out_shape += [
        jax.ShapeDtypeStruct((n * ROW_TILE, LANES), F32),
        jax.ShapeDtypeStruct((n, LANES), F32),
        jax.ShapeDtypeStruct((n, LANES), jnp.int32),
        jax.ShapeDtypeStruct((n // tm, 1, LANES), jnp.int32),
    ]
    return pl.pallas_call(
        functools.partial(_router_kernel, with_attn=attn is not None),
        grid=(n // tm,),
        in_specs=in_specs,
        out_specs=out_specs,
        out_shape=out_shape,
        compiler_params=_cparams(("arbitrary",)),
        name="moe_router",
    )(*operands)


def _pos_kernel(code_ref, first_ref, pos_ref, *, tiles_per_step):
    i = pl.program_id(0)
    tm = TOKEN_TILE
    shift = N_EXPERTS.bit_length() - 1
    for s in range(tiles_per_step):
        code = code_ref[s * tm:(s + 1) * tm, :]
        expert = lax.bitwise_and(code, N_EXPERTS - 1)
        rank = lax.shift_right_logical(code, shift)
        off = first_ref[pl.ds(i * tiles_per_step + s, 1), :]
        lane = lax.broadcasted_iota(jnp.int32, code.shape, 1)
        pos = [jnp.sum(jnp.where(lane == expert[:, k:k + 1], off, 0), axis=1, keepdims=True) + rank[:, k:k + 1]
               for k in range(TOP_K_EXPERTS)]
        pos_ref[s * tm:(s + 1) * tm, :] = jnp.where(lane == 0, pos[0], jnp.where(lane == 1, pos[1], 0))


def _sorted_positions(code_pad, first):
    n = code_pad.shape[0]
    n_tiles = n // TOKEN_TILE
    tiles_per_step = max(t for t in range(1, 17) if n_tiles % t == 0)
    rows = tiles_per_step * TOKEN_TILE
    return pl.pallas_call(
        functools.partial(_pos_kernel, tiles_per_step=tiles_per_step),
        grid=(n_tiles // tiles_per_step,),
        in_specs=[
            pl.BlockSpec((rows, LANES), lambda i: (i, 0)),
            pl.BlockSpec(first.shape, lambda i: (0, 0)),
        ],
        out_specs=pl.BlockSpec((rows, LANES), lambda i: (i, 0)),
        out_shape=jax.ShapeDtypeStruct((n, LANES), jnp.int32),
        compiler_params=_cparams(("arbitrary",)),
        name="moe_pos",
    )(code_pad, first)


def _dispatch_kernel(pos_ref, gap_ref, xn_hbm, out_hbm, xbuf, xsems, zero, sems, zsem):
    i = pl.program_id(0)
    n = pl.num_programs(0)
    tile_rows = xbuf.shape[1]
    tm = tile_rows // ROW_TILE
    base = i * tm * TOP_K_EXPERTS
    block_rows = zero.shape[0] // ROW_TILE
    n_blocks = out_hbm.shape[0] // zero.shape[0]

    def tile_in(tile):
        slot = lax.rem(tile, DISPATCH_SLOTS)
        first = pl.multiple_of(tile * tile_rows, tile_rows)
        return pltpu.make_async_copy(xn_hbm.at[pl.ds(first, tile_rows), :], xbuf.at[slot], xsems.at[slot])

    def wait_rows_out(tile):
        slot = lax.rem(tile, DISPATCH_SLOTS)
        for _ in range(TOP_K_EXPERTS):
            pltpu.make_async_copy(xbuf.at[slot], out_hbm.at[pl.ds(0, tile_rows), :], sems.at[slot]).wait()

    @pl.when(i == 0)
    def _():
        tile_in(i).start()

    @pl.when(i >= 2)
    def _():
        wait_rows_out(i - 2)

    @pl.when(i + 1 < n)
    def _():
        tile_in(i + 1).start()

    @pl.when(i == 0)
    def _():
        zero[...] = jnp.zeros_like(zero)

        def each_gap(visit):
            def gap(e, carry):
                start = gap_ref[2 * e]

                def row(r, c):
                    visit(pltpu.make_async_copy(_row_tile(zero, 0), _row_tile(out_hbm, start + r), zsem))
                    return c

                lax.fori_loop(0, gap_ref[2 * e + 1], row, 0)
                return carry

            lax.fori_loop(0, N_EXPERTS, gap, 0)

            def unused_block(b, carry):
                first = pl.multiple_of(b * block_rows * ROW_TILE, ROW_TILE)
                visit(pltpu.make_async_copy(zero, out_hbm.at[pl.ds(first, block_rows * ROW_TILE), :], zsem))
                return carry

            lax.fori_loop(gap_ref[2 * N_EXPERTS], n_blocks, unused_block, 0)

        each_gap(lambda copy: copy.start())
        each_gap(lambda copy: copy.wait())

    slot = lax.rem(i, DISPATCH_SLOTS)
    tile_in(i).wait()

    def issue(t, carry):
        for k in range(TOP_K_EXPERTS):
            p = pos_ref[base + t * TOP_K_EXPERTS + k]
            pltpu.make_async_copy(_row_tile(xbuf.at[slot], t), _row_tile(out_hbm, p),
                                  sems.at[slot]).start(priority=k)
        return carry

    lax.fori_loop(0, tm, issue, 0, unroll=4)

    @pl.when(i == n - 1)
    def _():
        @pl.when(i >= 1)
        def _():
            wait_rows_out(i - 1)

        wait_rows_out(i)


def _dispatch(xn, pos, gaps, n_rows):
    tm = TOKEN_TILE
    n = xn.shape[0] // ROW_TILE
    assert DISPATCH_SLOTS >= 3
    grid_spec = pltpu.PrefetchScalarGridSpec(
        num_scalar_prefetch=2,
        grid=(n // tm,),
        in_specs=[pl.BlockSpec(memory_space=pl.ANY)],
        out_specs=pl.BlockSpec(memory_space=pl.ANY),
        scratch_shapes=[pltpu.VMEM((DISPATCH_SLOTS, tm * ROW_TILE, LANES), F32),
                        pltpu.SemaphoreType.DMA((DISPATCH_SLOTS,)),
                        pltpu.VMEM((FFN_ROWS * ROW_TILE, LANES), F32),
                        pltpu.SemaphoreType.DMA((DISPATCH_SLOTS,)),
                        pltpu.SemaphoreType.DMA(())],
    )
    return pl.pallas_call(
        _dispatch_kernel,
        grid_spec=grid_spec,
        out_shape=jax.ShapeDtypeStruct((n_rows * ROW_TILE, LANES), F32),
        compiler_params=_cparams(("arbitrary",)),
        name="moe_dispatch",
    )(pos, gaps, xn)


def _store_row_tiles(ref, x, first=0):
    for c in range(ROW_TILE):
        ref[pl.ds(first + c, x.shape[0], stride=ROW_TILE), :] = x[:, c * LANES:(c + 1) * LANES]


def _load_row_tiles(ref, first, rows, lead=()):
    return jnp.concatenate([ref[lead + (pl.ds(first + c, rows, stride=ROW_TILE), slice(None))]
                            for c in range(ROW_TILE)], axis=1)


def _row_tile(ref, r):
    return ref.at[pl.ds(pl.multiple_of(r * ROW_TILE, ROW_TILE), ROW_TILE), :]


def _ffn_kernel(blk_e_ref, nblk_ref, fresh_ref, run_ref, next_e_ref, x_hbm, wg_hbm, wu_hbm, wd_hbm, out_ref,
                xbuf, xsems, wg_f, wu_f, wd_f, wsems, wg_s, wu_s, wd_s, *, layer):
    i = pl.program_id(0)
    nblk = nblk_ref[0]
    tile_rows = out_ref.shape[0]
    rows = tile_rows // ROW_TILE
    live = i < nblk

    def x_copy(block, slot):
        first = pl.multiple_of(block * tile_rows, tile_rows)
        return pltpu.make_async_copy(x_hbm.at[pl.ds(first, tile_rows), :], xbuf.at[slot], xsems.at[slot])

    def w_copies(expert, slot):
        return [pltpu.make_async_copy(src.at[layer, expert], dst.at[slot], wsems.at[slot])
                for src, dst in ((wg_hbm, wg_f), (wu_hbm, wu_f), (wd_hbm, wd_f))]

    @pl.when(i == 0)
    def _():
        for b in range(FFN_X_SLOTS - 1):
            @pl.when(b < nblk)
            def _():
                x_copy(b, b).start()

        @pl.when(nblk > 0)
        def _():
            for copy in w_copies(blk_e_ref[0], 0):
                copy.start()

    ahead = i + (FFN_X_SLOTS - 1)

    @pl.when(ahead < nblk)
    def _():
        x_copy(ahead, lax.rem(ahead, FFN_X_SLOTS)).start()

    @pl.when(live & (fresh_ref[i] == 1))
    def _():
        slot = lax.rem(run_ref[i], 2)
        for copy in w_copies(0, slot):
            copy.wait()
        wg_s[...] = wg_f[slot].astype(BF16)
        wu_s[...] = wu_f[slot].astype(BF16)
        wd_s[...] = wd_f[slot].astype(BF16)

        @pl.when(next_e_ref[i] >= 0)
        def _():
            for copy in w_copies(next_e_ref[i], 1 - slot):
                copy.start()

    @pl.when(live)
    def _():
        slot = lax.rem(i, FFN_X_SLOTS)
        x_copy(0, slot).wait()
        x = _load_row_tiles(xbuf, 0, rows, lead=(slot,)).astype(BF16)
        gate = jnp.dot(x, wg_s[...], preferred_element_type=F32)
        up = jnp.dot(x, wu_s[...], preferred_element_type=F32)
        hid = (jax.nn.silu(gate) * up).astype(BF16)
        _store_row_tiles(out_ref, jnp.dot(hid, wd_s[...], preferred_element_type=F32))

    @pl.when(jnp.logical_not(live))
    def _():
        out_ref[...] = jnp.zeros_like(out_ref)


def _grouped_ffn(x_sorted, blk_e, nblk, fresh, run, next_e, w_gate, w_up, w_down, layer):
    rb = FFN_ROWS
    n_rows = x_sorted.shape[0] // ROW_TILE
    _, _, d, d_e = w_gate.shape
    assert d == ROW_TILE * LANES
    any_spec = pl.BlockSpec(memory_space=pl.ANY)
    grid_spec = pltpu.PrefetchScalarGridSpec(
        num_scalar_prefetch=5,
        grid=(n_rows // rb,),
        in_specs=[any_spec, any_spec, any_spec, any_spec],
        out_specs=pl.BlockSpec((rb * ROW_TILE, LANES), lambda i, *_: (i, 0)),
        scratch_shapes=[
            pltpu.VMEM((FFN_X_SLOTS, rb * ROW_TILE, LANES), F32), pltpu.SemaphoreType.DMA((FFN_X_SLOTS,)),
            pltpu.VMEM((2, d, d_e), F32), pltpu.VMEM((2, d, d_e), F32), pltpu.VMEM((2, d_e, d), F32),
            pltpu.SemaphoreType.DMA((2,)),
            pltpu.VMEM((d, d_e), BF16), pltpu.VMEM((d, d_e), BF16), pltpu.VMEM((d_e, d), BF16),
        ],
    )
    return pl.pallas_call(
        functools.partial(_ffn_kernel, layer=layer),
        grid_spec=grid_spec,
        out_shape=jax.ShapeDtypeStruct((n_rows * ROW_TILE, LANES), F32),
        compiler_params=_cparams(("arbitrary",)),
        name="moe_ffn",
    )(blk_e, nblk, fresh, run, next_e, x_sorted, w_gate, w_up, w_down)


def _combine_kernel(pos_ref, resid_ref, w_ref, src_hbm, *refs, split):
    out_refs, buf, sems = refs[:-2], refs[-2], refs[-1]
    i = pl.program_id(0)
    tm = resid_ref.shape[0]

    def gather(tile, slot):
        base = tile * tm * TOP_K_EXPERTS

        def issue(t, carry):
            for k in range(TOP_K_EXPERTS):
                p = pos_ref[base + t * TOP_K_EXPERTS + k]
                pltpu.make_async_copy(_row_tile(src_hbm, p), _row_tile(buf, (slot * TOP_K_EXPERTS + k) * tm + t),
                                      sems.at[slot]).start(priority=k)
            return carry

        lax.fori_loop(0, tm, issue, 0, unroll=4)

    @pl.when(i == 0)
    def _():
        gather(0, 0)

    @pl.when(i + 1 < pl.num_programs(0))
    def _():
        gather(i + 1, (i + 1) % 2)

    slot = i % 2
    firsts = [pl.multiple_of((slot * TOP_K_EXPERTS + k) * tm * ROW_TILE, ROW_TILE) for k in range(TOP_K_EXPERTS)]
    for first in firsts:
        pltpu.make_async_copy(src_hbm.at[pl.ds(0, tm * ROW_TILE), :],
                              buf.at[pl.ds(first, tm * ROW_TILE), :], sems.at[slot]).wait()
    acc = resid_ref[...]
    w = w_ref[...]
    for k, first in enumerate(firsts):
        acc = acc + w[:, k:k + 1] * _load_row_tiles(buf, first, tm)
    def put(ref):
        ref[...] = acc

    if split:
        _store_stream_tile(out_refs[0], out_refs[1], put)
    else:
        put(out_refs[0])


def _combine(resid, w_pad, src, pos, split):
    n, d = resid.shape
    assert d == ROW_TILE * LANES and TOP_K_EXPERTS == 2
    tm = TOKEN_TILE
    n_prompt_tiles = n // tm - 1
    if split:
        out_specs = [_prompt_spec((tm, d), n_prompt_tiles), _sample_spec((tm, d))]
        out_shape = [jax.ShapeDtypeStruct((n_prompt_tiles * tm, d), F32), jax.ShapeDtypeStruct((tm, d), F32)]
    else:
        out_specs = [pl.BlockSpec((tm, d), lambda i, *_: (i, 0))]
        out_shape = [jax.ShapeDtypeStruct((n, d), F32)]
    grid_spec = pltpu.PrefetchScalarGridSpec(
        num_scalar_prefetch=1,
        grid=(n // tm,),
        in_specs=[
            pl.BlockSpec((tm, d), lambda i, *_: (i, 0)),
            pl.BlockSpec((tm, LANES), lambda i, *_: (i, 0)),
            pl.BlockSpec(memory_space=pl.ANY),
        ],
        out_specs=out_specs,
        scratch_shapes=[pltpu.VMEM((2 * TOP_K_EXPERTS * tm * ROW_TILE, LANES), F32),
                        pltpu.SemaphoreType.DMA((2,))],
    )
    return pl.pallas_call(
        functools.partial(_combine_kernel, split=split),
        grid_spec=grid_spec,
        out_shape=out_shape,
        compiler_params=_cparams(("arbitrary",)),
        name="moe_combine",
    )(pos, resid, w_pad, src)


def _hier_moe(h, g, w_r, b_r, w_gate, w_up, w_down, layer, split=False, attn=None):
    assert N_EXPERTS & (N_EXPERTS - 1) == 0
    n, _ = h.shape
    rb = FFN_ROWS
    if attn is None:
        xn, w_pad, code_pad, hist_pad = _router(h, g, w_r, b_r)
    else:
        h, xn, w_pad, code_pad, hist_pad = _router(h, g, w_r, b_r, attn)
    hist = hist_pad[:, 0, :N_EXPERTS]
    counts = jnp.sum(hist, axis=0)
    padded = (counts + rb - 1) // rb * rb
    pend = jnp.cumsum(padded)
    first = ((pend - padded)[None, :] + jnp.cumsum(hist, axis=0) - hist).astype(jnp.int32)
    first = jnp.pad(first, ((0, 0), (0, LANES - N_EXPERTS)))
    pos = _sorted_positions(code_pad, first)[:, :TOP_K_EXPERTS].reshape(-1)
    n_blocks = (n * TOP_K_EXPERTS + N_EXPERTS * (rb - 1) + rb - 1) // rb
    block_first = jnp.arange(n_blocks, dtype=jnp.int32)[:, None] * rb
    blk_e = jnp.minimum(jnp.sum((pend[None, :] <= block_first).astype(jnp.int32), axis=1), N_EXPERTS - 1)
    fresh = jnp.concatenate([jnp.ones((1,), jnp.int32), (blk_e[1:] != blk_e[:-1]).astype(jnp.int32)])
    run = (jnp.cumsum(fresh) - 1).astype(jnp.int32)
    experts = jnp.arange(N_EXPERTS, dtype=jnp.int32)
    later = jnp.where((counts[None, :] > 0) & (experts[None, :] > experts[:, None]), experts[None, :], N_EXPERTS)
    next_owner = jnp.min(later, axis=1)
    next_e = jnp.where(next_owner < N_EXPERTS, next_owner, -1)[blk_e].astype(jnp.int32)
    nblk = (pend[-1:] // rb).astype(jnp.int32)
    gaps = jnp.concatenate([jnp.stack([pend - padded + counts, padded - counts], axis=1).reshape(-1),
                            nblk]).astype(jnp.int32)
    x_sorted = _dispatch(xn, pos, gaps, n_blocks * rb)
    out_sorted = _grouped_ffn(x_sorted, blk_e, nblk, fresh, run, next_e, w_gate, w_up, w_down, layer)
    out = _combine(h, w_pad, out_sorted, pos, split)
    return out if split else out[0]


def _store_heads_as_rows(ref, x):
    for j in range(N_KV_HEADS):
        ref[pl.ds(j, x.shape[0], stride=N_KV_HEADS), :] = x[:, j * HEAD_DIM:(j + 1) * HEAD_DIM]


def _proj_kernel(h_ref, gkv_ref, gq_ref, wkv_ref, wq_ref, kn_ref, qn_ref, cos_ref, sin_ref,
                 k_ref, v_ref, q_ref, kp_ref, ks_ref, vp_ref, vs_ref):
    hn = _rms(h_ref[...])
    cos = cos_ref[...]
    sin = sin_ref[...]

    def norm_rope(x, g):
        y = _rms(x) * g
        return y * cos + pltpu.roll(y, HEAD_DIM // 2, 1) * sin

    kv = jnp.dot((hn * gkv_ref[...]).astype(BF16), wkv_ref[...], preferred_element_type=F32)
    kw = N_KV_HEADS * HEAD_DIM
    k = jnp.concatenate(
        [norm_rope(kv[:, j * HEAD_DIM:(j + 1) * HEAD_DIM], kn_ref[...]) for j in range(N_KV_HEADS)], axis=1)
    v = kv[:, kw:]
    k_ref[...] = k
    v_ref[...] = v
    _store_stream_tile(kp_ref, ks_ref, lambda ref: _store_heads_as_rows(ref, k))
    _store_stream_tile(vp_ref, vs_ref, lambda ref: _store_heads_as_rows(ref, v))
    q = jnp.dot((hn * gq_ref[...]).astype(BF16), wq_ref[...], preferred_element_type=F32)
    q_ref[...] = jnp.concatenate(
        [norm_rope(q[:, j * HEAD_DIM:(j + 1) * HEAD_DIM], qn_ref[...]) for j in range(N_HEADS)], axis=1)


def _kvq_proj(h, g_kv, g_q, w_kv, w_q, k_norm, q_norm, cos, sin):
    n, d = h.shape
    tm = TOKEN_TILE
    kw = N_KV_HEADS * HEAD_DIM
    qw = N_HEADS * HEAD_DIM
    row = lambda i: (i, 0)
    fixed = lambda i: (0, 0)
    n_prompt_tiles = n // tm - 1
    seq_tiles = cos.shape[0] // tm - 1
    table_row = lambda i: (jnp.where(i < n_prompt_tiles, lax.rem(i, seq_tiles), seq_tiles), 0)
    head_rows = tm * N_KV_HEADS
    by_head = [_prompt_spec((head_rows, HEAD_DIM), n_prompt_tiles), _sample_spec((head_rows, HEAD_DIM))]
    by_head_shapes = [jax.ShapeDtypeStruct((n_prompt_tiles * head_rows, HEAD_DIM), F32),
                      jax.ShapeDtypeStruct((head_rows, HEAD_DIM), F32)]
    return pl.pallas_call(
        _proj_kernel,
        grid=(n // tm,),
        in_specs=[
            pl.BlockSpec((tm, d), row),
            pl.BlockSpec((1, d), fixed),
            pl.BlockSpec((1, d), fixed),
            pl.BlockSpec((d, 2 * kw), fixed),
            pl.BlockSpec((d, qw), fixed),
            pl.BlockSpec((1, HEAD_DIM), fixed),
            pl.BlockSpec((1, HEAD_DIM), fixed),
            pl.BlockSpec((tm, HEAD_DIM), table_row),
            pl.BlockSpec((tm, HEAD_DIM), table_row),
        ],
        out_specs=[
            pl.BlockSpec((tm, kw), row),
            pl.BlockSpec((tm, kw), row),
            pl.BlockSpec((tm, qw), row),
        ] + by_head + by_head,
        out_shape=[
            jax.ShapeDtypeStruct((n, kw), F32),
            jax.ShapeDtypeStruct((n, kw), F32),
            jax.ShapeDtypeStruct((n, qw), F32),
        ] + by_head_shapes + by_head_shapes,
        compiler_params=_cparams(("arbitrary",)),
        name="kvq_proj",
    )(h, g_kv, g_q, w_kv, w_q, k_norm, q_norm, cos, sin)


def _top_blocks(gate, axis):
    idx = lax.broadcasted_iota(jnp.int32, gate.shape, axis)
    big = jnp.int32(gate.shape[axis])
    sel = jnp.zeros(gate.shape, jnp.bool_)
    for _ in range(MOBA_TOP_K):
        top = jnp.max(gate, axis=axis, keepdims=True)
        first = jnp.min(jnp.where(gate == top, idx, big), axis=axis, keepdims=True)
        hit = idx == first
        sel = sel | (hit & (top > NEG_INF))
        gate = jnp.where(hit, NEG_INF, gate)
    return sel.astype(F32)


def _moba_prompt_kernel(q_ref, k_ref, v_ref, o_ref, kbf, vt, kmean, sel, s_a, s_b, m_scr, acc):
    j = pl.program_id(2)
    blk = MOBA_BLOCK
    grp = KEY_GROUP
    seq = k_ref.shape[0]
    n_blocks = seq // blk

    @pl.when(j == 0)
    def _():
        k = k_ref[...]
        kbf[...] = k.astype(BF16)
        kmean[...] = jnp.mean(k.reshape(n_blocks, blk, HEAD_DIM), axis=1)
        for n in range(n_blocks):
            vt[:HEAD_DIM, n * blk:(n + 1) * blk] = v_ref[n * blk:(n + 1) * blk, :].T.astype(BF16)
        r = lax.broadcasted_iota(jnp.int32, (ONES_ROWS, seq), 0)
        vt[HEAD_DIM:, :] = jnp.where(r == 0, 1.0, 0.0).astype(BF16)

    q2 = q_ref[...]
    qs = jnp.concatenate([q2[:, h * HEAD_DIM:(h + 1) * HEAD_DIM] for h in range(Q_PER_KV)], axis=0)
    nq = qs.shape[0]
    qt = (qs * (HEAD_DIM ** -0.5 * LOG2_E)).T.astype(BF16)

    def scores(start, n_keys):
        return jnp.dot(kbf[pl.ds(start, n_keys), :], qt, preferred_element_type=F32)

    key = lax.broadcasted_iota(jnp.int32, (blk, nq), 0)
    qpos = lax.broadcasted_iota(jnp.int32, (blk, nq), 1) % blk
    own = pl.multiple_of(j * blk, blk)
    s_own = jnp.where(key <= qpos, scores(own, blk), NEG_INF)
    m_own = jnp.max(s_own, axis=0, keepdims=True)
    m_scr[...] = m_own
    acc[...] = jnp.dot(vt[:, pl.ds(own, blk)], jnp.exp2(s_own - m_own).astype(BF16), preferred_element_type=F32)

    gate = _dot_nt_3pass(kmean[...], qs)
    row = lax.broadcasted_iota(jnp.int32, gate.shape, 0)
    sel[...] = _top_blocks(jnp.where(row < j, gate, NEG_INF), 0)

    def update(tiles, start):
        m_old = m_scr[...]
        m_new = m_old
        for s in tiles:
            m_new = jnp.maximum(m_new, jnp.max(s, axis=0, keepdims=True))
        p = jnp.concatenate([jnp.exp2(s - m_new).astype(BF16) for s in tiles], axis=0)
        alpha = jnp.exp2(m_old - m_new)
        pv = jnp.dot(vt[:, pl.ds(start, len(tiles) * blk)], p, preferred_element_type=F32)
        acc[...] = alpha * acc[...] + pv
        m_scr[...] = m_new

    def fill(buf, g):
        start = pl.multiple_of(g * (grp * blk), grp * blk)
        buf[...] = scores(start, grp * blk)

    def consume(buf, g):
        start = pl.multiple_of(g * (grp * blk), grp * blk)
        update([jnp.where(sel[pl.ds(g * grp + i, 1), :] > 0.0, buf[i * blk:(i + 1) * blk, :], NEG_INF)
                for i in range(grp)], start)

    n_groups = lax.div(j + (grp - 1), grp)
    n_pairs = lax.div(n_groups - 1, 2)
    fill(s_a, 0)

    @pl.when(n_groups > 0)
    def _():
        def pair(h, carry):
            g = 2 * h
            fill(s_b, g + 1)
            consume(s_a, g)
            fill(s_a, g + 2)
            consume(s_b, g + 1)
            return carry

        lax.fori_loop(0, n_pairs, pair, 0)
        g = 2 * n_pairs

        @pl.when(n_groups - g == 1)
        def _():
            consume(s_a, g)

        @pl.when(n_groups - g == 2)
        def _():
            fill(s_b, g + 1)
            consume(s_a, g)
            consume(s_b, g + 1)

    a = acc[...]
    o = (a[:HEAD_DIM] / a[HEAD_DIM:HEAD_DIM + 1]).T
    o_ref[...] = jnp.concatenate([o[h * blk:(h + 1) * blk, :] for h in range(Q_PER_KV)], axis=1)


def _moba_prompt(q, k, v, batch, seq):
    blk = MOBA_BLOCK
    assert seq % (KEY_GROUP * blk) == 0
    n_steps = seq // blk
    qw = Q_PER_KV * HEAD_DIM
    nq = Q_PER_KV * blk
    return pl.pallas_call(
        _moba_prompt_kernel,
        grid=(batch, N_KV_HEADS, n_steps),
        in_specs=[
            pl.BlockSpec((blk, qw), lambda b, c, j: (b * n_steps + j, c)),
            pl.BlockSpec((seq, HEAD_DIM), lambda b, c, j: (b, c)),
            pl.BlockSpec((seq, HEAD_DIM), lambda b, c, j: (b, c)),
        ],
        out_specs=pl.BlockSpec((blk, qw), lambda b, c, j: (b * n_steps + j, c)),
        out_shape=jax.ShapeDtypeStruct((batch * seq, N_HEADS * HEAD_DIM), F32),
        scratch_shapes=[
            pltpu.VMEM((seq, HEAD_DIM), BF16),
            pltpu.VMEM((HEAD_DIM + ONES_ROWS, seq), BF16),
            pltpu.VMEM((seq // blk, HEAD_DIM), F32),
            pltpu.VMEM((seq // blk, nq), F32),
            pltpu.VMEM((KEY_GROUP * blk, nq), F32),
            pltpu.VMEM((KEY_GROUP * blk, nq), F32),
            pltpu.VMEM((1, nq), F32),
            pltpu.VMEM((HEAD_DIM + ONES_ROWS, nq), F32),
        ],
        compiler_params=_cparams(("arbitrary", "arbitrary", "arbitrary")),
        name="moba_prompt",
    )(q, k, v)


def _stack_heads(q8):
    return jnp.concatenate([q8[:, h * HEAD_DIM:(h + 1) * HEAD_DIM] for h in range(N_HEADS)], axis=0)


def _sample_attn_kernel(pt_ref, q_ref, kn_ref, vn_ref, ck_hbm, cv_hbm, o_ref, pages, sems,
                        s_scr, means, sel_scr, qs_scr, m_scr, l_scr, acc, *, dec_seq, n_pages):
    pps = PAGES_PER_STEP
    t = pl.program_id(1)
    steps = pl.num_programs(1)
    n_k_steps = n_pages // pps
    rows = N_HEADS * dec_seq
    rkv = Q_PER_KV * dec_seq
    ppb = MOBA_BLOCK // PAGE_SIZE
    bps = pps // ppb
    n_blocks = n_pages // ppb
    nt_dims = (((1,), (1,)), ((), ()))

    chunk = pl.program_id(0) * steps + t
    n_chunks = pl.num_programs(0) * steps

    def start_chunk(ci):
        seq = lax.div(ci, steps)
        step = lax.rem(ci, steps)
        slot = lax.rem(ci, PAGE_SLOTS)

        def start_pages(cache_hbm, first_page):
            for r in range(pps):
                pltpu.make_async_copy(cache_hbm.at[pt_ref[seq, first_page + r]], pages.at[slot, r],
                                      sems.at[slot]).start()

        @pl.when(step < n_k_steps)
        def _():
            start_pages(ck_hbm, step * pps)

        @pl.when(step >= n_k_steps)
        def _():
            start_pages(cv_hbm, (step - n_k_steps) * pps)

    @pl.when(chunk == 0)
    def _():
        for ci in range(PAGE_SLOTS - 1):
            start_chunk(jnp.int32(ci))

    @pl.when(chunk + (PAGE_SLOTS - 1) < n_chunks)
    def _():
        start_chunk(chunk + (PAGE_SLOTS - 1))

    slot = lax.rem(chunk, PAGE_SLOTS)
    pltpu.make_async_copy(ck_hbm.at[pl.ds(0, pps)], pages.at[slot], sems.at[slot]).wait()

    def head_rows(p, c):
        return pages[slot, p, pl.ds(c, PAGE_SIZE, stride=N_KV_HEADS), :]

    @pl.when(t == 0)
    def _():
        qs_scr[...] = (_stack_heads(q_ref[...]) * (HEAD_DIM ** -0.5 * LOG2_E)).astype(BF16)
        m_scr[...] = jnp.full(m_scr.shape, NEG_INF, F32)
        l_scr[...] = jnp.zeros(l_scr.shape, F32)
        acc[...] = jnp.zeros(acc.shape, F32)

    @pl.when(t < n_k_steps)
    def _():
        qs = qs_scr[...]
        col = pl.multiple_of(t * (pps * PAGE_SIZE), pps * PAGE_SIZE)
        for c in range(N_KV_HEADS):
            kc = jnp.concatenate([head_rows(p, c) for p in range(pps)], axis=0)
            s_scr[c * rkv:(c + 1) * rkv, pl.ds(col, pps * PAGE_SIZE)] = lax.dot_general(
                qs[c * rkv:(c + 1) * rkv], kc.astype(BF16), nt_dims, preferred_element_type=F32)
            means[c, pl.ds(t * bps, bps), :] = jnp.sum(kc.reshape(bps, MOBA_BLOCK, HEAD_DIM), axis=1) / MOBA_BLOCK

    @pl.when(t == n_k_steps - 1)
    def _():
        qf = _stack_heads(q_ref[...])
        gate = jnp.concatenate(
            [lax.dot_general(qf[c * rkv:(c + 1) * rkv], means[c], nt_dims, precision=HIGHEST,
                             preferred_element_type=F32) for c in range(N_KV_HEADS)], axis=0)
        chosen = _top_blocks(gate, 1)
        sel_scr[...] = jnp.concatenate([chosen, jnp.zeros((rows, LANES - n_blocks), F32)], axis=1)

    def softmax_step(tiles):
        m_old = m_scr[...]
        m_new = m_old
        for s in tiles:
            m_new = jnp.maximum(m_new, jnp.max(s, axis=1, keepdims=True))
        m_safe = jnp.where(m_new == NEG_INF, 0.0, m_new)
        alpha = jnp.exp2(m_old - m_safe)
        l_new = alpha * l_scr[...]
        probs = []
        for s in tiles:
            p = jnp.exp2(s - m_safe)
            l_new = l_new + jnp.sum(p, axis=1, keepdims=True)
            probs.append(p.astype(BF16))
        l_scr[...] = l_new
        m_scr[...] = m_new
        return probs, alpha

    @pl.when(t >= n_k_steps)
    def _():
        tv = t - n_k_steps
        sel = sel_scr[...]
        lane = lax.broadcasted_iota(jnp.int32, sel.shape, 1)
        tiles = []
        for b in range(bps):
            n = tv * bps + b
            chosen = jnp.sum(jnp.where(lane == n, sel, 0.0), axis=1, keepdims=True) > 0.0
            col = pl.multiple_of(n * MOBA_BLOCK, MOBA_BLOCK)
            tiles.append(jnp.where(chosen, s_scr[:, pl.ds(col, MOBA_BLOCK)], NEG_INF))
        probs, alpha = softmax_step(tiles)
        prob = jnp.concatenate(probs, axis=1)
        pv = []
        for c in range(N_KV_HEADS):
            vc = jnp.concatenate([head_rows(p, c) for p in range(pps)], axis=0).astype(BF16)
            pv.append(jnp.dot(prob[c * rkv:(c + 1) * rkv], vc, preferred_element_type=F32))
        acc[...] = alpha * acc[...] + jnp.concatenate(pv, axis=0)

    @pl.when(t == pl.num_programs(1) - 1)
    def _():
        qs = qs_scr[...]
        kn = kn_ref[...].astype(BF16)
        vn = vn_ref[...].astype(BF16)
        s = jnp.concatenate(
            [lax.dot_general(qs[c * rkv:(c + 1) * rkv], kn[:, c * HEAD_DIM:(c + 1) * HEAD_DIM], nt_dims,
                             preferred_element_type=F32) for c in range(N_KV_HEADS)], axis=0)
        r2 = lax.broadcasted_iota(jnp.int32, s.shape, 0)
        c2 = lax.broadcasted_iota(jnp.int32, s.shape, 1)
        probs, alpha = softmax_step([jnp.where(c2 <= r2 % dec_seq, s, NEG_INF)])
        pv = [jnp.dot(probs[0][c * rkv:(c + 1) * rkv], vn[:, c * HEAD_DIM:(c + 1) * HEAD_DIM],
                      preferred_element_type=F32) for c in range(N_KV_HEADS)]
        o = (alpha * acc[...] + jnp.concatenate(pv, axis=0)) / l_scr[...]
        o_ref[...] = jnp.concatenate([o[h * dec_seq:(h + 1) * dec_seq, :] for h in range(N_HEADS)], axis=1)


def _sample_attn(page_table, q, k, v, row0, cache_k2, cache_v2, dec_seq):
    n_seq, n_pages = page_table.shape
    pps = PAGES_PER_STEP
    n_k_steps = n_pages // pps
    n_blocks = n_pages * PAGE_SIZE // MOBA_BLOCK
    assert n_pages % pps == 0 and n_blocks <= LANES
    rows = N_HEADS * dec_seq
    kw = N_KV_HEADS * HEAD_DIM
    qw = N_HEADS * HEAD_DIM
    grid_spec = pltpu.PrefetchScalarGridSpec(
        num_scalar_prefetch=1,
        grid=(n_seq, 2 * n_k_steps),
        in_specs=[
            pl.BlockSpec((dec_seq, qw), lambda s, t, pt: (row0 + s, 0)),
            pl.BlockSpec((dec_seq, kw), lambda s, t, pt: (row0 + s, 0)),
            pl.BlockSpec((dec_seq, kw), lambda s, t, pt: (row0 + s, 0)),
            pl.BlockSpec(memory_space=pl.ANY),
            pl.BlockSpec(memory_space=pl.ANY),
        ],
        out_specs=pl.BlockSpec((dec_seq, qw), lambda s, t, pt: (s, 0)),
        scratch_shapes=[
            pltpu.VMEM((PAGE_SLOTS, pps, PAGE_SIZE * N_KV_HEADS, HEAD_DIM), F32),
            pltpu.SemaphoreType.DMA((PAGE_SLOTS,)),
            pltpu.VMEM((rows, n_pages * PAGE_SIZE), F32),
            pltpu.VMEM((N_KV_HEADS, n_blocks, HEAD_DIM), F32),
            pltpu.VMEM((rows, LANES), F32),
            pltpu.VMEM((rows, HEAD_DIM), BF16),
            pltpu.VMEM((rows, 1), F32),
            pltpu.VMEM((rows, 1), F32),
            pltpu.VMEM((rows, HEAD_DIM), F32),
        ],
    )
    return pl.pallas_call(
        functools.partial(_sample_attn_kernel, dec_seq=dec_seq, n_pages=n_pages),
        grid_spec=grid_spec,
        out_shape=jax.ShapeDtypeStruct((n_seq * dec_seq, qw), F32),
        compiler_params=_cparams(("arbitrary", "arbitrary")),
        name="sample_attn",
    )(page_table, q, k, v, cache_k2, cache_v2)


def _mix_tables(w_s, b_s, dec_seq):
    tm = TOKEN_TILE
    causal = jnp.tril(jnp.ones((GMLP_CHUNK, GMLP_CHUNK), dtype=bool))
    w = jnp.where(causal[None], w_s, jnp.zeros_like(w_s))
    eye_p = jnp.eye(tm // GMLP_CHUNK, dtype=w.dtype)
    mix_p = jnp.einsum("ab,gts->gatbs", eye_p, w).reshape(GMLP_GROUPS, tm, tm)
    eye_s = jnp.eye(tm // dec_seq, dtype=w.dtype)
    mix_s = jnp.einsum("ab,gts->gatbs", eye_s, w[:, :dec_seq, :dec_seq]).reshape(GMLP_GROUPS, tm, tm)
    mix = jnp.stack([mix_p, mix_s]).astype(BF16)
    bias_p = jnp.tile(b_s.T, (tm // GMLP_CHUNK, 1))
    bias_s = jnp.tile(b_s.T[:dec_seq], (tm // dec_seq, 1))
    bias = jnp.stack([bias_p, bias_s])
    bias = jnp.pad(bias, ((0, 0), (0, 0), (0, LANES - GMLP_GROUPS)))
    return mix, bias


def _rope_tables(pos):
    half = HEAD_DIM // 2
    inv = ROPE_THETA ** (-jnp.arange(half, dtype=F32) * 2.0 / HEAD_DIM)
    ang = pos.astype(F32)[:, None] * inv[None, :]
    cos = jnp.cos(ang)
    sin = jnp.sin(ang)
    return jnp.concatenate([cos, cos], axis=1), jnp.concatenate([-sin, sin], axis=1)


def _router_tables(w_grp, b_grp, w_rt, b_rt):
    w = jnp.concatenate([w_grp, w_rt], axis=1)
    b = jnp.concatenate([b_grp, b_rt], axis=0)
    pad = LANES - w.shape[1]
    return jnp.pad(w, ((0, 0), (0, pad))), jnp.pad(b, (0, pad)).reshape(1, LANES)


def kernel(x_prompt, x_sample, cache_k, cache_v, page_table, norm_mix, norm_ffn, a_w_in, a_ln_g, a_ln_b,
           a_w_s, a_b_s, a_w_out, kv_norm, w_kv, k_norm, b_w_q, b_q_norm, b_w_o, moe_w_grp, moe_b_grp,
           moe_w_rt, moe_b_rt, moe_w_gate, moe_w_up, moe_w_down):
    batch, seq, d = x_prompt.shape
    n_seq, dec_seq, _ = x_sample.shape
    n_prompt = batch * seq
    n_sample = n_seq * dec_seq
    assert n_prompt % TOKEN_TILE == 0 and n_sample == TOKEN_TILE and seq % MOBA_BLOCK == 0
    past_len = page_table.shape[1] * PAGE_SIZE
    assert past_len % MOBA_BLOCK == 0 and dec_seq <= MOBA_BLOCK

    pos = jnp.concatenate([jnp.arange(seq), jnp.tile(past_len + jnp.arange(dec_seq), n_seq)])
    cos, sin = _rope_tables(pos)
    row = lambda a: a.reshape(1, -1)

    mix, bias = _mix_tables(a_w_s[0], a_b_s[0], dec_seq)
    h, vg_sample = _gmlp_layer(x_prompt.reshape(n_prompt, d), x_sample.reshape(n_sample, d), row(norm_mix[0]),
                               a_w_in[0].astype(BF16), row(a_ln_g[0]), row(a_ln_b[0]), mix, bias,
                               a_w_out[0].astype(BF16))
    moe = []
    for layer in range(2):
        w_r, b_r = _router_tables(moe_w_grp[layer], moe_b_grp[layer], moe_w_rt[layer], moe_b_rt[layer])
        moe.append((row(norm_ffn[layer]), w_r, b_r, moe_w_gate, moe_w_up, moe_w_down, layer))
    h = _hier_moe(h, *moe[0])

    k, v, q, k_p, k_s, v_p, v_s = _kvq_proj(h, row(kv_norm), row(norm_mix[1]), w_kv.astype(BF16),
                                            b_w_q[0].astype(BF16), row(k_norm), row(b_q_norm[0]), cos, sin)

    o_prompt = _moba_prompt(q, k, v, batch, seq)
    n_phys = cache_k.shape[0]
    cache_k2 = cache_k.reshape(n_phys, PAGE_SIZE * N_KV_HEADS, HEAD_DIM)
    cache_v2 = cache_v.reshape(n_phys, PAGE_SIZE * N_KV_HEADS, HEAD_DIM)
    o_sample = _sample_attn(page_table, q, k, v, n_prompt // dec_seq, cache_k2, cache_v2, dec_seq)
    y_prompt, y_sample = _hier_moe(h, *moe[1], split=True, attn=(o_prompt, o_sample, b_w_o[0].astype(BF16)))

    n_pages_new = seq // PAGE_SIZE
    return (y_prompt.reshape(batch, seq, d),
            y_sample.reshape(n_seq, dec_seq, d),
            k_p.reshape(batch, n_pages_new, PAGE_SIZE, N_KV_HEADS, HEAD_DIM),
            v_p.reshape(batch, n_pages_new, PAGE_SIZE, N_KV_HEADS, HEAD_DIM),
            k_s.reshape(n_seq, dec_seq, N_KV_HEADS, HEAD_DIM),
            v_s.reshape(n_seq, dec_seq, N_KV_HEADS, HEAD_DIM),
            vg_sample.reshape(1, n_seq, dec_seq, -1))
```

```python
import functools
import math

import jax
import jax.numpy as jnp
from jax import lax
from jax.experimental import pallas as pl
from jax.experimental.pallas import tpu as pltpu

F32 = jnp.float32
BF16 = jnp.bfloat16
HIGHEST = lax.Precision.HIGHEST

GMLP_CHUNK = 128
GMLP_GROUPS = 8
N_HEADS = 8
N_KV_HEADS = 4
HEAD_DIM = 128
Q_PER_KV = N_HEADS // N_KV_HEADS
MOBA_BLOCK = 256
MOBA_TOP_K = 3
ROPE_THETA = 10000.0
N_GROUPS = 4
EXPERTS_PER_GROUP = 8
N_EXPERTS = N_GROUPS * EXPERTS_PER_GROUP
TOP_K_EXPERTS = 2
PAGE_SIZE = 128
EPS = 1e-6

LANES = 128
ROW_TILE = 8
TOKEN_TILE = 256
FFN_ROWS = 256
FFN_X_SLOTS = 4
DISPATCH_SLOTS = 4
PAGES_PER_STEP = 16
PAGE_SLOTS = 4
KEY_GROUP = 4
ONES_ROWS = 16
LOG2_E = math.log2(math.e)
VMEM_LIMIT = 56 * 1024 * 1024

NEG_INF = float("-inf")


def _cparams(sem):
    return pltpu.CompilerParams(dimension_semantics=sem, vmem_limit_bytes=VMEM_LIMIT)


def _rms(x):
    return x * lax.rsqrt(jnp.mean(x * x, axis=-1, keepdims=True) + EPS)


def _dot_3pass(a, b, dims):
    a_hi = a.astype(BF16)
    b_hi = b.astype(BF16)
    a_lo = (a - a_hi.astype(F32)).astype(BF16)
    b_lo = (b - b_hi.astype(F32)).astype(BF16)
    dot = functools.partial(lax.dot_general, dimension_numbers=dims, preferred_element_type=F32)
    return dot(a_hi, b_hi) + (dot(a_hi, b_lo) + dot(a_lo, b_hi))


def _dot_nt_3pass(a, b):
    return _dot_3pass(a, b, (((1,), (1,)), ((), ())))


def _prompt_spec(block, n_prompt_tiles):
    return pl.BlockSpec(block, lambda i, *_: (jnp.minimum(i, n_prompt_tiles - 1), 0))


def _sample_spec(block):
    return pl.BlockSpec(block, lambda i, *_: (0, 0))


def _is_sample_tile():
    return pl.program_id(0) == pl.num_programs(0) - 1


def _stream_tile(prompt_ref, sample_ref):
    return jnp.where(_is_sample_tile(), sample_ref[...], prompt_ref[...])


def _store_stream_tile(prompt_ref, sample_ref, store):
    @pl.when(jnp.logical_not(_is_sample_tile()))
    def _():
        store(prompt_ref)

    @pl.when(_is_sample_tile())
    def _():
        store(sample_ref)


def _gmlp_kernel(xp_ref, xs_ref, g_ref, win_ref, lng_ref, lnb_ref, mix_ref, bias_ref, wout_ref,
                 h_ref, vg_ref, *, d_gate, n_groups):
    i = pl.program_id(0)
    x = _stream_tile(xp_ref, xs_ref)
    xb = (_rms(x) * g_ref[...]).astype(BF16)
    vp = jax.nn.gelu(jnp.dot(xb, win_ref[:, d_gate:], preferred_element_type=F32))
    vc = vp - jnp.mean(vp, axis=-1, keepdims=True)
    var = jnp.mean(vc * vc, axis=-1, keepdims=True)
    vg = vc * lax.rsqrt(var + EPS) * lng_ref[...] + lnb_ref[...]
    vgb = vg.astype(BF16)
    cw = d_gate // n_groups
    bias = bias_ref[0]
    u = jax.nn.gelu(jnp.dot(xb, win_ref[:, :d_gate], preferred_element_type=F32))
    parts = []
    for g in range(n_groups):
        mixed = jnp.dot(mix_ref[0, g], vgb[:, g * cw:(g + 1) * cw], preferred_element_type=F32)
        mixed = mixed + bias[:, g:g + 1]
        parts.append((u[:, g * cw:(g + 1) * cw] * mixed).astype(BF16))
    gated = jnp.concatenate(parts, axis=1)
    h_ref[...] = x + jnp.dot(gated, wout_ref[...], preferred_element_type=F32)

    @pl.when(i == pl.num_programs(0) - 1)
    def _():
        vg_ref[...] = vg


def _gmlp_layer(x_prompt, x_sample, g, w_in, ln_g, ln_b, mix, bias, w_out):
    d = x_prompt.shape[1]
    d_gate = w_out.shape[0]
    tm = TOKEN_TILE
    n_prompt_tiles = x_prompt.shape[0] // tm
    n_tiles = n_prompt_tiles + 1
    n = n_tiles * tm
    kind = lambda i: jnp.where(i < n_prompt_tiles, 0, 1)
    return pl.pallas_call(
        functools.partial(_gmlp_kernel, d_gate=d_gate, n_groups=GMLP_GROUPS),
        grid=(n_tiles,),
        in_specs=[
            _prompt_spec((tm, d), n_prompt_tiles),
            _sample_spec((tm, d)),
            pl.BlockSpec((1, d), lambda i: (0, 0)),
            pl.BlockSpec((d, 2 * d_gate), lambda i: (0, 0)),
            pl.BlockSpec((1, d_gate), lambda i: (0, 0)),
            pl.BlockSpec((1, d_gate), lambda i: (0, 0)),
            pl.BlockSpec((1, GMLP_GROUPS, tm, tm), lambda i: (kind(i), 0, 0, 0)),
            pl.BlockSpec((1, tm, LANES), lambda i: (kind(i), 0, 0)),
            pl.BlockSpec((d_gate, d), lambda i: (0, 0)),
        ],
        out_specs=[
            pl.BlockSpec((tm, d), lambda i: (i, 0)),
            pl.BlockSpec((tm, d_gate), lambda i: (0, 0)),
        ],
        out_shape=[
            jax.ShapeDtypeStruct((n, d), F32),
            jax.ShapeDtypeStruct((tm, d_gate), F32),
        ],
        compiler_params=_cparams(("arbitrary",)),
        name="gmlp_layer",
    )(x_prompt, x_sample, g, w_in, ln_g, ln_b, mix, bias, w_out)


def _router_kernel(*refs, with_attn):
    if with_attn:
        h_ref, op_ref, os_ref, wo_ref, g_ref, wr_ref, br_ref, h_out, xn_ref, w_ref, code_ref, hist_ref = refs
        o = _stream_tile(op_ref, os_ref).astype(BF16)
        h = h_ref[...] + jnp.dot(o, wo_ref[...], preferred_element_type=F32)
        h_out[...] = h
    else:
        h_ref, g_ref, wr_ref, br_ref, xn_ref, w_ref, code_ref, hist_ref = refs
        h = h_ref[...]
    xn = _rms(h) * g_ref[...]
    _store_row_tiles(xn_ref, xn)
    logits = _dot_3pass(xn, wr_ref[...], (((1,), (0,)), ((), ()))) + br_ref[...]
    lane = lax.broadcasted_iota(jnp.int32, logits.shape, 1)
    big = jnp.int32(LANES)
    is_grp = lane < N_GROUPS
    gl = jnp.where(is_grp, logits, NEG_INF)
    gmax = jnp.max(gl, axis=1, keepdims=True)
    gidx = jnp.min(jnp.where(is_grp & (logits == gmax), lane, big), axis=1, keepdims=True)
    p_g = 1.0 / jnp.sum(jnp.where(is_grp, jnp.exp(gl - gmax), 0.0), axis=1, keepdims=True)
    lo = N_GROUPS + gidx * EXPERTS_PER_GROUP
    in_grp = (lane >= lo) & (lane < lo + EXPERTS_PER_GROUP)
    v0 = jnp.max(jnp.where(in_grp, logits, NEG_INF), axis=1, keepdims=True)
    i0 = jnp.min(jnp.where(in_grp & (logits == v0), lane, big), axis=1, keepdims=True)
    rest = in_grp & (lane != i0)
    v1 = jnp.max(jnp.where(rest, logits, NEG_INF), axis=1, keepdims=True)
    i1 = jnp.min(jnp.where(rest & (logits == v1), lane, big), axis=1, keepdims=True)
    t = jnp.exp(v1 - v0)
    w0 = p_g * (1.0 / (1.0 + t))
    w1 = p_g * (t / (1.0 + t))
    e0 = i0 - N_GROUPS
    e1 = i1 - N_GROUPS
    w_ref[...] = jnp.where(lane == 0, w0, jnp.where(lane == 1, w1, 0.0))
    tm = logits.shape[0]
    onehot = jnp.concatenate([(lane == e0).astype(F32), (lane == e1).astype(F32)], axis=0)
    a_row = lax.broadcasted_iota(jnp.int32, (2 * tm, 2 * tm), 0)
    a_col = lax.broadcasted_iota(jnp.int32, (2 * tm, 2 * tm), 1)
    earlier = (a_col < a_row).astype(BF16)
    before = jnp.dot(earlier, onehot.astype(BF16), preferred_element_type=F32)
    rank = jnp.sum(before * onehot, axis=1, keepdims=True).astype(jnp.int32)
    code_ref[...] = jnp.where(lane == 0, rank[:tm] * N_EXPERTS + e0,
                              jnp.where(lane == 1, rank[tm:] * N_EXPERTS + e1, 0))
    hist = jnp.sum(onehot, axis=0, keepdims=True).astype(jnp.int32)
    hist_ref[0] = hist


def _router(h, g, w_r, b_r, attn=None):
    n, d = h.shape
    tm = TOKEN_TILE
    row = lambda i: (i, 0)
    fixed = lambda i: (0, 0)
    in_specs = [pl.BlockSpec((tm, d), row)]
    out_specs, out_shape, operands = [], [], [h]
    if attn is not None:
        o_prompt, o_sample, w_o = attn
        ow = o_prompt.shape[1]
        in_specs += [_prompt_spec((tm, ow), n // tm - 1), _sample_spec((tm, ow)), pl.BlockSpec(w_o.shape, fixed)]
        operands += [o_prompt, o_sample, w_o]
        out_specs.append(pl.BlockSpec((tm, d), row))
        out_shape.append(jax.ShapeDtypeStruct((n, d), F32))
    in_specs += [pl.BlockSpec((1, d), fixed), pl.BlockSpec((d, LANES), fixed), pl.BlockSpec((1, LANES), fixed)]
    operands += [g, w_r, b_r]
    out_specs += [
        pl.BlockSpec((tm * ROW_TILE, LANES), row),
        pl.BlockSpec((tm, LANES), row),
        pl.BlockSpec((tm, LANES), row),
        pl.BlockSpec((1, 1, LANES), lambda i: (i, 0, 0)),
    ]
    out_shape += [
        jax.ShapeDtypeStruct((n * ROW_TILE, LANES), F32),
        jax.ShapeDtypeStruct((n, LANES), F32),
        jax.ShapeDtypeStruct((n, LANES), jnp.int32),
        jax.ShapeDtypeStruct((n // tm, 1, LANES), jnp.int32),
    ]
    return pl.pallas_call(
        functools.partial(_router_kernel, with_attn=attn is not None),
        grid=(n // tm,),
        in_specs=in_specs,
        out_specs=out_specs,
        out_shape=out_shape,
        compiler_params=_cparams(("arbitrary",)),
        name="moe_router",
    )(*operands)


def _pos_kernel(code_ref, first_ref, pos_ref, *, tiles_per_step):
    i = pl.program_id(0)
    tm = TOKEN_TILE
    shift = N_EXPERTS.bit_length() - 1
    for s in range(tiles_per_step):
        code = code_ref[s * tm:(s + 1) * tm, :]
        expert = lax.bitwise_and(code, N_EXPERTS - 1)
        rank = lax.shift_right_logical(code, shift)
        off = first_ref[pl.ds(i * tiles_per_step + s, 1), :]
        lane = lax.broadcasted_iota(jnp.int32, code.shape, 1)
        pos = [jnp.sum(jnp.where(lane == expert[:, k:k + 1], off, 0), axis=1, keepdims=True) + rank[:, k:k + 1]
               for k in range(TOP_K_EXPERTS)]
        pos_ref[s * tm:(s + 1) * tm, :] = jnp.where(lane == 0, pos[0], jnp.where(lane == 1, pos[1], 0))


def _sorted_positions(code_pad, first):
    n = code_pad.shape[0]
    n_tiles = n // TOKEN_TILE
    tiles_per_step = max(t for t in range(1, 17) if n_tiles % t == 0)
    rows = tiles_per_step * TOKEN_TILE
    return pl.pallas_call(
        functools.partial(_pos_kernel, tiles_per_step=tiles_per_step),
        grid=(n_tiles // tiles_per_step,),
        in_specs=[
            pl.BlockSpec((rows, LANES), lambda i: (i, 0)),
            pl.BlockSpec(first.shape, lambda i: (0, 0)),
        ],
        out_specs=pl.BlockSpec((rows, LANES), lambda i: (i, 0)),
        out_shape=jax.ShapeDtypeStruct((n, LANES), jnp.int32),
        compiler_params=_cparams(("arbitrary",)),
        name="moe_pos",
    )(code_pad, first)


def _dispatch_kernel(pos_ref, gap_ref, xn_hbm, out_hbm, xbuf, xsems, zero, sems, zsem):
    i = pl.program_id(0)
    n = pl.num_programs(0)
    tile_rows = xbuf.shape[1]
    tm = tile_rows // ROW_TILE
    base = i * tm * TOP_K_EXPERTS
    block_rows = zero.shape[0] // ROW_TILE
    n_blocks = out_hbm.shape[0] // zero.shape[0]

    def tile_in(tile):
        slot = lax.rem(tile, DISPATCH_SLOTS)
        first = pl.multiple_of(tile * tile_rows, tile_rows)
        return pltpu.make_async_copy(xn_hbm.at[pl.ds(first, tile_rows), :], xbuf.at[slot], xsems.at[slot])

    def wait_rows_out(tile):
        slot = lax.rem(tile, DISPATCH_SLOTS)
        for _ in range(TOP_K_EXPERTS):
            pltpu.make_async_copy(xbuf.at[slot], out_hbm.at[pl.ds(0, tile_rows), :], sems.at[slot]).wait()

    @pl.when(i == 0)
    def _():
        tile_in(i).start()

    @pl.when(i >= 2)
    def _():
        wait_rows_out(i - 2)

    @pl.when(i + 1 < n)
    def _():
        tile_in(i + 1).start()

    @pl.when(i == 0)
    def _():
        zero[...] = jnp.zeros_like(zero)

        def each_gap(visit):
            def gap(e, carry):
                start = gap_ref[2 * e]

                def row(r, c):
                    visit(pltpu.make_async_copy(_row_tile(zero, 0), _row_tile(out_hbm, start + r), zsem))
                    return c

                lax.fori_loop(0, gap_ref[2 * e + 1], row, 0)
                return carry

            lax.fori_loop(0, N_EXPERTS, gap, 0)

            def unused_block(b, carry):
                first = pl.multiple_of(b * block_rows * ROW_TILE, ROW_TILE)
                visit(pltpu.make_async_copy(zero, out_hbm.at[pl.ds(first, block_rows * ROW_TILE), :], zsem))
                return carry

            lax.fori_loop(gap_ref[2 * N_EXPERTS], n_blocks, unused_block, 0)

        each_gap(lambda copy: copy.start())
        each_gap(lambda copy: copy.wait())

    slot = lax.rem(i, DISPATCH_SLOTS)
    tile_in(i).wait()

    def issue(t, carry):
        for k in range(TOP_K_EXPERTS):
            p = pos_ref[base + t * TOP_K_EXPERTS + k]
            pltpu.make_async_copy(_row_tile(xbuf.at[slot], t), _row_tile(out_hbm, p),
                                  sems.at[slot]).start(priority=k)
        return carry

    lax.fori_loop(0, tm, issue, 0, unroll=4)

    @pl.when(i == n - 1)
    def _():
        @pl.when(i >= 1)
        def _():
            wait_rows_out(i - 1)

        wait_rows_out(i)


def _dispatch(xn, pos, gaps, n_rows):
    tm = TOKEN_TILE
    n = xn.shape[0] // ROW_TILE
    assert DISPATCH_SLOTS >= 3
    grid_spec = pltpu.PrefetchScalarGridSpec(
        num_scalar_prefetch=2,
        grid=(n // tm,),
        in_specs=[pl.BlockSpec(memory_space=pl.ANY)],
        out_specs=pl.BlockSpec(memory_space=pl.ANY),
        scratch_shapes=[pltpu.VMEM((DISPATCH_SLOTS, tm * ROW_TILE, LANES), F32),
                        pltpu.SemaphoreType.DMA((DISPATCH_SLOTS,)),
                        pltpu.VMEM((FFN_ROWS * ROW_TILE, LANES), F32),
                        pltpu.SemaphoreType.DMA((DISPATCH_SLOTS,)),
                        pltpu.SemaphoreType.DMA(())],
    )
    return pl.pallas_call(
        _dispatch_kernel,
        grid_spec=grid_spec,
        out_shape=jax.ShapeDtypeStruct((n_rows * ROW_TILE, LANES), F32),
        compiler_params=_cparams(("arbitrary",)),
        name="moe_dispatch",
    )(pos, gaps, xn)


def _store_row_tiles(ref, x, first=0):
    for c in range(ROW_TILE):
        ref[pl.ds(first + c, x.shape[0], stride=ROW_TILE), :] = x[:, c * LANES:(c + 1) * LANES]


def _load_row_tiles(ref, first, rows, lead=()):
    return jnp.concatenate([ref[lead + (pl.ds(first + c, rows, stride=ROW_TILE), slice(None))]
                            for c in range(ROW_TILE)], axis=1)


def _row_tile(ref, r):
    return ref.at[pl.ds(pl.multiple_of(r * ROW_TILE, ROW_TILE), ROW_TILE), :]


def _ffn_kernel(blk_e_ref, nblk_ref, fresh_ref, run_ref, next_e_ref, x_hbm, wg_hbm, wu_hbm, wd_hbm, out_ref,
                xbuf, xsems, wg_f, wu_f, wd_f, wsems, wg_s, wu_s, wd_s, *, layer):
    i = pl.program_id(0)
    nblk = nblk_ref[0]
    tile_rows = out_ref.shape[0]
    rows = tile_rows // ROW_TILE
    live = i < nblk

    def x_copy(block, slot):
        first = pl.multiple_of(block * tile_rows, tile_rows)
        return pltpu.make_async_copy(x_hbm.at[pl.ds(first, tile_rows), :], xbuf.at[slot], xsems.at[slot])

    def w_copies(expert, slot):
        return [pltpu.make_async_copy(src.at[layer, expert], dst.at[slot], wsems.at[slot])
                for src, dst in ((wg_hbm, wg_f), (wu_hbm, wu_f), (wd_hbm, wd_f))]

    @pl.when(i == 0)
    def _():
        for b in range(FFN_X_SLOTS - 1):
            @pl.when(b < nblk)
            def _():
                x_copy(b, b).start()

        @pl.when(nblk > 0)
        def _():
            for copy in w_copies(blk_e_ref[0], 0):
                copy.start()

    ahead = i + (FFN_X_SLOTS - 1)

    @pl.when(ahead < nblk)
    def _():
        x_copy(ahead, lax.rem(ahead, FFN_X_SLOTS)).start()

    @pl.when(live & (fresh_ref[i] == 1))
    def _():
        slot = lax.rem(run_ref[i], 2)
        for copy in w_copies(0, slot):
            copy.wait()
        wg_s[...] = wg_f[slot].astype(BF16)
        wu_s[...] = wu_f[slot].astype(BF16)
        wd_s[...] = wd_f[slot].astype(BF16)

        @pl.when(next_e_ref[i] >= 0)
        def _():
            for copy in w_copies(next_e_ref[i], 1 - slot):
                copy.start()

    @pl.when(live)
    def _():
        slot = lax.rem(i, FFN_X_SLOTS)
        x_copy(0, slot).wait()
        x = _load_row_tiles(xbuf, 0, rows, lead=(slot,)).astype(BF16)
        gate = jnp.dot(x, wg_s[...], preferred_element_type=F32)
        up = jnp.dot(x, wu_s[...], preferred_element_type=F32)
        hid = (jax.nn.silu(gate) * up).astype(BF16)
        _store_row_tiles(out_ref, jnp.dot(hid, wd_s[...], preferred_element_type=F32))

    @pl.when(jnp.logical_not(live))
    def _():
        out_ref[...] = jnp.zeros_like(out_ref)


def _grouped_ffn(x_sorted, blk_e, nblk, fresh, run, next_e, w_gate, w_up, w_down, layer):
    rb = FFN_ROWS
    n_rows = x_sorted.shape[0] // ROW_TILE
    _, _, d, d_e = w_gate.shape
    assert d == ROW_TILE * LANES
    any_spec = pl.BlockSpec(memory_space=pl.ANY)
    grid_spec = pltpu.PrefetchScalarGridSpec(
        num_scalar_prefetch=5,
        grid=(n_rows // rb,),
        in_specs=[any_spec, any_spec, any_spec, any_spec],
        out_specs=pl.BlockSpec((rb * ROW_TILE, LANES), lambda i, *_: (i, 0)),
        scratch_shapes=[
            pltpu.VMEM((FFN_X_SLOTS, rb * ROW_TILE, LANES), F32), pltpu.SemaphoreType.DMA((FFN_X_SLOTS,)),
            pltpu.VMEM((2, d, d_e), F32), pltpu.VMEM((2, d, d_e), F32), pltpu.VMEM((2, d_e, d), F32),
            pltpu.SemaphoreType.DMA((2,)),
            pltpu.VMEM((d, d_e), BF16), pltpu.VMEM((d, d_e), BF16), pltpu.VMEM((d_e, d), BF16),
        ],
    )
    return pl.pallas_call(
        functools.partial(_ffn_kernel, layer=layer),
        grid_spec=grid_spec,
        out_shape=jax.ShapeDtypeStruct((n_rows * ROW_TILE, LANES), F32),
        compiler_params=_cparams(("arbitrary",)),
        name="moe_ffn",
    )(blk_e, nblk, fresh, run, next_e, x_sorted, w_gate, w_up, w_down)


def _combine_kernel(pos_ref, resid_ref, w_ref, src_hbm, *refs, split):
    out_refs, buf, sems = refs[:-2], refs[-2], refs[-1]
    i = pl.program_id(0)
    tm = resid_ref.shape[0]

    def gather(tile, slot):
        base = tile * tm * TOP_K_EXPERTS

        def issue(t, carry):
            for k in range(TOP_K_EXPERTS):
                p = pos_ref[base + t * TOP_K_EXPERTS + k]
                pltpu.make_async_copy(_row_tile(src_hbm, p), _row_tile(buf, (slot * TOP_K_EXPERTS + k) * tm + t),
                                      sems.at[slot]).start(priority=k)
            return carry

        lax.fori_loop(0, tm, issue, 0, unroll=4)

    @pl.when(i == 0)
    def _():
        gather(0, 0)

    @pl.when(i + 1 < pl.num_programs(0))
    def _():
        gather(i + 1, (i + 1) % 2)

    slot = i % 2
    firsts = [pl.multiple_of((slot * TOP_K_EXPERTS + k) * tm * ROW_TILE, ROW_TILE) for k in range(TOP_K_EXPERTS)]
    for first in firsts:
        pltpu.make_async_copy(src_hbm.at[pl.ds(0, tm * ROW_TILE), :],
                              buf.at[pl.ds(first, tm * ROW_TILE), :], sems.at[slot]).wait()
    acc = resid_ref[...]
    w = w_ref[...]
    for k, first in enumerate(firsts):
        acc = acc + w[:, k:k + 1] * _load_row_tiles(buf, first, tm)
    def put(ref):
        ref[...] = acc

    if split:
        _store_stream_tile(out_refs[0], out_refs[1], put)
    else:
        put(out_refs[0])


def _combine(resid, w_pad, src, pos, split):
    n, d = resid.shape
    assert d == ROW_TILE * LANES and TOP_K_EXPERTS == 2
    tm = TOKEN_TILE
    n_prompt_tiles = n // tm - 1
    if split:
        out_specs = [_prompt_spec((tm, d), n_prompt_tiles), _sample_spec((tm, d))]
        out_shape = [jax.ShapeDtypeStruct((n_prompt_tiles * tm, d), F32), jax.ShapeDtypeStruct((tm, d), F32)]
    else:
        out_specs = [pl.BlockSpec((tm, d), lambda i, *_: (i, 0))]
        out_shape = [jax.ShapeDtypeStruct((n, d), F32)]
    grid_spec = pltpu.PrefetchScalarGridSpec(
        num_scalar_prefetch=1,
        grid=(n // tm,),
        in_specs=[
            pl.BlockSpec((tm, d), lambda i, *_: (i, 0)),
            pl.BlockSpec((tm, LANES), lambda i, *_: (i, 0)),
            pl.BlockSpec(memory_space=pl.ANY),
        ],
        out_specs=out_specs,
        scratch_shapes=[pltpu.VMEM((2 * TOP_K_EXPERTS * tm * ROW_TILE, LANES), F32),
                        pltpu.SemaphoreType.DMA((2,))],
    )
    return pl.pallas_call(
        functools.partial(_combine_kernel, split=split),
        grid_spec=grid_spec,
        out_shape=out_shape,
        compiler_params=_cparams(("arbitrary",)),
        name="moe_combine",
    )(pos, resid, w_pad, src)


def _hier_moe(h, g, w_r, b_r, w_gate, w_up, w_down, layer, split=False, attn=None):
    assert N_EXPERTS & (N_EXPERTS - 1) == 0
    n, _ = h.shape
    rb = FFN_ROWS
    if attn is None:
        xn, w_pad, code_pad, hist_pad = _router(h, g, w_r, b_r)
    else:
        h, xn, w_pad, code_pad, hist_pad = _router(h, g, w_r, b_r, attn)
    hist = hist_pad[:, 0, :N_EXPERTS]
    counts = jnp.sum(hist, axis=0)
    padded = (counts + rb - 1) // rb * rb
    pend = jnp.cumsum(padded)
    first = ((pend - padded)[None, :] + jnp.cumsum(hist, axis=0) - hist).astype(jnp.int32)
    first = jnp.pad(first, ((0, 0), (0, LANES - N_EXPERTS)))
    pos = _sorted_positions(code_pad, first)[:, :TOP_K_EXPERTS].reshape(-1)
    n_blocks = (n * TOP_K_EXPERTS + N_EXPERTS * (rb - 1) + rb - 1) // rb
    block_first = jnp.arange(n_blocks, dtype=jnp.int32)[:, None] * rb
    blk_e = jnp.minimum(jnp.sum((pend[None, :] <= block_first).astype(jnp.int32), axis=1), N_EXPERTS - 1)
    fresh = jnp.concatenate([jnp.ones((1,), jnp.int32), (blk_e[1:] != blk_e[:-1]).astype(jnp.int32)])
    run = (jnp.cumsum(fresh) - 1).astype(jnp.int32)
    experts = jnp.arange(N_EXPERTS, dtype=jnp.int32)
    later = jnp.where((counts[None, :] > 0) & (experts[None, :] > experts[:, None]), experts[None, :], N_EXPERTS)
    next_owner = jnp.min(later, axis=1)
    next_e = jnp.where(next_owner < N_EXPERTS, next_owner, -1)[blk_e].astype(jnp.int32)
    nblk = (pend[-1:] // rb).astype(jnp.int32)
    gaps = jnp.concatenate([jnp.stack([pend - padded + counts, padded - counts], axis=1).reshape(-1),
                            nblk]).astype(jnp.int32)
    x_sorted = _dispatch(xn, pos, gaps, n_blocks * rb)
    out_sorted = _grouped_ffn(x_sorted, blk_e, nblk, fresh, run, next_e, w_gate, w_up, w_down, layer)
    out = _combine(h, w_pad, out_sorted, pos, split)
    return out if split else out[0]


def _store_heads_as_rows(ref, x):
    for j in range(N_KV_HEADS):
        ref[pl.ds(j, x.shape[0], stride=N_KV_HEADS), :] = x[:, j * HEAD_DIM:(j + 1) * HEAD_DIM]


def _proj_kernel(h_ref, gkv_ref, gq_ref, wkv_ref, wq_ref, kn_ref, qn_ref, cos_ref, sin_ref,
                 k_ref, v_ref, q_ref, kp_ref, ks_ref, vp_ref, vs_ref):
    hn = _rms(h_ref[...])
    cos = cos_ref[...]
    sin = sin_ref[...]

    def norm_rope(x, g):
        y = _rms(x) * g
        return y * cos + pltpu.roll(y, HEAD_DIM // 2, 1) * sin

    kv = jnp.dot((hn * gkv_ref[...]).astype(BF16), wkv_ref[...], preferred_element_type=F32)
    kw = N_KV_HEADS * HEAD_DIM
    k = jnp.concatenate(
        [norm_rope(kv[:, j * HEAD_DIM:(j + 1) * HEAD_DIM], kn_ref[...]) for j in range(N_KV_HEADS)], axis=1)
    v = kv[:, kw:]
    k_ref[...] = k
    v_ref[...] = v
    q = jnp.dot((hn * gq_ref[...]).astype(BF16), wq_ref[...], preferred_element_type=F32)
    q_ref[...] = jnp.concatenate(
        [norm_rope(q[:, j * HEAD_DIM:(j + 1) * HEAD_DIM], qn_ref[...]) for j in range(N_HEADS)], axis=1)

    def store_by_head(kv_refs):
        _store_heads_as_rows(kv_refs[0], k)
        _store_heads_as_rows(kv_refs[1], v)

    _store_stream_tile((kp_ref, vp_ref), (ks_ref, vs_ref), store_by_head)


def _kvq_proj(h, g_kv, g_q, w_kv, w_q, k_norm, q_norm, cos, sin):
    n, d = h.shape
    tm = TOKEN_TILE
    kw = N_KV_HEADS * HEAD_DIM
    qw = N_HEADS * HEAD_DIM
    row = lambda i: (i, 0)
    fixed = lambda i: (0, 0)
    n_prompt_tiles = n // tm - 1
    seq_tiles = cos.shape[0] // tm - 1
    table_row = lambda i: (jnp.where(i < n_prompt_tiles, lax.rem(i, seq_tiles), seq_tiles), 0)
    head_rows = tm * N_KV_HEADS
    by_head = [_prompt_spec((head_rows, HEAD_DIM), n_prompt_tiles), _sample_spec((head_rows, HEAD_DIM))]
    by_head_shapes = [jax.ShapeDtypeStruct((n_prompt_tiles * head_rows, HEAD_DIM), F32),
                      jax.ShapeDtypeStruct((head_rows, HEAD_DIM), F32)]
    return pl.pallas_call(
        _proj_kernel,
        grid=(n // tm,),
        in_specs=[
            pl.BlockSpec((tm, d), row),
            pl.BlockSpec((1, d), fixed),
            pl.BlockSpec((1, d), fixed),
            pl.BlockSpec((d, 2 * kw), fixed),
            pl.BlockSpec((d, qw), fixed),
            pl.BlockSpec((1, HEAD_DIM), fixed),
            pl.BlockSpec((1, HEAD_DIM), fixed),
            pl.BlockSpec((tm, HEAD_DIM), table_row),
            pl.BlockSpec((tm, HEAD_DIM), table_row),
        ],
        out_specs=[
            pl.BlockSpec((tm, kw), row),
            pl.BlockSpec((tm, kw), row),
            pl.BlockSpec((tm, qw), row),
        ] + by_head + by_head,
        out_shape=[
            jax.ShapeDtypeStruct((n, kw), F32),
            jax.ShapeDtypeStruct((n, kw), F32),
            jax.ShapeDtypeStruct((n, qw), F32),
        ] + by_head_shapes + by_head_shapes,
        compiler_params=_cparams(("arbitrary",)),
        name="kvq_proj",
    )(h, g_kv, g_q, w_kv, w_q, k_norm, q_norm, cos, sin)


def _top_blocks(gate, axis):
    idx = lax.broadcasted_iota(jnp.int32, gate.shape, axis)
    big = jnp.int32(gate.shape[axis])
    sel = jnp.zeros(gate.shape, jnp.bool_)
    for _ in range(MOBA_TOP_K):
        top = jnp.max(gate, axis=axis, keepdims=True)
        first = jnp.min(jnp.where(gate == top, idx, big), axis=axis, keepdims=True)
        hit = idx == first
        sel = sel | (hit & (top > NEG_INF))
        gate = jnp.where(hit, NEG_INF, gate)
    return sel.astype(F32)


def _moba_prompt_kernel(q_ref, k_ref, v_ref, o_ref, kbf, vt, kmean, sel, s_a, s_b, m_scr, acc):
    j = pl.program_id(2)
    blk = MOBA_BLOCK
    grp = KEY_GROUP
    seq = k_ref.shape[0]
    n_blocks = seq // blk

    @pl.when(j == 0)
    def _():
        k = k_ref[...]
        kbf[...] = k.astype(BF16)
        kmean[...] = jnp.mean(k.reshape(n_blocks, blk, HEAD_DIM), axis=1)
        for n in range(n_blocks):
            vt[:HEAD_DIM, n * blk:(n + 1) * blk] = v_ref[n * blk:(n + 1) * blk, :].T.astype(BF16)
        r = lax.broadcasted_iota(jnp.int32, (ONES_ROWS, seq), 0)
        vt[HEAD_DIM:, :] = jnp.where(r == 0, 1.0, 0.0).astype(BF16)

    q2 = q_ref[...]
    qs = jnp.concatenate([q2[:, h * HEAD_DIM:(h + 1) * HEAD_DIM] for h in range(Q_PER_KV)], axis=0)
    nq = qs.shape[0]
    qt = (qs * (HEAD_DIM ** -0.5 * LOG2_E)).T.astype(BF16)

    def scores(start, n_keys):
        return jnp.dot(kbf[pl.ds(start, n_keys), :], qt, preferred_element_type=F32)

    gate = _dot_nt_3pass(kmean[...], qs)
    row = lax.broadcasted_iota(jnp.int32, gate.shape, 0)
    sel[...] = _top_blocks(jnp.where(row < j, gate, NEG_INF), 0)

    key = lax.broadcasted_iota(jnp.int32, (blk, nq), 0)
    qpos = lax.broadcasted_iota(jnp.int32, (blk, nq), 1) % blk
    own = pl.multiple_of(j * blk, blk)
    s_own = jnp.where(key <= qpos, scores(own, blk), NEG_INF)
    m_own = jnp.max(s_own, axis=0, keepdims=True)
    m_scr[...] = m_own
    acc[...] = jnp.dot(vt[:, pl.ds(own, blk)], jnp.exp2(s_own - m_own).astype(BF16), preferred_element_type=F32)

    def update(tiles, start):
        m_old = m_scr[...]
        m_new = m_old
        for s in tiles:
            m_new = jnp.maximum(m_new, jnp.max(s, axis=0, keepdims=True))
        p = jnp.concatenate([jnp.exp2(s - m_new).astype(BF16) for s in tiles], axis=0)
        alpha = jnp.exp2(m_old - m_new)
        pv = jnp.dot(vt[:, pl.ds(start, len(tiles) * blk)], p, preferred_element_type=F32)
        acc[...] = alpha * acc[...] + pv
        m_scr[...] = m_new

    def fill(buf, g):
        start = pl.multiple_of(g * (grp * blk), grp * blk)
        buf[...] = scores(start, grp * blk)

    def consume(buf, g):
        start = pl.multiple_of(g * (grp * blk), grp * blk)
        update([jnp.where(sel[pl.ds(g * grp + i, 1), :] > 0.0, buf[i * blk:(i + 1) * blk, :], NEG_INF)
                for i in range(grp)], start)

    n_groups = lax.div(j + (grp - 1), grp)
    n_pairs = lax.div(n_groups - 1, 2)
    fill(s_a, 0)

    @pl.when(n_groups > 0)
    def _():
        def pair(h, carry):
            g = 2 * h
            fill(s_b, g + 1)
            consume(s_a, g)
            fill(s_a, g + 2)
            consume(s_b, g + 1)
            return carry

        lax.fori_loop(0, n_pairs, pair, 0)
        g = 2 * n_pairs

        @pl.when(n_groups - g == 1)
        def _():
            consume(s_a, g)

        @pl.when(n_groups - g == 2)
        def _():
            fill(s_b, g + 1)
            consume(s_a, g)
            consume(s_b, g + 1)

    a = acc[...]
    o = (a[:HEAD_DIM] / a[HEAD_DIM:HEAD_DIM + 1]).T
    o_ref[...] = jnp.concatenate([o[h * blk:(h + 1) * blk, :] for h in range(Q_PER_KV)], axis=1)


def _moba_prompt(q, k, v, batch, seq):
    blk = MOBA_BLOCK
    assert seq % (KEY_GROUP * blk) == 0
    n_steps = seq // blk
    qw = Q_PER_KV * HEAD_DIM
    nq = Q_PER_KV * blk
    return pl.pallas_call(
        _moba_prompt_kernel,
        grid=(batch, N_KV_HEADS, n_steps),
        in_specs=[
            pl.BlockSpec((blk, qw), lambda b, c, j: (b * n_steps + j, c)),
            pl.BlockSpec((seq, HEAD_DIM), lambda b, c, j: (b, c)),
            pl.BlockSpec((seq, HEAD_DIM), lambda b, c, j: (b, c)),
        ],
        out_specs=pl.BlockSpec((blk, qw), lambda b, c, j: (b * n_steps + j, c)),
        out_shape=jax.ShapeDtypeStruct((batch * seq, N_HEADS * HEAD_DIM), F32),
        scratch_shapes=[
            pltpu.VMEM((seq, HEAD_DIM), BF16),
            pltpu.VMEM((HEAD_DIM + ONES_ROWS, seq), BF16),
            pltpu.VMEM((seq // blk, HEAD_DIM), F32),
            pltpu.VMEM((seq // blk, nq), F32),
            pltpu.VMEM((KEY_GROUP * blk, nq), F32),
            pltpu.VMEM((KEY_GROUP * blk, nq), F32),
            pltpu.VMEM((1, nq), F32),
            pltpu.VMEM((HEAD_DIM + ONES_ROWS, nq), F32),
        ],
        compiler_params=_cparams(("arbitrary", "arbitrary", "arbitrary")),
        name="moba_prompt",
    )(q, k, v)


def _stack_heads(q8):
    return jnp.concatenate([q8[:, h * HEAD_DIM:(h + 1) * HEAD_DIM] for h in range(N_HEADS)], axis=0)


def _sample_attn_kernel(pt_ref, q_ref, kn_ref, vn_ref, ck_hbm, cv_hbm, o_ref, pages, sems,
                        s_scr, means, sel_scr, qs_scr, m_scr, l_scr, acc, *, dec_seq, n_pages):
    pps = PAGES_PER_STEP
    t = pl.program_id(1)
    steps = pl.num_programs(1)
    n_k_steps = n_pages // pps
    rows = N_HEADS * dec_seq
    rkv = Q_PER_KV * dec_seq
    ppb = MOBA_BLOCK // PAGE_SIZE
    bps = pps // ppb
    n_blocks = n_pages // ppb
    nt_dims = (((1,), (1,)), ((), ()))

    chunk = pl.program_id(0) * steps + t
    n_chunks = pl.num_programs(0) * steps

    def start_chunk(ci):
        seq = lax.div(ci, steps)
        step = lax.rem(ci, steps)
        slot = lax.rem(ci, PAGE_SLOTS)

        def start_pages(cache_hbm, first_page):
            for r in range(pps):
                pltpu.make_async_copy(cache_hbm.at[pt_ref[seq, first_page + r]], pages.at[slot, r],
                                      sems.at[slot]).start()

        @pl.when(step < n_k_steps)
        def _():
            start_pages(ck_hbm, step * pps)

        @pl.when(step >= n_k_steps)
        def _():
            start_pages(cv_hbm, (step - n_k_steps) * pps)

    @pl.when(chunk == 0)
    def _():
        for ci in range(PAGE_SLOTS - 1):
            start_chunk(jnp.int32(ci))

    @pl.when(chunk + (PAGE_SLOTS - 1) < n_chunks)
    def _():
        start_chunk(chunk + (PAGE_SLOTS - 1))

    slot = lax.rem(chunk, PAGE_SLOTS)
    pltpu.make_async_copy(ck_hbm.at[pl.ds(0, pps)], pages.at[slot], sems.at[slot]).wait()

    def head_rows(p, c):
        return pages[slot, p, pl.ds(c, PAGE_SIZE, stride=N_KV_HEADS), :]

    @pl.when(t == 0)
    def _():
        qs_scr[...] = (_stack_heads(q_ref[...]) * (HEAD_DIM ** -0.5 * LOG2_E)).astype(BF16)
        m_scr[...] = jnp.full(m_scr.shape, NEG_INF, F32)
        l_scr[...] = jnp.zeros(l_scr.shape, F32)
        acc[...] = jnp.zeros(acc.shape, F32)

    @pl.when(t < n_k_steps)
    def _():
        qs = qs_scr[...]
        col = pl.multiple_of(t * (pps * PAGE_SIZE), pps * PAGE_SIZE)
        for c in range(N_KV_HEADS):
            kc = jnp.concatenate([head_rows(p, c) for p in range(pps)], axis=0)
            s_scr[c * rkv:(c + 1) * rkv, pl.ds(col, pps * PAGE_SIZE)] = lax.dot_general(
                qs[c * rkv:(c + 1) * rkv], kc.astype(BF16), nt_dims, preferred_element_type=F32)
            means[c, pl.ds(t * bps, bps), :] = jnp.sum(kc.reshape(bps, MOBA_BLOCK, HEAD_DIM), axis=1) / MOBA_BLOCK

    @pl.when(t == n_k_steps - 1)
    def _():
        qf = _stack_heads(q_ref[...])
        gate = jnp.concatenate(
            [lax.dot_general(qf[c * rkv:(c + 1) * rkv], means[c], nt_dims, precision=HIGHEST,
                             preferred_element_type=F32) for c in range(N_KV_HEADS)], axis=0)
        chosen = _top_blocks(gate, 1)
        sel_scr[...] = jnp.concatenate([chosen, jnp.zeros((rows, LANES - n_blocks), F32)], axis=1)

    def softmax_step(tiles):
        m_old = m_scr[...]
        m_new = m_old
        for s in tiles:
            m_new = jnp.maximum(m_new, jnp.max(s, axis=1, keepdims=True))
        m_safe = jnp.where(m_new == NEG_INF, 0.0, m_new)
        alpha = jnp.exp2(m_old - m_safe)
        l_new = alpha * l_scr[...]
        probs = []
        for s in tiles:
            p = jnp.exp2(s - m_safe)
            l_new = l_new + jnp.sum(p, axis=1, keepdims=True)
            probs.append(p.astype(BF16))
        l_scr[...] = l_new
        m_scr[...] = m_new
        return probs, alpha

    @pl.when(t >= n_k_steps)
    def _():
        tv = t - n_k_steps
        sel = sel_scr[...]
        lane = lax.broadcasted_iota(jnp.int32, sel.shape, 1)
        tiles = []
        for b in range(bps):
            n = tv * bps + b
            chosen = jnp.sum(jnp.where(lane == n, sel, 0.0), axis=1, keepdims=True) > 0.0
            col = pl.multiple_of(n * MOBA_BLOCK, MOBA_BLOCK)
            tiles.append(jnp.where(chosen, s_scr[:, pl.ds(col, MOBA_BLOCK)], NEG_INF))
        probs, alpha = softmax_step(tiles)
        prob = jnp.concatenate(probs, axis=1)
        pv = []
        for c in range(N_KV_HEADS):
            vc = jnp.concatenate([head_rows(p, c) for p in range(pps)], axis=0).astype(BF16)
            pv.append(jnp.dot(prob[c * rkv:(c + 1) * rkv], vc, preferred_element_type=F32))
        acc[...] = alpha * acc[...] + jnp.concatenate(pv, axis=0)

    @pl.when(t == pl.num_programs(1) - 1)
    def _():
        qs = qs_scr[...]
        kn = kn_ref[...].astype(BF16)
        vn = vn_ref[...].astype(BF16)
        s = jnp.concatenate(
            [lax.dot_general(qs[c * rkv:(c + 1) * rkv], kn[:, c * HEAD_DIM:(c + 1) * HEAD_DIM], nt_dims,
                             preferred_element_type=F32) for c in range(N_KV_HEADS)], axis=0)
        r2 = lax.broadcasted_iota(jnp.int32, s.shape, 0)
        c2 = lax.broadcasted_iota(jnp.int32, s.shape, 1)
        probs, alpha = softmax_step([jnp.where(c2 <= r2 % dec_seq, s, NEG_INF)])
        pv = [jnp.dot(probs[0][c * rkv:(c + 1) * rkv], vn[:, c * HEAD_DIM:(c + 1) * HEAD_DIM],
                      preferred_element_type=F32) for c in range(N_KV_HEADS)]
        o = (alpha * acc[...] + jnp.concatenate(pv, axis=0)) / l_scr[...]
        o_ref[...] = jnp.concatenate([o[h * dec_seq:(h + 1) * dec_seq, :] for h in range(N_HEADS)], axis=1)


def _sample_attn(page_table, q, k, v, row0, cache_k2, cache_v2, dec_seq):
    n_seq, n_pages = page_table.shape
    pps = PAGES_PER_STEP
    n_k_steps = n_pages // pps
    n_blocks = n_pages * PAGE_SIZE // MOBA_BLOCK
    assert n_pages % pps == 0 and n_blocks <= LANES
    rows = N_HEADS * dec_seq
    kw = N_KV_HEADS * HEAD_DIM
    qw = N_HEADS * HEAD_DIM
    grid_spec = pltpu.PrefetchScalarGridSpec(
        num_scalar_prefetch=1,
        grid=(n_seq, 2 * n_k_steps),
        in_specs=[
            pl.BlockSpec((dec_seq, qw), lambda s, t, pt: (row0 + s, 0)),
            pl.BlockSpec((dec_seq, kw), lambda s, t, pt: (row0 + s, 0)),
            pl.BlockSpec((dec_seq, kw), lambda s, t, pt: (row0 + s, 0)),
            pl.BlockSpec(memory_space=pl.ANY),
            pl.BlockSpec(memory_space=pl.ANY),
        ],
        out_specs=pl.BlockSpec((dec_seq, qw), lambda s, t, pt: (s, 0)),
        scratch_shapes=[
            pltpu.VMEM((PAGE_SLOTS, pps, PAGE_SIZE * N_KV_HEADS, HEAD_DIM), F32),
            pltpu.SemaphoreType.DMA((PAGE_SLOTS,)),
            pltpu.VMEM((rows, n_pages * PAGE_SIZE), F32),
            pltpu.VMEM((N_KV_HEADS, n_blocks, HEAD_DIM), F32),
            pltpu.VMEM((rows, LANES), F32),
            pltpu.VMEM((rows, HEAD_DIM), BF16),
            pltpu.VMEM((rows, 1), F32),
            pltpu.VMEM((rows, 1), F32),
            pltpu.VMEM((rows, HEAD_DIM), F32),
        ],
    )
    return pl.pallas_call(
        functools.partial(_sample_attn_kernel, dec_seq=dec_seq, n_pages=n_pages),
        grid_spec=grid_spec,
        out_shape=jax.ShapeDtypeStruct((n_seq * dec_seq, qw), F32),
        compiler_params=_cparams(("arbitrary", "arbitrary")),
        name="sample_attn",
    )(page_table, q, k, v, cache_k2, cache_v2)


def _mix_tables(w_s, b_s, dec_seq):
    tm = TOKEN_TILE
    causal = jnp.tril(jnp.ones((GMLP_CHUNK, GMLP_CHUNK), dtype=bool))
    w = jnp.where(causal[None], w_s, jnp.zeros_like(w_s))
    eye_p = jnp.eye(tm // GMLP_CHUNK, dtype=w.dtype)
    mix_p = jnp.einsum("ab,gts->gatbs", eye_p, w).reshape(GMLP_GROUPS, tm, tm)
    eye_s = jnp.eye(tm // dec_seq, dtype=w.dtype)
    mix_s = jnp.einsum("ab,gts->gatbs", eye_s, w[:, :dec_seq, :dec_seq]).reshape(GMLP_GROUPS, tm, tm)
    mix = jnp.stack([mix_p, mix_s]).astype(BF16)
    bias_p = jnp.tile(b_s.T, (tm // GMLP_CHUNK, 1))
    bias_s = jnp.tile(b_s.T[:dec_seq], (tm // dec_seq, 1))
    bias = jnp.stack([bias_p, bias_s])
    bias = jnp.pad(bias, ((0, 0), (0, 0), (0, LANES - GMLP_GROUPS)))
    return mix, bias


def _rope_tables(pos):
    half = HEAD_DIM // 2
    inv = ROPE_THETA ** (-jnp.arange(half, dtype=F32) * 2.0 / HEAD_DIM)
    ang = pos.astype(F32)[:, None] * inv[None, :]
    cos = jnp.cos(ang)
    sin = jnp.sin(ang)
    return jnp.concatenate([cos, cos], axis=1), jnp.concatenate([-sin, sin], axis=1)


def _router_tables(w_grp, b_grp, w_rt, b_rt):
    w = jnp.concatenate([w_grp, w_rt], axis=1)
    b = jnp.concatenate([b_grp, b_rt], axis=0)
    pad = LANES - w.shape[1]
    return jnp.pad(w, ((0, 0), (0, pad))), jnp.pad(b, (0, pad)).reshape(1, LANES)


def kernel(x_prompt, x_sample, cache_k, cache_v, page_table, norm_mix, norm_ffn, a_w_in, a_ln_g, a_ln_b,
           a_w_s, a_b_s, a_w_out, kv_norm, w_kv, k_norm, b_w_q, b_q_norm, b_w_o, moe_w_grp, moe_b_grp,
           moe_w_rt, moe_b_rt, moe_w_gate, moe_w_up, moe_w_down):
    batch, seq, d = x_prompt.shape
    n_seq, dec_seq, _ = x_sample.shape
    n_prompt = batch * seq
    n_sample = n_seq * dec_seq
    assert n_prompt % TOKEN_TILE == 0 and n_sample == TOKEN_TILE and seq % MOBA_BLOCK == 0
    past_len = page_table.shape[1] * PAGE_SIZE
    assert past_len % MOBA_BLOCK == 0 and dec_seq <= MOBA_BLOCK

    pos = jnp.concatenate([jnp.arange(seq), jnp.tile(past_len + jnp.arange(dec_seq), n_seq)])
    cos, sin = _rope_tables(pos)
    row = lambda a: a.reshape(1, -1)

    mix, bias = _mix_tables(a_w_s[0], a_b_s[0], dec_seq)
    h, vg_sample = _gmlp_layer(x_prompt.reshape(n_prompt, d), x_sample.reshape(n_sample, d), row(norm_mix[0]),
                               a_w_in[0].astype(BF16), row(a_ln_g[0]), row(a_ln_b[0]), mix, bias,
                               a_w_out[0].astype(BF16))
    moe = []
    for layer in range(2):
        w_r, b_r = _router_tables(moe_w_grp[layer], moe_b_grp[layer], moe_w_rt[layer], moe_b_rt[layer])
        moe.append((row(norm_ffn[layer]), w_r, b_r, moe_w_gate, moe_w_up, moe_w_down, layer))
    h = _hier_moe(h, *moe[0])

    k, v, q, k_p, k_s, v_p, v_s = _kvq_proj(h, row(kv_norm), row(norm_mix[1]), w_kv.astype(BF16),
                                            b_w_q[0].astype(BF16), row(k_norm), row(b_q_norm[0]), cos, sin)

    o_prompt = _moba_prompt(q, k, v, batch, seq)
    n_phys = cache_k.shape[0]
    cache_k2 = cache_k.reshape(n_phys, PAGE_SIZE * N_KV_HEADS, HEAD_DIM)
    cache_v2 = cache_v.reshape(n_phys, PAGE_SIZE * N_KV_HEADS, HEAD_DIM)
    o_sample = _sample_attn(page_table, q, k, v, n_prompt // dec_seq, cache_k2, cache_v2, dec_seq)
    y_prompt, y_sample = _hier_moe(h, *moe[1], split=True, attn=(o_prompt, o_sample, b_w_o[0].astype(BF16)))

    n_pages_new = seq // PAGE_SIZE
    return (y_prompt.reshape(batch, seq, d),
            y_sample.reshape(n_seq, dec_seq, d),
            k_p.reshape(batch, n_pages_new, PAGE_SIZE, N_KV_HEADS, HEAD_DIM),
            v_p.reshape(batch, n_pages_new, PAGE_SIZE, N_KV_HEADS, HEAD_DIM),
            k_s.reshape(n_seq, dec_seq, N_KV_HEADS, HEAD_DIM),
            v_s.reshape(n_seq, dec_seq, N_KV_HEADS, HEAD_DIM),
            vg_sample.reshape(1, n_seq, dec_seq, -1))
```

```python
import functools
import math

import jax
import jax.numpy as jnp
from jax import lax
from jax.experimental import pallas as pl
from jax.experimental.pallas import tpu as pltpu

F32 = jnp.float32
BF16 = jnp.bfloat16
HIGHEST = lax.Precision.HIGHEST

GMLP_CHUNK = 128
GMLP_GROUPS = 8
N_HEADS = 8
N_KV_HEADS = 4
HEAD_DIM = 128
Q_PER_KV = N_HEADS // N_KV_HEADS
MOBA_BLOCK = 256
MOBA_TOP_K = 3
ROPE_THETA = 10000.0
N_GROUPS = 4
EXPERTS_PER_GROUP = 8
N_EXPERTS = N_GROUPS * EXPERTS_PER_GROUP
TOP_K_EXPERTS = 2
PAGE_SIZE = 128
EPS = 1e-6

LANES = 128
ROW_TILE = 8
TOKEN_TILE = 256
FFN_ROWS = 256
FFN_X_SLOTS = 4
DISPATCH_SLOTS = 4
PAGES_PER_STEP = 16
PAGE_SLOTS = 4
KEY_GROUP = 4
ONES_ROWS = 16
LOG2_E = math.log2(math.e)
VMEM_LIMIT = 56 * 1024 * 1024

NEG_INF = float("-inf")


def _cparams(sem):
    return pltpu.CompilerParams(dimension_semantics=sem, vmem_limit_bytes=VMEM_LIMIT)


def _rms(x):
    return x * lax.rsqrt(jnp.mean(x * x, axis=-1, keepdims=True) + EPS)


def _dot_3pass(a, b, dims):
    a_hi = a.astype(BF16)
    b_hi = b.astype(BF16)
    a_lo = (a - a_hi.astype(F32)).astype(BF16)
    b_lo = (b - b_hi.astype(F32)).astype(BF16)
    dot = functools.partial(lax.dot_general, dimension_numbers=dims, preferred_element_type=F32)
    return dot(a_hi, b_hi) + (dot(a_hi, b_lo) + dot(a_lo, b_hi))


def _dot_nt_3pass(a, b):
    return _dot_3pass(a, b, (((1,), (1,)), ((), ())))


def _prompt_spec(block, n_prompt_tiles):
    return pl.BlockSpec(block, lambda i, *_: (jnp.minimum(i, n_prompt_tiles - 1), 0))


def _sample_spec(block):
    return pl.BlockSpec(block, lambda i, *_: (0, 0))


def _is_sample_tile():
    return pl.program_id(0) == pl.num_programs(0) - 1


def _stream_tile(prompt_ref, sample_ref):
    return jnp.where(_is_sample_tile(), sample_ref[...], prompt_ref[...])


def _store_stream_tile(prompt_ref, sample_ref, store):
    @pl.when(jnp.logical_not(_is_sample_tile()))
    def _():
        store(prompt_ref)

    @pl.when(_is_sample_tile())
    def _():
        store(sample_ref)


def _gmlp_kernel(xp_ref, xs_ref, g_ref, win_ref, lng_ref, lnb_ref, mix_ref, bias_ref, wout_ref,
                 h_ref, vg_ref, *, d_gate, n_groups):
    i = pl.program_id(0)
    x = _stream_tile(xp_ref, xs_ref)
    xb = (_rms(x) * g_ref[...]).astype(BF16)
    vp = jax.nn.gelu(jnp.dot(xb, win_ref[:, d_gate:], preferred_element_type=F32))
    vc = vp - jnp.mean(vp, axis=-1, keepdims=True)
    var = jnp.mean(vc * vc, axis=-1, keepdims=True)
    vg = vc * lax.rsqrt(var + EPS) * lng_ref[...] + lnb_ref[...]
    vgb = vg.astype(BF16)
    cw = d_gate // n_groups
    bias = bias_ref[0]
    u = jax.nn.gelu(jnp.dot(xb, win_ref[:, :d_gate], preferred_element_type=F32))
    parts = []
    for g in range(n_groups):
        mixed = jnp.dot(mix_ref[0, g], vgb[:, g * cw:(g + 1) * cw], preferred_element_type=F32)
        mixed = mixed + bias[:, g:g + 1]
        parts.append((u[:, g * cw:(g + 1) * cw] * mixed).astype(BF16))
    gated = jnp.concatenate(parts, axis=1)
    h_ref[...] = x + jnp.dot(gated, wout_ref[...], preferred_element_type=F32)

    @pl.when(i == pl.num_programs(0) - 1)
    def _():
        vg_ref[...] = vg


def _gmlp_layer(x_prompt, x_sample, g, w_in, ln_g, ln_b, mix, bias, w_out):
    d = x_prompt.shape[1]
    d_gate = w_out.shape[0]
    tm = TOKEN_TILE
    n_prompt_tiles = x_prompt.shape[0] // tm
    n_tiles = n_prompt_tiles + 1
    n = n_tiles * tm
    kind = lambda i: jnp.where(i < n_prompt_tiles, 0, 1)
    return pl.pallas_call(
        functools.partial(_gmlp_kernel, d_gate=d_gate, n_groups=GMLP_GROUPS),
        grid=(n_tiles,),
        in_specs=[
            _prompt_spec((tm, d), n_prompt_tiles),
            _sample_spec((tm, d)),
            pl.BlockSpec((1, d), lambda i: (0, 0)),
            pl.BlockSpec((d, 2 * d_gate), lambda i: (0, 0)),
            pl.BlockSpec((1, d_gate), lambda i: (0, 0)),
            pl.BlockSpec((1, d_gate), lambda i: (0, 0)),
            pl.BlockSpec((1, GMLP_GROUPS, tm, tm), lambda i: (kind(i), 0, 0, 0)),
            pl.BlockSpec((1, tm, LANES), lambda i: (kind(i), 0, 0)),
            pl.BlockSpec((d_gate, d), lambda i: (0, 0)),
        ],
        out_specs=[
            pl.BlockSpec((tm, d), lambda i: (i, 0)),
            pl.BlockSpec((tm, d_gate), lambda i: (0, 0)),
        ],
        out_shape=[
            jax.ShapeDtypeStruct((n, d), F32),
            jax.ShapeDtypeStruct((tm, d_gate), F32),
        ],
        compiler_params=_cparams(("arbitrary",)),
        name="gmlp_layer",
    )(x_prompt, x_sample, g, w_in, ln_g, ln_b, mix, bias, w_out)


def _router_kernel(*refs, with_attn):
    if with_attn:
        h_ref, op_ref, os_ref, wo_ref, g_ref, wr_ref, br_ref, h_out, xn_ref, w_ref, code_ref, hist_ref = refs
        o = _stream_tile(op_ref, os_ref).astype(BF16)
        h = h_ref[...] + jnp.dot(o, wo_ref[...], preferred_element_type=F32)
        h_out[...] = h
    else:
        h_ref, g_ref, wr_ref, br_ref, xn_ref, w_ref, code_ref, hist_ref = refs
        h = h_ref[...]
    xn = _rms(h) * g_ref[...]
    _store_row_tiles(xn_ref, xn)
    logits = _dot_3pass(xn, wr_ref[...], (((1,), (0,)), ((), ()))) + br_ref[...]
    lane = lax.broadcasted_iota(jnp.int32, logits.shape, 1)
    lane_f = lane.astype(F32)

    def first_lane(hit):
        return jnp.min(jnp.where(hit, lane_f, float(LANES)), axis=1, keepdims=True).astype(jnp.int32)

    is_grp = lane < N_GROUPS
    gl = jnp.where(is_grp, logits, NEG_INF)
    gmax = jnp.max(gl, axis=1, keepdims=True)
    gidx = first_lane(is_grp & (logits == gmax))
    p_g = 1.0 / jnp.sum(jnp.where(is_grp, jnp.exp(gl - gmax), 0.0), axis=1, keepdims=True)
    lo = N_GROUPS + gidx * EXPERTS_PER_GROUP
    in_grp = (lane >= lo) & (lane < lo + EXPERTS_PER_GROUP)
    v0 = jnp.max(jnp.where(in_grp, logits, NEG_INF), axis=1, keepdims=True)
    i0 = first_lane(in_grp & (logits == v0))
    rest = in_grp & (lane != i0)
    v1 = jnp.max(jnp.where(rest, logits, NEG_INF), axis=1, keepdims=True)
    i1 = first_lane(rest & (logits == v1))
    t = jnp.exp(v1 - v0)
    w0 = p_g * (1.0 / (1.0 + t))
    w1 = p_g * (t / (1.0 + t))
    e0 = i0 - N_GROUPS
    e1 = i1 - N_GROUPS
    w_ref[...] = jnp.where(lane == 0, w0, jnp.where(lane == 1, w1, 0.0))
    tm = logits.shape[0]
    onehot = jnp.concatenate([(lane == e0).astype(F32), (lane == e1).astype(F32)], axis=0)
    a_row = lax.broadcasted_iota(jnp.int32, (2 * tm, 2 * tm), 0)
    a_col = lax.broadcasted_iota(jnp.int32, (2 * tm, 2 * tm), 1)
    earlier = (a_col < a_row).astype(BF16)
    before = jnp.dot(earlier, onehot.astype(BF16), preferred_element_type=F32)
    rank = jnp.sum(before * onehot, axis=1, keepdims=True).astype(jnp.int32)
    code_ref[...] = jnp.where(lane == 0, rank[:tm] * N_EXPERTS + e0,
                              jnp.where(lane == 1, rank[tm:] * N_EXPERTS + e1, 0))
    hist = jnp.sum(onehot, axis=0, keepdims=True).astype(jnp.int32)
    hist_ref[0] = hist


def _router(h, g, w_r, b_r, attn=None):
    n, d = h.shape
    tm = TOKEN_TILE
    row = lambda i: (i, 0)
    fixed = lambda i: (0, 0)
    in_specs = [pl.BlockSpec((tm, d), row)]
    out_specs, out_shape, operands = [], [], [h]
    if attn is not None:
        o_prompt, o_sample, w_o = attn
        ow = o_prompt.shape[1]
        in_specs += [_prompt_spec((tm, ow), n // tm - 1), _sample_spec((tm, ow)), pl.BlockSpec(w_o.shape, fixed)]
        operands += [o_prompt, o_sample, w_o]
        out_specs.append(pl.BlockSpec((tm, d), row))
        out_shape.append(jax.ShapeDtypeStruct((n, d), F32))
    in_specs += [pl.BlockSpec((1, d), fixed), pl.BlockSpec((d, LANES), fixed), pl.BlockSpec((1, LANES), fixed)]
    operands += [g, w_r, b_r]
    out_specs += [
        pl.BlockSpec((tm * ROW_TILE, LANES), row),
        pl.BlockSpec((tm, LANES), row),
        pl.BlockSpec((tm, LANES), row),
        pl.BlockSpec((1, 1, LANES), lambda i: (i, 0, 0)),
    ]
    out_shape += [
        jax.ShapeDtypeStruct((n * ROW_TILE, LANES), F32),
        jax.ShapeDtypeStruct((n, LANES), F32),
        jax.ShapeDtypeStruct((n, LANES), jnp.int32),
        jax.ShapeDtypeStruct((n // tm, 1, LANES), jnp.int32),
    ]
    return pl.pallas_call(
        functools.partial(_router_kernel, with_attn=attn is not None),
        grid=(n // tm,),
        in_specs=in_specs,
        out_specs=out_specs,
        out_shape=out_shape,
        compiler_params=_cparams(("arbitrary",)),
        name="moe_router",
    )(*operands)


def _pos_kernel(code_ref, first_ref, pos_ref, *, tiles_per_step):
    i = pl.program_id(0)
    tm = TOKEN_TILE
    shift = N_EXPERTS.bit_length() - 1
    for s in range(tiles_per_step):
        code = code_ref[s * tm:(s + 1) * tm, :]
        expert = lax.bitwise_and(code, N_EXPERTS - 1)
        rank = lax.shift_right_logical(code, shift)
        off = first_ref[pl.ds(i * tiles_per_step + s, 1), :]
        lane = lax.broadcasted_iota(jnp.int32, code.shape, 1)
        pos = [jnp.sum(jnp.where(lane == expert[:, k:k + 1], off, 0), axis=1, keepdims=True) + rank[:, k:k + 1]
               for k in range(TOP_K_EXPERTS)]
        pos_ref[s * tm:(s + 1) * tm, :] = jnp.where(lane == 0, pos[0], jnp.where(lane == 1, pos[1], 0))


def _sorted_positions(code_pad, first):
    n = code_pad.shape[0]
    n_tiles = n // TOKEN_TILE
    tiles_per_step = max(t for t in range(1, 17) if n_tiles % t == 0)
    rows = tiles_per_step * TOKEN_TILE
    return pl.pallas_call(
        functools.partial(_pos_kernel, tiles_per_step=tiles_per_step),
        grid=(n_tiles // tiles_per_step,),
        in_specs=[
            pl.BlockSpec((rows, LANES), lambda i: (i, 0)),
            pl.BlockSpec(first.shape, lambda i: (0, 0)),
        ],
        out_specs=pl.BlockSpec((rows, LANES), lambda i: (i, 0)),
        out_shape=jax.ShapeDtypeStruct((n, LANES), jnp.int32),
        compiler_params=_cparams(("arbitrary",)),
        name="moe_pos",
    )(code_pad, first)


def _dispatch_kernel(pos_ref, gap_ref, xn_hbm, out_hbm, xbuf, xsems, zero, sems, zsem):
    i = pl.program_id(0)
    n = pl.num_programs(0)
    tile_rows = xbuf.shape[1]
    tm = tile_rows // ROW_TILE
    base = i * tm * TOP_K_EXPERTS
    block_rows = zero.shape[0] // ROW_TILE
    n_blocks = out_hbm.shape[0] // zero.shape[0]

    def tile_in(tile):
        slot = lax.rem(tile, DISPATCH_SLOTS)
        first = pl.multiple_of(tile * tile_rows, tile_rows)
        return pltpu.make_async_copy(xn_hbm.at[pl.ds(first, tile_rows), :], xbuf.at[slot], xsems.at[slot])

    def wait_rows_out(tile):
        slot = lax.rem(tile, DISPATCH_SLOTS)
        for _ in range(TOP_K_EXPERTS):
            pltpu.make_async_copy(xbuf.at[slot], out_hbm.at[pl.ds(0, tile_rows), :], sems.at[slot]).wait()

    @pl.when(i == 0)
    def _():
        tile_in(i).start()

    @pl.when(i >= 2)
    def _():
        wait_rows_out(i - 2)

    @pl.when(i + 1 < n)
    def _():
        tile_in(i + 1).start()

    @pl.when(i == 0)
    def _():
        zero[...] = jnp.zeros_like(zero)

        def each_gap(visit):
            def gap(e, carry):
                start = gap_ref[2 * e]

                def row(r, c):
                    visit(pltpu.make_async_copy(_row_tile(zero, 0), _row_tile(out_hbm, start + r), zsem))
                    return c

                lax.fori_loop(0, gap_ref[2 * e + 1], row, 0)
                return carry

            lax.fori_loop(0, N_EXPERTS, gap, 0)

            def unused_block(b, carry):
                first = pl.multiple_of(b * block_rows * ROW_TILE, ROW_TILE)
                visit(pltpu.make_async_copy(zero, out_hbm.at[pl.ds(first, block_rows * ROW_TILE), :], zsem))
                return carry

            lax.fori_loop(gap_ref[2 * N_EXPERTS], n_blocks, unused_block, 0)

        each_gap(lambda copy: copy.start())
        each_gap(lambda copy: copy.wait())

    slot = lax.rem(i, DISPATCH_SLOTS)
    tile_in(i).wait()

    def issue(t, carry):
        for k in range(TOP_K_EXPERTS):
            p = pos_ref[base + t * TOP_K_EXPERTS + k]
            pltpu.make_async_copy(_row_tile(xbuf.at[slot], t), _row_tile(out_hbm, p),
                                  sems.at[slot]).start(priority=k)
        return carry

    lax.fori_loop(0, tm, issue, 0, unroll=4)

    @pl.when(i == n - 1)
    def _():
        @pl.when(i >= 1)
        def _():
            wait_rows_out(i - 1)

        wait_rows_out(i)


def _dispatch(xn, pos, gaps, n_rows):
    tm = TOKEN_TILE
    n = xn.shape[0] // ROW_TILE
    assert DISPATCH_SLOTS >= 3
    grid_spec = pltpu.PrefetchScalarGridSpec(
        num_scalar_prefetch=2,
        grid=(n // tm,),
        in_specs=[pl.BlockSpec(memory_space=pl.ANY)],
        out_specs=pl.BlockSpec(memory_space=pl.ANY),
        scratch_shapes=[pltpu.VMEM((DISPATCH_SLOTS, tm * ROW_TILE, LANES), F32),
                        pltpu.SemaphoreType.DMA((DISPATCH_SLOTS,)),
                        pltpu.VMEM((FFN_ROWS * ROW_TILE, LANES), F32),
                        pltpu.SemaphoreType.DMA((DISPATCH_SLOTS,)),
                        pltpu.SemaphoreType.DMA(())],
    )
    return pl.pallas_call(
        _dispatch_kernel,
        grid_spec=grid_spec,
        out_shape=jax.ShapeDtypeStruct((n_rows * ROW_TILE, LANES), F32),
        compiler_params=_cparams(("arbitrary",)),
        name="moe_dispatch",
    )(pos, gaps, xn)


def _store_row_tiles(ref, x, first=0):
    for c in range(ROW_TILE):
        ref[pl.ds(first + c, x.shape[0], stride=ROW_TILE), :] = x[:, c * LANES:(c + 1) * LANES]


def _load_row_tiles(ref, first, rows, lead=()):
    return jnp.concatenate([ref[lead + (pl.ds(first + c, rows, stride=ROW_TILE), slice(None))]
                            for c in range(ROW_TILE)], axis=1)


def _row_tile(ref, r):
    return ref.at[pl.ds(pl.multiple_of(r * ROW_TILE, ROW_TILE), ROW_TILE), :]


def _ffn_kernel(blk_e_ref, nblk_ref, fresh_ref, run_ref, next_e_ref, x_hbm, wg_hbm, wu_hbm, wd_hbm, out_ref,
                xbuf, xsems, wg_f, wu_f, wd_f, wsems, wg_s, wu_s, wd_s, *, layer):
    i = pl.program_id(0)
    nblk = nblk_ref[0]
    tile_rows = out_ref.shape[0]
    rows = tile_rows // ROW_TILE
    live = i < nblk

    def x_copy(block, slot):
        first = pl.multiple_of(block * tile_rows, tile_rows)
        return pltpu.make_async_copy(x_hbm.at[pl.ds(first, tile_rows), :], xbuf.at[slot], xsems.at[slot])

    def w_copies(expert, slot):
        return [pltpu.make_async_copy(src.at[layer, expert], dst.at[slot], wsems.at[slot])
                for src, dst in ((wg_hbm, wg_f), (wu_hbm, wu_f), (wd_hbm, wd_f))]

    @pl.when(i == 0)
    def _():
        for b in range(FFN_X_SLOTS - 1):
            @pl.when(b < nblk)
            def _():
                x_copy(b, b).start()

        @pl.when(nblk > 0)
        def _():
            for copy in w_copies(blk_e_ref[0], 0):
                copy.start()

    ahead = i + (FFN_X_SLOTS - 1)

    @pl.when(ahead < nblk)
    def _():
        x_copy(ahead, lax.rem(ahead, FFN_X_SLOTS)).start()

    @pl.when(live & (fresh_ref[i] == 1))
    def _():
        slot = lax.rem(run_ref[i], 2)
        for copy in w_copies(0, slot):
            copy.wait()
        wg_s[...] = wg_f[slot].astype(BF16)
        wu_s[...] = wu_f[slot].astype(BF16)
        wd_s[...] = wd_f[slot].astype(BF16)

        @pl.when(next_e_ref[i] >= 0)
        def _():
            for copy in w_copies(next_e_ref[i], 1 - slot):
                copy.start()

    @pl.when(live)
    def _():
        slot = lax.rem(i, FFN_X_SLOTS)
        x_copy(0, slot).wait()
        x = _load_row_tiles(xbuf, 0, rows, lead=(slot,)).astype(BF16)
        gate = jnp.dot(x, wg_s[...], preferred_element_type=F32)
        up = jnp.dot(x, wu_s[...], preferred_element_type=F32)
        hid = (jax.nn.silu(gate) * up).astype(BF16)
        _store_row_tiles(out_ref, jnp.dot(hid, wd_s[...], preferred_element_type=F32))

    @pl.when(jnp.logical_not(live))
    def _():
        out_ref[...] = jnp.zeros_like(out_ref)


def _grouped_ffn(x_sorted, blk_e, nblk, fresh, run, next_e, w_gate, w_up, w_down, layer):
    rb = FFN_ROWS
    n_rows = x_sorted.shape[0] // ROW_TILE
    _, _, d, d_e = w_gate.shape
    assert d == ROW_TILE * LANES
    any_spec = pl.BlockSpec(memory_space=pl.ANY)
    grid_spec = pltpu.PrefetchScalarGridSpec(
        num_scalar_prefetch=5,
        grid=(n_rows // rb,),
        in_specs=[any_spec, any_spec, any_spec, any_spec],
        out_specs=pl.BlockSpec((rb * ROW_TILE, LANES), lambda i, *_: (i, 0)),
        scratch_shapes=[
            pltpu.VMEM((FFN_X_SLOTS, rb * ROW_TILE, LANES), F32), pltpu.SemaphoreType.DMA((FFN_X_SLOTS,)),
            pltpu.VMEM((2, d, d_e), F32), pltpu.VMEM((2, d, d_e), F32), pltpu.VMEM((2, d_e, d), F32),
            pltpu.SemaphoreType.DMA((2,)),
            pltpu.VMEM((d, d_e), BF16), pltpu.VMEM((d, d_e), BF16), pltpu.VMEM((d_e, d), BF16),
        ],
    )
    return pl.pallas_call(
        functools.partial(_ffn_kernel, layer=layer),
        grid_spec=grid_spec,
        out_shape=jax.ShapeDtypeStruct((n_rows * ROW_TILE, LANES), F32),
        compiler_params=_cparams(("arbitrary",)),
        name="moe_ffn",
    )(blk_e, nblk, fresh, run, next_e, x_sorted, w_gate, w_up, w_down)


def _combine_kernel(pos_ref, resid_ref, w_ref, src_hbm, *refs, split):
    out_refs, buf, sems = refs[:-2], refs[-2], refs[-1]
    i = pl.program_id(0)
    tm = resid_ref.shape[0]

    def gather(tile, slot):
        base = tile * tm * TOP_K_EXPERTS

        def issue(t, carry):
            for k in range(TOP_K_EXPERTS):
                p = pos_ref[base + t * TOP_K_EXPERTS + k]
                pltpu.make_async_copy(_row_tile(src_hbm, p), _row_tile(buf, (slot * TOP_K_EXPERTS + k) * tm + t),
                                      sems.at[slot]).start(priority=k)
            return carry

        lax.fori_loop(0, tm, issue, 0, unroll=4)

    @pl.when(i == 0)
    def _():
        gather(0, 0)

    @pl.when(i + 1 < pl.num_programs(0))
    def _():
        gather(i + 1, (i + 1) % 2)

    slot = i % 2
    firsts = [pl.multiple_of((slot * TOP_K_EXPERTS + k) * tm * ROW_TILE, ROW_TILE) for k in range(TOP_K_EXPERTS)]
    for first in firsts:
        pltpu.make_async_copy(src_hbm.at[pl.ds(0, tm * ROW_TILE), :],
                              buf.at[pl.ds(first, tm * ROW_TILE), :], sems.at[slot]).wait()
    acc = resid_ref[...]
    w = w_ref[...]
    for k, first in enumerate(firsts):
        acc = acc + w[:, k:k + 1] * _load_row_tiles(buf, first, tm)
    def put(ref):
        ref[...] = acc

    if split:
        _store_stream_tile(out_refs[0], out_refs[1], put)
    else:
        put(out_refs[0])


def _combine(resid, w_pad, src, pos, split):
    n, d = resid.shape
    assert d == ROW_TILE * LANES and TOP_K_EXPERTS == 2
    tm = TOKEN_TILE
    n_prompt_tiles = n // tm - 1
    if split:
        out_specs = [_prompt_spec((tm, d), n_prompt_tiles), _sample_spec((tm, d))]
        out_shape = [jax.ShapeDtypeStruct((n_prompt_tiles * tm, d), F32), jax.ShapeDtypeStruct((tm, d), F32)]
    else:
        out_specs = [pl.BlockSpec((tm, d), lambda i, *_: (i, 0))]
        out_shape = [jax.ShapeDtypeStruct((n, d), F32)]
    grid_spec = pltpu.PrefetchScalarGridSpec(
        num_scalar_prefetch=1,
        grid=(n // tm,),
        in_specs=[
            pl.BlockSpec((tm, d), lambda i, *_: (i, 0)),
            pl.BlockSpec((tm, LANES), lambda i, *_: (i, 0)),
            pl.BlockSpec(memory_space=pl.ANY),
        ],
        out_specs=out_specs,
        scratch_shapes=[pltpu.VMEM((2 * TOP_K_EXPERTS * tm * ROW_TILE, LANES), F32),
                        pltpu.SemaphoreType.DMA((2,))],
    )
    return pl.pallas_call(
        functools.partial(_combine_kernel, split=split),
        grid_spec=grid_spec,
        out_shape=out_shape,
        compiler_params=_cparams(("arbitrary",)),
        name="moe_combine",
    )(pos, resid, w_pad, src)


def _hier_moe(h, g, w_r, b_r, w_gate, w_up, w_down, layer, split=False, attn=None):
    assert N_EXPERTS & (N_EXPERTS - 1) == 0
    n, _ = h.shape
    rb = FFN_ROWS
    if attn is None:
        xn, w_pad, code_pad, hist_pad = _router(h, g, w_r, b_r)
    else:
        h, xn, w_pad, code_pad, hist_pad = _router(h, g, w_r, b_r, attn)
    hist = hist_pad[:, 0, :N_EXPERTS]
    counts = jnp.sum(hist, axis=0)
    padded = (counts + rb - 1) // rb * rb
    pend = jnp.cumsum(padded)
    first = ((pend - padded)[None, :] + jnp.cumsum(hist, axis=0) - hist).astype(jnp.int32)
    first = jnp.pad(first, ((0, 0), (0, LANES - N_EXPERTS)))
    pos = _sorted_positions(code_pad, first)[:, :TOP_K_EXPERTS].reshape(-1)
    n_blocks = (n * TOP_K_EXPERTS + N_EXPERTS * (rb - 1) + rb - 1) // rb
    block_first = jnp.arange(n_blocks, dtype=jnp.int32)[:, None] * rb
    blk_e = jnp.minimum(jnp.sum((pend[None, :] <= block_first).astype(jnp.int32), axis=1), N_EXPERTS - 1)
    fresh = jnp.concatenate([jnp.ones((1,), jnp.int32), (blk_e[1:] != blk_e[:-1]).astype(jnp.int32)])
    run = (jnp.cumsum(fresh) - 1).astype(jnp.int32)
    experts = jnp.arange(N_EXPERTS, dtype=jnp.int32)
    later = jnp.where((counts[None, :] > 0) & (experts[None, :] > experts[:, None]), experts[None, :], N_EXPERTS)
    next_owner = jnp.min(later, axis=1)
    next_e = jnp.where(next_owner < N_EXPERTS, next_owner, -1)[blk_e].astype(jnp.int32)
    nblk = (pend[-1:] // rb).astype(jnp.int32)
    gaps = jnp.concatenate([jnp.stack([pend - padded + counts, padded - counts], axis=1).reshape(-1),
                            nblk]).astype(jnp.int32)
    x_sorted = _dispatch(xn, pos, gaps, n_blocks * rb)
    out_sorted = _grouped_ffn(x_sorted, blk_e, nblk, fresh, run, next_e, w_gate, w_up, w_down, layer)
    out = _combine(h, w_pad, out_sorted, pos, split)
    return out if split else out[0]


def _store_heads_as_rows(ref, x):
    for j in range(N_KV_HEADS):
        ref[pl.ds(j, x.shape[0], stride=N_KV_HEADS), :] = x[:, j * HEAD_DIM:(j + 1) * HEAD_DIM]


def _proj_kernel(h_ref, gkv_ref, gq_ref, wkv_ref, wq_ref, kn_ref, qn_ref, cos_ref, sin_ref,
                 k_ref, v_ref, q_ref, kp_ref, ks_ref, vp_ref, vs_ref):
    hn = _rms(h_ref[...])
    cos = cos_ref[...]
    sin = sin_ref[...]

    def norm_rope(x, g):
        y = _rms(x) * g
        return y * cos + pltpu.roll(y, HEAD_DIM // 2, 1) * sin

    kv = jnp.dot((hn * gkv_ref[...]).astype(BF16), wkv_ref[...], preferred_element_type=F32)
    kw = N_KV_HEADS * HEAD_DIM
    k = jnp.concatenate(
        [norm_rope(kv[:, j * HEAD_DIM:(j + 1) * HEAD_DIM], kn_ref[...]) for j in range(N_KV_HEADS)], axis=1)
    v = kv[:, kw:]
    k_ref[...] = k
    v_ref[...] = v
    q = jnp.dot((hn * gq_ref[...]).astype(BF16), wq_ref[...], preferred_element_type=F32)
    q_ref[...] = jnp.concatenate(
        [norm_rope(q[:, j * HEAD_DIM:(j + 1) * HEAD_DIM], qn_ref[...]) for j in range(N_HEADS)], axis=1)

    def store_by_head(kv_refs):
        _store_heads_as_rows(kv_refs[0], k)
        _store_heads_as_rows(kv_refs[1], v)

    _store_stream_tile((kp_ref, vp_ref), (ks_ref, vs_ref), store_by_head)


def _kvq_proj(h, g_kv, g_q, w_kv, w_q, k_norm, q_norm, cos, sin):
    n, d = h.shape
    tm = TOKEN_TILE
    kw = N_KV_HEADS * HEAD_DIM
    qw = N_HEADS * HEAD_DIM
    row = lambda i: (i, 0)
    fixed = lambda i: (0, 0)
    n_prompt_tiles = n // tm - 1
    seq_tiles = cos.shape[0] // tm - 1
    table_row = lambda i: (jnp.where(i < n_prompt_tiles, lax.rem(i, seq_tiles), seq_tiles), 0)
    head_rows = tm * N_KV_HEADS
    by_head = [_prompt_spec((head_rows, HEAD_DIM), n_prompt_tiles), _sample_spec((head_rows, HEAD_DIM))]
    by_head_shapes = [jax.ShapeDtypeStruct((n_prompt_tiles * head_rows, HEAD_DIM), F32),
                      jax.ShapeDtypeStruct((head_rows, HEAD_DIM), F32)]
    return pl.pallas_call(
        _proj_kernel,
        grid=(n // tm,),
        in_specs=[
            pl.BlockSpec((tm, d), row),
            pl.BlockSpec((1, d), fixed),
            pl.BlockSpec((1, d), fixed),
            pl.BlockSpec((d, 2 * kw), fixed),
            pl.BlockSpec((d, qw), fixed),
            pl.BlockSpec((1, HEAD_DIM), fixed),
            pl.BlockSpec((1, HEAD_DIM), fixed),
            pl.BlockSpec((tm, HEAD_DIM), table_row),
            pl.BlockSpec((tm, HEAD_DIM), table_row),
        ],
        out_specs=[
            pl.BlockSpec((tm, kw), row),
            pl.BlockSpec((tm, kw), row),
            pl.BlockSpec((tm, qw), row),
        ] + by_head + by_head,
        out_shape=[
            jax.ShapeDtypeStruct((n, kw), F32),
            jax.ShapeDtypeStruct((n, kw), F32),
            jax.ShapeDtypeStruct((n, qw), F32),
        ] + by_head_shapes + by_head_shapes,
        compiler_params=_cparams(("arbitrary",)),
        name="kvq_proj",
    )(h, g_kv, g_q, w_kv, w_q, k_norm, q_norm, cos, sin)


def _top_blocks(gate, axis):
    idx = lax.broadcasted_iota(jnp.int32, gate.shape, axis).astype(F32)
    big = float(gate.shape[axis])
    sel = jnp.zeros(gate.shape, jnp.bool_)
    for _ in range(MOBA_TOP_K):
        top = jnp.max(gate, axis=axis, keepdims=True)
        first = jnp.min(jnp.where(gate == top, idx, big), axis=axis, keepdims=True)
        hit = idx == first
        sel = sel | (hit & (top > NEG_INF))
        gate = jnp.where(hit, NEG_INF, gate)
    return sel.astype(F32)


def _moba_prompt_kernel(q_ref, k_ref, v_ref, o_ref, kbf, vt, kmean, sel, s_a, s_b, m_scr, acc):
    j = pl.program_id(2)
    blk = MOBA_BLOCK
    grp = KEY_GROUP
    seq = k_ref.shape[0]
    n_blocks = seq // blk

    @pl.when(j == 0)
    def _():
        k = k_ref[...]
        kbf[...] = k.astype(BF16)
        kmean[...] = jnp.mean(k.reshape(n_blocks, blk, HEAD_DIM), axis=1)
        for n in range(n_blocks):
            vt[:HEAD_DIM, n * blk:(n + 1) * blk] = v_ref[n * blk:(n + 1) * blk, :].T.astype(BF16)
        r = lax.broadcasted_iota(jnp.int32, (ONES_ROWS, seq), 0)
        vt[HEAD_DIM:, :] = jnp.where(r == 0, 1.0, 0.0).astype(BF16)

    q2 = q_ref[...]
    qs = jnp.concatenate([q2[:, h * HEAD_DIM:(h + 1) * HEAD_DIM] for h in range(Q_PER_KV)], axis=0)
    nq = qs.shape[0]
    qt = (qs * (HEAD_DIM ** -0.5 * LOG2_E)).T.astype(BF16)

    def scores(start, n_keys):
        return jnp.dot(kbf[pl.ds(start, n_keys), :], qt, preferred_element_type=F32)

    gate = _dot_nt_3pass(kmean[...], qs)
    row = lax.broadcasted_iota(jnp.int32, gate.shape, 0)
    sel[...] = _top_blocks(jnp.where(row < j, gate, NEG_INF), 0)

    key = lax.broadcasted_iota(jnp.int32, (blk, nq), 0)
    qpos = lax.broadcasted_iota(jnp.int32, (blk, nq), 1) % blk
    own = pl.multiple_of(j * blk, blk)
    s_own = jnp.where(key <= qpos, scores(own, blk), NEG_INF)
    m_own = jnp.max(s_own, axis=0, keepdims=True)
    m_scr[...] = m_own
    acc[...] = jnp.dot(vt[:, pl.ds(own, blk)], jnp.exp2(s_own - m_own).astype(BF16), preferred_element_type=F32)

    def update(tiles, start):
        m_old = m_scr[...]
        m_new = m_old
        for s in tiles:
            m_new = jnp.maximum(m_new, jnp.max(s, axis=0, keepdims=True))
        p = jnp.concatenate([jnp.exp2(s - m_new).astype(BF16) for s in tiles], axis=0)
        alpha = jnp.exp2(m_old - m_new)
        pv = jnp.dot(vt[:, pl.ds(start, len(tiles) * blk)], p, preferred_element_type=F32)
        acc[...] = alpha * acc[...] + pv
        m_scr[...] = m_new

    def fill(buf, g):
        start = pl.multiple_of(g * (grp * blk), grp * blk)
        buf[...] = scores(start, grp * blk)

    def consume(buf, g):
        start = pl.multiple_of(g * (grp * blk), grp * blk)
        update([jnp.where(sel[pl.ds(g * grp + i, 1), :] > 0.0, buf[i * blk:(i + 1) * blk, :], NEG_INF)
                for i in range(grp)], start)

    n_groups = lax.div(j + (grp - 1), grp)
    n_pairs = lax.div(n_groups - 1, 2)
    fill(s_a, 0)

    @pl.when(n_groups > 0)
    def _():
        def pair(h, carry):
            g = 2 * h
            fill(s_b, g + 1)
            consume(s_a, g)
            fill(s_a, g + 2)
            consume(s_b, g + 1)
            return carry

        lax.fori_loop(0, n_pairs, pair, 0)
        g = 2 * n_pairs

        @pl.when(n_groups - g == 1)
        def _():
            consume(s_a, g)

        @pl.when(n_groups - g == 2)
        def _():
            fill(s_b, g + 1)
            consume(s_a, g)
            consume(s_b, g + 1)

    a = acc[...]
    o = (a[:HEAD_DIM] / a[HEAD_DIM:HEAD_DIM + 1]).T
    o_ref[...] = jnp.concatenate([o[h * blk:(h + 1) * blk, :] for h in range(Q_PER_KV)], axis=1)


def _moba_prompt(q, k, v, batch, seq):
    blk = MOBA_BLOCK
    assert seq % (KEY_GROUP * blk) == 0
    n_steps = seq // blk
    qw = Q_PER_KV * HEAD_DIM
    nq = Q_PER_KV * blk
    return pl.pallas_call(
        _moba_prompt_kernel,
        grid=(batch, N_KV_HEADS, n_steps),
        in_specs=[
            pl.BlockSpec((blk, qw), lambda b, c, j: (b * n_steps + j, c)),
            pl.BlockSpec((seq, HEAD_DIM), lambda b, c, j: (b, c)),
            pl.BlockSpec((seq, HEAD_DIM), lambda b, c, j: (b, c)),
        ],
        out_specs=pl.BlockSpec((blk, qw), lambda b, c, j: (b * n_steps + j, c)),
        out_shape=jax.ShapeDtypeStruct((batch * seq, N_HEADS * HEAD_DIM), F32),
        scratch_shapes=[
            pltpu.VMEM((seq, HEAD_DIM), BF16),
            pltpu.VMEM((HEAD_DIM + ONES_ROWS, seq), BF16),
            pltpu.VMEM((seq // blk, HEAD_DIM), F32),
            pltpu.VMEM((seq // blk, nq), F32),
            pltpu.VMEM((KEY_GROUP * blk, nq), F32),
            pltpu.VMEM((KEY_GROUP * blk, nq), F32),
            pltpu.VMEM((1, nq), F32),
            pltpu.VMEM((HEAD_DIM + ONES_ROWS, nq), F32),
        ],
        compiler_params=_cparams(("arbitrary", "arbitrary", "arbitrary")),
        name="moba_prompt",
    )(q, k, v)


def _stack_heads(q8):
    return jnp.concatenate([q8[:, h * HEAD_DIM:(h + 1) * HEAD_DIM] for h in range(N_HEADS)], axis=0)


def _sample_attn_kernel(pt_ref, q_ref, kn_ref, vn_ref, ck_hbm, cv_hbm, o_ref, pages, sems,
                        s_scr, means, sel_scr, qs_scr, m_scr, l_scr, acc, *, dec_seq, n_pages):
    pps = PAGES_PER_STEP
    t = pl.program_id(1)
    steps = pl.num_programs(1)
    n_k_steps = n_pages // pps
    rows = N_HEADS * dec_seq
    rkv = Q_PER_KV * dec_seq
    ppb = MOBA_BLOCK // PAGE_SIZE
    bps = pps // ppb
    n_blocks = n_pages // ppb
    nt_dims = (((1,), (1,)), ((), ()))

    chunk = pl.program_id(0) * steps + t
    n_chunks = pl.num_programs(0) * steps

    def start_chunk(ci):
        seq = lax.div(ci, steps)
        step = lax.rem(ci, steps)
        slot = lax.rem(ci, PAGE_SLOTS)

        def start_pages(cache_hbm, first_page):
            for r in range(pps):
                pltpu.make_async_copy(cache_hbm.at[pt_ref[seq, first_page + r]], pages.at[slot, r],
                                      sems.at[slot]).start()

        @pl.when(step < n_k_steps)
        def _():
            start_pages(ck_hbm, step * pps)

        @pl.when(step >= n_k_steps)
        def _():
            start_pages(cv_hbm, (step - n_k_steps) * pps)

    @pl.when(chunk == 0)
    def _():
        for ci in range(PAGE_SLOTS - 1):
            start_chunk(jnp.int32(ci))

    @pl.when(chunk + (PAGE_SLOTS - 1) < n_chunks)
    def _():
        start_chunk(chunk + (PAGE_SLOTS - 1))

    slot = lax.rem(chunk, PAGE_SLOTS)
    pltpu.make_async_copy(ck_hbm.at[pl.ds(0, pps)], pages.at[slot], sems.at[slot]).wait()

    def head_rows(p, c):
        return pages[slot, p, pl.ds(c, PAGE_SIZE, stride=N_KV_HEADS), :]

    @pl.when(t == 0)
    def _():
        qs_scr[...] = (_stack_heads(q_ref[...]) * (HEAD_DIM ** -0.5 * LOG2_E)).astype(BF16)
        m_scr[...] = jnp.full(m_scr.shape, NEG_INF, F32)
        l_scr[...] = jnp.zeros(l_scr.shape, F32)
        acc[...] = jnp.zeros(acc.shape, F32)

    @pl.when(t < n_k_steps)
    def _():
        qs = qs_scr[...]
        col = pl.multiple_of(t * (pps * PAGE_SIZE), pps * PAGE_SIZE)
        for c in range(N_KV_HEADS):
            kc = jnp.concatenate([head_rows(p, c) for p in range(pps)], axis=0)
            s_scr[c * rkv:(c + 1) * rkv, pl.ds(col, pps * PAGE_SIZE)] = lax.dot_general(
                qs[c * rkv:(c + 1) * rkv], kc.astype(BF16), nt_dims, preferred_element_type=F32)
            means[c, pl.ds(t * bps, bps), :] = jnp.sum(kc.reshape(bps, MOBA_BLOCK, HEAD_DIM), axis=1) / MOBA_BLOCK

    @pl.when(t == n_k_steps - 1)
    def _():
        qf = _stack_heads(q_ref[...])
        gate = jnp.concatenate(
            [lax.dot_general(qf[c * rkv:(c + 1) * rkv], means[c], nt_dims, precision=HIGHEST,
                             preferred_element_type=F32) for c in range(N_KV_HEADS)], axis=0)
        chosen = _top_blocks(gate, 1)
        sel_scr[...] = jnp.concatenate([chosen, jnp.zeros((rows, LANES - n_blocks), F32)], axis=1)

    def softmax_step(tiles):
        m_old = m_scr[...]
        m_new = m_old
        for s in tiles:
            m_new = jnp.maximum(m_new, jnp.max(s, axis=1, keepdims=True))
        m_safe = jnp.where(m_new == NEG_INF, 0.0, m_new)
        alpha = jnp.exp2(m_old - m_safe)
        l_new = alpha * l_scr[...]
        probs = []
        for s in tiles:
            p = jnp.exp2(s - m_safe)
            l_new = l_new + jnp.sum(p, axis=1, keepdims=True)
            probs.append(p.astype(BF16))
        l_scr[...] = l_new
        m_scr[...] = m_new
        return probs, alpha

    @pl.when(t >= n_k_steps)
    def _():
        tv = t - n_k_steps
        sel = sel_scr[...]
        lane = lax.broadcasted_iota(jnp.int32, sel.shape, 1)
        tiles = []
        for b in range(bps):
            n = tv * bps + b
            chosen = jnp.sum(jnp.where(lane == n, sel, 0.0), axis=1, keepdims=True) > 0.0
            col = pl.multiple_of(n * MOBA_BLOCK, MOBA_BLOCK)
            tiles.append(jnp.where(chosen, s_scr[:, pl.ds(col, MOBA_BLOCK)], NEG_INF))
        probs, alpha = softmax_step(tiles)
        prob = jnp.concatenate(probs, axis=1)
        pv = []
        for c in range(N_KV_HEADS):
            vc = jnp.concatenate([head_rows(p, c) for p in range(pps)], axis=0).astype(BF16)
            pv.append(jnp.dot(prob[c * rkv:(c + 1) * rkv], vc, preferred_element_type=F32))
        acc[...] = alpha * acc[...] + jnp.concatenate(pv, axis=0)

    @pl.when(t == pl.num_programs(1) - 1)
    def _():
        qs = qs_scr[...]
        kn = kn_ref[...].astype(BF16)
        vn = vn_ref[...].astype(BF16)
        s = jnp.concatenate(
            [lax.dot_general(qs[c * rkv:(c + 1) * rkv], kn[:, c * HEAD_DIM:(c + 1) * HEAD_DIM], nt_dims,
                             preferred_element_type=F32) for c in range(N_KV_HEADS)], axis=0)
        r2 = lax.broadcasted_iota(jnp.int32, s.shape, 0)
        c2 = lax.broadcasted_iota(jnp.int32, s.shape, 1)
        probs, alpha = softmax_step([jnp.where(c2 <= r2 % dec_seq, s, NEG_INF)])
        pv = [jnp.dot(probs[0][c * rkv:(c + 1) * rkv], vn[:, c * HEAD_DIM:(c + 1) * HEAD_DIM],
                      preferred_element_type=F32) for c in range(N_KV_HEADS)]
        o = (alpha * acc[...] + jnp.concatenate(pv, axis=0)) / l_scr[...]
        o_ref[...] = jnp.concatenate([o[h * dec_seq:(h + 1) * dec_seq, :] for h in range(N_HEADS)], axis=1)


def _sample_attn(page_table, q, k, v, row0, cache_k2, cache_v2, dec_seq):
    n_seq, n_pages = page_table.shape
    pps = PAGES_PER_STEP
    n_k_steps = n_pages // pps
    n_blocks = n_pages * PAGE_SIZE // MOBA_BLOCK
    assert n_pages % pps == 0 and n_blocks <= LANES
    rows = N_HEADS * dec_seq
    kw = N_KV_HEADS * HEAD_DIM
    qw = N_HEADS * HEAD_DIM
    grid_spec = pltpu.PrefetchScalarGridSpec(
        num_scalar_prefetch=1,
        grid=(n_seq, 2 * n_k_steps),
        in_specs=[
            pl.BlockSpec((dec_seq, qw), lambda s, t, pt: (row0 + s, 0)),
            pl.BlockSpec((dec_seq, kw), lambda s, t, pt: (row0 + s, 0)),
            pl.BlockSpec((dec_seq, kw), lambda s, t, pt: (row0 + s, 0)),
            pl.BlockSpec(memory_space=pl.ANY),
            pl.BlockSpec(memory_space=pl.ANY),
        ],
        out_specs=pl.BlockSpec((dec_seq, qw), lambda s, t, pt: (s, 0)),
        scratch_shapes=[
            pltpu.VMEM((PAGE_SLOTS, pps, PAGE_SIZE * N_KV_HEADS, HEAD_DIM), F32),
            pltpu.SemaphoreType.DMA((PAGE_SLOTS,)),
            pltpu.VMEM((rows, n_pages * PAGE_SIZE), F32),
            pltpu.VMEM((N_KV_HEADS, n_blocks, HEAD_DIM), F32),
            pltpu.VMEM((rows, LANES), F32),
            pltpu.VMEM((rows, HEAD_DIM), BF16),
            pltpu.VMEM((rows, 1), F32),
            pltpu.VMEM((rows, 1), F32),
            pltpu.VMEM((rows, HEAD_DIM), F32),
        ],
    )
    return pl.pallas_call(
        functools.partial(_sample_attn_kernel, dec_seq=dec_seq, n_pages=n_pages),
        grid_spec=grid_spec,
        out_shape=jax.ShapeDtypeStruct((n_seq * dec_seq, qw), F32),
        compiler_params=_cparams(("arbitrary", "arbitrary")),
        name="sample_attn",
    )(page_table, q, k, v, cache_k2, cache_v2)


def _mix_tables(w_s, b_s, dec_seq):
    tm = TOKEN_TILE
    causal = jnp.tril(jnp.ones((GMLP_CHUNK, GMLP_CHUNK), dtype=bool))
    w = jnp.where(causal[None], w_s, jnp.zeros_like(w_s))
    eye_p = jnp.eye(tm // GMLP_CHUNK, dtype=w.dtype)
    mix_p = jnp.einsum("ab,gts->gatbs", eye_p, w).reshape(GMLP_GROUPS, tm, tm)
    eye_s = jnp.eye(tm // dec_seq, dtype=w.dtype)
    mix_s = jnp.einsum("ab,gts->gatbs", eye_s, w[:, :dec_seq, :dec_seq]).reshape(GMLP_GROUPS, tm, tm)
    mix = jnp.stack([mix_p, mix_s]).astype(BF16)
    bias_p = jnp.tile(b_s.T, (tm // GMLP_CHUNK, 1))
    bias_s = jnp.tile(b_s.T[:dec_seq], (tm // dec_seq, 1))
    bias = jnp.stack([bias_p, bias_s])
    bias = jnp.pad(bias, ((0, 0), (0, 0), (0, LANES - GMLP_GROUPS)))
    return mix, bias


def _rope_tables(pos):
    half = HEAD_DIM // 2
    inv = ROPE_THETA ** (-jnp.arange(half, dtype=F32) * 2.0 / HEAD_DIM)
    ang = pos.astype(F32)[:, None] * inv[None, :]
    cos = jnp.cos(ang)
    sin = jnp.sin(ang)
    return jnp.concatenate([cos, cos], axis=1), jnp.concatenate([-sin, sin], axis=1)


def _router_tables(w_grp, b_grp, w_rt, b_rt):
    w = jnp.concatenate([w_grp, w_rt], axis=1)
    b = jnp.concatenate([b_grp, b_rt], axis=0)
    pad = LANES - w.shape[1]
    return jnp.pad(w, ((0, 0), (0, pad))), jnp.pad(b, (0, pad)).reshape(1, LANES)


def kernel(x_prompt, x_sample, cache_k, cache_v, page_table, norm_mix, norm_ffn, a_w_in, a_ln_g, a_ln_b,
           a_w_s, a_b_s, a_w_out, kv_norm, w_kv, k_norm, b_w_q, b_q_norm, b_w_o, moe_w_grp, moe_b_grp,
           moe_w_rt, moe_b_rt, moe_w_gate, moe_w_up, moe_w_down):
    batch, seq, d = x_prompt.shape
    n_seq, dec_seq, _ = x_sample.shape
    n_prompt = batch * seq
    n_sample = n_seq * dec_seq
    assert n_prompt % TOKEN_TILE == 0 and n_sample == TOKEN_TILE and seq % MOBA_BLOCK == 0
    past_len = page_table.shape[1] * PAGE_SIZE
    assert past_len % MOBA_BLOCK == 0 and dec_seq <= MOBA_BLOCK

    pos = jnp.concatenate([jnp.arange(seq), jnp.tile(past_len + jnp.arange(dec_seq), n_seq)])
    cos, sin = _rope_tables(pos)
    row = lambda a: a.reshape(1, -1)

    mix, bias = _mix_tables(a_w_s[0], a_b_s[0], dec_seq)
    h, vg_sample = _gmlp_layer(x_prompt.reshape(n_prompt, d), x_sample.reshape(n_sample, d), row(norm_mix[0]),
                               a_w_in[0].astype(BF16), row(a_ln_g[0]), row(a_ln_b[0]), mix, bias,
                               a_w_out[0].astype(BF16))
    moe = []
    for layer in range(2):
        w_r, b_r = _router_tables(moe_w_grp[layer], moe_b_grp[layer], moe_w_rt[layer], moe_b_rt[layer])
        moe.append((row(norm_ffn[layer]), w_r, b_r, moe_w_gate, moe_w_up, moe_w_down, layer))
    h = _hier_moe(h, *moe[0])

    k, v, q, k_p, k_s, v_p, v_s = _kvq_proj(h, row(kv_norm), row(norm_mix[1]), w_kv.astype(BF16),
                                            b_w_q[0].astype(BF16), row(k_norm), row(b_q_norm[0]), cos, sin)

    o_prompt = _moba_prompt(q, k, v, batch, seq)
    n_phys = cache_k.shape[0]
    cache_k2 = cache_k.reshape(n_phys, PAGE_SIZE * N_KV_HEADS, HEAD_DIM)
    cache_v2 = cache_v.reshape(n_phys, PAGE_SIZE * N_KV_HEADS, HEAD_DIM)
    o_sample = _sample_attn(page_table, q, k, v, n_prompt // dec_seq, cache_k2, cache_v2, dec_seq)
    y_prompt, y_sample = _hier_moe(h, *moe[1], split=True, attn=(o_prompt, o_sample, b_w_o[0].astype(BF16)))

    n_pages_new = seq // PAGE_SIZE
    return (y_prompt.reshape(batch, seq, d),
            y_sample.reshape(n_seq, dec_seq, d),
            k_p.reshape(batch, n_pages_new, PAGE_SIZE, N_KV_HEADS, HEAD_DIM),
            v_p.reshape(batch, n_pages_new, PAGE_SIZE, N_KV_HEADS, HEAD_DIM),
            k_s.reshape(n_seq, dec_seq, N_KV_HEADS, HEAD_DIM),
            v_s.reshape(n_seq, dec_seq, N_KV_HEADS, HEAD_DIM),
            vg_sample.reshape(1, n_seq, dec_seq, -1))
```

```python
import functools
import math

import jax
import jax.numpy as jnp
from jax import lax
from jax.experimental import pallas as pl
from jax.experimental.pallas import tpu as pltpu

F32 = jnp.float32
BF16 = jnp.bfloat16
HIGHEST = lax.Precision.HIGHEST

GMLP_CHUNK = 128
GMLP_GROUPS = 8
N_HEADS = 8
N_KV_HEADS = 4
HEAD_DIM = 128
Q_PER_KV = N_HEADS // N_KV_HEADS
MOBA_BLOCK = 256
MOBA_TOP_K = 3
ROPE_THETA = 10000.0
N_GROUPS = 4
EXPERTS_PER_GROUP = 8
N_EXPERTS = N_GROUPS * EXPERTS_PER_GROUP
TOP_K_EXPERTS = 2
PAGE_SIZE = 128
EPS = 1e-6

LANES = 128
ROW_TILE = 8
TOKEN_TILE = 256
FFN_ROWS = 256
FFN_X_SLOTS = 4
DISPATCH_SLOTS = 4
PAGES_PER_STEP = 16
PAGE_SLOTS = 4
KEY_GROUP = 4
ONES_ROWS = 16
LOG2_E = math.log2(math.e)
VMEM_LIMIT = 56 * 1024 * 1024

NEG_INF = float("-inf")


def _cparams(sem):
    return pltpu.CompilerParams(dimension_semantics=sem, vmem_limit_bytes=VMEM_LIMIT)


def _rms(x):
    return x * lax.rsqrt(jnp.mean(x * x, axis=-1, keepdims=True) + EPS)


def _dot_3pass(a, b, dims):
    a_hi = a.astype(BF16)
    b_hi = b.astype(BF16)
    a_lo = (a - a_hi.astype(F32)).astype(BF16)
    b_lo = (b - b_hi.astype(F32)).astype(BF16)
    dot = functools.partial(lax.dot_general, dimension_numbers=dims, preferred_element_type=F32)
    return dot(a_hi, b_hi) + (dot(a_hi, b_lo) + dot(a_lo, b_hi))


def _dot_nt_3pass(a, b):
    return _dot_3pass(a, b, (((1,), (1,)), ((), ())))


def _prompt_spec(block, n_prompt_tiles):
    return pl.BlockSpec(block, lambda i, *_: (jnp.minimum(i, n_prompt_tiles - 1), 0))


def _sample_spec(block):
    return pl.BlockSpec(block, lambda i, *_: (0, 0))


def _is_sample_tile():
    return pl.program_id(0) == pl.num_programs(0) - 1


def _stream_tile(prompt_ref, sample_ref):
    return jnp.where(_is_sample_tile(), sample_ref[...], prompt_ref[...])


def _store_stream_tile(prompt_ref, sample_ref, store):
    @pl.when(jnp.logical_not(_is_sample_tile()))
    def _():
        store(prompt_ref)

    @pl.when(_is_sample_tile())
    def _():
        store(sample_ref)


def _gmlp_kernel(xp_ref, xs_ref, g_ref, win_ref, lng_ref, lnb_ref, mix_ref, bias_ref, wout_ref,
                 h_ref, vg_ref, *, d_gate, n_groups):
    i = pl.program_id(0)
    x = _stream_tile(xp_ref, xs_ref)
    xb = (_rms(x) * g_ref[...]).astype(BF16)
    vp = jax.nn.gelu(jnp.dot(xb, win_ref[:, d_gate:], preferred_element_type=F32))
    vc = vp - jnp.mean(vp, axis=-1, keepdims=True)
    var = jnp.mean(vc * vc, axis=-1, keepdims=True)
    vg = vc * lax.rsqrt(var + EPS) * lng_ref[...] + lnb_ref[...]
    vgb = vg.astype(BF16)
    cw = d_gate // n_groups
    bias = bias_ref[0]
    u = jax.nn.gelu(jnp.dot(xb, win_ref[:, :d_gate], preferred_element_type=F32))
    parts = []
    for g in range(n_groups):
        mixed = jnp.dot(mix_ref[0, g], vgb[:, g * cw:(g + 1) * cw], preferred_element_type=F32)
        mixed = mixed + bias[:, g:g + 1]
        parts.append((u[:, g * cw:(g + 1) * cw] * mixed).astype(BF16))
    gated = jnp.concatenate(parts, axis=1)
    h_ref[...] = x + jnp.dot(gated, wout_ref[...], preferred_element_type=F32)

    @pl.when(i == pl.num_programs(0) - 1)
    def _():
        vg_ref[...] = vg


def _gmlp_layer(x_prompt, x_sample, g, w_in, ln_g, ln_b, mix, bias, w_out):
    d = x_prompt.shape[1]
    d_gate = w_out.shape[0]
    tm = TOKEN_TILE
    n_prompt_tiles = x_prompt.shape[0] // tm
    n_tiles = n_prompt_tiles + 1
    n = n_tiles * tm
    kind = lambda i: jnp.where(i < n_prompt_tiles, 0, 1)
    return pl.pallas_call(
        functools.partial(_gmlp_kernel, d_gate=d_gate, n_groups=GMLP_GROUPS),
        grid=(n_tiles,),
        in_specs=[
            _prompt_spec((tm, d), n_prompt_tiles),
            _sample_spec((tm, d)),
            pl.BlockSpec((1, d), lambda i: (0, 0)),
            pl.BlockSpec((d, 2 * d_gate), lambda i: (0, 0)),
            pl.BlockSpec((1, d_gate), lambda i: (0, 0)),
            pl.BlockSpec((1, d_gate), lambda i: (0, 0)),
            pl.BlockSpec((1, GMLP_GROUPS, tm, tm), lambda i: (kind(i), 0, 0, 0)),
            pl.BlockSpec((1, tm, LANES), lambda i: (kind(i), 0, 0)),
            pl.BlockSpec((d_gate, d), lambda i: (0, 0)),
        ],
        out_specs=[
            pl.BlockSpec((tm, d), lambda i: (i, 0)),
            pl.BlockSpec((tm, d_gate), lambda i: (0, 0)),
        ],
        out_shape=[
            jax.ShapeDtypeStruct((n, d), F32),
            jax.ShapeDtypeStruct((tm, d_gate), F32),
        ],
        compiler_params=_cparams(("arbitrary",)),
        name="gmlp_layer",
    )(x_prompt, x_sample, g, w_in, ln_g, ln_b, mix, bias, w_out)


def _router_kernel(*refs, with_attn):
    if with_attn:
        h_ref, op_ref, os_ref, wo_ref, g_ref, wr_ref, br_ref, h_out, xn_ref, w_ref, code_ref, hist_ref = refs
        o = _stream_tile(op_ref, os_ref).astype(BF16)
        h = h_ref[...] + jnp.dot(o, wo_ref[...], preferred_element_type=F32)
        h_out[...] = h
    else:
        h_ref, g_ref, wr_ref, br_ref, xn_ref, w_ref, code_ref, hist_ref = refs
        h = h_ref[...]
    xn = _rms(h) * g_ref[...]
    _store_row_tiles(xn_ref, xn)
    logits = _dot_3pass(xn, wr_ref[...], (((1,), (0,)), ((), ()))) + br_ref[...]
    lane = lax.broadcasted_iota(jnp.int32, logits.shape, 1)
    lane_f = lane.astype(F32)

    def first_lane(hit):
        return jnp.min(jnp.where(hit, lane_f, float(LANES)), axis=1, keepdims=True).astype(jnp.int32)

    is_grp = lane < N_GROUPS
    gl = jnp.where(is_grp, logits, NEG_INF)
    gmax = jnp.max(gl, axis=1, keepdims=True)
    gidx = first_lane(is_grp & (logits == gmax))
    p_g = 1.0 / jnp.sum(jnp.where(is_grp, jnp.exp(gl - gmax), 0.0), axis=1, keepdims=True)
    lo = N_GROUPS + gidx * EXPERTS_PER_GROUP
    in_grp = (lane >= lo) & (lane < lo + EXPERTS_PER_GROUP)
    v0 = jnp.max(jnp.where(in_grp, logits, NEG_INF), axis=1, keepdims=True)
    i0 = first_lane(in_grp & (logits == v0))
    rest = in_grp & (lane != i0)
    v1 = jnp.max(jnp.where(rest, logits, NEG_INF), axis=1, keepdims=True)
    i1 = first_lane(rest & (logits == v1))
    t = jnp.exp(v1 - v0)
    w0 = p_g * (1.0 / (1.0 + t))
    w1 = p_g * (t / (1.0 + t))
    e0 = i0 - N_GROUPS
    e1 = i1 - N_GROUPS
    w_ref[...] = jnp.where(lane == 0, w0, jnp.where(lane == 1, w1, 0.0))
    tm = logits.shape[0]
    onehot = jnp.concatenate([(lane == e0).astype(F32), (lane == e1).astype(F32)], axis=0)
    a_row = lax.broadcasted_iota(jnp.int32, (2 * tm, 2 * tm), 0)
    a_col = lax.broadcasted_iota(jnp.int32, (2 * tm, 2 * tm), 1)
    earlier = (a_col < a_row).astype(BF16)
    before = jnp.dot(earlier, onehot.astype(BF16), preferred_element_type=F32)
    rank = jnp.sum(before * onehot, axis=1, keepdims=True).astype(jnp.int32)
    code_ref[...] = jnp.where(lane == 0, rank[:tm] * N_EXPERTS + e0,
                              jnp.where(lane == 1, rank[tm:] * N_EXPERTS + e1, 0))
    hist = jnp.sum(onehot, axis=0, keepdims=True).astype(jnp.int32)
    hist_ref[0] = hist


def _router(h, g, w_r, b_r, attn=None):
    n, d = h.shape
    tm = TOKEN_TILE
    row = lambda i: (i, 0)
    fixed = lambda i: (0, 0)
    in_specs = [pl.BlockSpec((tm, d), row)]
    out_specs, out_shape, operands = [], [], [h]
    if attn is not None:
        o_prompt, o_sample, w_o = attn
        ow = o_prompt.shape[1]
        in_specs += [_prompt_spec((tm, ow), n // tm - 1), _sample_spec((tm, ow)), pl.BlockSpec(w_o.shape, fixed)]
        operands += [o_prompt, o_sample, w_o]
        out_specs.append(pl.BlockSpec((tm, d), row))
        out_shape.append(jax.ShapeDtypeStruct((n, d), F32))
    in_specs += [pl.BlockSpec((1, d), fixed), pl.BlockSpec((d, LANES), fixed), pl.BlockSpec((1, LANES), fixed)]
    operands += [g, w_r, b_r]
    out_specs += [
        pl.BlockSpec((tm * ROW_TILE, LANES), row),
        pl.BlockSpec((tm, LANES), row),
        pl.BlockSpec((tm, LANES), row),
        pl.BlockSpec((1, 1, LANES), lambda i: (i, 0, 0)),
    ]
    out_shape += [
        jax.ShapeDtypeStruct((n * ROW_TILE, LANES), F32),
        jax.ShapeDtypeStruct((n, LANES), F32),
        jax.ShapeDtypeStruct((n, LANES), jnp.int32),
        jax.ShapeDtypeStruct((n // tm, 1, LANES), jnp.int32),
    ]
    return pl.pallas_call(
        functools.partial(_router_kernel, with_attn=attn is not None),
        grid=(n // tm,),
        in_specs=in_specs,
        out_specs=out_specs,
        out_shape=out_shape,
        compiler_params=_cparams(("arbitrary",)),
        name="moe_router",
    )(*operands)


def _pos_kernel(code_ref, first_ref, pos_ref, *, tiles_per_step):
    i = pl.program_id(0)
    tm = TOKEN_TILE
    shift = N_EXPERTS.bit_length() - 1
    for s in range(tiles_per_step):
        code = code_ref[s * tm:(s + 1) * tm, :]
        expert = lax.bitwise_and(code, N_EXPERTS - 1)
        rank = lax.shift_right_logical(code, shift)
        off = first_ref[pl.ds(i * tiles_per_step + s, 1), :]
        lane = lax.broadcasted_iota(jnp.int32, code.shape, 1)
        pos = [jnp.sum(jnp.where(lane == expert[:, k:k + 1], off, 0), axis=1, keepdims=True) + rank[:, k:k + 1]
               for k in range(TOP_K_EXPERTS)]
        pos_ref[s * tm:(s + 1) * tm, :] = jnp.where(lane == 0, pos[0], jnp.where(lane == 1, pos[1], 0))


def _sorted_positions(code_pad, first):
    n = code_pad.shape[0]
    n_tiles = n // TOKEN_TILE
    tiles_per_step = max(t for t in range(1, 17) if n_tiles % t == 0)
    rows = tiles_per_step * TOKEN_TILE
    return pl.pallas_call(
        functools.partial(_pos_kernel, tiles_per_step=tiles_per_step),
        grid=(n_tiles // tiles_per_step,),
        in_specs=[
            pl.BlockSpec((rows, LANES), lambda i: (i, 0)),
            pl.BlockSpec(first.shape, lambda i: (0, 0)),
        ],
        out_specs=pl.BlockSpec((rows, LANES), lambda i: (i, 0)),
        out_shape=jax.ShapeDtypeStruct((n, LANES), jnp.int32),
        compiler_params=_cparams(("arbitrary",)),
        name="moe_pos",
    )(code_pad, first)


def _dispatch_kernel(pos_ref, gap_ref, xn_hbm, out_hbm, xbuf, xsems, zero, sems, zsem):
    i = pl.program_id(0)
    n = pl.num_programs(0)
    tile_rows = xbuf.shape[1]
    tm = tile_rows // ROW_TILE
    base = i * tm * TOP_K_EXPERTS
    block_rows = zero.shape[0] // ROW_TILE
    n_blocks = out_hbm.shape[0] // zero.shape[0]

    def tile_in(tile):
        slot = lax.rem(tile, DISPATCH_SLOTS)
        first = pl.multiple_of(tile * tile_rows, tile_rows)
        return pltpu.make_async_copy(xn_hbm.at[pl.ds(first, tile_rows), :], xbuf.at[slot], xsems.at[slot])

    def wait_rows_out(tile):
        slot = lax.rem(tile, DISPATCH_SLOTS)
        for _ in range(TOP_K_EXPERTS):
            pltpu.make_async_copy(xbuf.at[slot], out_hbm.at[pl.ds(0, tile_rows), :], sems.at[slot]).wait()

    @pl.when(i == 0)
    def _():
        tile_in(i).start()

    @pl.when(i >= 2)
    def _():
        wait_rows_out(i - 2)

    @pl.when(i + 1 < n)
    def _():
        tile_in(i + 1).start()

    @pl.when(i == 0)
    def _():
        zero[...] = jnp.zeros_like(zero)

        def each_gap(visit):
            def gap(e, carry):
                start = gap_ref[2 * e]

                def row(r, c):
                    visit(pltpu.make_async_copy(_row_tile(zero, 0), _row_tile(out_hbm, start + r), zsem))
                    return c

                lax.fori_loop(0, gap_ref[2 * e + 1], row, 0)
                return carry

            lax.fori_loop(0, N_EXPERTS, gap, 0)

            def unused_block(b, carry):
                first = pl.multiple_of(b * block_rows * ROW_TILE, ROW_TILE)
                visit(pltpu.make_async_copy(zero, out_hbm.at[pl.ds(first, block_rows * ROW_TILE), :], zsem))
                return carry

            lax.fori_loop(gap_ref[2 * N_EXPERTS], n_blocks, unused_block, 0)

        each_gap(lambda copy: copy.start())
        each_gap(lambda copy: copy.wait())

    slot = lax.rem(i, DISPATCH_SLOTS)
    tile_in(i).wait()

    def issue(t, carry):
        for k in range(TOP_K_EXPERTS):
            p = pos_ref[base + t * TOP_K_EXPERTS + k]
            pltpu.make_async_copy(_row_tile(xbuf.at[slot], t), _row_tile(out_hbm, p),
                                  sems.at[slot]).start(priority=k)
        return carry

    lax.fori_loop(0, tm, issue, 0, unroll=4)

    @pl.when(i == n - 1)
    def _():
        @pl.when(i >= 1)
        def _():
            wait_rows_out(i - 1)

        wait_rows_out(i)


def _dispatch(xn, pos, gaps, n_rows):
    tm = TOKEN_TILE
    n = xn.shape[0] // ROW_TILE
    assert DISPATCH_SLOTS >= 3
    grid_spec = pltpu.PrefetchScalarGridSpec(
        num_scalar_prefetch=2,
        grid=(n // tm,),
        in_specs=[pl.BlockSpec(memory_space=pl.ANY)],
        out_specs=pl.BlockSpec(memory_space=pl.ANY),
        scratch_shapes=[pltpu.VMEM((DISPATCH_SLOTS, tm * ROW_TILE, LANES), F32),
                        pltpu.SemaphoreType.DMA((DISPATCH_SLOTS,)),
                        pltpu.VMEM((FFN_ROWS * ROW_TILE, LANES), F32),
                        pltpu.SemaphoreType.DMA((DISPATCH_SLOTS,)),
                        pltpu.SemaphoreType.DMA(())],
    )
    return pl.pallas_call(
        _dispatch_kernel,
        grid_spec=grid_spec,
        out_shape=jax.ShapeDtypeStruct((n_rows * ROW_TILE, LANES), F32),
        compiler_params=_cparams(("arbitrary",)),
        name="moe_dispatch",
    )(pos, gaps, xn)


def _store_row_tiles(ref, x, first=0):
    for c in range(ROW_TILE):
        ref[pl.ds(first + c, x.shape[0], stride=ROW_TILE), :] = x[:, c * LANES:(c + 1) * LANES]


def _load_row_tiles(ref, first, rows, lead=()):
    return jnp.concatenate([ref[lead + (pl.ds(first + c, rows, stride=ROW_TILE), slice(None))]
                            for c in range(ROW_TILE)], axis=1)


def _row_tile(ref, r):
    return ref.at[pl.ds(pl.multiple_of(r * ROW_TILE, ROW_TILE), ROW_TILE), :]


def _ffn_kernel(blk_e_ref, nblk_ref, fresh_ref, run_ref, next_e_ref, x_hbm, wg_hbm, wu_hbm, wd_hbm, out_ref,
                xbuf, xsems, wg_f, wu_f, wd_f, wsems, wg_s, wu_s, wd_s, *, layer):
    i = pl.program_id(0)
    nblk = nblk_ref[0]
    tile_rows = out_ref.shape[0]
    rows = tile_rows // ROW_TILE
    live = i < nblk

    def x_copy(block, slot):
        first = pl.multiple_of(block * tile_rows, tile_rows)
        return pltpu.make_async_copy(x_hbm.at[pl.ds(first, tile_rows), :], xbuf.at[slot], xsems.at[slot])

    def w_copies(expert, slot):
        return [pltpu.make_async_copy(src.at[layer, expert], dst.at[slot], wsems.at[slot])
                for src, dst in ((wg_hbm, wg_f), (wu_hbm, wu_f), (wd_hbm, wd_f))]

    @pl.when(i == 0)
    def _():
        for b in range(FFN_X_SLOTS - 1):
            @pl.when(b < nblk)
            def _():
                x_copy(b, b).start()

        @pl.when(nblk > 0)
        def _():
            for copy in w_copies(blk_e_ref[0], 0):
                copy.start()

    ahead = i + (FFN_X_SLOTS - 1)

    @pl.when(ahead < nblk)
    def _():
        x_copy(ahead, lax.rem(ahead, FFN_X_SLOTS)).start()

    @pl.when(live & (fresh_ref[i] == 1))
    def _():
        slot = lax.rem(run_ref[i], 2)
        for copy in w_copies(0, slot):
            copy.wait()
        wg_s[...] = wg_f[slot].astype(BF16)
        wu_s[...] = wu_f[slot].astype(BF16)
        wd_s[...] = wd_f[slot].astype(BF16)

        @pl.when(next_e_ref[i] >= 0)
        def _():
            for copy in w_copies(next_e_ref[i], 1 - slot):
                copy.start()

    @pl.when(live)
    def _():
        slot = lax.rem(i, FFN_X_SLOTS)
        x_copy(0, slot).wait()
        x = _load_row_tiles(xbuf, 0, rows, lead=(slot,)).astype(BF16)
        gate = jnp.dot(x, wg_s[...], preferred_element_type=F32)
        up = jnp.dot(x, wu_s[...], preferred_element_type=F32)
        hid = (jax.nn.silu(gate) * up).astype(BF16)
        _store_row_tiles(out_ref, jnp.dot(hid, wd_s[...], preferred_element_type=F32))

    @pl.when(jnp.logical_not(live))
    def _():
        out_ref[...] = jnp.zeros_like(out_ref)


def _grouped_ffn(x_sorted, blk_e, nblk, fresh, run, next_e, w_gate, w_up, w_down, layer):
    rb = FFN_ROWS
    n_rows = x_sorted.shape[0] // ROW_TILE
    _, _, d, d_e = w_gate.shape
    assert d == ROW_TILE * LANES
    any_spec = pl.BlockSpec(memory_space=pl.ANY)
    grid_spec = pltpu.PrefetchScalarGridSpec(
        num_scalar_prefetch=5,
        grid=(n_rows // rb,),
        in_specs=[any_spec, any_spec, any_spec, any_spec],
        out_specs=pl.BlockSpec((rb * ROW_TILE, LANES), lambda i, *_: (i, 0)),
        scratch_shapes=[
            pltpu.VMEM((FFN_X_SLOTS, rb * ROW_TILE, LANES), F32), pltpu.SemaphoreType.DMA((FFN_X_SLOTS,)),
            pltpu.VMEM((2, d, d_e), F32), pltpu.VMEM((2, d, d_e), F32), pltpu.VMEM((2, d_e, d), F32),
            pltpu.SemaphoreType.DMA((2,)),
            pltpu.VMEM((d, d_e), BF16), pltpu.VMEM((d, d_e), BF16), pltpu.VMEM((d_e, d), BF16),
        ],
    )
    return pl.pallas_call(
        functools.partial(_ffn_kernel, layer=layer),
        grid_spec=grid_spec,
        out_shape=jax.ShapeDtypeStruct((n_rows * ROW_TILE, LANES), F32),
        compiler_params=_cparams(("arbitrary",)),
        name="moe_ffn",
    )(blk_e, nblk, fresh, run, next_e, x_sorted, w_gate, w_up, w_down)


def _combine_kernel(pos_ref, resid_ref, w_ref, src_hbm, *refs, split):
    out_refs, buf, sems = refs[:-2], refs[-2], refs[-1]
    i = pl.program_id(0)
    tm = resid_ref.shape[0]

    def gather(tile, slot):
        base = tile * tm * TOP_K_EXPERTS

        def issue(t, carry):
            for k in range(TOP_K_EXPERTS):
                p = pos_ref[base + t * TOP_K_EXPERTS + k]
                pltpu.make_async_copy(_row_tile(src_hbm, p), _row_tile(buf, (slot * TOP_K_EXPERTS + k) * tm + t),
                                      sems.at[slot]).start(priority=k)
            return carry

        lax.fori_loop(0, tm, issue, 0, unroll=4)

    @pl.when(i == 0)
    def _():
        gather(0, 0)

    @pl.when(i + 1 < pl.num_programs(0))
    def _():
        gather(i + 1, (i + 1) % 2)

    slot = i % 2
    firsts = [pl.multiple_of((slot * TOP_K_EXPERTS + k) * tm * ROW_TILE, ROW_TILE) for k in range(TOP_K_EXPERTS)]
    for first in firsts:
        pltpu.make_async_copy(src_hbm.at[pl.ds(0, tm * ROW_TILE), :],
                              buf.at[pl.ds(first, tm * ROW_TILE), :], sems.at[slot]).wait()
    acc = resid_ref[...]
    w = w_ref[...]
    for k, first in enumerate(firsts):
        acc = acc + w[:, k:k + 1] * _load_row_tiles(buf, first, tm)
    def put(ref):
        ref[...] = acc

    if split:
        _store_stream_tile(out_refs[0], out_refs[1], put)
    else:
        put(out_refs[0])


def _combine(resid, w_pad, src, pos, split):
    n, d = resid.shape
    assert d == ROW_TILE * LANES and TOP_K_EXPERTS == 2
    tm = TOKEN_TILE
    n_prompt_tiles = n // tm - 1
    if split:
        out_specs = [_prompt_spec((tm, d), n_prompt_tiles), _sample_spec((tm, d))]
        out_shape = [jax.ShapeDtypeStruct((n_prompt_tiles * tm, d), F32), jax.ShapeDtypeStruct((tm, d), F32)]
    else:
        out_specs = [pl.BlockSpec((tm, d), lambda i, *_: (i, 0))]
        out_shape = [jax.ShapeDtypeStruct((n, d), F32)]
    grid_spec = pltpu.PrefetchScalarGridSpec(
        num_scalar_prefetch=1,
        grid=(n // tm,),
        in_specs=[
            pl.BlockSpec((tm, d), lambda i, *_: (i, 0)),
            pl.BlockSpec((tm, LANES), lambda i, *_: (i, 0)),
            pl.BlockSpec(memory_space=pl.ANY),
        ],
        out_specs=out_specs,
        scratch_shapes=[pltpu.VMEM((2 * TOP_K_EXPERTS * tm * ROW_TILE, LANES), F32),
                        pltpu.SemaphoreType.DMA((2,))],
    )
    return pl.pallas_call(
        functools.partial(_combine_kernel, split=split),
        grid_spec=grid_spec,
        out_shape=out_shape,
        compiler_params=_cparams(("arbitrary",)),
        name="moe_combine",
    )(pos, resid, w_pad, src)


def _hier_moe(h, g, w_r, b_r, w_gate, w_up, w_down, layer, split=False, attn=None):
    assert N_EXPERTS & (N_EXPERTS - 1) == 0
    n, _ = h.shape
    rb = FFN_ROWS
    if attn is None:
        xn, w_pad, code_pad, hist_pad = _router(h, g, w_r, b_r)
    else:
        h, xn, w_pad, code_pad, hist_pad = _router(h, g, w_r, b_r, attn)
    hist = hist_pad[:, 0, :N_EXPERTS]
    counts = jnp.sum(hist, axis=0)
    padded = (counts + rb - 1) // rb * rb
    pend = jnp.cumsum(padded)
    first = ((pend - padded)[None, :] + jnp.cumsum(hist, axis=0) - hist).astype(jnp.int32)
    first = jnp.pad(first, ((0, 0), (0, LANES - N_EXPERTS)))
    pos = _sorted_positions(code_pad, first)[:, :TOP_K_EXPERTS].reshape(-1)
    n_blocks = (n * TOP_K_EXPERTS + N_EXPERTS * (rb - 1) + rb - 1) // rb
    block_first = jnp.arange(n_blocks, dtype=jnp.int32)[:, None] * rb
    blk_e = jnp.minimum(jnp.sum((pend[None, :] <= block_first).astype(jnp.int32), axis=1), N_EXPERTS - 1)
    fresh = jnp.concatenate([jnp.ones((1,), jnp.int32), (blk_e[1:] != blk_e[:-1]).astype(jnp.int32)])
    run = (jnp.cumsum(fresh) - 1).astype(jnp.int32)
    experts = jnp.arange(N_EXPERTS, dtype=jnp.int32)
    later = jnp.where((counts[None, :] > 0) & (experts[None, :] > experts[:, None]), experts[None, :], N_EXPERTS)
    next_owner = jnp.min(later, axis=1)
    next_owner = jnp.where(next_owner < N_EXPERTS, next_owner, -1)
    next_e = jnp.sum(jnp.where(blk_e[:, None] == experts[None, :], next_owner[None, :], 0), axis=1).astype(jnp.int32)
    nblk = (pend[-1:] // rb).astype(jnp.int32)
    gaps = jnp.concatenate([jnp.stack([pend - padded + counts, padded - counts], axis=1).reshape(-1),
                            nblk]).astype(jnp.int32)
    x_sorted = _dispatch(xn, pos, gaps, n_blocks * rb)
    out_sorted = _grouped_ffn(x_sorted, blk_e, nblk, fresh, run, next_e, w_gate, w_up, w_down, layer)
    out = _combine(h, w_pad, out_sorted, pos, split)
    return out if split else out[0]


def _store_heads_as_rows(ref, x):
    for j in range(N_KV_HEADS):
        ref[pl.ds(j, x.shape[0], stride=N_KV_HEADS), :] = x[:, j * HEAD_DIM:(j + 1) * HEAD_DIM]


def _proj_kernel(h_ref, gkv_ref, gq_ref, wkv_ref, wq_ref, kn_ref, qn_ref, cos_ref, sin_ref,
                 k_ref, v_ref, q_ref, kp_ref, ks_ref, vp_ref, vs_ref):
    hn = _rms(h_ref[...])
    cos = cos_ref[...]
    sin = sin_ref[...]

    def norm_rope(x, g):
        y = _rms(x) * g
        return y * cos + pltpu.roll(y, HEAD_DIM // 2, 1) * sin

    kv = jnp.dot((hn * gkv_ref[...]).astype(BF16), wkv_ref[...], preferred_element_type=F32)
    kw = N_KV_HEADS * HEAD_DIM
    k = jnp.concatenate(
        [norm_rope(kv[:, j * HEAD_DIM:(j + 1) * HEAD_DIM], kn_ref[...]) for j in range(N_KV_HEADS)], axis=1)
    v = kv[:, kw:]
    k_ref[...] = k
    v_ref[...] = v
    q = jnp.dot((hn * gq_ref[...]).astype(BF16), wq_ref[...], preferred_element_type=F32)
    q_ref[...] = jnp.concatenate(
        [norm_rope(q[:, j * HEAD_DIM:(j + 1) * HEAD_DIM], qn_ref[...]) for j in range(N_HEADS)], axis=1)

    def store_by_head(kv_refs):
        _store_heads_as_rows(kv_refs[0], k)
        _store_heads_as_rows(kv_refs[1], v)

    _store_stream_tile((kp_ref, vp_ref), (ks_ref, vs_ref), store_by_head)


def _kvq_proj(h, g_kv, g_q, w_kv, w_q, k_norm, q_norm, cos, sin):
    n, d = h.shape
    tm = TOKEN_TILE
    kw = N_KV_HEADS * HEAD_DIM
    qw = N_HEADS * HEAD_DIM
    row = lambda i: (i, 0)
    fixed = lambda i: (0, 0)
    n_prompt_tiles = n // tm - 1
    seq_tiles = cos.shape[0] // tm - 1
    table_row = lambda i: (jnp.where(i < n_prompt_tiles, lax.rem(i, seq_tiles), seq_tiles), 0)
    head_rows = tm * N_KV_HEADS
    by_head = [_prompt_spec((head_rows, HEAD_DIM), n_prompt_tiles), _sample_spec((head_rows, HEAD_DIM))]
    by_head_shapes = [jax.ShapeDtypeStruct((n_prompt_tiles * head_rows, HEAD_DIM), F32),
                      jax.ShapeDtypeStruct((head_rows, HEAD_DIM), F32)]
    return pl.pallas_call(
        _proj_kernel,
        grid=(n // tm,),
        in_specs=[
            pl.BlockSpec((tm, d), row),
            pl.BlockSpec((1, d), fixed),
            pl.BlockSpec((1, d), fixed),
            pl.BlockSpec((d, 2 * kw), fixed),
            pl.BlockSpec((d, qw), fixed),
            pl.BlockSpec((1, HEAD_DIM), fixed),
            pl.BlockSpec((1, HEAD_DIM), fixed),
            pl.BlockSpec((tm, HEAD_DIM), table_row),
            pl.BlockSpec((tm, HEAD_DIM), table_row),
        ],
        out_specs=[
            pl.BlockSpec((tm, kw), row),
            pl.BlockSpec((tm, kw), row),
            pl.BlockSpec((tm, qw), row),
        ] + by_head + by_head,
        out_shape=[
            jax.ShapeDtypeStruct((n, kw), F32),
            jax.ShapeDtypeStruct((n, kw), F32),
            jax.ShapeDtypeStruct((n, qw), F32),
        ] + by_head_shapes + by_head_shapes,
        compiler_params=_cparams(("arbitrary",)),
        name="kvq_proj",
    )(h, g_kv, g_q, w_kv, w_q, k_norm, q_norm, cos, sin)


def _top_blocks(gate, axis):
    idx = lax.broadcasted_iota(jnp.int32, gate.shape, axis).astype(F32)
    big = float(gate.shape[axis])
    sel = jnp.zeros(gate.shape, jnp.bool_)
    for _ in range(MOBA_TOP_K):
        top = jnp.max(gate, axis=axis, keepdims=True)
        first = jnp.min(jnp.where(gate == top, idx, big), axis=axis, keepdims=True)
        hit = idx == first
        sel = sel | (hit & (top > NEG_INF))
        gate = jnp.where(hit, NEG_INF, gate)
    return sel.astype(F32)


def _moba_prompt_kernel(q_ref, k_ref, v_ref, o_ref, kbf, vt, kmean, sel, s_a, s_b, m_scr, acc):
    j = pl.program_id(2)
    blk = MOBA_BLOCK
    grp = KEY_GROUP
    seq = k_ref.shape[0]
    n_blocks = seq // blk

    @pl.when(j == 0)
    def _():
        k = k_ref[...]
        kbf[...] = k.astype(BF16)
        kmean[...] = jnp.mean(k.reshape(n_blocks, blk, HEAD_DIM), axis=1)
        for n in range(n_blocks):
            vt[:HEAD_DIM, n * blk:(n + 1) * blk] = v_ref[n * blk:(n + 1) * blk, :].T.astype(BF16)
        r = lax.broadcasted_iota(jnp.int32, (ONES_ROWS, seq), 0)
        vt[HEAD_DIM:, :] = jnp.where(r == 0, 1.0, 0.0).astype(BF16)

    q2 = q_ref[...]
    qs = jnp.concatenate([q2[:, h * HEAD_DIM:(h + 1) * HEAD_DIM] for h in range(Q_PER_KV)], axis=0)
    nq = qs.shape[0]
    qt = (qs * (HEAD_DIM ** -0.5 * LOG2_E)).T.astype(BF16)

    def scores(start, n_keys):
        return jnp.dot(kbf[pl.ds(start, n_keys), :], qt, preferred_element_type=F32)

    gate = _dot_nt_3pass(kmean[...], qs)
    row = lax.broadcasted_iota(jnp.int32, gate.shape, 0)
    sel[...] = _top_blocks(jnp.where(row < j, gate, NEG_INF), 0)

    key = lax.broadcasted_iota(jnp.int32, (blk, nq), 0)
    qpos = lax.broadcasted_iota(jnp.int32, (blk, nq), 1) % blk
    own = pl.multiple_of(j * blk, blk)
    s_own = jnp.where(key <= qpos, scores(own, blk), NEG_INF)
    m_own = jnp.max(s_own, axis=0, keepdims=True)
    m_scr[...] = m_own
    acc[...] = jnp.dot(vt[:, pl.ds(own, blk)], jnp.exp2(s_own - m_own).astype(BF16), preferred_element_type=F32)

    def update(tiles, start):
        m_old = m_scr[...]
        m_new = m_old
        for s in tiles:
            m_new = jnp.maximum(m_new, jnp.max(s, axis=0, keepdims=True))
        p = jnp.concatenate([jnp.exp2(s - m_new).astype(BF16) for s in tiles], axis=0)
        alpha = jnp.exp2(m_old - m_new)
        pv = jnp.dot(vt[:, pl.ds(start, len(tiles) * blk)], p, preferred_element_type=F32)
        acc[...] = alpha * acc[...] + pv
        m_scr[...] = m_new

    def fill(buf, g):
        start = pl.multiple_of(g * (grp * blk), grp * blk)
        buf[...] = scores(start, grp * blk)

    def consume(buf, g):
        start = pl.multiple_of(g * (grp * blk), grp * blk)
        update([jnp.where(sel[pl.ds(g * grp + i, 1), :] > 0.0, buf[i * blk:(i + 1) * blk, :], NEG_INF)
                for i in range(grp)], start)

    n_groups = lax.div(j + (grp - 1), grp)
    n_pairs = lax.div(n_groups - 1, 2)
    fill(s_a, 0)

    @pl.when(n_groups > 0)
    def _():
        def pair(h, carry):
            g = 2 * h
            fill(s_b, g + 1)
            consume(s_a, g)
            fill(s_a, g + 2)
            consume(s_b, g + 1)
            return carry

        lax.fori_loop(0, n_pairs, pair, 0)
        g = 2 * n_pairs

        @pl.when(n_groups - g == 1)
        def _():
            consume(s_a, g)

        @pl.when(n_groups - g == 2)
        def _():
            fill(s_b, g + 1)
            consume(s_a, g)
            consume(s_b, g + 1)

    a = acc[...]
    o = (a[:HEAD_DIM] / a[HEAD_DIM:HEAD_DIM + 1]).T
    o_ref[...] = jnp.concatenate([o[h * blk:(h + 1) * blk, :] for h in range(Q_PER_KV)], axis=1)


def _moba_prompt(q, k, v, batch, seq):
    blk = MOBA_BLOCK
    assert seq % (KEY_GROUP * blk) == 0
    n_steps = seq // blk
    qw = Q_PER_KV * HEAD_DIM
    nq = Q_PER_KV * blk
    return pl.pallas_call(
        _moba_prompt_kernel,
        grid=(batch, N_KV_HEADS, n_steps),
        in_specs=[
            pl.BlockSpec((blk, qw), lambda b, c, j: (b * n_steps + j, c)),
            pl.BlockSpec((seq, HEAD_DIM), lambda b, c, j: (b, c)),
            pl.BlockSpec((seq, HEAD_DIM), lambda b, c, j: (b, c)),
        ],
        out_specs=pl.BlockSpec((blk, qw), lambda b, c, j: (b * n_steps + j, c)),
        out_shape=jax.ShapeDtypeStruct((batch * seq, N_HEADS * HEAD_DIM), F32),
        scratch_shapes=[
            pltpu.VMEM((seq, HEAD_DIM), BF16),
            pltpu.VMEM((HEAD_DIM + ONES_ROWS, seq), BF16),
            pltpu.VMEM((seq // blk, HEAD_DIM), F32),
            pltpu.VMEM((seq // blk, nq), F32),
            pltpu.VMEM((KEY_GROUP * blk, nq), F32),
            pltpu.VMEM((KEY_GROUP * blk, nq), F32),
            pltpu.VMEM((1, nq), F32),
            pltpu.VMEM((HEAD_DIM + ONES_ROWS, nq), F32),
        ],
        compiler_params=_cparams(("arbitrary", "arbitrary", "arbitrary")),
        name="moba_prompt",
    )(q, k, v)


def _stack_heads(q8):
    return jnp.concatenate([q8[:, h * HEAD_DIM:(h + 1) * HEAD_DIM] for h in range(N_HEADS)], axis=0)


def _sample_attn_kernel(pt_ref, q_ref, kn_ref, vn_ref, ck_hbm, cv_hbm, o_ref, pages, sems,
                        s_scr, means, sel_scr, qs_scr, m_scr, l_scr, acc, *, dec_seq, n_pages):
    pps = PAGES_PER_STEP
    t = pl.program_id(1)
    steps = pl.num_programs(1)
    n_k_steps = n_pages // pps
    rows = N_HEADS * dec_seq
    rkv = Q_PER_KV * dec_seq
    ppb = MOBA_BLOCK // PAGE_SIZE
    bps = pps // ppb
    n_blocks = n_pages // ppb
    nt_dims = (((1,), (1,)), ((), ()))

    chunk = pl.program_id(0) * steps + t
    n_chunks = pl.num_programs(0) * steps

    def start_chunk(ci):
        seq = lax.div(ci, steps)
        step = lax.rem(ci, steps)
        slot = lax.rem(ci, PAGE_SLOTS)

        def start_pages(cache_hbm, first_page):
            for r in range(pps):
                pltpu.make_async_copy(cache_hbm.at[pt_ref[seq, first_page + r]], pages.at[slot, r],
                                      sems.at[slot]).start()

        @pl.when(step < n_k_steps)
        def _():
            start_pages(ck_hbm, step * pps)

        @pl.when(step >= n_k_steps)
        def _():
            start_pages(cv_hbm, (step - n_k_steps) * pps)

    @pl.when(chunk == 0)
    def _():
        for ci in range(PAGE_SLOTS - 1):
            start_chunk(jnp.int32(ci))

    @pl.when(chunk + (PAGE_SLOTS - 1) < n_chunks)
    def _():
        start_chunk(chunk + (PAGE_SLOTS - 1))

    slot = lax.rem(chunk, PAGE_SLOTS)
    pltpu.make_async_copy(ck_hbm.at[pl.ds(0, pps)], pages.at[slot], sems.at[slot]).wait()

    def head_rows(p, c):
        return pages[slot, p, pl.ds(c, PAGE_SIZE, stride=N_KV_HEADS), :]

    @pl.when(t == 0)
    def _():
        qs_scr[...] = (_stack_heads(q_ref[...]) * (HEAD_DIM ** -0.5 * LOG2_E)).astype(BF16)
        m_scr[...] = jnp.full(m_scr.shape, NEG_INF, F32)
        l_scr[...] = jnp.zeros(l_scr.shape, F32)
        acc[...] = jnp.zeros(acc.shape, F32)

    @pl.when(t < n_k_steps)
    def _():
        qs = qs_scr[...]
        col = pl.multiple_of(t * (pps * PAGE_SIZE), pps * PAGE_SIZE)
        for c in range(N_KV_HEADS):
            kc = jnp.concatenate([head_rows(p, c) for p in range(pps)], axis=0)
            s_scr[c * rkv:(c + 1) * rkv, pl.ds(col, pps * PAGE_SIZE)] = lax.dot_general(
                qs[c * rkv:(c + 1) * rkv], kc.astype(BF16), nt_dims, preferred_element_type=F32)
            means[c, pl.ds(t * bps, bps), :] = jnp.sum(kc.reshape(bps, MOBA_BLOCK, HEAD_DIM), axis=1) / MOBA_BLOCK

    @pl.when(t == n_k_steps - 1)
    def _():
        qf = _stack_heads(q_ref[...])
        gate = jnp.concatenate(
            [lax.dot_general(qf[c * rkv:(c + 1) * rkv], means[c], nt_dims, precision=HIGHEST,
                             preferred_element_type=F32) for c in range(N_KV_HEADS)], axis=0)
        chosen = _top_blocks(gate, 1)
        sel_scr[...] = jnp.concatenate([chosen, jnp.zeros((rows, LANES - n_blocks), F32)], axis=1)

    def softmax_step(tiles):
        m_old = m_scr[...]
        m_new = m_old
        for s in tiles:
            m_new = jnp.maximum(m_new, jnp.max(s, axis=1, keepdims=True))
        m_safe = jnp.where(m_new == NEG_INF, 0.0, m_new)
        alpha = jnp.exp2(m_old - m_safe)
        l_new = alpha * l_scr[...]
        probs = []
        for s in tiles:
            p = jnp.exp2(s - m_safe)
            l_new = l_new + jnp.sum(p, axis=1, keepdims=True)
            probs.append(p.astype(BF16))
        l_scr[...] = l_new
        m_scr[...] = m_new
        return probs, alpha

    @pl.when(t >= n_k_steps)
    def _():
        tv = t - n_k_steps
        sel = sel_scr[...]
        lane = lax.broadcasted_iota(jnp.int32, sel.shape, 1)
        tiles = []
        for b in range(bps):
            n = tv * bps + b
            chosen = jnp.sum(jnp.where(lane == n, sel, 0.0), axis=1, keepdims=True) > 0.0
            col = pl.multiple_of(n * MOBA_BLOCK, MOBA_BLOCK)
            tiles.append(jnp.where(chosen, s_scr[:, pl.ds(col, MOBA_BLOCK)], NEG_INF))
        probs, alpha = softmax_step(tiles)
        prob = jnp.concatenate(probs, axis=1)
        pv = []
        for c in range(N_KV_HEADS):
            vc = jnp.concatenate([head_rows(p, c) for p in range(pps)], axis=0).astype(BF16)
            pv.append(jnp.dot(prob[c * rkv:(c + 1) * rkv], vc, preferred_element_type=F32))
        acc[...] = alpha * acc[...] + jnp.concatenate(pv, axis=0)

    @pl.when(t == pl.num_programs(1) - 1)
    def _():
        qs = qs_scr[...]
        kn = kn_ref[...].astype(BF16)
        vn = vn_ref[...].astype(BF16)
        s = jnp.concatenate(
            [lax.dot_general(qs[c * rkv:(c + 1) * rkv], kn[:, c * HEAD_DIM:(c + 1) * HEAD_DIM], nt_dims,
                             preferred_element_type=F32) for c in range(N_KV_HEADS)], axis=0)
        r2 = lax.broadcasted_iota(jnp.int32, s.shape, 0)
        c2 = lax.broadcasted_iota(jnp.int32, s.shape, 1)
        probs, alpha = softmax_step([jnp.where(c2 <= r2 % dec_seq, s, NEG_INF)])
        pv = [jnp.dot(probs[0][c * rkv:(c + 1) * rkv], vn[:, c * HEAD_DIM:(c + 1) * HEAD_DIM],
                      preferred_element_type=F32) for c in range(N_KV_HEADS)]
        o = (alpha * acc[...] + jnp.concatenate(pv, axis=0)) / l_scr[...]
        o_ref[...] = jnp.concatenate([o[h * dec_seq:(h + 1) * dec_seq, :] for h in range(N_HEADS)], axis=1)


def _sample_attn(page_table, q, k, v, row0, cache_k2, cache_v2, dec_seq):
    n_seq, n_pages = page_table.shape
    pps = PAGES_PER_STEP
    n_k_steps = n_pages // pps
    n_blocks = n_pages * PAGE_SIZE // MOBA_BLOCK
    assert n_pages % pps == 0 and n_blocks <= LANES
    rows = N_HEADS * dec_seq
    kw = N_KV_HEADS * HEAD_DIM
    qw = N_HEADS * HEAD_DIM
    grid_spec = pltpu.PrefetchScalarGridSpec(
        num_scalar_prefetch=1,
        grid=(n_seq, 2 * n_k_steps),
        in_specs=[
            pl.BlockSpec((dec_seq, qw), lambda s, t, pt: (row0 + s, 0)),
            pl.BlockSpec((dec_seq, kw), lambda s, t, pt: (row0 + s, 0)),
            pl.BlockSpec((dec_seq, kw), lambda s, t, pt: (row0 + s, 0)),
            pl.BlockSpec(memory_space=pl.ANY),
            pl.BlockSpec(memory_space=pl.ANY),
        ],
        out_specs=pl.BlockSpec((dec_seq, qw), lambda s, t, pt: (s, 0)),
        scratch_shapes=[
            pltpu.VMEM((PAGE_SLOTS, pps, PAGE_SIZE * N_KV_HEADS, HEAD_DIM), F32),
            pltpu.SemaphoreType.DMA((PAGE_SLOTS,)),
            pltpu.VMEM((rows, n_pages * PAGE_SIZE), F32),
            pltpu.VMEM((N_KV_HEADS, n_blocks, HEAD_DIM), F32),
            pltpu.VMEM((rows, LANES), F32),
            pltpu.VMEM((rows, HEAD_DIM), BF16),
            pltpu.VMEM((rows, 1), F32),
            pltpu.VMEM((rows, 1), F32),
            pltpu.VMEM((rows, HEAD_DIM), F32),
        ],
    )
    return pl.pallas_call(
        functools.partial(_sample_attn_kernel, dec_seq=dec_seq, n_pages=n_pages),
        grid_spec=grid_spec,
        out_shape=jax.ShapeDtypeStruct((n_seq * dec_seq, qw), F32),
        compiler_params=_cparams(("arbitrary", "arbitrary")),
        name="sample_attn",
    )(page_table, q, k, v, cache_k2, cache_v2)


def _mix_tables(w_s, b_s, dec_seq):
    tm = TOKEN_TILE
    causal = jnp.tril(jnp.ones((GMLP_CHUNK, GMLP_CHUNK), dtype=bool))
    w = jnp.where(causal[None], w_s, jnp.zeros_like(w_s))
    eye_p = jnp.eye(tm // GMLP_CHUNK, dtype=w.dtype)
    mix_p = jnp.einsum("ab,gts->gatbs", eye_p, w).reshape(GMLP_GROUPS, tm, tm)
    eye_s = jnp.eye(tm // dec_seq, dtype=w.dtype)
    mix_s = jnp.einsum("ab,gts->gatbs", eye_s, w[:, :dec_seq, :dec_seq]).reshape(GMLP_GROUPS, tm, tm)
    mix = jnp.stack([mix_p, mix_s]).astype(BF16)
    bias_p = jnp.tile(b_s.T, (tm // GMLP_CHUNK, 1))
    bias_s = jnp.tile(b_s.T[:dec_seq], (tm // dec_seq, 1))
    bias = jnp.stack([bias_p, bias_s])
    bias = jnp.pad(bias, ((0, 0), (0, 0), (0, LANES - GMLP_GROUPS)))
    return mix, bias


def _rope_tables(pos):
    half = HEAD_DIM // 2
    inv = ROPE_THETA ** (-jnp.arange(half, dtype=F32) * 2.0 / HEAD_DIM)
    ang = pos.astype(F32)[:, None] * inv[None, :]
    cos = jnp.cos(ang)
    sin = jnp.sin(ang)
    return jnp.concatenate([cos, cos], axis=1), jnp.concatenate([-sin, sin], axis=1)


def _router_tables(w_grp, b_grp, w_rt, b_rt):
    w = jnp.concatenate([w_grp, w_rt], axis=1)
    b = jnp.concatenate([b_grp, b_rt], axis=0)
    pad = LANES - w.shape[1]
    return jnp.pad(w, ((0, 0), (0, pad))), jnp.pad(b, (0, pad)).reshape(1, LANES)


def kernel(x_prompt, x_sample, cache_k, cache_v, page_table, norm_mix, norm_ffn, a_w_in, a_ln_g, a_ln_b,
           a_w_s, a_b_s, a_w_out, kv_norm, w_kv, k_norm, b_w_q, b_q_norm, b_w_o, moe_w_grp, moe_b_grp,
           moe_w_rt, moe_b_rt, moe_w_gate, moe_w_up, moe_w_down):
    batch, seq, d = x_prompt.shape
    n_seq, dec_seq, _ = x_sample.shape
    n_prompt = batch * seq
    n_sample = n_seq * dec_seq
    assert n_prompt % TOKEN_TILE == 0 and n_sample == TOKEN_TILE and seq % MOBA_BLOCK == 0
    past_len = page_table.shape[1] * PAGE_SIZE
    assert past_len % MOBA_BLOCK == 0 and dec_seq <= MOBA_BLOCK

    pos = jnp.concatenate([jnp.arange(seq), jnp.tile(past_len + jnp.arange(dec_seq), n_seq)])
    cos, sin = _rope_tables(pos)
    row = lambda a: a.reshape(1, -1)

    mix, bias = _mix_tables(a_w_s[0], a_b_s[0], dec_seq)
    h, vg_sample = _gmlp_layer(x_prompt.reshape(n_prompt, d), x_sample.reshape(n_sample, d), row(norm_mix[0]),
                               a_w_in[0].astype(BF16), row(a_ln_g[0]), row(a_ln_b[0]), mix, bias,
                               a_w_out[0].astype(BF16))
    moe = []
    for layer in range(2):
        w_r, b_r = _router_tables(moe_w_grp[layer], moe_b_grp[layer], moe_w_rt[layer], moe_b_rt[layer])
        moe.append((row(norm_ffn[layer]), w_r, b_r, moe_w_gate, moe_w_up, moe_w_down, layer))
    h = _hier_moe(h, *moe[0])

    k, v, q, k_p, k_s, v_p, v_s = _kvq_proj(h, row(kv_norm), row(norm_mix[1]), w_kv.astype(BF16),
                                            b_w_q[0].astype(BF16), row(k_norm), row(b_q_norm[0]), cos, sin)

    o_prompt = _moba_prompt(q, k, v, batch, seq)
    n_phys = cache_k.shape[0]
    cache_k2 = cache_k.reshape(n_phys, PAGE_SIZE * N_KV_HEADS, HEAD_DIM)
    cache_v2 = cache_v.reshape(n_phys, PAGE_SIZE * N_KV_HEADS, HEAD_DIM)
    o_sample = _sample_attn(page_table, q, k, v, n_prompt // dec_seq, cache_k2, cache_v2, dec_seq)
    y_prompt, y_sample = _hier_moe(h, *moe[1], split=True, attn=(o_prompt, o_sample, b_w_o[0].astype(BF16)))

    n_pages_new = seq // PAGE_SIZE
    return (y_prompt.reshape(batch, seq, d),
            y_sample.reshape(n_seq, dec_seq, d),
            k_p.reshape(batch, n_pages_new, PAGE_SIZE, N_KV_HEADS, HEAD_DIM),
            v_p.reshape(batch, n_pages_new, PAGE_SIZE, N_KV_HEADS, HEAD_DIM),
            k_s.reshape(n_seq, dec_seq, N_KV_HEADS, HEAD_DIM),
            v_s.reshape(n_seq, dec_seq, N_KV_HEADS, HEAD_DIM),
            vg_sample.reshape(1, n_seq, dec_seq, -1))
```
